```python
import jax, jax.numpy as jnp
from jax import lax
import numpy as np

D_MODEL = 1024
BATCH = 8
SEQ = 2048
DEPTH = 2
DEC_BATCH = 128
DEC_SEQ = 1
PAST_LEN = 16384
PAGE_SIZE = 128

EPS = 1e-6
N_MEM = 256
XA_HEADS = 4
XA_HEAD_DIM = D_MODEL // XA_HEADS
N_BRANCH = 3
BRANCH_W = D_MODEL // 2
GLA_HEADS = 4
GLA_DV = BRANCH_W // GLA_HEADS
GLA_DK = GLA_DV // 2
GLA_RANK = 16
GLA_GATE_NORM = 16.0
GLA_CHUNK = 32
MLSTM_HEADS = 4
MLSTM_DV = BRANCH_W // MLSTM_HEADS
MLSTM_DQK = MLSTM_DV // 2
MLSTM_CHUNK = 64
RG_WIDTH = BRANCH_W
RG_BLOCKS = 8
RG_BLOCK = RG_WIDTH // RG_BLOCKS
RG_CONV = 4
RG_C = 8.0
D_FF = -(-8 * D_MODEL // (3 * 256)) * 256
IN_WIDTHS = (GLA_HEADS * GLA_DK, GLA_HEADS * GLA_DK, GLA_HEADS * GLA_DV, GLA_HEADS * GLA_DV, GLA_RANK,
             MLSTM_HEADS * MLSTM_DQK, MLSTM_HEADS * MLSTM_DQK, MLSTM_HEADS * MLSTM_DV, MLSTM_HEADS * MLSTM_DV,
             MLSTM_HEADS, MLSTM_HEADS, RG_WIDTH, RG_WIDTH, N_BRANCH * D_MODEL)
W_IN_COLS = sum(IN_WIDTHS)

kernel_name = 'hybrid_gla_mlstm_rglru_decoder_step'


def rmsnorm(x, g):
    xf = x.astype(jnp.float32)
    y = xf * lax.rsqrt(jnp.mean(xf * xf, axis=-1, keepdims=True) + EPS)
    return (y * g.astype(jnp.float32)).astype(x.dtype)


def _split_points():
    pts, acc = [], 0
    for w in IN_WIDTHS[:-1]:
        acc += w
        pts.append(acc)
    return pts


def _pad_time(x, pad, value=0.0):
    if pad == 0:
        return x
    widths = [(0, 0)] * x.ndim
    widths[1] = (0, pad)
    return jnp.pad(x, widths, constant_values=value)


def _to_chunks(x, L):
    B, T = x.shape[:2]
    x = x.reshape((B, T // L, L) + x.shape[2:])
    return x.transpose((1, 0, 3, 2) + tuple(range(4, x.ndim)))


def _from_chunks(x, T):
    n, B, H, L = x.shape[:4]
    x = x.transpose((1, 0, 3, 2) + tuple(range(4, x.ndim))).reshape((B, n * L, H) + x.shape[4:])
    return x[:, :T]


def gla_chunked(q, k, v, log_a, s0):
    T = q.shape[1]
    L = min(GLA_CHUNK, T)
    pad = (-T) % L
    q, k, v, log_a = [_to_chunks(_pad_time(t.astype(jnp.float32), pad), L) for t in (q, k, v, log_a)]
    b_cum = jnp.cumsum(log_a, axis=3)
    b_end = b_cum[:, :, :, -1:, :]
    q_in = q * jnp.exp(b_cum)
    k_in = k * jnp.exp(-b_cum)
    k_end = k * jnp.exp(b_end - b_cum)
    decay_end = jnp.exp(b_end[:, :, :, 0, :])
    causal = jnp.tril(jnp.ones((L, L), jnp.bool_))

    def step(S, inp):
        qi, ki, ke, vc, de = inp
        att = jnp.where(causal, jnp.einsum('bhtk,bhsk->bhts', qi, ki), 0.0)
        o = jnp.einsum('bhtk,bhkv->bhtv', qi, S) + jnp.einsum('bhts,bhsv->bhtv', att, vc)
        S = de[..., None] * S + jnp.einsum('bhsk,bhsv->bhkv', ke, vc)
        return S, o

    s_final, o = lax.scan(step, s0.astype(jnp.float32), (q_in, k_in, k_end, v, decay_end))
    return _from_chunks(o, T), s_final


def mlstm_chunked(q, k, v, i_pre, log_f, c0, n0, m0):
    T = q.shape[1]
    L = min(MLSTM_CHUNK, T)
    pad = (-T) % L
    q, k, v = [_to_chunks(_pad_time(t.astype(jnp.float32), pad), L) for t in (q, k, v)]
    i_c = _to_chunks(_pad_time(i_pre.astype(jnp.float32), pad, -jnp.inf), L)
    f_cum = jnp.cumsum(_to_chunks(_pad_time(log_f.astype(jnp.float32), pad), L), axis=-1)
    causal = jnp.tril(jnp.ones((L, L), jnp.bool_))

    def step(carry, inp):
        C, nv, m = carry
        qc, kc, vc, ic, fc = inp
        d = jnp.where(causal, fc[..., :, None] - fc[..., None, :] + ic[..., None, :], -jnp.inf)
        inter = fc + m[..., None]
        m_t = jnp.maximum(inter, jnp.max(d, axis=-1))
        w_intra = jnp.exp(d - m_t[..., None])
        w_inter = jnp.exp(inter - m_t)
        s = jnp.einsum('bhtk,bhsk->bhts', qc, kc) * w_intra
        num = w_inter[..., None] * jnp.einsum('bhtk,bhkv->bhtv', qc, C) + jnp.einsum('bhts,bhsv->bhtv', s, vc)
        den = w_inter * jnp.einsum('bhtk,bhk->bht', qc, nv) + jnp.sum(s, axis=-1)
        h = num / jnp.maximum(jnp.abs(den), jnp.exp(-m_t))[..., None]
        f_end = fc[..., -1]
        m_end = m_t[..., -1]
        w_k = jnp.exp(f_end[..., None] - fc + ic - m_end[..., None])
        w_c = jnp.exp(f_end + m - m_end)
        C = w_c[..., None, None] * C + jnp.einsum('bhs,bhsk,bhsv->bhkv', w_k, kc, vc)
        nv = w_c[..., None] * nv + jnp.einsum('bhs,bhsk->bhk', w_k, kc)
        return (C, nv, m_end), h

    carry0 = (c0.astype(jnp.float32), n0.astype(jnp.float32), m0.astype(jnp.float32))
    (c_f, n_f, m_f), h = lax.scan(step, carry0, (q, k, v, i_c, f_cum))
    return _from_chunks(h, T), c_f, n_f, m_f


def rglru_branch(x_in, gate_pre, conv_buf, h0, conv_w, conv_b, w_r, b_r, w_i, b_i, lam):
    B, T, W = x_in.shape
    xcat = jnp.concatenate([conv_buf.astype(x_in.dtype), x_in], axis=1)
    xc = conv_b + sum(xcat[:, j:j + T] * conv_w[j] for j in range(RG_CONV))
    new_buf = xcat[:, T:]
    xb = xc.reshape(B, T, RG_BLOCKS, RG_BLOCK)
    r = jax.nn.sigmoid(jnp.einsum('btnd,nde->btne', xb, w_r).reshape(B, T, W) + b_r)
    i = jax.nn.sigmoid(jnp.einsum('btnd,nde->btne', xb, w_i).reshape(B, T, W) + b_i)
    log_a = (-RG_C * r.astype(jnp.float32)) * jax.nn.softplus(-lam.astype(jnp.float32))
    a = jnp.exp(log_a)
    b = jnp.sqrt(-jnp.expm1(2.0 * log_a)) * (i * xc).astype(jnp.float32)
    b = b.at[:, 0].add(a[:, 0] * h0.astype(jnp.float32))

    def combine(left, right):
        a_l, b_l = left
        a_r, b_r = right
        return a_l * a_r, a_r * b_l + b_r

    _, h = lax.associative_scan(combine, (a, b), axis=1)
    out = h.astype(x_in.dtype) * jax.nn.gelu(gate_pre)
    return out, h[:, -1], new_buf


def token_mixers(u, st, p):
    B, T, _ = u.shape
    gla_s, ml_c, ml_n, ml_m, rg_h, rg_buf = st
    proj = u @ p['w_in']
    (g_q, g_k, g_v, g_g, g_a, m_q, m_k, m_v, m_o, m_i, m_f, r_x, r_g, gates) = jnp.split(proj, _split_points(), axis=-1)
    q = g_q.reshape(B, T, GLA_HEADS, GLA_DK) * GLA_DK ** -0.5
    k = g_k.reshape(B, T, GLA_HEADS, GLA_DK)
    v = g_v.reshape(B, T, GLA_HEADS, GLA_DV)
    log_a = jax.nn.log_sigmoid((g_a @ p['gla_w_a2'] + p['gla_b_a']).astype(jnp.float32)) / GLA_GATE_NORM
    o_gla, gla_s = gla_chunked(q, k, v, log_a.reshape(B, T, GLA_HEADS, GLA_DK), gla_s)
    o_gla = rmsnorm(o_gla.astype(u.dtype), p['gla_norm']) * jax.nn.silu(g_g.reshape(B, T, GLA_HEADS, GLA_DV))
    o_gla = o_gla.reshape(B, T, BRANCH_W)
    q = m_q.reshape(B, T, MLSTM_HEADS, MLSTM_DQK) * MLSTM_DQK ** -0.5
    k = m_k.reshape(B, T, MLSTM_HEADS, MLSTM_DQK)
    v = m_v.reshape(B, T, MLSTM_HEADS, MLSTM_DV)
    i_pre = m_i + p['mlstm_b_i']
    log_f = jax.nn.log_sigmoid((m_f + p['mlstm_b_f']).astype(jnp.float32))
    h_ml, ml_c, ml_n, ml_m = mlstm_chunked(q, k, v, i_pre, log_f, ml_c, ml_n, ml_m)
    h_ml = rmsnorm(h_ml.astype(u.dtype), p['mlstm_norm'].reshape(MLSTM_HEADS, MLSTM_DV)) * jax.nn.sigmoid(m_o.reshape(B, T, MLSTM_HEADS, MLSTM_DV))
    h_ml = h_ml.reshape(B, T, BRANCH_W)
    o_rg, rg_h, rg_buf = rglru_branch(r_x, r_g, rg_buf, rg_h, p['rg_conv_w'], p['rg_conv_b'], p['rg_w_r'], p['rg_b_r'], p['rg_w_i'], p['rg_b_i'], p['rg_lambda'])
    branches = jnp.stack([o_gla, h_ml, o_rg], axis=2)
    z = jnp.einsum('btnc,ncd->btnd', branches, p['w_branch'])
    merged = jnp.sum(jax.nn.sigmoid(gates.reshape(B, T, N_BRANCH, D_MODEL)) * z, axis=2)
    return merged @ p['w_out'], (gla_s, ml_c, ml_n, ml_m, rg_h, rg_buf)


def memory_kv(mem, g, wk, wv):
    B = mem.shape[0]
    m = rmsnorm(mem, g)
    k = (m @ wk).reshape(B, N_MEM, XA_HEADS, XA_HEAD_DIM)
    v = (m @ wv).reshape(B, N_MEM, XA_HEADS, XA_HEAD_DIM)
    return k, v


def cross_attend(u, mem_k, mem_v, wq, wo):
    B, T, _ = u.shape
    q = (u @ wq).reshape(B, T, XA_HEADS, XA_HEAD_DIM)
    s = jnp.einsum('bthd,bmhd->bhtm', q, mem_k.astype(q.dtype)).astype(jnp.float32) * XA_HEAD_DIM ** -0.5
    pr = jax.nn.softmax(s, axis=-1).astype(u.dtype)
    o = jnp.einsum('bhtm,bmhd->bthd', pr, mem_v.astype(u.dtype)).reshape(B, T, D_MODEL)
    return o @ wo


def swiglu(u, wg, wu, wd):
    return (jax.nn.silu(u @ wg) * (u @ wu)) @ wd


def decoder_layer(x, mem_k, mem_v, st, p):
    h, st = token_mixers(rmsnorm(x, p['norm_mix']), st, p)
    x = x + h
    x = x + cross_attend(rmsnorm(x, p['norm_xa']), mem_k, mem_v, p['xa_wq'], p['xa_wo'])
    x = x + swiglu(rmsnorm(x, p['norm_ffn']), p['ffn_w_gate'], p['ffn_w_up'], p['ffn_w_down'])
    return x, st


def zero_state(b):
    return (jnp.zeros((b, GLA_HEADS, GLA_DK, GLA_DV), jnp.float32),
            jnp.zeros((b, MLSTM_HEADS, MLSTM_DQK, MLSTM_DV), jnp.float32),
            jnp.zeros((b, MLSTM_HEADS, MLSTM_DQK), jnp.float32),
            jnp.zeros((b, MLSTM_HEADS), jnp.float32),
            jnp.zeros((b, RG_WIDTH), jnp.float32),
            jnp.zeros((b, RG_CONV - 1, RG_WIDTH), jnp.float32))


def setup_inputs(seed: int = 0) -> dict:
    key = jax.random.key(seed)
    ks = iter(jax.random.split(key, 64))
    f32 = jnp.float32

    def nrm(shape, scale):
        return scale * jax.random.normal(next(ks), shape, f32)

    def gain(shape):
        return 1.0 + 0.05 * jax.random.normal(next(ks), shape, f32)

    a0 = jax.random.uniform(next(ks), (DEPTH, RG_WIDTH), f32, 0.9, 0.999)
    pw = a0 ** (1.0 / RG_C)
    rg_lambda = jnp.log(pw) - jnp.log1p(-pw)
    return {
        'x_prompt': nrm((BATCH, SEQ, D_MODEL), 1.0),
        'x_sample': nrm((DEC_BATCH, DEC_SEQ, D_MODEL), 1.0),
        'mem_prompt': nrm((BATCH, N_MEM, D_MODEL), 1.0),
        'cache_mem_k': nrm((DEPTH, DEC_BATCH, N_MEM, XA_HEADS, XA_HEAD_DIM), 1.0),
        'cache_mem_v': nrm((DEPTH, DEC_BATCH, N_MEM, XA_HEADS, XA_HEAD_DIM), 1.0),
        'state_gla': nrm((DEPTH, DEC_BATCH, GLA_HEADS, GLA_DK, GLA_DV), 0.5),
        'state_mlstm_c': nrm((DEPTH, DEC_BATCH, MLSTM_HEADS, MLSTM_DQK, MLSTM_DV), 0.5),
        'state_mlstm_n': nrm((DEPTH, DEC_BATCH, MLSTM_HEADS, MLSTM_DQK), 0.5),
        'state_mlstm_m': nrm((DEPTH, DEC_BATCH, MLSTM_HEADS), 1.0),
        'state_rglru_h': nrm((DEPTH, DEC_BATCH, RG_WIDTH), 0.5),
        'state_rglru_conv': nrm((DEPTH, DEC_BATCH, RG_CONV - 1, RG_WIDTH), 1.0),
        'norm_mix': gain((DEPTH, D_MODEL)),
        'w_in': nrm((DEPTH, D_MODEL, W_IN_COLS), D_MODEL ** -0.5),
        'gla_w_a2': nrm((DEPTH, GLA_RANK, GLA_HEADS * GLA_DK), GLA_RANK ** -0.5),
        'gla_b_a': nrm((DEPTH, GLA_HEADS * GLA_DK), 0.1),
        'gla_norm': gain((DEPTH, GLA_DV)),
        'mlstm_b_i': nrm((DEPTH, MLSTM_HEADS), 0.1),
        'mlstm_b_f': jnp.linspace(3.0, 6.0, MLSTM_HEADS, dtype=f32) + nrm((DEPTH, MLSTM_HEADS), 0.1),
        'mlstm_norm': gain((DEPTH, MLSTM_HEADS * MLSTM_DV)),
        'rg_conv_w': nrm((DEPTH, RG_CONV, RG_WIDTH), RG_CONV ** -0.5),
        'rg_conv_b': nrm((DEPTH, RG_WIDTH), 0.02),
        'rg_w_r': nrm((DEPTH, RG_BLOCKS, RG_BLOCK, RG_BLOCK), RG_BLOCK ** -0.5),
        'rg_b_r': nrm((DEPTH, RG_WIDTH), 0.02),
        'rg_w_i': nrm((DEPTH, RG_BLOCKS, RG_BLOCK, RG_BLOCK), RG_BLOCK ** -0.5),
        'rg_b_i': nrm((DEPTH, RG_WIDTH), 0.02),
        'rg_lambda': rg_lambda,
        'w_branch': nrm((DEPTH, N_BRANCH, BRANCH_W, D_MODEL), BRANCH_W ** -0.5),
        'w_out': nrm((DEPTH, D_MODEL, D_MODEL), D_MODEL ** -0.5),
        'norm_xa': gain((DEPTH, D_MODEL)),
        'norm_mem': gain((DEPTH, D_MODEL)),
        'xa_wq': nrm((DEPTH, D_MODEL, D_MODEL), D_MODEL ** -0.5),
        'xa_wk': nrm((DEPTH, D_MODEL, D_MODEL), D_MODEL ** -0.5),
        'xa_wv': nrm((DEPTH, D_MODEL, D_MODEL), D_MODEL ** -0.5),
        'xa_wo': nrm((DEPTH, D_MODEL, D_MODEL), D_MODEL ** -0.5),
        'norm_ffn': gain((DEPTH, D_MODEL)),
        'ffn_w_gate': nrm((DEPTH, D_MODEL, D_FF), D_MODEL ** -0.5),
        'ffn_w_up': nrm((DEPTH, D_MODEL, D_FF), D_MODEL ** -0.5),
        'ffn_w_down': nrm((DEPTH, D_FF, D_MODEL), D_FF ** -0.5),
        'norm_final': gain((D_MODEL,)),
    }


def reference(x_prompt, x_sample, mem_prompt, cache_mem_k, cache_mem_v, state_gla, state_mlstm_c,
              state_mlstm_n, state_mlstm_m, state_rglru_h, state_rglru_conv,
              norm_mix, w_in, gla_w_a2, gla_b_a, gla_norm, mlstm_b_i, mlstm_b_f, mlstm_norm,
              rg_conv_w, rg_conv_b, rg_w_r, rg_b_r, rg_w_i, rg_b_i, rg_lambda, w_branch, w_out,
              norm_xa, norm_mem, xa_wq, xa_wk, xa_wv, xa_wo, norm_ffn, ffn_w_gate, ffn_w_up,
              ffn_w_down, norm_final):
    xp, xs = x_prompt, x_sample
    n_prompt = xp.shape[0]
    new_p = [[] for _ in range(8)]
    new_s = [[] for _ in range(6)]
    for l in range(DEPTH):
        p = {'norm_mix': norm_mix[l], 'w_in': w_in[l], 'gla_w_a2': gla_w_a2[l], 'gla_b_a': gla_b_a[l],
             'gla_norm': gla_norm[l], 'mlstm_b_i': mlstm_b_i[l], 'mlstm_b_f': mlstm_b_f[l],
             'mlstm_norm': mlstm_norm[l], 'rg_conv_w': rg_conv_w[l], 'rg_conv_b': rg_conv_b[l],
             'rg_w_r': rg_w_r[l], 'rg_b_r': rg_b_r[l], 'rg_w_i': rg_w_i[l], 'rg_b_i': rg_b_i[l],
             'rg_lambda': rg_lambda[l], 'w_branch': w_branch[l], 'w_out': w_out[l],
             'norm_xa': norm_xa[l], 'xa_wq': xa_wq[l], 'xa_wo': xa_wo[l], 'norm_ffn': norm_ffn[l],
             'ffn_w_gate': ffn_w_gate[l], 'ffn_w_up': ffn_w_up[l], 'ffn_w_down': ffn_w_down[l]}
        k_p, v_p = memory_kv(mem_prompt, norm_mem[l], xa_wk[l], xa_wv[l])
        xp, st_p = decoder_layer(xp, k_p, v_p, zero_state(n_prompt), p)
        st_s = (state_gla[l], state_mlstm_c[l], state_mlstm_n[l], state_mlstm_m[l], state_rglru_h[l], state_rglru_conv[l])
        xs, st_s = decoder_layer(xs, cache_mem_k[l], cache_mem_v[l], st_s, p)
        for lst, val in zip(new_p, (k_p, v_p) + st_p):
            lst.append(val)
        for lst, val in zip(new_s, st_s):
            lst.append(val)
    y_prompt = rmsnorm(xp, norm_final)
    y_sample = rmsnorm(xs, norm_final)
    mem_k_p, mem_v_p, gla_p, c_p, n_p, m_p, h_p, conv_p = [jnp.stack(v, axis=0) for v in new_p]
    gla_s, c_s, n_s, m_s, h_s, conv_s = [jnp.stack(v, axis=0) for v in new_s]
    return (y_prompt, y_sample, mem_k_p, mem_v_p, gla_p, c_p, n_p, m_p, h_p, conv_p,
            gla_s, c_s, n_s, m_s, h_s, conv_s)
```

```python
import functools
import math

import jax
import jax.numpy as jnp
from jax import lax
from jax.experimental import pallas as pl
from jax.experimental.pallas import tpu as pltpu

F32 = jnp.float32
BF16 = jnp.bfloat16

D_MODEL = 1024
DEPTH = 2
EPS = 1e-6
N_MEM = 256
XA_HEADS = 4
XA_HEAD_DIM = D_MODEL // XA_HEADS
N_BRANCH = 3
BRANCH_W = D_MODEL // 2
HEADS = 4
DK = 64
DV = 128
GLA_RANK = 16
GLA_GATE_NORM = 16.0
GLA_CHUNK = 32
MLSTM_CHUNK = 64
RG_WIDTH = BRANCH_W
RG_BLOCKS = 8
RG_BLOCK = RG_WIDTH // RG_BLOCKS
RG_CONV = 4
RG_C = 8.0
D_FF = 2816

C_GQ, C_GK, C_GV, C_GG = 0, 256, 512, 1024
C_MQ, C_MK, C_MV, C_MO = 1536, 1792, 2048, 2560
C_RX, C_RG = 3072, 3584
MIX_COLS = 4096
GATE_COLS = N_BRANCH * D_MODEL
MAIN_COLS = MIX_COLS + GATE_COLS
SMALL_COLS = 128
L_GA = 0
L_MI = 16
L_MF = 20

VMEM_LIMIT = 48 * 1024 * 1024

HIGHEST = lax.Precision.HIGHEST


def _softplus(x):
    return jnp.maximum(x, 0.0) + jnp.log1p(jnp.exp(-jnp.abs(x)))


def _log_sigmoid(x):
    return -_softplus(-x)


def _sigmoid(x):
    return 1.0 / (1.0 + jnp.exp(-x))


def _silu(x):
    return x * _sigmoid(x)


def _gelu_tanh(x):
    c = math.sqrt(2.0 / math.pi)
    return x * (0.5 * (1.0 + jnp.tanh(c * (x + 0.044715 * (x * x * x)))))


def _neg_expm1(x):
    return -jnp.tanh(0.5 * x) * (jnp.exp(x) + 1.0)


def _rms(x, g):
    ms = jnp.mean(x * x, axis=-1, keepdims=True)
    return x * lax.rsqrt(ms + EPS) * g


def _dot(a, b):
    return jnp.dot(a, b, preferred_element_type=F32)


def _dot_nt(a, b):
    return lax.dot_general(a, b, (((1,), (1,)), ((), ())), preferred_element_type=F32)


def _dot_tn(a, b):
    return lax.dot_general(a, b, (((0,), (0,)), ((), ())), preferred_element_type=F32)


def _eye(n):
    return lax.broadcasted_iota(jnp.int32, (n, n), 0) == lax.broadcasted_iota(jnp.int32, (n, n), 1)


def _row_to_col(row, eye):
    n = eye.shape[0]
    return jnp.sum(jnp.where(eye, jnp.broadcast_to(row, (n, n)), 0.0), axis=1, keepdims=True)


def _col_to_row(col, eye):
    n = eye.shape[0]
    return jnp.sum(jnp.where(eye, jnp.broadcast_to(col, (n, n)), 0.0), axis=0, keepdims=True)


def _chunk_masks(n, chunk):
    shift = chunk.bit_length() - 1
    r = lax.broadcasted_iota(jnp.int32, (n, n), 0)
    c = lax.broadcasted_iota(jnp.int32, (n, n), 1)
    same = lax.shift_right_logical(r, shift) == lax.shift_right_logical(c, shift)
    tril = jnp.where(same & (c <= r), 1.0, 0.0).astype(F32)
    full = jnp.where(same, 1.0, 0.0).astype(F32)
    return tril, full


def _norm_matmul_kernel(x_ref, g_ref, w_ref, o_ref, u_ref):
    @pl.when(pl.program_id(1) == 0)
    def _():
        u_ref[...] = _rms(x_ref[...], g_ref[...]).astype(BF16)

    o_ref[...] = _dot(u_ref[...], w_ref[...]).astype(o_ref.dtype)


def _norm_matmul(x, g, w, *, tm, tn, out_dtype=F32):
    n, d = x.shape
    nout = w.shape[1]
    return pl.pallas_call(
        _norm_matmul_kernel,
        grid=(n // tm, nout // tn),
        in_specs=[pl.BlockSpec((tm, d), lambda i, j: (i, 0)),
                  pl.BlockSpec((1, d), lambda i, j: (0, 0)),
                  pl.BlockSpec((d, tn), lambda i, j: (0, j))],
        out_specs=pl.BlockSpec((tm, tn), lambda i, j: (i, j)),
        out_shape=jax.ShapeDtypeStruct((n, nout), out_dtype),
        scratch_shapes=[pltpu.VMEM((tm, d), BF16)],
        compiler_params=pltpu.CompilerParams(
            dimension_semantics=("parallel", "arbitrary"), vmem_limit_bytes=VMEM_LIMIT),
        name="norm_matmul",
    )(x, g.reshape(1, d), w)


def _in_proj_kernel(x_ref, g_ref, w_ref, ws_ref, o_ref, os_ref, u_ref):
    @pl.when(pl.program_id(1) == 0)
    def _():
        u = _rms(x_ref[...], g_ref[...]).astype(BF16)
        u_ref[...] = u
        os_ref[...] = _dot(u, ws_ref[...])

    o_ref[...] = _dot(u_ref[...], w_ref[...])


def _in_proj(x, g, w_main, w_small, *, tm, tn):
    n, d = x.shape
    return pl.pallas_call(
        _in_proj_kernel,
        grid=(n // tm, MAIN_COLS // tn),
        in_specs=[pl.BlockSpec((tm, d), lambda i, j: (i, 0)),
                  pl.BlockSpec((1, d), lambda i, j: (0, 0)),
                  pl.BlockSpec((d, tn), lambda i, j: (0, j)),
                  pl.BlockSpec((d, SMALL_COLS), lambda i, j: (0, 0))],
        out_specs=[pl.BlockSpec((tm, tn), lambda i, j: (i, j)),
                   pl.BlockSpec((tm, SMALL_COLS), lambda i, j: (i, 0))],
        out_shape=[jax.ShapeDtypeStruct((n, MAIN_COLS), F32),
                   jax.ShapeDtypeStruct((n, SMALL_COLS), F32)],
        scratch_shapes=[pltpu.VMEM((tm, d), BF16)],
        compiler_params=pltpu.CompilerParams(
            dimension_semantics=("parallel", "arbitrary"), vmem_limit_bytes=VMEM_LIMIT),
        name="in_proj",
    )(x, g.reshape(1, d), w_main, w_small)


def _rg_gates(xc, wr_ref, br_ref, wi_ref, bi_ref, lam_ref):
    xcb = xc.astype(BF16)
    r = _sigmoid(_dot(xcb, wr_ref[...]) + br_ref[...])
    i = _sigmoid(_dot(xcb, wi_ref[...]) + bi_ref[...])
    log_a = (-RG_C * r) * _softplus(-lam_ref[...])
    a = jnp.exp(log_a)
    b = jnp.sqrt(_neg_expm1(2.0 * log_a)) * (i * xc)
    return a, b


def _mix_prompt_kernel(pm_ref, ps_ref, wa2_ref, ba_ref, gnorm_ref, bif_ref, mnorm_ref,
                       convw_ref, convb_ref, wr_ref, br_ref, wi_ref, bi_ref, lam_ref,
                       br_out_ref, s_ref, c_ref, n_ref, m_ref, h_ref, buf_ref,
                       qi_ref, ki_ref, ke_ref, de_ref, og_ref, fc_ref, pre_ref, xpad_ref):
    tt = pm_ref.shape[0]
    j = pl.program_id(1)

    @pl.when(j == 0)
    def _():
        s_ref[...] = jnp.zeros_like(s_ref)
        c_ref[...] = jnp.zeros_like(c_ref)
        n_ref[...] = jnp.zeros_like(n_ref)
        m_ref[...] = jnp.zeros_like(m_ref)
        h_ref[...] = jnp.zeros_like(h_ref)
        xpad_ref[pl.ds(0, 8), :] = jnp.zeros((8, RG_WIDTH), F32)

    ps = ps_ref[...]

    a_pre = _dot(ps.astype(BF16), wa2_ref[...]) + ba_ref[...]
    log_a = _log_sigmoid(a_pre) * (1.0 / GLA_GATE_NORM)
    tril32, full32 = _chunk_masks(tt, GLA_CHUNK)
    b_cum = jnp.dot(tril32, log_a, precision=HIGHEST, preferred_element_type=F32)
    b_end = jnp.dot(full32, log_a, precision=HIGHEST, preferred_element_type=F32)
    q = pm_ref[:, C_GQ:C_GQ + HEADS * DK] * (DK ** -0.5)
    k = pm_ref[:, C_GK:C_GK + HEADS * DK]
    qi_ref[...] = q * jnp.exp(b_cum)
    ki_ref[...] = k * jnp.exp(-b_cum)
    ke_ref[...] = k * jnp.exp(b_end - b_cum)
    de_ref[...] = jnp.exp(b_end)

    eye = _eye(DK)
    causal32 = (lax.broadcasted_iota(jnp.int32, (GLA_CHUNK, GLA_CHUNK), 1)
                <= lax.broadcasted_iota(jnp.int32, (GLA_CHUNK, GLA_CHUNK), 0))

    def gla_chunk(c, carry):
        r0 = pl.multiple_of(c * GLA_CHUNK, GLA_CHUNK)
        rows = pl.ds(r0, GLA_CHUNK)
        qi_all = qi_ref[rows, :].astype(BF16)
        ki_all = ki_ref[rows, :].astype(BF16)
        ke_all = ke_ref[rows, :].astype(BF16)
        de_all = de_ref[pl.ds(r0, 1), :]
        for h in range(HEADS):
            kcols = slice(h * DK, (h + 1) * DK)
            vcols = slice(C_GV + h * DV, C_GV + (h + 1) * DV)
            qi = qi_all[:, kcols]
            ki = ki_all[:, kcols]
            ke = ke_all[:, kcols]
            vc = pm_ref[rows, vcols].astype(BF16)
            att = jnp.where(causal32, _dot_nt(qi, ki), 0.0)
            s_old = s_ref[h]
            o = _dot(qi, s_old.astype(BF16)) + _dot(att.astype(BF16), vc)
            de_col = _row_to_col(de_all[:, kcols], eye)
            s_ref[h] = de_col * s_old + _dot_tn(ke, vc)
            og_ref[rows, h * DV:(h + 1) * DV] = o
        return carry

    lax.fori_loop(0, tt // GLA_CHUNK, gla_chunk, 0)

    gnorm = gnorm_ref[...]
    for h in range(HEADS):
        cols = slice(h * DV, (h + 1) * DV)
        y = _rms(og_ref[:, cols], gnorm) * _silu(pm_ref[:, C_GG + h * DV:C_GG + (h + 1) * DV])
        br_out_ref[:, cols] = y.astype(BF16)

    pre = ps + bif_ref[...]
    tril64, _ = _chunk_masks(tt, MLSTM_CHUNK)
    pre_ref[...] = pre
    fc_ref[...] = jnp.dot(tril64, _log_sigmoid(pre), precision=HIGHEST, preferred_element_type=F32)
    causal64 = (lax.broadcasted_iota(jnp.int32, (MLSTM_CHUNK, MLSTM_CHUNK), 1)
                <= lax.broadcasted_iota(jnp.int32, (MLSTM_CHUNK, MLSTM_CHUNK), 0))
    last = MLSTM_CHUNK - 1

    def mlstm_chunk(c, carry):
        r0 = pl.multiple_of(c * MLSTM_CHUNK, MLSTM_CHUNK)
        rows = pl.ds(r0, MLSTM_CHUNK)
        fc_all = fc_ref[rows, :]
        pre_all = pre_ref[rows, :]
        q_all = pm_ref[rows, C_MQ:C_MQ + HEADS * DK] * (DK ** -0.5)
        k_all = pm_ref[rows, C_MK:C_MK + HEADS * DK]
        for h in range(HEADS):
            kcols = slice(h * DK, (h + 1) * DK)
            fc_col = fc_all[:, L_MF + h:L_MF + h + 1]
            ic_col = pre_all[:, L_MI + h:L_MI + h + 1]
            fc_row = _col_to_row(fc_col, eye)
            ic_row = _col_to_row(ic_col, eye)
            m_prev = m_ref[:, h:h + 1]
            d = jnp.where(causal64, fc_col - fc_row + ic_row, -jnp.inf)
            inter = fc_col + m_prev
            m_t = jnp.maximum(inter, jnp.max(d, axis=1, keepdims=True))
            w_intra = jnp.exp(d - m_t)
            w_inter = jnp.exp(inter - m_t)
            qf = q_all[:, kcols]
            kf = k_all[:, kcols]
            qc = qf.astype(BF16)
            vc = pm_ref[rows, C_MV + h * DV:C_MV + (h + 1) * DV].astype(BF16)
            s = _dot_nt(qc, kf.astype(BF16)) * w_intra
            c_old = c_ref[h]
            n_old = n_ref[h]
            num = w_inter * _dot(qc, c_old.astype(BF16)) + _dot(s.astype(BF16), vc)
            den = (w_inter * jnp.sum(qf * n_old, axis=1, keepdims=True)
                   + jnp.sum(s, axis=1, keepdims=True))
            hh = num / jnp.maximum(jnp.abs(den), jnp.exp(-m_t))
            og_ref[rows, h * DV:(h + 1) * DV] = hh
            f_end = fc_col[last:last + 1, :]
            m_end = m_t[last:last + 1, :]
            w_k = jnp.exp(f_end - fc_col + ic_col - m_end)
            w_c = jnp.exp(f_end + m_prev - m_end)
            kw = w_k * kf
            c_ref[h] = w_c * c_old + _dot_tn(kw.astype(BF16), vc)
            n_ref[h] = w_c * n_old + jnp.sum(kw, axis=0, keepdims=True)
            m_ref[:, h:h + 1] = m_end
        return carry

    lax.fori_loop(0, tt // MLSTM_CHUNK, mlstm_chunk, 0)

    for h in range(HEADS):
        cols = slice(h * DV, (h + 1) * DV)
        y = (_rms(og_ref[:, cols], mnorm_ref[:, cols])
             * _sigmoid(pm_ref[:, C_MO + h * DV:C_MO + (h + 1) * DV]))
        br_out_ref[:, BRANCH_W + h * DV:BRANCH_W + (h + 1) * DV] = y.astype(BF16)

    x = pm_ref[:, C_RX:C_RX + RG_WIDTH]
    xpad_ref[pl.ds(8, tt), :] = x
    xc = (convb_ref[...] + convw_ref[3:4, :] * x
          + convw_ref[2:3, :] * xpad_ref[pl.ds(7, tt), :]
          + convw_ref[1:2, :] * xpad_ref[pl.ds(6, tt), :]
          + convw_ref[0:1, :] * xpad_ref[pl.ds(5, tt), :])
    buf_ref[...] = xpad_ref[pl.ds(tt + 8 - (RG_CONV - 1), RG_CONV - 1), :]
    xpad_ref[pl.ds(0, 8), :] = xpad_ref[pl.ds(tt, 8), :]
    a, b = _rg_gates(xc, wr_ref, br_ref, wi_ref, bi_ref, lam_ref)
    ridx = lax.broadcasted_iota(jnp.int32, (tt, RG_WIDTH), 0)
    sh = 1
    while sh < tt:
        keep = ridx >= sh
        a_sh = jnp.where(keep, pltpu.roll(a, sh, 0), 1.0)
        b_sh = jnp.where(keep, pltpu.roll(b, sh, 0), 0.0)
        b = a * b_sh + b
        a = a * a_sh
        sh *= 2
    hseq = a * h_ref[...] + b
    h_ref[...] = hseq[tt - 1:tt, :]
    y = hseq * _gelu_tanh(pm_ref[:, C_RG:C_RG + RG_WIDTH])
    br_out_ref[:, 2 * BRANCH_W:3 * BRANCH_W] = y.astype(BF16)


def _mix_prompt(pm, ps, lw, *, nb, t, tt):
    const = lambda shape: pl.BlockSpec(shape, lambda b, j: (0,) * len(shape))
    nt = t // tt
    return pl.pallas_call(
        _mix_prompt_kernel,
        grid=(nb, nt),
        in_specs=[pl.BlockSpec((tt, MIX_COLS), lambda b, j: (b * nt + j, 0)),
                  pl.BlockSpec((tt, SMALL_COLS), lambda b, j: (b * nt + j, 0)),
                  const((SMALL_COLS, HEADS * DK)), const((1, HEADS * DK)), const((1, DV)),
                  const((1, SMALL_COLS)), const((1, BRANCH_W)),
                  const((RG_CONV, RG_WIDTH)), const((1, RG_WIDTH)),
                  const((RG_WIDTH, RG_WIDTH)), const((1, RG_WIDTH)),
                  const((RG_WIDTH, RG_WIDTH)), const((1, RG_WIDTH)), const((1, RG_WIDTH))],
        out_specs=[pl.BlockSpec((tt, N_BRANCH * BRANCH_W), lambda b, j: (b * nt + j, 0)),
                   pl.BlockSpec((None, HEADS, DK, DV), lambda b, j: (b, 0, 0, 0)),
                   pl.BlockSpec((None, HEADS, DK, DV), lambda b, j: (b, 0, 0, 0)),
                   pl.BlockSpec((None, HEADS, 1, DK), lambda b, j: (b, 0, 0, 0)),
                   pl.BlockSpec((None, 1, HEADS), lambda b, j: (b, 0, 0)),
                   pl.BlockSpec((None, 1, RG_WIDTH), lambda b, j: (b, 0, 0)),
                   pl.BlockSpec((None, RG_CONV - 1, RG_WIDTH), lambda b, j: (b, 0, 0))],
        out_shape=[jax.ShapeDtypeStruct((nb * t, N_BRANCH * BRANCH_W), BF16),
                   jax.ShapeDtypeStruct((nb, HEADS, DK, DV), F32),
                   jax.ShapeDtypeStruct((nb, HEADS, DK, DV), F32),
                   jax.ShapeDtypeStruct((nb, HEADS, 1, DK), F32),
                   jax.ShapeDtypeStruct((nb, 1, HEADS), F32),
                   jax.ShapeDtypeStruct((nb, 1, RG_WIDTH), F32),
                   jax.ShapeDtypeStruct((nb, RG_CONV - 1, RG_WIDTH), F32)],
        scratch_shapes=[pltpu.VMEM((tt, HEADS * DK), F32),
                        pltpu.VMEM((tt, HEADS * DK), F32),
                        pltpu.VMEM((tt, HEADS * DK), F32),
                        pltpu.VMEM((tt, HEADS * DK), F32),
                        pltpu.VMEM((tt, BRANCH_W), F32),
                        pltpu.VMEM((tt, SMALL_COLS), F32),
                        pltpu.VMEM((tt, SMALL_COLS), F32),
                        pltpu.VMEM((tt + 8, RG_WIDTH), F32)],
        compiler_params=pltpu.CompilerParams(
            dimension_semantics=("parallel", "arbitrary"), vmem_limit_bytes=VMEM_LIMIT),
        name="mix_prompt",
    )(pm, ps, lw["wa2"], lw["ba"], lw["gnorm"], lw["bif"], lw["mnorm"], lw["convw"], lw["convb"],
      lw["wr"], lw["br"], lw["wi"], lw["bi"], lw["lam"])


def _mix_sample_kernel(pm_ref, ps_ref, s0_ref, c0_ref, n0_ref, m0_ref, h0_ref, buf0_ref,
                       wa2_ref, ba_ref, gnorm_ref, bif_ref, mnorm_ref,
                       convw_ref, convb_ref, wr_ref, br_ref, wi_ref, bi_ref, lam_ref,
                       br_out_ref, s_ref, c_ref, n_ref, m_ref, h_ref, buf_ref,
                       la_ref, pre_ref, og_ref, om_ref):
    bs = pm_ref.shape[0]
    ps = ps_ref[...]
    a_pre = _dot(ps.astype(BF16), wa2_ref[...]) + ba_ref[...]
    la_ref[...] = _log_sigmoid(a_pre) * (1.0 / GLA_GATE_NORM)
    pre = ps + bif_ref[...]
    pre_ref[...] = pre
    eye = _eye(DK)

    for s in range(bs):
        row = slice(s, s + 1)
        pre_row = pre_ref[row, :]
        lf_row = _log_sigmoid(pre_row)
        m0_row = m0_ref[row, :]
        a_all = jnp.exp(la_ref[row, :])
        gq_all = pm_ref[row, C_GQ:C_GQ + HEADS * DK] * (DK ** -0.5)
        gk_all = pm_ref[row, C_GK:C_GK + HEADS * DK]
        mq_all = pm_ref[row, C_MQ:C_MQ + HEADS * DK] * (DK ** -0.5)
        mk_all = pm_ref[row, C_MK:C_MK + HEADS * DK]
        head_lane = lax.broadcasted_iota(jnp.int32, (1, HEADS), 1)
        m_row = m0_row
        for h in range(HEADS):
            kcols = slice(h * DK, (h + 1) * DK)
            a_row = a_all[:, kcols]
            q_row = gq_all[:, kcols]
            k_row = gk_all[:, kcols]
            v_row = pm_ref[row, C_GV + h * DV:C_GV + (h + 1) * DV]
            s_new = (_row_to_col(a_row, eye) * s0_ref[s, h]
                     + _row_to_col(k_row, eye) * v_row)
            s_ref[s, h] = s_new
            og_ref[row, h * DV:(h + 1) * DV] = jnp.sum(
                _row_to_col(q_row, eye) * s_new, axis=0, keepdims=True)
            ic = pre_row[:, L_MI + h:L_MI + h + 1]
            fc = lf_row[:, L_MF + h:L_MF + h + 1]
            m0 = m0_row[:, h:h + 1]
            inter = fc + m0
            m_t = jnp.maximum(inter, ic)
            w_intra = jnp.exp(ic - m_t)
            w_inter = jnp.exp(inter - m_t)
            q_row = mq_all[:, kcols]
            k_row = mk_all[:, kcols]
            v_row = pm_ref[row, C_MV + h * DV:C_MV + (h + 1) * DV]
            sc = jnp.sum(q_row * k_row, axis=1, keepdims=True) * w_intra
            c_old = c0_ref[s, h]
            n_old = n0_ref[s, h]
            qc = jnp.sum(_row_to_col(q_row, eye) * c_old, axis=0, keepdims=True)
            num = w_inter * qc + sc * v_row
            den = w_inter * jnp.sum(q_row * n_old, axis=1, keepdims=True) + sc
            om_ref[row, h * DV:(h + 1) * DV] = num / jnp.maximum(jnp.abs(den), jnp.exp(-m_t))
            c_ref[s, h] = w_inter * c_old + (w_intra * _row_to_col(k_row, eye)) * v_row
            n_ref[s, h] = w_inter * n_old + w_intra * k_row
            m_row = jnp.where(head_lane == h, m_t, m_row)
        m_ref[row, :] = m_row

    gnorm = gnorm_ref[...]
    for h in range(HEADS):
        cols = slice(h * DV, (h + 1) * DV)
        y = _rms(og_ref[:, cols], gnorm) * _silu(pm_ref[:, C_GG + h * DV:C_GG + (h + 1) * DV])
        br_out_ref[:, cols] = y.astype(BF16)
        y = (_rms(om_ref[:, cols], mnorm_ref[:, cols])
             * _sigmoid(pm_ref[:, C_MO + h * DV:C_MO + (h + 1) * DV]))
        br_out_ref[:, BRANCH_W + h * DV:BRANCH_W + (h + 1) * DV] = y.astype(BF16)

    x = pm_ref[:, C_RX:C_RX + RG_WIDTH]
    xc = convb_ref[...] + convw_ref[3:4, :] * x
    for jj in range(RG_CONV - 1):
        xc = xc + convw_ref[jj:jj + 1, :] * buf0_ref[:, jj * RG_WIDTH:(jj + 1) * RG_WIDTH]
    buf_ref[:, 0:2 * RG_WIDTH] = buf0_ref[:, RG_WIDTH:3 * RG_WIDTH]
    buf_ref[:, 2 * RG_WIDTH:3 * RG_WIDTH] = x
    a, b = _rg_gates(xc, wr_ref, br_ref, wi_ref, bi_ref, lam_ref)
    hnew = a * h0_ref[...] + b
    h_ref[...] = hnew
    y = hnew * _gelu_tanh(pm_ref[:, C_RG:C_RG + RG_WIDTH])
    br_out_ref[:, 2 * BRANCH_W:3 * BRANCH_W] = y.astype(BF16)


def _mix_sample(pm, ps, st, lw, *, bs):
    n = pm.shape[0]
    const = lambda shape: pl.BlockSpec(shape, lambda i: (0,) * len(shape))
    blk = lambda shape: pl.BlockSpec(shape, lambda i: (i,) + (0,) * (len(shape) - 1))
    s0, c0, n0, m0, h0, buf0 = st
    state_specs = [blk((bs, HEADS, DK, DV)), blk((bs, HEADS, DK, DV)), blk((bs, HEADS, 1, DK)),
                   blk((bs, HEADS)), blk((bs, RG_WIDTH)), blk((bs, (RG_CONV - 1) * RG_WIDTH))]
    return pl.pallas_call(
        _mix_sample_kernel,
        grid=(n // bs,),
        in_specs=[blk((bs, MIX_COLS)), blk((bs, SMALL_COLS))] + state_specs + [
            const((SMALL_COLS, HEADS * DK)), const((1, HEADS * DK)), const((1, DV)),
            const((1, SMALL_COLS)), const((1, BRANCH_W)),
            const((RG_CONV, RG_WIDTH)), const((1, RG_WIDTH)),
            const((RG_WIDTH, RG_WIDTH)), const((1, RG_WIDTH)),
            const((RG_WIDTH, RG_WIDTH)), const((1, RG_WIDTH)), const((1, RG_WIDTH))],
        out_specs=[blk((bs, N_BRANCH * BRANCH_W))] + state_specs,
        out_shape=[jax.ShapeDtypeStruct((n, N_BRANCH * BRANCH_W), BF16),
                   jax.ShapeDtypeStruct(s0.shape, F32), jax.ShapeDtypeStruct(c0.shape, F32),
                   jax.ShapeDtypeStruct(n0.shape, F32), jax.ShapeDtypeStruct(m0.shape, F32),
                   jax.ShapeDtypeStruct(h0.shape, F32), jax.ShapeDtypeStruct(buf0.shape, F32)],
        scratch_shapes=[pltpu.VMEM((bs, HEADS * DK), F32),
                        pltpu.VMEM((bs, SMALL_COLS), F32),
                        pltpu.VMEM((bs, BRANCH_W), F32),
                        pltpu.VMEM((bs, BRANCH_W), F32)],
        compiler_params=pltpu.CompilerParams(
            dimension_semantics=("parallel",), vmem_limit_bytes=VMEM_LIMIT),
        name="mix_sample",
    )(pm, ps, s0, c0, n0, m0, h0, buf0,
      lw["wa2"], lw["ba"], lw["gnorm"], lw["bif"], lw["mnorm"], lw["convw"], lw["convb"],
      lw["wr"], lw["br"], lw["wi"], lw["bi"], lw["lam"])


def _merge_kernel(x_ref, br_ref, g0_ref, g1_ref, g2_ref, wb_ref, wo_ref, o_ref):
    merged = None
    for b, g_ref in enumerate((g0_ref, g1_ref, g2_ref)):
        z = _dot(br_ref[:, b * BRANCH_W:(b + 1) * BRANCH_W], wb_ref[b])
        term = _sigmoid(g_ref[...]) * z
        merged = term if merged is None else merged + term
    o_ref[...] = x_ref[...] + _dot(merged.astype(BF16), wo_ref[...])


def _merge(x, br, pm, wb, wo, *, tm):
    n, d = x.shape
    gate_blk0 = MIX_COLS // d
    gate_spec = lambda b: pl.BlockSpec((tm, d), lambda i: (i, gate_blk0 + b))
    return pl.pallas_call(
        _merge_kernel,
        grid=(n // tm,),
        in_specs=[pl.BlockSpec((tm, d), lambda i: (i, 0)),
                  pl.BlockSpec((tm, N_BRANCH * BRANCH_W), lambda i: (i, 0)),
                  gate_spec(0), gate_spec(1), gate_spec(2),
                  pl.BlockSpec((N_BRANCH, BRANCH_W, d), lambda i: (0, 0, 0)),
                  pl.BlockSpec((d, d), lambda i: (0, 0))],
        out_specs=pl.BlockSpec((tm, d), lambda i: (i, 0)),
        out_shape=jax.ShapeDtypeStruct((n, d), F32),
        compiler_params=pltpu.CompilerParams(
            dimension_semantics=("parallel",), vmem_limit_bytes=VMEM_LIMIT),
        name="merge",
    )(x, br, pm, pm, pm, wb, wo)


def _xattn_prompt_kernel(x_ref, g_ref, wq_ref, wo_ref, k_ref, v_ref, o_ref, kb_ref, vb_ref):
    @pl.when(pl.program_id(1) == 0)
    def _():
        kb_ref[...] = k_ref[...].astype(BF16)
        vb_ref[...] = v_ref[...].astype(BF16)

    x = x_ref[...]
    q = _dot(_rms(x, g_ref[...]).astype(BF16), wq_ref[...]).astype(BF16)
    heads = []
    for h in range(XA_HEADS):
        cols = slice(h * XA_HEAD_DIM, (h + 1) * XA_HEAD_DIM)
        s = _dot_nt(q[:, cols], kb_ref[:, cols]) * (XA_HEAD_DIM ** -0.5)
        p = jnp.exp(s - jnp.max(s, axis=-1, keepdims=True))
        p = p / jnp.sum(p, axis=-1, keepdims=True)
        heads.append(_dot(p.astype(BF16), vb_ref[:, cols]))
    o = jnp.concatenate(heads, axis=1).astype(BF16)
    o_ref[...] = x + _dot(o, wo_ref[...])


def _xattn_prompt(x, g, wq, wo, mem_k, mem_v, *, nb, t, tm):
    d = x.shape[1]
    nt = t // tm
    return pl.pallas_call(
        _xattn_prompt_kernel,
        grid=(nb, nt),
        in_specs=[pl.BlockSpec((tm, d), lambda b, j: (b * nt + j, 0)),
                  pl.BlockSpec((1, d), lambda b, j: (0, 0)),
                  pl.BlockSpec((d, d), lambda b, j: (0, 0)),
                  pl.BlockSpec((d, d), lambda b, j: (0, 0)),
                  pl.BlockSpec((None, N_MEM, d), lambda b, j: (b, 0, 0)),
                  pl.BlockSpec((None, N_MEM, d), lambda b, j: (b, 0, 0))],
        out_specs=pl.BlockSpec((tm, d), lambda b, j: (b * nt + j, 0)),
        out_shape=jax.ShapeDtypeStruct(x.shape, F32),
        scratch_shapes=[pltpu.VMEM((N_MEM, d), BF16), pltpu.VMEM((N_MEM, d), BF16)],
        compiler_params=pltpu.CompilerParams(
            dimension_semantics=("parallel", "arbitrary"), vmem_limit_bytes=VMEM_LIMIT),
        name="xattn_prompt",
    )(x, g.reshape(1, d), wq, wo, mem_k, mem_v)


def _xattn_sample_kernel(q_ref, k_ref, v_ref, o_ref):
    bs = q_ref.shape[0]
    for s in range(bs):
        q_row = q_ref[s:s + 1, :]
        qk = k_ref[s] * q_row
        outs = []
        for h in range(XA_HEADS):
            cols = slice(h * XA_HEAD_DIM, (h + 1) * XA_HEAD_DIM)
            sc = jnp.sum(qk[:, cols], axis=1, keepdims=True) * (XA_HEAD_DIM ** -0.5)
            p = jnp.exp(sc - jnp.max(sc, axis=0, keepdims=True))
            p = p / jnp.sum(p, axis=0, keepdims=True)
            outs.append(jnp.sum(p * v_ref[s, :, cols], axis=0, keepdims=True))
        o_ref[s:s + 1, :] = jnp.concatenate(outs, axis=1)


def _xattn_sample(q, cache_k, cache_v, *, bs):
    n, d = q.shape
    return pl.pallas_call(
        _xattn_sample_kernel,
        grid=(n // bs,),
        in_specs=[pl.BlockSpec((bs, d), lambda i: (i, 0)),
                  pl.BlockSpec((bs, N_MEM, d), lambda i: (i, 0, 0)),
                  pl.BlockSpec((bs, N_MEM, d), lambda i: (i, 0, 0))],
        out_specs=pl.BlockSpec((bs, d), lambda i: (i, 0)),
        out_shape=jax.ShapeDtypeStruct((n, d), F32),
        compiler_params=pltpu.CompilerParams(
            dimension_semantics=("parallel",), vmem_limit_bytes=VMEM_LIMIT),
        name="xattn_sample",
    )(q, cache_k, cache_v)


def _matmul_residual_kernel(x_ref, a_ref, w_ref, o_ref):
    o_ref[...] = x_ref[...] + _dot(a_ref[...].astype(BF16), w_ref[...])


def _matmul_residual(x, a, w):
    n, d = x.shape
    return pl.pallas_call(
        _matmul_residual_kernel,
        out_shape=jax.ShapeDtypeStruct((n, d), F32),
        compiler_params=pltpu.CompilerParams(vmem_limit_bytes=VMEM_LIMIT),
        name="matmul_residual",
    )(x, a, w)


def _ffn_kernel(x_ref, g_ref, wg_ref, wu_ref, wd_ref, gf_ref, o_ref, u_ref, acc_ref, *, final_norm):
    j = pl.program_id(1)

    @pl.when(j == 0)
    def _():
        u_ref[...] = _rms(x_ref[...], g_ref[...]).astype(BF16)
        acc_ref[...] = x_ref[...]

    u = u_ref[...]
    hid = _silu(_dot(u, wg_ref[...])) * _dot(u, wu_ref[...])
    acc_ref[...] += _dot(hid.astype(BF16), wd_ref[...])

    @pl.when(j == pl.num_programs(1) - 1)
    def _():
        y = acc_ref[...]
        if final_norm:
            y = _rms(y, gf_ref[...])
        o_ref[...] = y


def _ffn(x, g, wg, wu, wd, gf, *, tm, tf, final_norm):
    n, d = x.shape
    f = wg.shape[1]
    return pl.pallas_call(
        functools.partial(_ffn_kernel, final_norm=final_norm),
        grid=(n // tm, f // tf),
        in_specs=[pl.BlockSpec((tm, d), lambda i, j: (i, 0)),
                  pl.BlockSpec((1, d), lambda i, j: (0, 0)),
                  pl.BlockSpec((d, tf), lambda i, j: (0, j)),
                  pl.BlockSpec((d, tf), lambda i, j: (0, j)),
                  pl.BlockSpec((tf, d), lambda i, j: (j, 0)),
                  pl.BlockSpec((1, d), lambda i, j: (0, 0))],
        out_specs=pl.BlockSpec((tm, d), lambda i, j: (i, 0)),
        out_shape=jax.ShapeDtypeStruct((n, d), F32),
        scratch_shapes=[pltpu.VMEM((tm, d), BF16), pltpu.VMEM((tm, d), F32)],
        compiler_params=pltpu.CompilerParams(
            dimension_semantics=("parallel", "arbitrary"), vmem_limit_bytes=VMEM_LIMIT),
        name="ffn",
    )(x, g.reshape(1, d), wg, wu, wd, gf.reshape(1, d))


def _split_w_in(w_in):
    widths = (HEADS * DK, HEADS * DK, HEADS * DV, HEADS * DV, GLA_RANK,
              HEADS * DK, HEADS * DK, HEADS * DV, HEADS * DV, HEADS, HEADS,
              RG_WIDTH, RG_WIDTH, GATE_COLS)
    parts, acc = [], 0
    for w in widths:
        parts.append(w_in[:, acc:acc + w])
        acc += w
    return parts


def _block_diag(w):
    eye = jnp.eye(RG_BLOCKS, dtype=w.dtype)
    return jnp.einsum("nde,nm->ndme", w, eye).reshape(RG_WIDTH, RG_WIDTH)


def _layer_weights(l, p):
    (g_q, g_k, g_v, g_g, g_a, m_q, m_k, m_v, m_o, m_i, m_f, r_x, r_g, gates) = _split_w_in(p["w_in"][l])
    d = D_MODEL
    w_main = jnp.concatenate([g_q, g_k, g_v, g_g, m_q, m_k, m_v, m_o, r_x, r_g, gates], axis=1)
    w_small = jnp.concatenate(
        [g_a, m_i, m_f, jnp.zeros((d, SMALL_COLS - GLA_RANK - 2 * HEADS), F32)], axis=1)
    wa2 = jnp.concatenate(
        [p["gla_w_a2"][l], jnp.zeros((SMALL_COLS - GLA_RANK, HEADS * DK), F32)], axis=0)
    bif = jnp.concatenate(
        [jnp.zeros((L_MI,), F32), p["mlstm_b_i"][l], p["mlstm_b_f"][l],
         jnp.zeros((SMALL_COLS - L_MF - HEADS,), F32)]).reshape(1, SMALL_COLS)
    return {
        "w_main": w_main.astype(BF16), "w_small": w_small.astype(BF16),
        "wa2": wa2.astype(BF16), "ba": p["gla_b_a"][l].reshape(1, -1),
        "gnorm": p["gla_norm"][l].reshape(1, DV), "bif": bif,
        "mnorm": p["mlstm_norm"][l].reshape(1, BRANCH_W),
        "convw": p["rg_conv_w"][l], "convb": p["rg_conv_b"][l].reshape(1, -1),
        "wr": _block_diag(p["rg_w_r"][l]).astype(BF16), "br": p["rg_b_r"][l].reshape(1, -1),
        "wi": _block_diag(p["rg_w_i"][l]).astype(BF16), "bi": p["rg_b_i"][l].reshape(1, -1),
        "lam": p["rg_lambda"][l].reshape(1, -1),
        "wb": p["w_branch"][l].astype(BF16), "wo": p["w_out"][l].astype(BF16),
        "xa_wq": p["xa_wq"][l].astype(BF16), "xa_wo": p["xa_wo"][l].astype(BF16),
        "xa_wkv": jnp.concatenate([p["xa_wk"][l], p["xa_wv"][l]], axis=1).astype(BF16),
        "wg": p["ffn_w_gate"][l].astype(BF16), "wu": p["ffn_w_up"][l].astype(BF16),
        "wd": p["ffn_w_down"][l].astype(BF16),
    }


def kernel(x_prompt, x_sample, mem_prompt, cache_mem_k, cache_mem_v, state_gla, state_mlstm_c, state_mlstm_n, state_mlstm_m, state_rglru_h, state_rglru_conv, norm_mix, w_in, gla_w_a2, gla_b_a, gla_norm, mlstm_b_i, mlstm_b_f, mlstm_norm, rg_conv_w, rg_conv_b, rg_w_r, rg_b_r, rg_w_i, rg_b_i, rg_lambda, w_branch, w_out, norm_xa, norm_mem, xa_wq, xa_wk, xa_wv, xa_wo, norm_ffn, ffn_w_gate, ffn_w_up, ffn_w_down, norm_final):
    p = {"w_in": w_in, "gla_w_a2": gla_w_a2, "gla_b_a": gla_b_a, "gla_norm": gla_norm,
         "mlstm_b_i": mlstm_b_i, "mlstm_b_f": mlstm_b_f, "mlstm_norm": mlstm_norm,
         "rg_conv_w": rg_conv_w, "rg_conv_b": rg_conv_b, "rg_w_r": rg_w_r, "rg_b_r": rg_b_r,
         "rg_w_i": rg_w_i, "rg_b_i": rg_b_i, "rg_lambda": rg_lambda, "w_branch": w_branch,
         "w_out": w_out, "xa_wq": xa_wq, "xa_wk": xa_wk, "xa_wv": xa_wv, "xa_wo": xa_wo,
         "ffn_w_gate": ffn_w_gate, "ffn_w_up": ffn_w_up, "ffn_w_down": ffn_w_down}
    nb, t, d = x_prompt.shape
    ns = x_sample.shape[0]
    xp = x_prompt.reshape(nb * t, d)
    xs = x_sample.reshape(ns, d)
    mem = mem_prompt.reshape(nb * N_MEM, d)

    new_p = [[] for _ in range(8)]
    new_s = [[] for _ in range(6)]
    for l in range(DEPTH):
        lw = _layer_weights(l, p)
        last = l == DEPTH - 1

        kv = _norm_matmul(mem, norm_mem[l], lw["xa_wkv"], tm=512, tn=1024)
        k_p = kv[:, :d].reshape(nb, N_MEM, d)
        v_p = kv[:, d:].reshape(nb, N_MEM, d)

        pm, ps = _in_proj(xp, norm_mix[l], lw["w_main"], lw["w_small"], tm=512, tn=1024)
        br, g_s, c_s, n_s, m_s, h_s, buf_s = _mix_prompt(pm, ps, lw, nb=nb, t=t, tt=256)
        xp = _merge(xp, br, pm, lw["wb"], lw["wo"], tm=512)
        xp = _xattn_prompt(xp, norm_xa[l], lw["xa_wq"], lw["xa_wo"], k_p, v_p, nb=nb, t=t, tm=512)
        xp = _ffn(xp, norm_ffn[l], lw["wg"], lw["wu"], lw["wd"], norm_final,
                  tm=512, tf=D_FF // 2, final_norm=last)
        for lst, val in zip(new_p, (k_p.reshape(nb, N_MEM, XA_HEADS, XA_HEAD_DIM),
                                    v_p.reshape(nb, N_MEM, XA_HEADS, XA_HEAD_DIM),
                                    g_s, c_s, n_s.reshape(nb, HEADS, DK), m_s.reshape(nb, HEADS),
                                    h_s.reshape(nb, RG_WIDTH), buf_s)):
            lst.append(val)

        pm, ps = _in_proj(xs, norm_mix[l], lw["w_main"], lw["w_small"], tm=ns, tn=1024)
        st = (state_gla[l], state_mlstm_c[l], state_mlstm_n[l].reshape(ns, HEADS, 1, DK),
              state_mlstm_m[l], state_rglru_h[l],
              state_rglru_conv[l].reshape(ns, (RG_CONV - 1) * RG_WIDTH))
        br, g_s, c_s, n_s, m_s, h_s, buf_s = _mix_sample(pm, ps, st, lw, bs=8)
        xs = _merge(xs, br, pm, lw["wb"], lw["wo"], tm=ns)
        q = _norm_matmul(xs, norm_xa[l], lw["xa_wq"], tm=ns, tn=d)
        att = _xattn_sample(q, cache_mem_k[l].reshape(ns, N_MEM, d),
                            cache_mem_v[l].reshape(ns, N_MEM, d), bs=8)
        xs = _matmul_residual(xs, att, lw["xa_wo"])
        xs = _ffn(xs, norm_ffn[l], lw["wg"], lw["wu"], lw["wd"], norm_final,
                  tm=ns, tf=D_FF // 2, final_norm=last)
        for lst, val in zip(new_s, (g_s, c_s, n_s.reshape(ns, HEADS, DK), m_s, h_s,
                                    buf_s.reshape(ns, RG_CONV - 1, RG_WIDTH))):
            lst.append(val)

    y_prompt = xp.reshape(nb, t, d)
    y_sample = xs.reshape(ns, 1, d)
    outs_p = [jnp.stack(v, axis=0) for v in new_p]
    outs_s = [jnp.stack(v, axis=0) for v in new_s]
    return (y_prompt, y_sample, *outs_p, *outs_s)
```

```python
import functools
import math

import jax
import jax.numpy as jnp
from jax import lax
from jax.experimental import pallas as pl
from jax.experimental.pallas import tpu as pltpu

F32 = jnp.float32
BF16 = jnp.bfloat16

D_MODEL = 1024
DEPTH = 2
EPS = 1e-6
N_MEM = 256
XA_HEADS = 4
XA_HEAD_DIM = D_MODEL // XA_HEADS
N_BRANCH = 3
BRANCH_W = D_MODEL // 2
HEADS = 4
DK = 64
DV = 128
GLA_RANK = 16
GLA_GATE_NORM = 16.0
GLA_CHUNK = 32
MLSTM_CHUNK = 64
RG_WIDTH = BRANCH_W
RG_BLOCKS = 8
RG_BLOCK = RG_WIDTH // RG_BLOCKS
RG_CONV = 4
RG_C = 8.0
D_FF = 2816

C_GQ, C_GK, C_GV, C_GG = 0, 256, 512, 1024
C_MQ, C_MK, C_MV, C_MO = 1536, 1792, 2048, 2560
C_RX, C_RG = 3072, 3584
MIX_COLS = 4096
GATE_COLS = N_BRANCH * D_MODEL
MAIN_COLS = MIX_COLS + GATE_COLS
SMALL_COLS = 128
L_GA = 0
L_MI = 16
L_MF = 20

VMEM_LIMIT = 48 * 1024 * 1024

HIGHEST = lax.Precision.HIGHEST


def _softplus(x):
    return jnp.maximum(x, 0.0) + jnp.log1p(jnp.exp(-jnp.abs(x)))


def _log_sigmoid(x):
    return -_softplus(-x)


def _sigmoid(x):
    return 1.0 / (1.0 + jnp.exp(-x))


def _silu(x):
    return x * _sigmoid(x)


def _gelu_tanh(x):
    c = math.sqrt(2.0 / math.pi)
    return x * (0.5 * (1.0 + jnp.tanh(c * (x + 0.044715 * (x * x * x)))))


def _neg_expm1(x):
    return -jnp.tanh(0.5 * x) * (jnp.exp(x) + 1.0)


def _rms(x, g):
    ms = jnp.mean(x * x, axis=-1, keepdims=True)
    return x * lax.rsqrt(ms + EPS) * g


def _dot(a, b):
    return jnp.dot(a, b, preferred_element_type=F32)


def _dot_nt(a, b):
    return lax.dot_general(a, b, (((1,), (1,)), ((), ())), preferred_element_type=F32)


def _dot_tn(a, b):
    return lax.dot_general(a, b, (((0,), (0,)), ((), ())), preferred_element_type=F32)


def _eye(n):
    return lax.broadcasted_iota(jnp.int32, (n, n), 0) == lax.broadcasted_iota(jnp.int32, (n, n), 1)


def _row_to_col(row, eye):
    n = eye.shape[0]
    return jnp.sum(jnp.where(eye, jnp.broadcast_to(row, (n, n)), 0.0), axis=1, keepdims=True)


def _col_to_row(col, eye):
    n = eye.shape[0]
    return jnp.sum(jnp.where(eye, jnp.broadcast_to(col, (n, n)), 0.0), axis=0, keepdims=True)


def _chunk_masks(n, chunk):
    shift = chunk.bit_length() - 1
    r = lax.broadcasted_iota(jnp.int32, (n, n), 0)
    c = lax.broadcasted_iota(jnp.int32, (n, n), 1)
    same = lax.shift_right_logical(r, shift) == lax.shift_right_logical(c, shift)
    tril = jnp.where(same & (c <= r), 1.0, 0.0).astype(F32)
    full = jnp.where(same, 1.0, 0.0).astype(F32)
    return tril, full


def _norm_matmul_kernel(x_ref, g_ref, w_ref, o_ref, u_ref):
    @pl.when(pl.program_id(1) == 0)
    def _():
        u_ref[...] = _rms(x_ref[...], g_ref[...]).astype(BF16)

    o_ref[...] = _dot(u_ref[...], w_ref[...]).astype(o_ref.dtype)


def _norm_matmul(x, g, w, *, tm, tn, out_dtype=F32):
    n, d = x.shape
    nout = w.shape[1]
    return pl.pallas_call(
        _norm_matmul_kernel,
        grid=(n // tm, nout // tn),
        in_specs=[pl.BlockSpec((tm, d), lambda i, j: (i, 0)),
                  pl.BlockSpec((1, d), lambda i, j: (0, 0)),
                  pl.BlockSpec((d, tn), lambda i, j: (0, j))],
        out_specs=pl.BlockSpec((tm, tn), lambda i, j: (i, j)),
        out_shape=jax.ShapeDtypeStruct((n, nout), out_dtype),
        scratch_shapes=[pltpu.VMEM((tm, d), BF16)],
        compiler_params=pltpu.CompilerParams(
            dimension_semantics=("parallel", "arbitrary"), vmem_limit_bytes=VMEM_LIMIT),
        name="norm_matmul",
    )(x, g.reshape(1, d), w)


def _in_proj_kernel(x_ref, g_ref, w_ref, ws_ref, o_ref, os_ref, u_ref):
    @pl.when(pl.program_id(1) == 0)
    def _():
        u = _rms(x_ref[...], g_ref[...]).astype(BF16)
        u_ref[...] = u
        os_ref[...] = _dot(u, ws_ref[...])

    o_ref[...] = _dot(u_ref[...], w_ref[...])


def _in_proj(x, g, w_main, w_small, *, tm, tn):
    n, d = x.shape
    return pl.pallas_call(
        _in_proj_kernel,
        grid=(n // tm, MAIN_COLS // tn),
        in_specs=[pl.BlockSpec((tm, d), lambda i, j: (i, 0)),
                  pl.BlockSpec((1, d), lambda i, j: (0, 0)),
                  pl.BlockSpec((d, tn), lambda i, j: (0, j)),
                  pl.BlockSpec((d, SMALL_COLS), lambda i, j: (0, 0))],
        out_specs=[pl.BlockSpec((tm, tn), lambda i, j: (i, j)),
                   pl.BlockSpec((tm, SMALL_COLS), lambda i, j: (i, 0))],
        out_shape=[jax.ShapeDtypeStruct((n, MAIN_COLS), F32),
                   jax.ShapeDtypeStruct((n, SMALL_COLS), F32)],
        scratch_shapes=[pltpu.VMEM((tm, d), BF16)],
        compiler_params=pltpu.CompilerParams(
            dimension_semantics=("parallel", "arbitrary"), vmem_limit_bytes=VMEM_LIMIT),
        name="in_proj",
    )(x, g.reshape(1, d), w_main, w_small)


def _rg_gates(xc, wr_ref, br_ref, wi_ref, bi_ref, lam_ref):
    xcb = xc.astype(BF16)
    r = _sigmoid(_dot(xcb, wr_ref[...]) + br_ref[...])
    i = _sigmoid(_dot(xcb, wi_ref[...]) + bi_ref[...])
    log_a = (-RG_C * r) * _softplus(-lam_ref[...])
    a = jnp.exp(log_a)
    b = jnp.sqrt(_neg_expm1(2.0 * log_a)) * (i * xc)
    return a, b


def _mix_prompt_kernel(pm_ref, ps_ref, wa2_ref, ba_ref, gnorm_ref, bif_ref, mnorm_ref,
                       convw_ref, convb_ref, wr_ref, br_ref, wi_ref, bi_ref, lam_ref,
                       br_out_ref, s_ref, c_ref, n_ref, m_ref, h_ref, buf_ref,
                       og_ref, xpad_ref):
    tt = pm_ref.shape[0]
    j = pl.program_id(1)

    @pl.when(j == 0)
    def _():
        s_ref[...] = jnp.zeros_like(s_ref)
        c_ref[...] = jnp.zeros_like(c_ref)
        n_ref[...] = jnp.zeros_like(n_ref)
        m_ref[...] = jnp.zeros_like(m_ref)
        h_ref[...] = jnp.zeros_like(h_ref)
        xpad_ref[pl.ds(0, 8), :] = jnp.zeros((8, RG_WIDTH), F32)

    ps = ps_ref[...]

    a_pre = _dot(ps.astype(BF16), wa2_ref[...]) + ba_ref[...]
    log_a = _log_sigmoid(a_pre) * (1.0 / GLA_GATE_NORM)
    tril32, full32 = _chunk_masks(tt, GLA_CHUNK)
    b_cum = jnp.dot(tril32, log_a, precision=HIGHEST, preferred_element_type=F32)
    b_end = jnp.dot(full32, log_a, precision=HIGHEST, preferred_element_type=F32)
    q = pm_ref[:, C_GQ:C_GQ + HEADS * DK] * (DK ** -0.5)
    k = pm_ref[:, C_GK:C_GK + HEADS * DK]
    qi_all = (q * jnp.exp(b_cum)).astype(BF16)
    ki_all = (k * jnp.exp(-b_cum)).astype(BF16)
    ke_all = (k * jnp.exp(b_end - b_cum)).astype(BF16)
    de_t = jnp.exp(b_end).T
    in_chunk = tril32 > 0.0
    n_chunks = tt // GLA_CHUNK
    for h in range(HEADS):
        kcols = slice(h * DK, (h + 1) * DK)
        qi = qi_all[:, kcols]
        ke = ke_all[:, kcols]
        v = pm_ref[:, C_GV + h * DV:C_GV + (h + 1) * DV].astype(BF16)
        att = jnp.where(in_chunk, _dot_nt(qi, ki_all[:, kcols]), 0.0)
        o_intra = _dot(att.astype(BF16), v)
        s_cur = s_ref[h]
        o_inter = []
        for c in range(n_chunks):
            rows = slice(c * GLA_CHUNK, (c + 1) * GLA_CHUNK)
            o_inter.append(_dot(qi[rows], s_cur.astype(BF16)))
            de_col = de_t[kcols, c * GLA_CHUNK:c * GLA_CHUNK + 1]
            s_cur = de_col * s_cur + _dot_tn(ke[rows], v[rows])
        s_ref[h] = s_cur
        og_ref[:, h * DV:(h + 1) * DV] = o_intra + jnp.concatenate(o_inter, axis=0)

    gnorm = gnorm_ref[...]
    for h in range(HEADS):
        cols = slice(h * DV, (h + 1) * DV)
        y = _rms(og_ref[:, cols], gnorm) * _silu(pm_ref[:, C_GG + h * DV:C_GG + (h + 1) * DV])
        br_out_ref[:, cols] = y.astype(BF16)

    pre = ps + bif_ref[...]
    rr = lax.broadcasted_iota(jnp.int32, (tt, tt), 0)
    cc = lax.broadcasted_iota(jnp.int32, (tt, tt), 1)
    causal = cc <= rr
    fcum = jnp.dot(jnp.where(causal, 1.0, 0.0).astype(F32), _log_sigmoid(pre),
                   precision=HIGHEST, preferred_element_type=F32)
    pre_t = pre.T
    fcum_t = fcum.T
    q_all = pm_ref[:, C_MQ:C_MQ + HEADS * DK] * (DK ** -0.5)
    k_all = pm_ref[:, C_MK:C_MK + HEADS * DK]
    for h in range(HEADS):
        kcols = slice(h * DK, (h + 1) * DK)
        fc_col = fcum[:, L_MF + h:L_MF + h + 1]
        ic_col = pre[:, L_MI + h:L_MI + h + 1]
        fc_row = fcum_t[L_MF + h:L_MF + h + 1, :]
        ic_row = pre_t[L_MI + h:L_MI + h + 1, :]
        m_prev = m_ref[:, h:h + 1]
        d = jnp.where(causal, fc_col + (ic_row - fc_row), -jnp.inf)
        inter = fc_col + m_prev
        m_t = jnp.maximum(inter, jnp.max(d, axis=1, keepdims=True))
        w_intra = jnp.exp(d - m_t)
        w_inter = jnp.exp(inter - m_t)
        qf = q_all[:, kcols]
        kf = k_all[:, kcols]
        qc = qf.astype(BF16)
        vc = pm_ref[:, C_MV + h * DV:C_MV + (h + 1) * DV].astype(BF16)
        s = _dot_nt(qc, kf.astype(BF16)) * w_intra
        c_old = c_ref[h]
        n_old = n_ref[h]
        num = w_inter * _dot(qc, c_old.astype(BF16)) + _dot(s.astype(BF16), vc)
        den = (w_inter * jnp.sum(qf * n_old, axis=1, keepdims=True)
               + jnp.sum(s, axis=1, keepdims=True))
        hh = num / jnp.maximum(jnp.abs(den), jnp.exp(-m_t))
        og_ref[:, h * DV:(h + 1) * DV] = hh
        f_end = fc_col[tt - 1:tt, :]
        m_end = m_t[tt - 1:tt, :]
        w_k = jnp.exp(f_end - fc_col + ic_col - m_end)
        w_c = jnp.exp(f_end + m_prev - m_end)
        kw = w_k * kf
        c_ref[h] = w_c * c_old + _dot_tn(kw.astype(BF16), vc)
        n_ref[h] = w_c * n_old + jnp.sum(kw, axis=0, keepdims=True)
        m_ref[:, h:h + 1] = m_end

    for h in range(HEADS):
        cols = slice(h * DV, (h + 1) * DV)
        y = (_rms(og_ref[:, cols], mnorm_ref[:, cols])
             * _sigmoid(pm_ref[:, C_MO + h * DV:C_MO + (h + 1) * DV]))
        br_out_ref[:, BRANCH_W + h * DV:BRANCH_W + (h + 1) * DV] = y.astype(BF16)

    x = pm_ref[:, C_RX:C_RX + RG_WIDTH]
    xpad_ref[pl.ds(8, tt), :] = x
    xc = (convb_ref[...] + convw_ref[3:4, :] * x
          + convw_ref[2:3, :] * xpad_ref[pl.ds(7, tt), :]
          + convw_ref[1:2, :] * xpad_ref[pl.ds(6, tt), :]
          + convw_ref[0:1, :] * xpad_ref[pl.ds(5, tt), :])
    buf_ref[...] = xpad_ref[pl.ds(tt + 8 - (RG_CONV - 1), RG_CONV - 1), :]
    xpad_ref[pl.ds(0, 8), :] = xpad_ref[pl.ds(tt, 8), :]
    a, b = _rg_gates(xc, wr_ref, br_ref, wi_ref, bi_ref, lam_ref)
    ridx = lax.broadcasted_iota(jnp.int32, (tt, RG_WIDTH), 0)
    sh = 1
    while sh < tt:
        keep = ridx >= sh
        a_sh = jnp.where(keep, pltpu.roll(a, sh, 0), 1.0)
        b_sh = jnp.where(keep, pltpu.roll(b, sh, 0), 0.0)
        b = a * b_sh + b
        a = a * a_sh
        sh *= 2
    hseq = a * h_ref[...] + b
    h_ref[...] = hseq[tt - 1:tt, :]
    y = hseq * _gelu_tanh(pm_ref[:, C_RG:C_RG + RG_WIDTH])
    br_out_ref[:, 2 * BRANCH_W:3 * BRANCH_W] = y.astype(BF16)


def _mix_prompt(pm, ps, lw, *, nb, t, tt):
    const = lambda shape: pl.BlockSpec(shape, lambda b, j: (0,) * len(shape))
    nt = t // tt
    return pl.pallas_call(
        _mix_prompt_kernel,
        grid=(nb, nt),
        in_specs=[pl.BlockSpec((tt, MIX_COLS), lambda b, j: (b * nt + j, 0)),
                  pl.BlockSpec((tt, SMALL_COLS), lambda b, j: (b * nt + j, 0)),
                  const((SMALL_COLS, HEADS * DK)), const((1, HEADS * DK)), const((1, DV)),
                  const((1, SMALL_COLS)), const((1, BRANCH_W)),
                  const((RG_CONV, RG_WIDTH)), const((1, RG_WIDTH)),
                  const((RG_WIDTH, RG_WIDTH)), const((1, RG_WIDTH)),
                  const((RG_WIDTH, RG_WIDTH)), const((1, RG_WIDTH)), const((1, RG_WIDTH))],
        out_specs=[pl.BlockSpec((tt, N_BRANCH * BRANCH_W), lambda b, j: (b * nt + j, 0)),
                   pl.BlockSpec((None, HEADS, DK, DV), lambda b, j: (b, 0, 0, 0)),
                   pl.BlockSpec((None, HEADS, DK, DV), lambda b, j: (b, 0, 0, 0)),
                   pl.BlockSpec((None, HEADS, 1, DK), lambda b, j: (b, 0, 0, 0)),
                   pl.BlockSpec((None, 1, HEADS), lambda b, j: (b, 0, 0)),
                   pl.BlockSpec((None, 1, RG_WIDTH), lambda b, j: (b, 0, 0)),
                   pl.BlockSpec((None, RG_CONV - 1, RG_WIDTH), lambda b, j: (b, 0, 0))],
        out_shape=[jax.ShapeDtypeStruct((nb * t, N_BRANCH * BRANCH_W), BF16),
                   jax.ShapeDtypeStruct((nb, HEADS, DK, DV), F32),
                   jax.ShapeDtypeStruct((nb, HEADS, DK, DV), F32),
                   jax.ShapeDtypeStruct((nb, HEADS, 1, DK), F32),
                   jax.ShapeDtypeStruct((nb, 1, HEADS), F32),
                   jax.ShapeDtypeStruct((nb, 1, RG_WIDTH), F32),
                   jax.ShapeDtypeStruct((nb, RG_CONV - 1, RG_WIDTH), F32)],
        scratch_shapes=[pltpu.VMEM((tt, BRANCH_W), F32),
                        pltpu.VMEM((tt + 8, RG_WIDTH), F32)],
        compiler_params=pltpu.CompilerParams(
            dimension_semantics=("parallel", "arbitrary"), vmem_limit_bytes=VMEM_LIMIT),
        name="mix_prompt",
    )(pm, ps, lw["wa2"], lw["ba"], lw["gnorm"], lw["bif"], lw["mnorm"], lw["convw"], lw["convb"],
      lw["wr"], lw["br"], lw["wi"], lw["bi"], lw["lam"])


def _mix_sample_kernel(pm_ref, ps_ref, s0_ref, c0_ref, n0_ref, m0_ref, h0_ref, buf0_ref,
                       wa2_ref, ba_ref, gnorm_ref, bif_ref, mnorm_ref,
                       convw_ref, convb_ref, wr_ref, br_ref, wi_ref, bi_ref, lam_ref,
                       br_out_ref, s_ref, c_ref, n_ref, m_ref, h_ref, buf_ref,
                       og_ref, om_ref):
    bs = pm_ref.shape[0]
    ps = ps_ref[...]
    a_pre = _dot(ps.astype(BF16), wa2_ref[...]) + ba_ref[...]
    a = jnp.exp(_log_sigmoid(a_pre) * (1.0 / GLA_GATE_NORM))
    a_hi = a.astype(BF16)
    a_r1 = a - a_hi.astype(F32)
    a_mid = a_r1.astype(BF16)
    a_lo = (a_r1 - a_mid.astype(F32)).astype(BF16)
    mq = pm_ref[:, C_MQ:C_MQ + HEADS * DK] * (DK ** -0.5)
    mk = pm_ref[:, C_MK:C_MK + HEADS * DK]
    kd = HEADS * DK
    cols_t = jnp.concatenate(
        [(pm_ref[:, C_GQ:C_GQ + kd] * (DK ** -0.5)).T, pm_ref[:, C_GK:C_GK + kd].T,
         mq.T, mk.T, a_hi.astype(F32).T, a_mid.astype(F32).T, a_lo.astype(F32).T],
        axis=0).astype(BF16)
    o_gq, o_gk, o_mq, o_mk, o_a0, o_a1, o_a2 = (i * kd for i in range(7))
    sample_id = lax.broadcasted_iota(jnp.int32, (bs, DV), 0)
    pre = ps + bif_ref[...]
    lf = _log_sigmoid(pre)
    m0_all = m0_ref[...]
    head_lane = lax.broadcasted_iota(jnp.int32, (1, HEADS), 1)

    for s in range(bs):
        row = slice(s, s + 1)
        bc = _dot(cols_t, jnp.where(sample_id == s, 1.0, 0.0).astype(BF16))
        m_row = m0_all[row, :]
        for h in range(HEADS):
            kcols = slice(h * DK, (h + 1) * DK)
            col = lambda off: bc[off + h * DK:off + (h + 1) * DK, :]
            v_row = pm_ref[row, C_GV + h * DV:C_GV + (h + 1) * DV]
            a_col = col(o_a0) + col(o_a1) + col(o_a2)
            s_new = a_col * s0_ref[s, h] + col(o_gk) * v_row
            s_ref[s, h] = s_new
            og_ref[row, h * DV:(h + 1) * DV] = jnp.sum(col(o_gq) * s_new,
                                                       axis=0, keepdims=True)
            ic = pre[row, L_MI + h:L_MI + h + 1]
            fc = lf[row, L_MF + h:L_MF + h + 1]
            m0 = m0_all[row, h:h + 1]
            inter = fc + m0
            m_t = jnp.maximum(inter, ic)
            w_intra = jnp.exp(ic - m_t)
            w_inter = jnp.exp(inter - m_t)
            q_row = mq[row, kcols]
            k_row = mk[row, kcols]
            v_row = pm_ref[row, C_MV + h * DV:C_MV + (h + 1) * DV]
            sc = jnp.sum(q_row * k_row, axis=1, keepdims=True) * w_intra
            c_old = c0_ref[s, h]
            n_old = n0_ref[s, h]
            qc = jnp.sum(col(o_mq) * c_old, axis=0, keepdims=True)
            num = w_inter * qc + sc * v_row
            den = w_inter * jnp.sum(q_row * n_old, axis=1, keepdims=True) + sc
            om_ref[row, h * DV:(h + 1) * DV] = num / jnp.maximum(jnp.abs(den), jnp.exp(-m_t))
            c_ref[s, h] = w_inter * c_old + (w_intra * col(o_mk)) * v_row
            n_ref[s, h] = w_inter * n_old + w_intra * k_row
            m_row = jnp.where(head_lane == h, m_t, m_row)
        m_ref[row, :] = m_row

    gnorm = gnorm_ref[...]
    for h in range(HEADS):
        cols = slice(h * DV, (h + 1) * DV)
        y = _rms(og_ref[:, cols], gnorm) * _silu(pm_ref[:, C_GG + h * DV:C_GG + (h + 1) * DV])
        br_out_ref[:, cols] = y.astype(BF16)
        y = (_rms(om_ref[:, cols], mnorm_ref[:, cols])
             * _sigmoid(pm_ref[:, C_MO + h * DV:C_MO + (h + 1) * DV]))
        br_out_ref[:, BRANCH_W + h * DV:BRANCH_W + (h + 1) * DV] = y.astype(BF16)

    x = pm_ref[:, C_RX:C_RX + RG_WIDTH]
    xc = convb_ref[...] + convw_ref[3:4, :] * x
    for jj in range(RG_CONV - 1):
        xc = xc + convw_ref[jj:jj + 1, :] * buf0_ref[:, jj * RG_WIDTH:(jj + 1) * RG_WIDTH]
    buf_ref[:, 0:2 * RG_WIDTH] = buf0_ref[:, RG_WIDTH:3 * RG_WIDTH]
    buf_ref[:, 2 * RG_WIDTH:3 * RG_WIDTH] = x
    a, b = _rg_gates(xc, wr_ref, br_ref, wi_ref, bi_ref, lam_ref)
    hnew = a * h0_ref[...] + b
    h_ref[...] = hnew
    y = hnew * _gelu_tanh(pm_ref[:, C_RG:C_RG + RG_WIDTH])
    br_out_ref[:, 2 * BRANCH_W:3 * BRANCH_W] = y.astype(BF16)


def _mix_sample(pm, ps, st, layer, lw, *, bs):
    n = pm.shape[0]
    const = lambda shape: pl.BlockSpec(shape, lambda i: (0,) * len(shape))
    blk = lambda shape: pl.BlockSpec(shape, lambda i: (i,) + (0,) * (len(shape) - 1))
    lblk = lambda shape: pl.BlockSpec((None,) + shape,
                                      lambda i: (layer, i) + (0,) * (len(shape) - 1))
    s0, c0, n0, m0, h0, buf0 = st
    shapes = [(bs, HEADS, DK, DV), (bs, HEADS, DK, DV), (bs, HEADS, 1, DK),
              (bs, HEADS), (bs, RG_WIDTH), (bs, (RG_CONV - 1) * RG_WIDTH)]
    return pl.pallas_call(
        _mix_sample_kernel,
        grid=(n // bs,),
        in_specs=[blk((bs, MIX_COLS)), blk((bs, SMALL_COLS))] + [lblk(sh) for sh in shapes] + [
            const((SMALL_COLS, HEADS * DK)), const((1, HEADS * DK)), const((1, DV)),
            const((1, SMALL_COLS)), const((1, BRANCH_W)),
            const((RG_CONV, RG_WIDTH)), const((1, RG_WIDTH)),
            const((RG_WIDTH, RG_WIDTH)), const((1, RG_WIDTH)),
            const((RG_WIDTH, RG_WIDTH)), const((1, RG_WIDTH)), const((1, RG_WIDTH))],
        out_specs=[blk((bs, N_BRANCH * BRANCH_W))] + [blk(sh) for sh in shapes],
        out_shape=[jax.ShapeDtypeStruct((n, N_BRANCH * BRANCH_W), BF16)]
        + [jax.ShapeDtypeStruct(a.shape[1:], F32) for a in st],
        scratch_shapes=[pltpu.VMEM((bs, BRANCH_W), F32),
                        pltpu.VMEM((bs, BRANCH_W), F32)],
        compiler_params=pltpu.CompilerParams(
            dimension_semantics=("parallel",), vmem_limit_bytes=VMEM_LIMIT),
        name="mix_sample",
    )(pm, ps, s0, c0, n0, m0, h0, buf0,
      lw["wa2"], lw["ba"], lw["gnorm"], lw["bif"], lw["mnorm"], lw["convw"], lw["convb"],
      lw["wr"], lw["br"], lw["wi"], lw["bi"], lw["lam"])


def _merge_kernel(x_ref, br_ref, g0_ref, g1_ref, g2_ref, wb_ref, wo_ref, o_ref):
    merged = None
    for b, g_ref in enumerate((g0_ref, g1_ref, g2_ref)):
        z = _dot(br_ref[:, b * BRANCH_W:(b + 1) * BRANCH_W], wb_ref[b])
        term = _sigmoid(g_ref[...]) * z
        merged = term if merged is None else merged + term
    o_ref[...] = x_ref[...] + _dot(merged.astype(BF16), wo_ref[...])


def _merge(x, br, pm, wb, wo, *, tm):
    n, d = x.shape
    gate_blk0 = MIX_COLS // d
    gate_spec = lambda b: pl.BlockSpec((tm, d), lambda i: (i, gate_blk0 + b))
    return pl.pallas_call(
        _merge_kernel,
        grid=(n // tm,),
        in_specs=[pl.BlockSpec((tm, d), lambda i: (i, 0)),
                  pl.BlockSpec((tm, N_BRANCH * BRANCH_W), lambda i: (i, 0)),
                  gate_spec(0), gate_spec(1), gate_spec(2),
                  pl.BlockSpec((N_BRANCH, BRANCH_W, d), lambda i: (0, 0, 0)),
                  pl.BlockSpec((d, d), lambda i: (0, 0))],
        out_specs=pl.BlockSpec((tm, d), lambda i: (i, 0)),
        out_shape=jax.ShapeDtypeStruct((n, d), F32),
        compiler_params=pltpu.CompilerParams(
            dimension_semantics=("parallel",), vmem_limit_bytes=VMEM_LIMIT),
        name="merge",
    )(x, br, pm, pm, pm, wb, wo)


def _xattn_prompt_kernel(x_ref, g_ref, wq_ref, wo_ref, k_ref, v_ref, o_ref, kb_ref, vb_ref):
    @pl.when(pl.program_id(1) == 0)
    def _():
        kb_ref[...] = k_ref[...].astype(BF16)
        vb_ref[...] = v_ref[...].astype(BF16)

    x = x_ref[...]
    q = _dot(_rms(x, g_ref[...]).astype(BF16), wq_ref[...]).astype(BF16)
    heads = []
    for h in range(XA_HEADS):
        cols = slice(h * XA_HEAD_DIM, (h + 1) * XA_HEAD_DIM)
        s = _dot_nt(q[:, cols], kb_ref[:, cols]) * (XA_HEAD_DIM ** -0.5)
        p = jnp.exp(s - jnp.max(s, axis=-1, keepdims=True))
        p = p / jnp.sum(p, axis=-1, keepdims=True)
        heads.append(_dot(p.astype(BF16), vb_ref[:, cols]))
    o = jnp.concatenate(heads, axis=1).astype(BF16)
    o_ref[...] = x + _dot(o, wo_ref[...])


def _xattn_prompt(x, g, wq, wo, mem_k, mem_v, *, nb, t, tm):
    d = x.shape[1]
    nt = t // tm
    return pl.pallas_call(
        _xattn_prompt_kernel,
        grid=(nb, nt),
        in_specs=[pl.BlockSpec((tm, d), lambda b, j: (b * nt + j, 0)),
                  pl.BlockSpec((1, d), lambda b, j: (0, 0)),
                  pl.BlockSpec((d, d), lambda b, j: (0, 0)),
                  pl.BlockSpec((d, d), lambda b, j: (0, 0)),
                  pl.BlockSpec((None, N_MEM, d), lambda b, j: (b, 0, 0)),
                  pl.BlockSpec((None, N_MEM, d), lambda b, j: (b, 0, 0))],
        out_specs=pl.BlockSpec((tm, d), lambda b, j: (b * nt + j, 0)),
        out_shape=jax.ShapeDtypeStruct(x.shape, F32),
        scratch_shapes=[pltpu.VMEM((N_MEM, d), BF16), pltpu.VMEM((N_MEM, d), BF16)],
        compiler_params=pltpu.CompilerParams(
            dimension_semantics=("parallel", "arbitrary"), vmem_limit_bytes=VMEM_LIMIT),
        name="xattn_prompt",
    )(x, g.reshape(1, d), wq, wo, mem_k, mem_v)


def _xattn_sample_kernel(q_ref, k_ref, v_ref, o_ref):
    bs = q_ref.shape[0]
    for s in range(bs):
        q = q_ref[s]
        sc = jnp.sum(k_ref[s] * q[None], axis=-1, keepdims=True) * (XA_HEAD_DIM ** -0.5)
        p = jnp.exp(sc - jnp.max(sc, axis=0, keepdims=True))
        p = p / jnp.sum(p, axis=0, keepdims=True)
        o_ref[s] = jnp.sum(p * v_ref[s], axis=0)


def _xattn_sample(q, cache_k, cache_v, layer, *, bs):
    n = q.shape[0]
    blk = (None, bs, N_MEM, XA_HEADS, XA_HEAD_DIM)
    return pl.pallas_call(
        _xattn_sample_kernel,
        grid=(n // bs,),
        in_specs=[pl.BlockSpec((bs, XA_HEADS, XA_HEAD_DIM), lambda i: (i, 0, 0)),
                  pl.BlockSpec(blk, lambda i: (layer, i, 0, 0, 0)),
                  pl.BlockSpec(blk, lambda i: (layer, i, 0, 0, 0))],
        out_specs=pl.BlockSpec((bs, XA_HEADS, XA_HEAD_DIM), lambda i: (i, 0, 0)),
        out_shape=jax.ShapeDtypeStruct((n, XA_HEADS, XA_HEAD_DIM), F32),
        compiler_params=pltpu.CompilerParams(
            dimension_semantics=("parallel",), vmem_limit_bytes=VMEM_LIMIT),
        name="xattn_sample",
    )(q, cache_k, cache_v)


def _matmul_residual_kernel(x_ref, a_ref, w_ref, o_ref):
    o_ref[...] = x_ref[...] + _dot(a_ref[...].astype(BF16), w_ref[...])


def _matmul_residual(x, a, w):
    n, d = x.shape
    return pl.pallas_call(
        _matmul_residual_kernel,
        out_shape=jax.ShapeDtypeStruct((n, d), F32),
        compiler_params=pltpu.CompilerParams(vmem_limit_bytes=VMEM_LIMIT),
        name="matmul_residual",
    )(x, a, w)


def _ffn_kernel(x_ref, g_ref, wg_ref, wu_ref, wd_ref, gf_ref, o_ref, u_ref, acc_ref, *, final_norm):
    j = pl.program_id(1)

    @pl.when(j == 0)
    def _():
        u_ref[...] = _rms(x_ref[...], g_ref[...]).astype(BF16)
        acc_ref[...] = x_ref[...]

    u = u_ref[...]
    hid = _silu(_dot(u, wg_ref[...])) * _dot(u, wu_ref[...])
    acc_ref[...] += _dot(hid.astype(BF16), wd_ref[...])

    @pl.when(j == pl.num_programs(1) - 1)
    def _():
        y = acc_ref[...]
        if final_norm:
            y = _rms(y, gf_ref[...])
        o_ref[...] = y


def _ffn(x, g, wg, wu, wd, gf, *, tm, tf, final_norm):
    n, d = x.shape
    f = wg.shape[1]
    return pl.pallas_call(
        functools.partial(_ffn_kernel, final_norm=final_norm),
        grid=(n // tm, f // tf),
        in_specs=[pl.BlockSpec((tm, d), lambda i, j: (i, 0)),
                  pl.BlockSpec((1, d), lambda i, j: (0, 0)),
                  pl.BlockSpec((d, tf), lambda i, j: (0, j)),
                  pl.BlockSpec((d, tf), lambda i, j: (0, j)),
                  pl.BlockSpec((tf, d), lambda i, j: (j, 0)),
                  pl.BlockSpec((1, d), lambda i, j: (0, 0))],
        out_specs=pl.BlockSpec((tm, d), lambda i, j: (i, 0)),
        out_shape=jax.ShapeDtypeStruct((n, d), F32),
        scratch_shapes=[pltpu.VMEM((tm, d), BF16), pltpu.VMEM((tm, d), F32)],
        compiler_params=pltpu.CompilerParams(
            dimension_semantics=("parallel", "arbitrary"), vmem_limit_bytes=VMEM_LIMIT),
        name="ffn",
    )(x, g.reshape(1, d), wg, wu, wd, gf.reshape(1, d))


def _split_w_in(w_in):
    widths = (HEADS * DK, HEADS * DK, HEADS * DV, HEADS * DV, GLA_RANK,
              HEADS * DK, HEADS * DK, HEADS * DV, HEADS * DV, HEADS, HEADS,
              RG_WIDTH, RG_WIDTH, GATE_COLS)
    parts, acc = [], 0
    for w in widths:
        parts.append(w_in[:, acc:acc + w])
        acc += w
    return parts


def _block_diag(w):
    eye = jnp.eye(RG_BLOCKS, dtype=w.dtype)
    return jnp.einsum("nde,nm->ndme", w, eye).reshape(RG_WIDTH, RG_WIDTH)


def _layer_weights(l, p):
    (g_q, g_k, g_v, g_g, g_a, m_q, m_k, m_v, m_o, m_i, m_f, r_x, r_g, gates) = _split_w_in(p["w_in"][l])
    d = D_MODEL
    w_main = jnp.concatenate([g_q, g_k, g_v, g_g, m_q, m_k, m_v, m_o, r_x, r_g, gates], axis=1)
    w_small = jnp.concatenate(
        [g_a, m_i, m_f, jnp.zeros((d, SMALL_COLS - GLA_RANK - 2 * HEADS), F32)], axis=1)
    wa2 = jnp.concatenate(
        [p["gla_w_a2"][l], jnp.zeros((SMALL_COLS - GLA_RANK, HEADS * DK), F32)], axis=0)
    bif = jnp.concatenate(
        [jnp.zeros((L_MI,), F32), p["mlstm_b_i"][l], p["mlstm_b_f"][l],
         jnp.zeros((SMALL_COLS - L_MF - HEADS,), F32)]).reshape(1, SMALL_COLS)
    return {
        "w_main": w_main.astype(BF16), "w_small": w_small.astype(BF16),
        "wa2": wa2.astype(BF16), "ba": p["gla_b_a"][l].reshape(1, -1),
        "gnorm": p["gla_norm"][l].reshape(1, DV), "bif": bif,
        "mnorm": p["mlstm_norm"][l].reshape(1, BRANCH_W),
        "convw": p["rg_conv_w"][l], "convb": p["rg_conv_b"][l].reshape(1, -1),
        "wr": _block_diag(p["rg_w_r"][l]).astype(BF16), "br": p["rg_b_r"][l].reshape(1, -1),
        "wi": _block_diag(p["rg_w_i"][l]).astype(BF16), "bi": p["rg_b_i"][l].reshape(1, -1),
        "lam": p["rg_lambda"][l].reshape(1, -1),
        "wb": p["w_branch"][l].astype(BF16), "wo": p["w_out"][l].astype(BF16),
        "xa_wq": p["xa_wq"][l].astype(BF16), "xa_wo": p["xa_wo"][l].astype(BF16),
        "xa_wkv": jnp.concatenate([p["xa_wk"][l], p["xa_wv"][l]], axis=1).astype(BF16),
        "wg": p["ffn_w_gate"][l].astype(BF16), "wu": p["ffn_w_up"][l].astype(BF16),
        "wd": p["ffn_w_down"][l].astype(BF16),
    }


def kernel(x_prompt, x_sample, mem_prompt, cache_mem_k, cache_mem_v, state_gla, state_mlstm_c, state_mlstm_n, state_mlstm_m, state_rglru_h, state_rglru_conv, norm_mix, w_in, gla_w_a2, gla_b_a, gla_norm, mlstm_b_i, mlstm_b_f, mlstm_norm, rg_conv_w, rg_conv_b, rg_w_r, rg_b_r, rg_w_i, rg_b_i, rg_lambda, w_branch, w_out, norm_xa, norm_mem, xa_wq, xa_wk, xa_wv, xa_wo, norm_ffn, ffn_w_gate, ffn_w_up, ffn_w_down, norm_final):
    p = {"w_in": w_in, "gla_w_a2": gla_w_a2, "gla_b_a": gla_b_a, "gla_norm": gla_norm,
         "mlstm_b_i": mlstm_b_i, "mlstm_b_f": mlstm_b_f, "mlstm_norm": mlstm_norm,
         "rg_conv_w": rg_conv_w, "rg_conv_b": rg_conv_b, "rg_w_r": rg_w_r, "rg_b_r": rg_b_r,
         "rg_w_i": rg_w_i, "rg_b_i": rg_b_i, "rg_lambda": rg_lambda, "w_branch": w_branch,
         "w_out": w_out, "xa_wq": xa_wq, "xa_wk": xa_wk, "xa_wv": xa_wv, "xa_wo": xa_wo,
         "ffn_w_gate": ffn_w_gate, "ffn_w_up": ffn_w_up, "ffn_w_down": ffn_w_down}
    nb, t, d = x_prompt.shape
    ns = x_sample.shape[0]
    xp = x_prompt.reshape(nb * t, d)
    xs = x_sample.reshape(ns, d)
    mem = mem_prompt.reshape(nb * N_MEM, d)

    st_sample = (state_gla, state_mlstm_c, state_mlstm_n.reshape(DEPTH, ns, HEADS, 1, DK),
                 state_mlstm_m, state_rglru_h,
                 state_rglru_conv.reshape(DEPTH, ns, (RG_CONV - 1) * RG_WIDTH))

    new_p = [[] for _ in range(8)]
    new_s = [[] for _ in range(6)]
    for l in range(DEPTH):
        lw = _layer_weights(l, p)
        last = l == DEPTH - 1

        kv = _norm_matmul(mem, norm_mem[l], lw["xa_wkv"], tm=512, tn=1024)
        k_p = kv[:, :d].reshape(nb, N_MEM, d)
        v_p = kv[:, d:].reshape(nb, N_MEM, d)

        pm, ps = _in_proj(xp, norm_mix[l], lw["w_main"], lw["w_small"], tm=512, tn=1024)
        br, g_s, c_s, n_s, m_s, h_s, buf_s = _mix_prompt(pm, ps, lw, nb=nb, t=t, tt=256)
        xp = _merge(xp, br, pm, lw["wb"], lw["wo"], tm=512)
        xp = _xattn_prompt(xp, norm_xa[l], lw["xa_wq"], lw["xa_wo"], k_p, v_p, nb=nb, t=t, tm=512)
        xp = _ffn(xp, norm_ffn[l], lw["wg"], lw["wu"], lw["wd"], norm_final,
                  tm=512, tf=D_FF // 2, final_norm=last)
        for lst, val in zip(new_p, (k_p.reshape(nb, N_MEM, XA_HEADS, XA_HEAD_DIM),
                                    v_p.reshape(nb, N_MEM, XA_HEADS, XA_HEAD_DIM),
                                    g_s, c_s, n_s.reshape(nb, HEADS, DK), m_s.reshape(nb, HEADS),
                                    h_s.reshape(nb, RG_WIDTH), buf_s)):
            lst.append(val)

        pm, ps = _in_proj(xs, norm_mix[l], lw["w_main"], lw["w_small"], tm=ns, tn=1024)
        br, g_s, c_s, n_s, m_s, h_s, buf_s = _mix_sample(pm, ps, st_sample, l, lw, bs=8)
        xs = _merge(xs, br, pm, lw["wb"], lw["wo"], tm=ns)
        q = _norm_matmul(xs, norm_xa[l], lw["xa_wq"], tm=ns, tn=d)
        att = _xattn_sample(q.reshape(ns, XA_HEADS, XA_HEAD_DIM), cache_mem_k, cache_mem_v, l, bs=4)
        xs = _matmul_residual(xs, att.reshape(ns, d), lw["xa_wo"])
        xs = _ffn(xs, norm_ffn[l], lw["wg"], lw["wu"], lw["wd"], norm_final,
                  tm=ns, tf=D_FF // 2, final_norm=last)
        for lst, val in zip(new_s, (g_s, c_s, n_s.reshape(ns, HEADS, DK), m_s, h_s,
                                    buf_s.reshape(ns, RG_CONV - 1, RG_WIDTH))):
            lst.append(val)

    y_prompt = xp.reshape(nb, t, d)
    y_sample = xs.reshape(ns, 1, d)
    outs_p = [jnp.stack(v, axis=0) for v in new_p]
    outs_s = [jnp.stack(v, axis=0) for v in new_s]
    return (y_prompt, y_sample, *outs_p, *outs_s)
```

```python
import functools
import math

import jax
import jax.numpy as jnp
from jax import lax
from jax.experimental import pallas as pl
from jax.experimental.pallas import tpu as pltpu

F32 = jnp.float32
BF16 = jnp.bfloat16

D_MODEL = 1024
DEPTH = 2
EPS = 1e-6
N_MEM = 256
XA_HEADS = 4
XA_HEAD_DIM = D_MODEL // XA_HEADS
N_BRANCH = 3
BRANCH_W = D_MODEL // 2
HEADS = 4
DK = 64
DV = 128
GLA_RANK = 16
GLA_GATE_NORM = 16.0
GLA_CHUNK = 32
MLSTM_CHUNK = 64
RG_WIDTH = BRANCH_W
RG_BLOCKS = 8
RG_BLOCK = RG_WIDTH // RG_BLOCKS
RG_CONV = 4
RG_C = 8.0
D_FF = 2816

C_GQ, C_GK, C_GV, C_GG = 0, 256, 512, 1024
C_MQ, C_MK, C_MV, C_MO = 1536, 1792, 2048, 2560
C_RX, C_RG = 3072, 3584
MIX_COLS = 4096
GATE_COLS = N_BRANCH * D_MODEL
MAIN_COLS = MIX_COLS + GATE_COLS
SMALL_COLS = 128
L_GA = 0
L_MI = 16
L_MF = 20

PROJ_TILE = 1024
VMEM_LIMIT = 48 * 1024 * 1024

HIGHEST = lax.Precision.HIGHEST


def _softplus(x):
    return jnp.maximum(x, 0.0) + jnp.log1p(jnp.exp(-jnp.abs(x)))


def _log_sigmoid(x):
    return -_softplus(-x)


def _sigmoid(x):
    return 1.0 / (1.0 + jnp.exp(-x))


def _silu(x):
    return x * _sigmoid(x)


def _gelu_tanh(x):
    c = math.sqrt(2.0 / math.pi)
    return x * (0.5 * (1.0 + jnp.tanh(c * (x + 0.044715 * (x * x * x)))))


def _neg_expm1(x):
    return -jnp.tanh(0.5 * x) * (jnp.exp(x) + 1.0)


def _rms(x, g):
    ms = jnp.mean(x * x, axis=-1, keepdims=True)
    return x * lax.rsqrt(ms + EPS) * g


def _dot(a, b):
    return jnp.dot(a, b, preferred_element_type=F32)


def _dot_nt(a, b):
    return lax.dot_general(a, b, (((1,), (1,)), ((), ())), preferred_element_type=F32)


def _dot_tn(a, b):
    return lax.dot_general(a, b, (((0,), (0,)), ((), ())), preferred_element_type=F32)


def _eye(n):
    return lax.broadcasted_iota(jnp.int32, (n, n), 0) == lax.broadcasted_iota(jnp.int32, (n, n), 1)


def _row_to_col(row, eye):
    n = eye.shape[0]
    return jnp.sum(jnp.where(eye, jnp.broadcast_to(row, (n, n)), 0.0), axis=1, keepdims=True)


def _col_to_row(col, eye):
    n = eye.shape[0]
    return jnp.sum(jnp.where(eye, jnp.broadcast_to(col, (n, n)), 0.0), axis=0, keepdims=True)


def _chunk_masks(n, chunk):
    shift = chunk.bit_length() - 1
    r = lax.broadcasted_iota(jnp.int32, (n, n), 0)
    c = lax.broadcasted_iota(jnp.int32, (n, n), 1)
    same = lax.shift_right_logical(r, shift) == lax.shift_right_logical(c, shift)
    tril = jnp.where(same & (c <= r), 1.0, 0.0).astype(F32)
    full = jnp.where(same, 1.0, 0.0).astype(F32)
    return tril, full


def _norm_matmul_kernel(x_ref, g_ref, w_ref, o_ref, u_ref):
    @pl.when(pl.program_id(1) == 0)
    def _():
        u_ref[...] = _rms(x_ref[...], g_ref[...]).astype(BF16)

    o_ref[...] = _dot(u_ref[...], w_ref[...]).astype(o_ref.dtype)


def _norm_matmul(x, g, w, *, tm, tn, out_dtype=F32):
    n, d = x.shape
    nout = w.shape[1]
    return pl.pallas_call(
        _norm_matmul_kernel,
        grid=(n // tm, nout // tn),
        in_specs=[pl.BlockSpec((tm, d), lambda i, j: (i, 0)),
                  pl.BlockSpec((1, d), lambda i, j: (0, 0)),
                  pl.BlockSpec((d, tn), lambda i, j: (0, j))],
        out_specs=pl.BlockSpec((tm, tn), lambda i, j: (i, j)),
        out_shape=jax.ShapeDtypeStruct((n, nout), out_dtype),
        scratch_shapes=[pltpu.VMEM((tm, d), BF16)],
        compiler_params=pltpu.CompilerParams(
            dimension_semantics=("parallel", "arbitrary"), vmem_limit_bytes=VMEM_LIMIT),
        name="norm_matmul",
    )(x, g.reshape(1, d), w)


def _in_proj_kernel(x_ref, g_ref, w_ref, ws_ref, o_ref, os_ref, u_ref):
    @pl.when(pl.program_id(1) == 0)
    def _():
        u = _rms(x_ref[...], g_ref[...]).astype(BF16)
        u_ref[...] = u
        os_ref[...] = _dot(u, ws_ref[...])

    o_ref[...] = _dot(u_ref[...], w_ref[...])


def _in_proj(x, g, w_main, w_small, *, tm, tn):
    n, d = x.shape
    return pl.pallas_call(
        _in_proj_kernel,
        grid=(n // tm, MAIN_COLS // tn),
        in_specs=[pl.BlockSpec((tm, d), lambda i, j: (i, 0)),
                  pl.BlockSpec((1, d), lambda i, j: (0, 0)),
                  pl.BlockSpec((d, tn), lambda i, j: (0, j)),
                  pl.BlockSpec((d, SMALL_COLS), lambda i, j: (0, 0))],
        out_specs=[pl.BlockSpec((tm, tn), lambda i, j: (i, j)),
                   pl.BlockSpec((tm, SMALL_COLS), lambda i, j: (i, 0))],
        out_shape=[jax.ShapeDtypeStruct((n, MAIN_COLS), F32),
                   jax.ShapeDtypeStruct((n, SMALL_COLS), F32)],
        scratch_shapes=[pltpu.VMEM((tm, d), BF16)],
        compiler_params=pltpu.CompilerParams(
            dimension_semantics=("parallel", "arbitrary"), vmem_limit_bytes=VMEM_LIMIT),
        name="in_proj",
    )(x, g.reshape(1, d), w_main, w_small)


def _rg_gates(xc, wr_ref, br_ref, wi_ref, bi_ref, lam_ref):
    xcb = xc.astype(BF16)
    r = _sigmoid(_dot(xcb, wr_ref[...]) + br_ref[...])
    i = _sigmoid(_dot(xcb, wi_ref[...]) + bi_ref[...])
    log_a = (-RG_C * r) * _softplus(-lam_ref[...])
    a = jnp.exp(log_a)
    b = jnp.sqrt(_neg_expm1(2.0 * log_a)) * (i * xc)
    return a, b


def _mix_prompt_kernel(x_ref, g_ref, wmain_ref, wsmall_ref,
                       wa2_ref, ba_ref, gnorm_ref, bif_ref, mnorm_ref,
                       convw_ref, convb_ref, wr_ref, br_ref, wi_ref, bi_ref, lam_ref,
                       wb_ref, wo_ref,
                       xo_ref, s_ref, c_ref, n_ref, m_ref, h_ref, buf_ref,
                       pm_ref, br_out_ref, og_ref, xpad_ref):
    tt = x_ref.shape[0]
    j = pl.program_id(1)

    @pl.when(j == 0)
    def _():
        s_ref[...] = jnp.zeros_like(s_ref)
        c_ref[...] = jnp.zeros_like(c_ref)
        n_ref[...] = jnp.zeros_like(n_ref)
        m_ref[...] = jnp.zeros_like(m_ref)
        h_ref[...] = jnp.zeros_like(h_ref)
        xpad_ref[pl.ds(0, 8), :] = jnp.zeros((8, RG_WIDTH), F32)

    u = _rms(x_ref[...], g_ref[...]).astype(BF16)
    for cb in range(MAIN_COLS // PROJ_TILE):
        cols = slice(cb * PROJ_TILE, (cb + 1) * PROJ_TILE)
        pm_ref[:, cols] = _dot(u, wmain_ref[:, cols])
    ps = _dot(u, wsmall_ref[...])

    a_pre = _dot(ps.astype(BF16), wa2_ref[...]) + ba_ref[...]
    log_a = _log_sigmoid(a_pre) * (1.0 / GLA_GATE_NORM)
    tril32, full32 = _chunk_masks(tt, GLA_CHUNK)
    b_cum = jnp.dot(tril32, log_a, precision=HIGHEST, preferred_element_type=F32)
    b_end = jnp.dot(full32, log_a, precision=HIGHEST, preferred_element_type=F32)
    q = pm_ref[:, C_GQ:C_GQ + HEADS * DK] * (DK ** -0.5)
    k = pm_ref[:, C_GK:C_GK + HEADS * DK]
    qi_all = (q * jnp.exp(b_cum)).astype(BF16)
    ki_all = (k * jnp.exp(-b_cum)).astype(BF16)
    ke_all = (k * jnp.exp(b_end - b_cum)).astype(BF16)
    de_t = jnp.exp(b_end).T
    in_chunk = tril32 > 0.0
    n_chunks = tt // GLA_CHUNK
    for h in range(HEADS):
        kcols = slice(h * DK, (h + 1) * DK)
        qi = qi_all[:, kcols]
        ke = ke_all[:, kcols]
        v = pm_ref[:, C_GV + h * DV:C_GV + (h + 1) * DV].astype(BF16)
        att = jnp.where(in_chunk, _dot_nt(qi, ki_all[:, kcols]), 0.0)
        o_intra = _dot(att.astype(BF16), v)
        s_cur = s_ref[h]
        o_inter = []
        for c in range(n_chunks):
            rows = slice(c * GLA_CHUNK, (c + 1) * GLA_CHUNK)
            o_inter.append(_dot(qi[rows], s_cur.astype(BF16)))
            de_col = de_t[kcols, c * GLA_CHUNK:c * GLA_CHUNK + 1]
            s_cur = de_col * s_cur + _dot_tn(ke[rows], v[rows])
        s_ref[h] = s_cur
        og_ref[:, h * DV:(h + 1) * DV] = o_intra + jnp.concatenate(o_inter, axis=0)

    gnorm = gnorm_ref[...]
    for h in range(HEADS):
        cols = slice(h * DV, (h + 1) * DV)
        y = _rms(og_ref[:, cols], gnorm) * _silu(pm_ref[:, C_GG + h * DV:C_GG + (h + 1) * DV])
        br_out_ref[:, cols] = y.astype(BF16)

    pre = ps + bif_ref[...]
    rr = lax.broadcasted_iota(jnp.int32, (tt, tt), 0)
    cc = lax.broadcasted_iota(jnp.int32, (tt, tt), 1)
    causal = cc <= rr
    fcum = jnp.dot(jnp.where(causal, 1.0, 0.0).astype(F32), _log_sigmoid(pre),
                   precision=HIGHEST, preferred_element_type=F32)
    pre_t = pre.T
    fcum_t = fcum.T
    q_all = pm_ref[:, C_MQ:C_MQ + HEADS * DK] * (DK ** -0.5)
    k_all = pm_ref[:, C_MK:C_MK + HEADS * DK]
    for h in range(HEADS):
        kcols = slice(h * DK, (h + 1) * DK)
        fc_col = fcum[:, L_MF + h:L_MF + h + 1]
        ic_col = pre[:, L_MI + h:L_MI + h + 1]
        fc_row = fcum_t[L_MF + h:L_MF + h + 1, :]
        ic_row = pre_t[L_MI + h:L_MI + h + 1, :]
        m_prev = m_ref[:, h:h + 1]
        d = jnp.where(causal, fc_col + (ic_row - fc_row), -jnp.inf)
        inter = fc_col + m_prev
        m_t = jnp.maximum(inter, jnp.max(d, axis=1, keepdims=True))
        w_intra = jnp.exp(d - m_t)
        w_inter = jnp.exp(inter - m_t)
        qf = q_all[:, kcols]
        kf = k_all[:, kcols]
        qc = qf.astype(BF16)
        vc = pm_ref[:, C_MV + h * DV:C_MV + (h + 1) * DV].astype(BF16)
        s = _dot_nt(qc, kf.astype(BF16)) * w_intra
        c_old = c_ref[h]
        n_old = n_ref[h]
        num = w_inter * _dot(qc, c_old.astype(BF16)) + _dot(s.astype(BF16), vc)
        den = (w_inter * jnp.sum(qf * n_old, axis=1, keepdims=True)
               + jnp.sum(s, axis=1, keepdims=True))
        hh = num / jnp.maximum(jnp.abs(den), jnp.exp(-m_t))
        og_ref[:, h * DV:(h + 1) * DV] = hh
        f_end = fc_col[tt - 1:tt, :]
        m_end = m_t[tt - 1:tt, :]
        w_k = jnp.exp(f_end - fc_col + ic_col - m_end)
        w_c = jnp.exp(f_end + m_prev - m_end)
        kw = w_k * kf
        c_ref[h] = w_c * c_old + _dot_tn(kw.astype(BF16), vc)
        n_ref[h] = w_c * n_old + jnp.sum(kw, axis=0, keepdims=True)
        m_ref[:, h:h + 1] = m_end

    for h in range(HEADS):
        cols = slice(h * DV, (h + 1) * DV)
        y = (_rms(og_ref[:, cols], mnorm_ref[:, cols])
             * _sigmoid(pm_ref[:, C_MO + h * DV:C_MO + (h + 1) * DV]))
        br_out_ref[:, BRANCH_W + h * DV:BRANCH_W + (h + 1) * DV] = y.astype(BF16)

    x = pm_ref[:, C_RX:C_RX + RG_WIDTH]
    xpad_ref[pl.ds(8, tt), :] = x
    xc = (convb_ref[...] + convw_ref[3:4, :] * x
          + convw_ref[2:3, :] * xpad_ref[pl.ds(7, tt), :]
          + convw_ref[1:2, :] * xpad_ref[pl.ds(6, tt), :]
          + convw_ref[0:1, :] * xpad_ref[pl.ds(5, tt), :])
    buf_ref[...] = xpad_ref[pl.ds(tt + 8 - (RG_CONV - 1), RG_CONV - 1), :]
    xpad_ref[pl.ds(0, 8), :] = xpad_ref[pl.ds(tt, 8), :]
    a, b = _rg_gates(xc, wr_ref, br_ref, wi_ref, bi_ref, lam_ref)
    ridx = lax.broadcasted_iota(jnp.int32, (tt, RG_WIDTH), 0)
    sh = 1
    while sh < tt:
        keep = ridx >= sh
        a_sh = jnp.where(keep, pltpu.roll(a, sh, 0), 1.0)
        b_sh = jnp.where(keep, pltpu.roll(b, sh, 0), 0.0)
        b = a * b_sh + b
        a = a * a_sh
        sh *= 2
    hseq = a * h_ref[...] + b
    h_ref[...] = hseq[tt - 1:tt, :]
    y = hseq * _gelu_tanh(pm_ref[:, C_RG:C_RG + RG_WIDTH])
    br_out_ref[:, 2 * BRANCH_W:3 * BRANCH_W] = y.astype(BF16)

    merged = None
    for b in range(N_BRANCH):
        z = _dot(br_out_ref[:, b * BRANCH_W:(b + 1) * BRANCH_W], wb_ref[b])
        gate = pm_ref[:, MIX_COLS + b * D_MODEL:MIX_COLS + (b + 1) * D_MODEL]
        term = _sigmoid(gate) * z
        merged = term if merged is None else merged + term
    xo_ref[...] = x_ref[...] + _dot(merged.astype(BF16), wo_ref[...])


def _mix_prompt(x, g, lw, *, nb, t, tt):
    d = x.shape[1]
    const = lambda shape: pl.BlockSpec(shape, lambda b, j: (0,) * len(shape),
                                       pipeline_mode=pl.Buffered(1))
    nt = t // tt
    return pl.pallas_call(
        _mix_prompt_kernel,
        grid=(nb, nt),
        in_specs=[pl.BlockSpec((tt, d), lambda b, j: (b * nt + j, 0)),
                  const((1, d)), const((d, MAIN_COLS)), const((d, SMALL_COLS)),
                  const((SMALL_COLS, HEADS * DK)), const((1, HEADS * DK)), const((1, DV)),
                  const((1, SMALL_COLS)), const((1, BRANCH_W)),
                  const((RG_CONV, RG_WIDTH)), const((1, RG_WIDTH)),
                  const((RG_WIDTH, RG_WIDTH)), const((1, RG_WIDTH)),
                  const((RG_WIDTH, RG_WIDTH)), const((1, RG_WIDTH)), const((1, RG_WIDTH)),
                  const((N_BRANCH, BRANCH_W, d)), const((d, d))],
        out_specs=[pl.BlockSpec((tt, d), lambda b, j: (b * nt + j, 0)),
                   pl.BlockSpec((None, HEADS, DK, DV), lambda b, j: (b, 0, 0, 0)),
                   pl.BlockSpec((None, HEADS, DK, DV), lambda b, j: (b, 0, 0, 0)),
                   pl.BlockSpec((None, HEADS, 1, DK), lambda b, j: (b, 0, 0, 0)),
                   pl.BlockSpec((None, 1, HEADS), lambda b, j: (b, 0, 0)),
                   pl.BlockSpec((None, 1, RG_WIDTH), lambda b, j: (b, 0, 0)),
                   pl.BlockSpec((None, RG_CONV - 1, RG_WIDTH), lambda b, j: (b, 0, 0))],
        out_shape=[jax.ShapeDtypeStruct((nb * t, d), F32),
                   jax.ShapeDtypeStruct((nb, HEADS, DK, DV), F32),
                   jax.ShapeDtypeStruct((nb, HEADS, DK, DV), F32),
                   jax.ShapeDtypeStruct((nb, HEADS, 1, DK), F32),
                   jax.ShapeDtypeStruct((nb, 1, HEADS), F32),
                   jax.ShapeDtypeStruct((nb, 1, RG_WIDTH), F32),
                   jax.ShapeDtypeStruct((nb, RG_CONV - 1, RG_WIDTH), F32)],
        scratch_shapes=[pltpu.VMEM((tt, MAIN_COLS), F32),
                        pltpu.VMEM((tt, N_BRANCH * BRANCH_W), BF16),
                        pltpu.VMEM((tt, BRANCH_W), F32),
                        pltpu.VMEM((tt + 8, RG_WIDTH), F32)],
        compiler_params=pltpu.CompilerParams(
            dimension_semantics=("parallel", "arbitrary"), vmem_limit_bytes=VMEM_LIMIT),
        name="mix_prompt",
    )(x, g.reshape(1, d), lw["w_main"], lw["w_small"],
      lw["wa2"], lw["ba"], lw["gnorm"], lw["bif"], lw["mnorm"], lw["convw"], lw["convb"],
      lw["wr"], lw["br"], lw["wi"], lw["bi"], lw["lam"], lw["wb"], lw["wo"])


def _mix_sample_kernel(pm_ref, ps_ref, s0_ref, c0_ref, n0_ref, m0_ref, h0_ref, buf0_ref,
                       wa2_ref, ba_ref, gnorm_ref, bif_ref, mnorm_ref,
                       convw_ref, convb_ref, wr_ref, br_ref, wi_ref, bi_ref, lam_ref,
                       br_out_ref, s_ref, c_ref, n_ref, m_ref, h_ref, buf_ref,
                       og_ref, om_ref):
    bs = pm_ref.shape[0]
    ps = ps_ref[...]
    a_pre = _dot(ps.astype(BF16), wa2_ref[...]) + ba_ref[...]
    a = jnp.exp(_log_sigmoid(a_pre) * (1.0 / GLA_GATE_NORM))
    a_hi = a.astype(BF16)
    a_r1 = a - a_hi.astype(F32)
    a_mid = a_r1.astype(BF16)
    a_lo = (a_r1 - a_mid.astype(F32)).astype(BF16)
    mq = pm_ref[:, C_MQ:C_MQ + HEADS * DK] * (DK ** -0.5)
    mk = pm_ref[:, C_MK:C_MK + HEADS * DK]
    kd = HEADS * DK
    cols_t = jnp.concatenate(
        [(pm_ref[:, C_GQ:C_GQ + kd] * (DK ** -0.5)).T, pm_ref[:, C_GK:C_GK + kd].T,
         mq.T, mk.T, a_hi.astype(F32).T, a_mid.astype(F32).T, a_lo.astype(F32).T],
        axis=0).astype(BF16)
    o_gq, o_gk, o_mq, o_mk, o_a0, o_a1, o_a2 = (i * kd for i in range(7))
    sample_id = lax.broadcasted_iota(jnp.int32, (bs, DV), 0)
    pre = ps + bif_ref[...]
    lf = _log_sigmoid(pre)
    m0_all = m0_ref[...]
    head_lane = lax.broadcasted_iota(jnp.int32, (1, HEADS), 1)

    for s in range(bs):
        row = slice(s, s + 1)
        bc = _dot(cols_t, jnp.where(sample_id == s, 1.0, 0.0).astype(BF16))
        m_row = m0_all[row, :]
        for h in range(HEADS):
            kcols = slice(h * DK, (h + 1) * DK)
            col = lambda off: bc[off + h * DK:off + (h + 1) * DK, :]
            v_row = pm_ref[row, C_GV + h * DV:C_GV + (h + 1) * DV]
            a_col = col(o_a0) + col(o_a1) + col(o_a2)
            s_new = a_col * s0_ref[s, h] + col(o_gk) * v_row
            s_ref[s, h] = s_new
            og_ref[row, h * DV:(h + 1) * DV] = jnp.sum(col(o_gq) * s_new,
                                                       axis=0, keepdims=True)
            ic = pre[row, L_MI + h:L_MI + h + 1]
            fc = lf[row, L_MF + h:L_MF + h + 1]
            m0 = m0_all[row, h:h + 1]
            inter = fc + m0
            m_t = jnp.maximum(inter, ic)
            w_intra = jnp.exp(ic - m_t)
            w_inter = jnp.exp(inter - m_t)
            q_row = mq[row, kcols]
            k_row = mk[row, kcols]
            v_row = pm_ref[row, C_MV + h * DV:C_MV + (h + 1) * DV]
            sc = jnp.sum(q_row * k_row, axis=1, keepdims=True) * w_intra
            c_old = c0_ref[s, h]
            n_old = n0_ref[s, h]
            qc = jnp.sum(col(o_mq) * c_old, axis=0, keepdims=True)
            num = w_inter * qc + sc * v_row
            den = w_inter * jnp.sum(q_row * n_old, axis=1, keepdims=True) + sc
            om_ref[row, h * DV:(h + 1) * DV] = num / jnp.maximum(jnp.abs(den), jnp.exp(-m_t))
            c_ref[s, h] = w_inter * c_old + (w_intra * col(o_mk)) * v_row
            n_ref[s, h] = w_inter * n_old + w_intra * k_row
            m_row = jnp.where(head_lane == h, m_t, m_row)
        m_ref[row, :] = m_row

    gnorm = gnorm_ref[...]
    for h in range(HEADS):
        cols = slice(h * DV, (h + 1) * DV)
        y = _rms(og_ref[:, cols], gnorm) * _silu(pm_ref[:, C_GG + h * DV:C_GG + (h + 1) * DV])
        br_out_ref[:, cols] = y.astype(BF16)
        y = (_rms(om_ref[:, cols], mnorm_ref[:, cols])
             * _sigmoid(pm_ref[:, C_MO + h * DV:C_MO + (h + 1) * DV]))
        br_out_ref[:, BRANCH_W + h * DV:BRANCH_W + (h + 1) * DV] = y.astype(BF16)

    x = pm_ref[:, C_RX:C_RX + RG_WIDTH]
    xc = convb_ref[...] + convw_ref[3:4, :] * x
    for jj in range(RG_CONV - 1):
        xc = xc + convw_ref[jj:jj + 1, :] * buf0_ref[:, jj * RG_WIDTH:(jj + 1) * RG_WIDTH]
    buf_ref[:, 0:2 * RG_WIDTH] = buf0_ref[:, RG_WIDTH:3 * RG_WIDTH]
    buf_ref[:, 2 * RG_WIDTH:3 * RG_WIDTH] = x
    a, b = _rg_gates(xc, wr_ref, br_ref, wi_ref, bi_ref, lam_ref)
    hnew = a * h0_ref[...] + b
    h_ref[...] = hnew
    y = hnew * _gelu_tanh(pm_ref[:, C_RG:C_RG + RG_WIDTH])
    br_out_ref[:, 2 * BRANCH_W:3 * BRANCH_W] = y.astype(BF16)


def _mix_sample(pm, ps, st, layer, lw, *, bs):
    n = pm.shape[0]
    const = lambda shape: pl.BlockSpec(shape, lambda i: (0,) * len(shape))
    blk = lambda shape: pl.BlockSpec(shape, lambda i: (i,) + (0,) * (len(shape) - 1))
    lblk = lambda shape: pl.BlockSpec((None,) + shape,
                                      lambda i: (layer, i) + (0,) * (len(shape) - 1))
    s0, c0, n0, m0, h0, buf0 = st
    shapes = [(bs, HEADS, DK, DV), (bs, HEADS, DK, DV), (bs, HEADS, 1, DK),
              (bs, HEADS), (bs, RG_WIDTH), (bs, (RG_CONV - 1) * RG_WIDTH)]
    return pl.pallas_call(
        _mix_sample_kernel,
        grid=(n // bs,),
        in_specs=[blk((bs, MIX_COLS)), blk((bs, SMALL_COLS))] + [lblk(sh) for sh in shapes] + [
            const((SMALL_COLS, HEADS * DK)), const((1, HEADS * DK)), const((1, DV)),
            const((1, SMALL_COLS)), const((1, BRANCH_W)),
            const((RG_CONV, RG_WIDTH)), const((1, RG_WIDTH)),
            const((RG_WIDTH, RG_WIDTH)), const((1, RG_WIDTH)),
            const((RG_WIDTH, RG_WIDTH)), const((1, RG_WIDTH)), const((1, RG_WIDTH))],
        out_specs=[blk((bs, N_BRANCH * BRANCH_W))] + [blk(sh) for sh in shapes],
        out_shape=[jax.ShapeDtypeStruct((n, N_BRANCH * BRANCH_W), BF16)]
        + [jax.ShapeDtypeStruct(a.shape[1:], F32) for a in st],
        scratch_shapes=[pltpu.VMEM((bs, BRANCH_W), F32),
                        pltpu.VMEM((bs, BRANCH_W), F32)],
        compiler_params=pltpu.CompilerParams(
            dimension_semantics=("parallel",), vmem_limit_bytes=VMEM_LIMIT),
        name="mix_sample",
    )(pm, ps, s0, c0, n0, m0, h0, buf0,
      lw["wa2"], lw["ba"], lw["gnorm"], lw["bif"], lw["mnorm"], lw["convw"], lw["convb"],
      lw["wr"], lw["br"], lw["wi"], lw["bi"], lw["lam"])


def _merge_kernel(x_ref, br_ref, g0_ref, g1_ref, g2_ref, wb_ref, wo_ref, o_ref):
    merged = None
    for b, g_ref in enumerate((g0_ref, g1_ref, g2_ref)):
        z = _dot(br_ref[:, b * BRANCH_W:(b + 1) * BRANCH_W], wb_ref[b])
        term = _sigmoid(g_ref[...]) * z
        merged = term if merged is None else merged + term
    o_ref[...] = x_ref[...] + _dot(merged.astype(BF16), wo_ref[...])


def _merge(x, br, pm, wb, wo, *, tm):
    n, d = x.shape
    gate_blk0 = MIX_COLS // d
    gate_spec = lambda b: pl.BlockSpec((tm, d), lambda i: (i, gate_blk0 + b))
    return pl.pallas_call(
        _merge_kernel,
        grid=(n // tm,),
        in_specs=[pl.BlockSpec((tm, d), lambda i: (i, 0)),
                  pl.BlockSpec((tm, N_BRANCH * BRANCH_W), lambda i: (i, 0)),
                  gate_spec(0), gate_spec(1), gate_spec(2),
                  pl.BlockSpec((N_BRANCH, BRANCH_W, d), lambda i: (0, 0, 0)),
                  pl.BlockSpec((d, d), lambda i: (0, 0))],
        out_specs=pl.BlockSpec((tm, d), lambda i: (i, 0)),
        out_shape=jax.ShapeDtypeStruct((n, d), F32),
        compiler_params=pltpu.CompilerParams(
            dimension_semantics=("parallel",), vmem_limit_bytes=VMEM_LIMIT),
        name="merge",
    )(x, br, pm, pm, pm, wb, wo)


def _xattn_prompt_kernel(x_ref, g_ref, wq_ref, wo_ref, k_ref, v_ref, o_ref, kb_ref, vb_ref):
    @pl.when(pl.program_id(1) == 0)
    def _():
        kb_ref[...] = k_ref[...].astype(BF16)
        vb_ref[...] = v_ref[...].astype(BF16)

    x = x_ref[...]
    q = _dot(_rms(x, g_ref[...]).astype(BF16), wq_ref[...]).astype(BF16)
    heads = []
    for h in range(XA_HEADS):
        cols = slice(h * XA_HEAD_DIM, (h + 1) * XA_HEAD_DIM)
        s = _dot_nt(q[:, cols], kb_ref[:, cols]) * (XA_HEAD_DIM ** -0.5)
        p = jnp.exp(s - jnp.max(s, axis=-1, keepdims=True))
        p = p / jnp.sum(p, axis=-1, keepdims=True)
        heads.append(_dot(p.astype(BF16), vb_ref[:, cols]))
    o = jnp.concatenate(heads, axis=1).astype(BF16)
    o_ref[...] = x + _dot(o, wo_ref[...])


def _xattn_prompt(x, g, wq, wo, mem_k, mem_v, *, nb, t, tm):
    d = x.shape[1]
    nt = t // tm
    return pl.pallas_call(
        _xattn_prompt_kernel,
        grid=(nb, nt),
        in_specs=[pl.BlockSpec((tm, d), lambda b, j: (b * nt + j, 0)),
                  pl.BlockSpec((1, d), lambda b, j: (0, 0)),
                  pl.BlockSpec((d, d), lambda b, j: (0, 0)),
                  pl.BlockSpec((d, d), lambda b, j: (0, 0)),
                  pl.BlockSpec((None, N_MEM, d), lambda b, j: (b, 0, 0)),
                  pl.BlockSpec((None, N_MEM, d), lambda b, j: (b, 0, 0))],
        out_specs=pl.BlockSpec((tm, d), lambda b, j: (b * nt + j, 0)),
        out_shape=jax.ShapeDtypeStruct(x.shape, F32),
        scratch_shapes=[pltpu.VMEM((N_MEM, d), BF16), pltpu.VMEM((N_MEM, d), BF16)],
        compiler_params=pltpu.CompilerParams(
            dimension_semantics=("parallel", "arbitrary"), vmem_limit_bytes=VMEM_LIMIT),
        name="xattn_prompt",
    )(x, g.reshape(1, d), wq, wo, mem_k, mem_v)


def _xattn_sample_kernel(q_ref, k_ref, v_ref, o_ref):
    bs = q_ref.shape[0]
    for s in range(bs):
        q = q_ref[s]
        sc = jnp.sum(k_ref[s] * q[None], axis=-1, keepdims=True) * (XA_HEAD_DIM ** -0.5)
        p = jnp.exp(sc - jnp.max(sc, axis=0, keepdims=True))
        p = p / jnp.sum(p, axis=0, keepdims=True)
        o_ref[s] = jnp.sum(p * v_ref[s], axis=0)


def _xattn_sample(q, cache_k, cache_v, layer, *, bs):
    n = q.shape[0]
    blk = (None, bs, N_MEM, XA_HEADS, XA_HEAD_DIM)
    return pl.pallas_call(
        _xattn_sample_kernel,
        grid=(n // bs,),
        in_specs=[pl.BlockSpec((bs, XA_HEADS, XA_HEAD_DIM), lambda i: (i, 0, 0)),
                  pl.BlockSpec(blk, lambda i: (layer, i, 0, 0, 0)),
                  pl.BlockSpec(blk, lambda i: (layer, i, 0, 0, 0))],
        out_specs=pl.BlockSpec((bs, XA_HEADS, XA_HEAD_DIM), lambda i: (i, 0, 0)),
        out_shape=jax.ShapeDtypeStruct((n, XA_HEADS, XA_HEAD_DIM), F32),
        compiler_params=pltpu.CompilerParams(
            dimension_semantics=("parallel",), vmem_limit_bytes=VMEM_LIMIT),
        name="xattn_sample",
    )(q, cache_k, cache_v)


def _matmul_residual_kernel(x_ref, a_ref, w_ref, o_ref):
    o_ref[...] = x_ref[...] + _dot(a_ref[...].astype(BF16), w_ref[...])


def _matmul_residual(x, a, w):
    n, d = x.shape
    return pl.pallas_call(
        _matmul_residual_kernel,
        out_shape=jax.ShapeDtypeStruct((n, d), F32),
        compiler_params=pltpu.CompilerParams(vmem_limit_bytes=VMEM_LIMIT),
        name="matmul_residual",
    )(x, a, w)


def _ffn_kernel(x_ref, g_ref, wg_ref, wu_ref, wd_ref, gf_ref, o_ref, u_ref, acc_ref, *, final_norm):
    j = pl.program_id(1)

    @pl.when(j == 0)
    def _():
        u_ref[...] = _rms(x_ref[...], g_ref[...]).astype(BF16)
        acc_ref[...] = x_ref[...]

    u = u_ref[...]
    hid = _silu(_dot(u, wg_ref[...])) * _dot(u, wu_ref[...])
    acc_ref[...] += _dot(hid.astype(BF16), wd_ref[...])

    @pl.when(j == pl.num_programs(1) - 1)
    def _():
        y = acc_ref[...]
        if final_norm:
            y = _rms(y, gf_ref[...])
        o_ref[...] = y


def _ffn(x, g, wg, wu, wd, gf, *, tm, tf, final_norm):
    n, d = x.shape
    f = wg.shape[1]
    return pl.pallas_call(
        functools.partial(_ffn_kernel, final_norm=final_norm),
        grid=(n // tm, f // tf),
        in_specs=[pl.BlockSpec((tm, d), lambda i, j: (i, 0)),
                  pl.BlockSpec((1, d), lambda i, j: (0, 0)),
                  pl.BlockSpec((d, tf), lambda i, j: (0, j)),
                  pl.BlockSpec((d, tf), lambda i, j: (0, j)),
                  pl.BlockSpec((tf, d), lambda i, j: (j, 0)),
                  pl.BlockSpec((1, d), lambda i, j: (0, 0))],
        out_specs=pl.BlockSpec((tm, d), lambda i, j: (i, 0)),
        out_shape=jax.ShapeDtypeStruct((n, d), F32),
        scratch_shapes=[pltpu.VMEM((tm, d), BF16), pltpu.VMEM((tm, d), F32)],
        compiler_params=pltpu.CompilerParams(
            dimension_semantics=("parallel", "arbitrary"), vmem_limit_bytes=VMEM_LIMIT),
        name="ffn",
    )(x, g.reshape(1, d), wg, wu, wd, gf.reshape(1, d))


def _split_w_in(w_in):
    widths = (HEADS * DK, HEADS * DK, HEADS * DV, HEADS * DV, GLA_RANK,
              HEADS * DK, HEADS * DK, HEADS * DV, HEADS * DV, HEADS, HEADS,
              RG_WIDTH, RG_WIDTH, GATE_COLS)
    parts, acc = [], 0
    for w in widths:
        parts.append(w_in[:, acc:acc + w])
        acc += w
    return parts


def _block_diag(w):
    eye = jnp.eye(RG_BLOCKS, dtype=w.dtype)
    return jnp.einsum("nde,nm->ndme", w, eye).reshape(RG_WIDTH, RG_WIDTH)


def _layer_weights(l, p):
    (g_q, g_k, g_v, g_g, g_a, m_q, m_k, m_v, m_o, m_i, m_f, r_x, r_g, gates) = _split_w_in(p["w_in"][l])
    d = D_MODEL
    w_main = jnp.concatenate([g_q, g_k, g_v, g_g, m_q, m_k, m_v, m_o, r_x, r_g, gates], axis=1)
    w_small = jnp.concatenate(
        [g_a, m_i, m_f, jnp.zeros((d, SMALL_COLS - GLA_RANK - 2 * HEADS), F32)], axis=1)
    wa2 = jnp.concatenate(
        [p["gla_w_a2"][l], jnp.zeros((SMALL_COLS - GLA_RANK, HEADS * DK), F32)], axis=0)
    bif = jnp.concatenate(
        [jnp.zeros((L_MI,), F32), p["mlstm_b_i"][l], p["mlstm_b_f"][l],
         jnp.zeros((SMALL_COLS - L_MF - HEADS,), F32)]).reshape(1, SMALL_COLS)
    return {
        "w_main": w_main.astype(BF16), "w_small": w_small.astype(BF16),
        "wa2": wa2.astype(BF16), "ba": p["gla_b_a"][l].reshape(1, -1),
        "gnorm": p["gla_norm"][l].reshape(1, DV), "bif": bif,
        "mnorm": p["mlstm_norm"][l].reshape(1, BRANCH_W),
        "convw": p["rg_conv_w"][l], "convb": p["rg_conv_b"][l].reshape(1, -1),
        "wr": _block_diag(p["rg_w_r"][l]).astype(BF16), "br": p["rg_b_r"][l].reshape(1, -1),
        "wi": _block_diag(p["rg_w_i"][l]).astype(BF16), "bi": p["rg_b_i"][l].reshape(1, -1),
        "lam": p["rg_lambda"][l].reshape(1, -1),
        "wb": p["w_branch"][l].astype(BF16), "wo": p["w_out"][l].astype(BF16),
        "xa_wq": p["xa_wq"][l].astype(BF16), "xa_wo": p["xa_wo"][l].astype(BF16),
        "xa_wkv": jnp.concatenate([p["xa_wk"][l], p["xa_wv"][l]], axis=1).astype(BF16),
        "wg": p["ffn_w_gate"][l].astype(BF16), "wu": p["ffn_w_up"][l].astype(BF16),
        "wd": p["ffn_w_down"][l].astype(BF16),
    }


def kernel(x_prompt, x_sample, mem_prompt, cache_mem_k, cache_mem_v, state_gla, state_mlstm_c, state_mlstm_n, state_mlstm_m, state_rglru_h, state_rglru_conv, norm_mix, w_in, gla_w_a2, gla_b_a, gla_norm, mlstm_b_i, mlstm_b_f, mlstm_norm, rg_conv_w, rg_conv_b, rg_w_r, rg_b_r, rg_w_i, rg_b_i, rg_lambda, w_branch, w_out, norm_xa, norm_mem, xa_wq, xa_wk, xa_wv, xa_wo, norm_ffn, ffn_w_gate, ffn_w_up, ffn_w_down, norm_final):
    p = {"w_in": w_in, "gla_w_a2": gla_w_a2, "gla_b_a": gla_b_a, "gla_norm": gla_norm,
         "mlstm_b_i": mlstm_b_i, "mlstm_b_f": mlstm_b_f, "mlstm_norm": mlstm_norm,
         "rg_conv_w": rg_conv_w, "rg_conv_b": rg_conv_b, "rg_w_r": rg_w_r, "rg_b_r": rg_b_r,
         "rg_w_i": rg_w_i, "rg_b_i": rg_b_i, "rg_lambda": rg_lambda, "w_branch": w_branch,
         "w_out": w_out, "xa_wq": xa_wq, "xa_wk": xa_wk, "xa_wv": xa_wv, "xa_wo": xa_wo,
         "ffn_w_gate": ffn_w_gate, "ffn_w_up": ffn_w_up, "ffn_w_down": ffn_w_down}
    nb, t, d = x_prompt.shape
    ns = x_sample.shape[0]
    xp = x_prompt.reshape(nb * t, d)
    xs = x_sample.reshape(ns, d)
    mem = mem_prompt.reshape(nb * N_MEM, d)

    st_sample = (state_gla, state_mlstm_c, state_mlstm_n.reshape(DEPTH, ns, HEADS, 1, DK),
                 state_mlstm_m, state_rglru_h,
                 state_rglru_conv.reshape(DEPTH, ns, (RG_CONV - 1) * RG_WIDTH))

    new_p = [[] for _ in range(8)]
    new_s = [[] for _ in range(6)]
    for l in range(DEPTH):
        lw = _layer_weights(l, p)
        last = l == DEPTH - 1

        kv = _norm_matmul(mem, norm_mem[l], lw["xa_wkv"], tm=512, tn=1024)
        k_p = kv[:, :d].reshape(nb, N_MEM, d)
        v_p = kv[:, d:].reshape(nb, N_MEM, d)

        xp, g_s, c_s, n_s, m_s, h_s, buf_s = _mix_prompt(xp, norm_mix[l], lw, nb=nb, t=t, tt=256)
        xp = _xattn_prompt(xp, norm_xa[l], lw["xa_wq"], lw["xa_wo"], k_p, v_p, nb=nb, t=t, tm=512)
        xp = _ffn(xp, norm_ffn[l], lw["wg"], lw["wu"], lw["wd"], norm_final,
                  tm=512, tf=D_FF // 2, final_norm=last)
        for lst, val in zip(new_p, (k_p.reshape(nb, N_MEM, XA_HEADS, XA_HEAD_DIM),
                                    v_p.reshape(nb, N_MEM, XA_HEADS, XA_HEAD_DIM),
                                    g_s, c_s, n_s.reshape(nb, HEADS, DK), m_s.reshape(nb, HEADS),
                                    h_s.reshape(nb, RG_WIDTH), buf_s)):
            lst.append(val)

        pm, ps = _in_proj(xs, norm_mix[l], lw["w_main"], lw["w_small"], tm=ns, tn=1024)
        br, g_s, c_s, n_s, m_s, h_s, buf_s = _mix_sample(pm, ps, st_sample, l, lw, bs=8)
        xs = _merge(xs, br, pm, lw["wb"], lw["wo"], tm=ns)
        q = _norm_matmul(xs, norm_xa[l], lw["xa_wq"], tm=ns, tn=d)
        att = _xattn_sample(q.reshape(ns, XA_HEADS, XA_HEAD_DIM), cache_mem_k, cache_mem_v, l, bs=4)
        xs = _matmul_residual(xs, att.reshape(ns, d), lw["xa_wo"])
        xs = _ffn(xs, norm_ffn[l], lw["wg"], lw["wu"], lw["wd"], norm_final,
                  tm=ns, tf=D_FF // 2, final_norm=last)
        for lst, val in zip(new_s, (g_s, c_s, n_s.reshape(ns, HEADS, DK), m_s, h_s,
                                    buf_s.reshape(ns, RG_CONV - 1, RG_WIDTH))):
            lst.append(val)

    y_prompt = xp.reshape(nb, t, d)
    y_sample = xs.reshape(ns, 1, d)
    outs_p = [jnp.stack(v, axis=0) for v in new_p]
    outs_s = [jnp.stack(v, axis=0) for v in new_s]
    return (y_prompt, y_sample, *outs_p, *outs_s)
```

```python
import functools
import math

import jax
import jax.numpy as jnp
from jax import lax
from jax.experimental import pallas as pl
from jax.experimental.pallas import tpu as pltpu

F32 = jnp.float32
BF16 = jnp.bfloat16

D_MODEL = 1024
DEPTH = 2
EPS = 1e-6
N_MEM = 256
XA_HEADS = 4
XA_HEAD_DIM = D_MODEL // XA_HEADS
N_BRANCH = 3
BRANCH_W = D_MODEL // 2
HEADS = 4
DK = 64
DV = 128
GLA_RANK = 16
GLA_GATE_NORM = 16.0
GLA_CHUNK = 32
MLSTM_CHUNK = 64
RG_WIDTH = BRANCH_W
RG_BLOCKS = 8
RG_BLOCK = RG_WIDTH // RG_BLOCKS
RG_CONV = 4
RG_C = 8.0
D_FF = 2816

C_GQ, C_GK, C_GV, C_GG = 0, 256, 512, 1024
C_MQ, C_MK, C_MV, C_MO = 1536, 1792, 2048, 2560
C_RX, C_RG = 3072, 3584
MIX_COLS = 4096
GATE_COLS = N_BRANCH * D_MODEL
MAIN_COLS = MIX_COLS + GATE_COLS
SMALL_COLS = 128
L_GA = 0
L_MI = 16
L_MF = 20

PROJ_TILE = 256
MERGE_TILE = 256
FFN_TILE = 256
VMEM_LIMIT = 48 * 1024 * 1024


def _softplus(x):
    return jnp.maximum(x, 0.0) + jnp.log1p(jnp.exp(-jnp.abs(x)))


def _log_sigmoid(x):
    return -_softplus(-x)


def _sigmoid(x):
    return 1.0 / (1.0 + jnp.exp(-x))


def _silu(x):
    return x * _sigmoid(x)


def _gelu_tanh(x):
    c = math.sqrt(2.0 / math.pi)
    return x * (0.5 * (1.0 + jnp.tanh(c * (x + 0.044715 * (x * x * x)))))


def _neg_expm1(x):
    return -jnp.tanh(0.5 * x) * (jnp.exp(x) + 1.0)


def _rms(x, g):
    ms = jnp.mean(x * x, axis=-1, keepdims=True)
    return x * lax.rsqrt(ms + EPS) * g


def _dot(a, b):
    return jnp.dot(a, b, preferred_element_type=F32)


def _dot_nt(a, b):
    return lax.dot_general(a, b, (((1,), (1,)), ((), ())), preferred_element_type=F32)


def _dot_tn(a, b):
    return lax.dot_general(a, b, (((0,), (0,)), ((), ())), preferred_element_type=F32)


def _dot_mask_exact(mask, x):
    hi = x.astype(BF16)
    rest = x - hi.astype(F32)
    mid = rest.astype(BF16)
    lo = (rest - mid.astype(F32)).astype(BF16)
    return _dot(mask, hi) + _dot(mask, mid) + _dot(mask, lo)


def _chunk_masks(n, chunk):
    shift = chunk.bit_length() - 1
    r = lax.broadcasted_iota(jnp.int32, (n, n), 0)
    c = lax.broadcasted_iota(jnp.int32, (n, n), 1)
    same = lax.shift_right_logical(r, shift) == lax.shift_right_logical(c, shift)
    tril = jnp.where(same & (c <= r), 1.0, 0.0).astype(F32)
    full = jnp.where(same, 1.0, 0.0).astype(F32)
    return tril, full


def _norm_matmul_kernel(x_ref, g_ref, w_ref, o_ref, u_ref):
    @pl.when(pl.program_id(1) == 0)
    def _():
        u_ref[...] = _rms(x_ref[...], g_ref[...]).astype(BF16)

    o_ref[...] = _dot(u_ref[...], w_ref[...]).astype(o_ref.dtype)


def _norm_matmul(x, g, w, *, tm, tn, out_dtype=F32):
    n, d = x.shape
    nout = w.shape[1]
    return pl.pallas_call(
        _norm_matmul_kernel,
        grid=(n // tm, nout // tn),
        in_specs=[pl.BlockSpec((tm, d), lambda i, j: (i, 0)),
                  pl.BlockSpec((1, d), lambda i, j: (0, 0)),
                  pl.BlockSpec((d, tn), lambda i, j: (0, j))],
        out_specs=pl.BlockSpec((tm, tn), lambda i, j: (i, j)),
        out_shape=jax.ShapeDtypeStruct((n, nout), out_dtype),
        scratch_shapes=[pltpu.VMEM((tm, d), BF16)],
        compiler_params=pltpu.CompilerParams(
            dimension_semantics=("parallel", "arbitrary"), vmem_limit_bytes=VMEM_LIMIT),
        name="norm_matmul",
    )(x, g.reshape(1, d), w)


def _in_proj_kernel(x_ref, g_ref, w_ref, ws_ref, o_ref, os_ref, u_ref):
    @pl.when(pl.program_id(1) == 0)
    def _():
        u = _rms(x_ref[...], g_ref[...]).astype(BF16)
        u_ref[...] = u
        os_ref[...] = _dot(u, ws_ref[...])

    o_ref[...] = _dot(u_ref[...], w_ref[...])


def _in_proj(x, g, w_main, w_small, *, tm, tn):
    n, d = x.shape
    return pl.pallas_call(
        _in_proj_kernel,
        grid=(n // tm, MAIN_COLS // tn),
        in_specs=[pl.BlockSpec((tm, d), lambda i, j: (i, 0)),
                  pl.BlockSpec((1, d), lambda i, j: (0, 0)),
                  pl.BlockSpec((d, tn), lambda i, j: (0, j)),
                  pl.BlockSpec((d, SMALL_COLS), lambda i, j: (0, 0))],
        out_specs=[pl.BlockSpec((tm, tn), lambda i, j: (i, j)),
                   pl.BlockSpec((tm, SMALL_COLS), lambda i, j: (i, 0))],
        out_shape=[jax.ShapeDtypeStruct((n, MAIN_COLS), F32),
                   jax.ShapeDtypeStruct((n, SMALL_COLS), F32)],
        scratch_shapes=[pltpu.VMEM((tm, d), BF16)],
        compiler_params=pltpu.CompilerParams(
            dimension_semantics=("parallel", "arbitrary"), vmem_limit_bytes=VMEM_LIMIT),
        name="in_proj",
    )(x, g.reshape(1, d), w_main, w_small)


def _rg_gates(xc, wr_ref, br_ref, wi_ref, bi_ref, lam_ref):
    xcb = xc.astype(BF16)
    r = _sigmoid(_dot(xcb, wr_ref[...]) + br_ref[...])
    i = _sigmoid(_dot(xcb, wi_ref[...]) + bi_ref[...])
    log_a = (-RG_C * r) * _softplus(-lam_ref[...])
    a = jnp.exp(log_a)
    b = jnp.sqrt(_neg_expm1(2.0 * log_a)) * (i * xc)
    return a, b


def _project_stage(x_ref, g_ref, wmain_ref, wsmall_ref, pm_ref, ps_ref, xs_ref):
    x = x_ref[...]
    xs_ref[...] = x
    u = _rms(x, g_ref[...]).astype(BF16)
    ps_ref[...] = _dot(u, wsmall_ref[...])
    for cb in range(MAIN_COLS // PROJ_TILE):
        cols = slice(cb * PROJ_TILE, (cb + 1) * PROJ_TILE)
        pm_ref[:, cols] = _dot(u, wmain_ref[:, cols])


def _mixer_stage(pm_ref, ps_ref,
                 wa2_ref, ba_ref, gnorm_ref, bif_ref, mnorm_ref,
                 convw_ref, convb_ref, wr_ref, br_ref, wi_ref, bi_ref, lam_ref,
                 s_ref, c_ref, n_ref, m_ref, h_ref, buf_ref,
                 br_out_ref, sg_ref, og_ref, xpad_ref):
    tt = pm_ref.shape[0]
    ps = ps_ref[...]
    a_pre = _dot(ps.astype(BF16), wa2_ref[...]) + ba_ref[...]
    log_a = _log_sigmoid(a_pre) * (1.0 / GLA_GATE_NORM)
    tril32, full32 = _chunk_masks(tt, GLA_CHUNK)
    sums = _dot_mask_exact(jnp.concatenate([tril32, full32], axis=0).astype(BF16), log_a)
    b_cum = sums[:tt]
    b_end = sums[tt:]
    q = pm_ref[:, C_GQ:C_GQ + HEADS * DK] * (DK ** -0.5)
    k = pm_ref[:, C_GK:C_GK + HEADS * DK]
    qi_f = q * jnp.exp(b_cum)
    qi_all = qi_f.astype(BF16)
    ki_all = (k * jnp.exp(-b_cum)).astype(BF16)
    ke_t = (k * jnp.exp(b_end - b_cum)).T
    de_t = jnp.exp(b_end).T
    in_chunk = tril32 > 0.0
    n_chunks = tt // GLA_CHUNK
    cshift = GLA_CHUNK.bit_length() - 1
    chunk_of_lane = lax.shift_right_logical(lax.broadcasted_iota(jnp.int32, (DK, tt), 1), cshift)
    chunk_of_row = lax.shift_right_logical(lax.broadcasted_iota(jnp.int32, (tt, 2 * DK), 0), cshift)
    upper_half = (lax.broadcasted_iota(jnp.int32, (tt, 2 * DK), 1) >= DK).astype(jnp.int32)
    for h in range(HEADS):
        kcols = slice(h * DK, (h + 1) * DK)
        v = pm_ref[:, C_GV + h * DV:C_GV + (h + 1) * DV].astype(BF16)
        att = jnp.where(in_chunk, _dot_nt(qi_all[:, kcols], ki_all[:, kcols]), 0.0)
        o_intra = _dot(att.astype(BF16), v)
        ke_h = ke_t[kcols, :]
        ke_blk = jnp.concatenate(
            [jnp.where(chunk_of_lane == c, ke_h, 0.0) for c in range(n_chunks)], axis=0)
        upd = _dot(ke_blk.astype(BF16), v)
        s_cur = s_ref[h]
        s_start = []
        for c in range(n_chunks):
            s_start.append(s_cur)
            de_col = de_t[kcols, c * GLA_CHUNK:c * GLA_CHUNK + 1]
            s_cur = de_col * s_cur + upd[c * DK:(c + 1) * DK, :]
        s_ref[h] = s_cur
        qi_h = qi_f[:, kcols]
        qi_dup = jnp.concatenate([qi_h, qi_h], axis=1)
        q_blk = jnp.concatenate(
            [jnp.where(chunk_of_row == 2 * jj + upper_half, qi_dup, 0.0)
             for jj in range(n_chunks // 2)], axis=1)
        o_inter = _dot(q_blk.astype(BF16), jnp.concatenate(s_start, axis=0).astype(BF16))
        og_ref[:, h * DV:(h + 1) * DV] = o_intra + o_inter

    gnorm = gnorm_ref[...]
    for h in range(HEADS):
        cols = slice(h * DV, (h + 1) * DV)
        y = _rms(og_ref[:, cols], gnorm) * _silu(pm_ref[:, C_GG + h * DV:C_GG + (h + 1) * DV])
        br_out_ref[:, cols] = y.astype(BF16)

    pre = ps + bif_ref[...]
    rr = lax.broadcasted_iota(jnp.int32, (tt, tt), 0)
    cc = lax.broadcasted_iota(jnp.int32, (tt, tt), 1)
    causal = cc <= rr
    fcum = _dot_mask_exact(jnp.where(causal, 1.0, 0.0).astype(BF16), _log_sigmoid(pre))
    pre_t = pre.T
    fcum_t = fcum.T
    q_all = pm_ref[:, C_MQ:C_MQ + HEADS * DK] * (DK ** -0.5)
    k_all = pm_ref[:, C_MK:C_MK + HEADS * DK]
    for h in range(HEADS):
        kcols = slice(h * DK, (h + 1) * DK)
        fc_col = fcum[:, L_MF + h:L_MF + h + 1]
        ic_col = pre[:, L_MI + h:L_MI + h + 1]
        fc_row = fcum_t[L_MF + h:L_MF + h + 1, :]
        ic_row = pre_t[L_MI + h:L_MI + h + 1, :]
        m_prev = m_ref[:, h:h + 1]
        d = jnp.where(causal, fc_col + (ic_row - fc_row), -jnp.inf)
        inter = fc_col + m_prev
        m_t = jnp.maximum(inter, jnp.max(d, axis=1, keepdims=True))
        w_intra = jnp.exp(d - m_t)
        w_inter = jnp.exp(inter - m_t)
        qf = q_all[:, kcols]
        kf = k_all[:, kcols]
        qc = qf.astype(BF16)
        vc = pm_ref[:, C_MV + h * DV:C_MV + (h + 1) * DV].astype(BF16)
        s = _dot_nt(qc, kf.astype(BF16)) * w_intra
        c_old = c_ref[h]
        n_old = n_ref[h]
        num = w_inter * _dot(qc, c_old.astype(BF16)) + _dot(s.astype(BF16), vc)
        den = (w_inter * jnp.sum(qf * n_old, axis=1, keepdims=True)
               + jnp.sum(s, axis=1, keepdims=True))
        hh = num / jnp.maximum(jnp.abs(den), jnp.exp(-m_t))
        og_ref[:, h * DV:(h + 1) * DV] = hh
        f_end = fc_col[tt - 1:tt, :]
        m_end = m_t[tt - 1:tt, :]
        w_k = jnp.exp(f_end - fc_col + ic_col - m_end)
        w_c = jnp.exp(f_end + m_prev - m_end)
        kw = w_k * kf
        c_ref[h] = w_c * c_old + _dot_tn(kw.astype(BF16), vc)
        n_ref[h] = w_c * n_old + jnp.sum(kw, axis=0, keepdims=True)
        m_ref[:, h:h + 1] = m_end

    for h in range(HEADS):
        cols = slice(h * DV, (h + 1) * DV)
        y = (_rms(og_ref[:, cols], mnorm_ref[:, cols])
             * _sigmoid(pm_ref[:, C_MO + h * DV:C_MO + (h + 1) * DV]))
        br_out_ref[:, BRANCH_W + h * DV:BRANCH_W + (h + 1) * DV] = y.astype(BF16)

    x = pm_ref[:, C_RX:C_RX + RG_WIDTH]
    xpad_ref[pl.ds(8, tt), :] = x
    xc = (convb_ref[...] + convw_ref[3:4, :] * x
          + convw_ref[2:3, :] * xpad_ref[pl.ds(7, tt), :]
          + convw_ref[1:2, :] * xpad_ref[pl.ds(6, tt), :]
          + convw_ref[0:1, :] * xpad_ref[pl.ds(5, tt), :])
    buf_ref[...] = xpad_ref[pl.ds(tt + 8 - (RG_CONV - 1), RG_CONV - 1), :]
    xpad_ref[pl.ds(0, 8), :] = xpad_ref[pl.ds(tt, 8), :]
    a, b = _rg_gates(xc, wr_ref, br_ref, wi_ref, bi_ref, lam_ref)
    ridx = lax.broadcasted_iota(jnp.int32, (tt, RG_WIDTH), 0)
    sh = 1
    while sh < tt:
        keep = ridx >= sh
        a_sh = jnp.where(keep, pltpu.roll(a, sh, 0), 1.0)
        b_sh = jnp.where(keep, pltpu.roll(b, sh, 0), 0.0)
        b = a * b_sh + b
        a = a * a_sh
        sh *= 2
    hseq = a * h_ref[...] + b
    h_ref[...] = hseq[tt - 1:tt, :]
    y = hseq * _gelu_tanh(pm_ref[:, C_RG:C_RG + RG_WIDTH])
    br_out_ref[:, 2 * BRANCH_W:3 * BRANCH_W] = y.astype(BF16)

    for b in range(N_BRANCH):
        gcols = slice(b * D_MODEL, (b + 1) * D_MODEL)
        sg_ref[:, gcols] = _sigmoid(pm_ref[:, MIX_COLS + b * D_MODEL:MIX_COLS + (b + 1) * D_MODEL])


def _merge_stage(br_ref, sg_ref, xr_ref, wb_ref, wo_ref, xo_ref, mg_ref):
    n_tiles = D_MODEL // MERGE_TILE
    for jt in range(n_tiles):
        cols = slice(jt * MERGE_TILE, (jt + 1) * MERGE_TILE)
        merged = None
        for b in range(N_BRANCH):
            z = _dot(br_ref[:, b * BRANCH_W:(b + 1) * BRANCH_W], wb_ref[b, :, cols])
            term = sg_ref[:, b * D_MODEL + jt * MERGE_TILE:b * D_MODEL + (jt + 1) * MERGE_TILE] * z
            merged = term if merged is None else merged + term
        mg_ref[:, cols] = merged.astype(BF16)
    for jt in range(n_tiles):
        cols = slice(jt * MERGE_TILE, (jt + 1) * MERGE_TILE)
        xo_ref[:, cols] = xr_ref[:, cols] + _dot(mg_ref[...], wo_ref[:, cols])


N_MIXER_WEIGHTS = 12


def _mix_prompt_kernel(x_ref, g_ref, wmain_ref, wsmall_ref, *refs):
    mixer_w = refs[:N_MIXER_WEIGHTS]
    wb_ref, wo_ref = refs[N_MIXER_WEIGHTS:N_MIXER_WEIGHTS + 2]
    xo_ref, s_ref, c_ref, n_ref, m_ref, h_ref, buf_ref = refs[N_MIXER_WEIGHTS + 2:N_MIXER_WEIGHTS + 9]
    pm_ref, ps_ref, xs_ref, br_ref, sg_ref, mg_ref, og_ref, xpad_ref = refs[N_MIXER_WEIGHTS + 9:]

    @pl.when(pl.program_id(1) == 0)
    def _():
        for ref in (s_ref, c_ref, n_ref, m_ref, h_ref):
            ref[...] = jnp.zeros_like(ref)
        xpad_ref[pl.ds(0, 8), :] = jnp.zeros((8, RG_WIDTH), F32)

    _project_stage(x_ref, g_ref, wmain_ref, wsmall_ref, pm_ref, ps_ref, xs_ref)
    _mixer_stage(pm_ref, ps_ref, *mixer_w, s_ref, c_ref, n_ref, m_ref, h_ref, buf_ref,
                 br_ref, sg_ref, og_ref, xpad_ref)
    _merge_stage(br_ref, sg_ref, xs_ref, wb_ref, wo_ref, xo_ref, mg_ref)


def _mix_prompt(x, g, lw, *, nb, t, tt):
    d = x.shape[1]
    const = lambda shape: pl.BlockSpec(shape, lambda b, j: (0,) * len(shape),
                                       pipeline_mode=pl.Buffered(1))
    nt = t // tt
    return pl.pallas_call(
        _mix_prompt_kernel,
        grid=(nb, nt),
        in_specs=[pl.BlockSpec((tt, d), lambda b, j: (b * nt + j, 0)),
                  const((1, d)), const((d, MAIN_COLS)), const((d, SMALL_COLS)),
                  const((SMALL_COLS, HEADS * DK)), const((1, HEADS * DK)), const((1, DV)),
                  const((1, SMALL_COLS)), const((1, BRANCH_W)),
                  const((RG_CONV, RG_WIDTH)), const((1, RG_WIDTH)),
                  const((RG_WIDTH, RG_WIDTH)), const((1, RG_WIDTH)),
                  const((RG_WIDTH, RG_WIDTH)), const((1, RG_WIDTH)), const((1, RG_WIDTH)),
                  const((N_BRANCH, BRANCH_W, d)), const((d, d))],
        out_specs=[pl.BlockSpec((tt, d), lambda b, j: (b * nt + j, 0)),
                   pl.BlockSpec((None, HEADS, DK, DV), lambda b, j: (b, 0, 0, 0)),
                   pl.BlockSpec((None, HEADS, DK, DV), lambda b, j: (b, 0, 0, 0)),
                   pl.BlockSpec((None, HEADS, 1, DK), lambda b, j: (b, 0, 0, 0)),
                   pl.BlockSpec((None, 1, HEADS), lambda b, j: (b, 0, 0)),
                   pl.BlockSpec((None, 1, RG_WIDTH), lambda b, j: (b, 0, 0)),
                   pl.BlockSpec((None, RG_CONV - 1, RG_WIDTH), lambda b, j: (b, 0, 0))],
        out_shape=[jax.ShapeDtypeStruct((nb * t, d), F32),
                   jax.ShapeDtypeStruct((nb, HEADS, DK, DV), F32),
                   jax.ShapeDtypeStruct((nb, HEADS, DK, DV), F32),
                   jax.ShapeDtypeStruct((nb, HEADS, 1, DK), F32),
                   jax.ShapeDtypeStruct((nb, 1, HEADS), F32),
                   jax.ShapeDtypeStruct((nb, 1, RG_WIDTH), F32),
                   jax.ShapeDtypeStruct((nb, RG_CONV - 1, RG_WIDTH), F32)],
        scratch_shapes=[pltpu.VMEM((tt, MAIN_COLS), F32),
                        pltpu.VMEM((tt, SMALL_COLS), F32),
                        pltpu.VMEM((tt, d), F32),
                        pltpu.VMEM((tt, N_BRANCH * BRANCH_W), BF16),
                        pltpu.VMEM((tt, GATE_COLS), F32),
                        pltpu.VMEM((tt, d), BF16),
                        pltpu.VMEM((tt, BRANCH_W), F32),
                        pltpu.VMEM((tt + 8, RG_WIDTH), F32)],
        compiler_params=pltpu.CompilerParams(
            dimension_semantics=("parallel", "arbitrary"), vmem_limit_bytes=VMEM_LIMIT),
        name="mix_prompt",
    )(x, g.reshape(1, d), lw["w_main"], lw["w_small"],
      lw["wa2"], lw["ba"], lw["gnorm"], lw["bif"], lw["mnorm"], lw["convw"], lw["convb"],
      lw["wr"], lw["br"], lw["wi"], lw["bi"], lw["lam"], lw["wb"], lw["wo"])


def _mix_sample_kernel(pm_ref, ps_ref, s0_ref, c0_ref, n0_ref, m0_ref, h0_ref, buf0_ref,
                       wa2_ref, ba_ref, gnorm_ref, bif_ref, mnorm_ref,
                       convw_ref, convb_ref, wr_ref, br_ref, wi_ref, bi_ref, lam_ref,
                       br_out_ref, s_ref, c_ref, n_ref, m_ref, h_ref, buf_ref,
                       og_ref, om_ref):
    bs = pm_ref.shape[0]
    ps = ps_ref[...]
    a_pre = _dot(ps.astype(BF16), wa2_ref[...]) + ba_ref[...]
    a = jnp.exp(_log_sigmoid(a_pre) * (1.0 / GLA_GATE_NORM))
    a_hi = a.astype(BF16)
    a_r1 = a - a_hi.astype(F32)
    a_mid = a_r1.astype(BF16)
    a_lo = (a_r1 - a_mid.astype(F32)).astype(BF16)
    mq = pm_ref[:, C_MQ:C_MQ + HEADS * DK] * (DK ** -0.5)
    mk = pm_ref[:, C_MK:C_MK + HEADS * DK]
    kd = HEADS * DK
    cols_t = jnp.concatenate(
        [(pm_ref[:, C_GQ:C_GQ + kd] * (DK ** -0.5)).T, pm_ref[:, C_GK:C_GK + kd].T,
         mq.T, mk.T, a_hi.astype(F32).T, a_mid.astype(F32).T, a_lo.astype(F32).T],
        axis=0).astype(BF16)
    o_gq, o_gk, o_mq, o_mk, o_a0, o_a1, o_a2 = (i * kd for i in range(7))
    sample_id = lax.broadcasted_iota(jnp.int32, (bs, DV), 0)
    pre = ps + bif_ref[...]
    lf = _log_sigmoid(pre)
    m0_all = m0_ref[...]
    head_lane = lax.broadcasted_iota(jnp.int32, (1, HEADS), 1)

    for s in range(bs):
        row = slice(s, s + 1)
        bc = _dot(cols_t, jnp.where(sample_id == s, 1.0, 0.0).astype(BF16))
        m_row = m0_all[row, :]
        for h in range(HEADS):
            kcols = slice(h * DK, (h + 1) * DK)
            col = lambda off: bc[off + h * DK:off + (h + 1) * DK, :]
            v_row = pm_ref[row, C_GV + h * DV:C_GV + (h + 1) * DV]
            a_col = col(o_a0) + col(o_a1) + col(o_a2)
            s_new = a_col * s0_ref[s, h] + col(o_gk) * v_row
            s_ref[s, h] = s_new
            og_ref[row, h * DV:(h + 1) * DV] = jnp.sum(col(o_gq) * s_new,
                                                       axis=0, keepdims=True)
            ic = pre[row, L_MI + h:L_MI + h + 1]
            fc = lf[row, L_MF + h:L_MF + h + 1]
            m0 = m0_all[row, h:h + 1]
            inter = fc + m0
            m_t = jnp.maximum(inter, ic)
            w_intra = jnp.exp(ic - m_t)
            w_inter = jnp.exp(inter - m_t)
            q_row = mq[row, kcols]
            k_row = mk[row, kcols]
            v_row = pm_ref[row, C_MV + h * DV:C_MV + (h + 1) * DV]
            sc = jnp.sum(q_row * k_row, axis=1, keepdims=True) * w_intra
            c_old = c0_ref[s, h]
            n_old = n0_ref[s, h]
            qc = jnp.sum(col(o_mq) * c_old, axis=0, keepdims=True)
            num = w_inter * qc + sc * v_row
            den = w_inter * jnp.sum(q_row * n_old, axis=1, keepdims=True) + sc
            om_ref[row, h * DV:(h + 1) * DV] = num / jnp.maximum(jnp.abs(den), jnp.exp(-m_t))
            c_ref[s, h] = w_inter * c_old + (w_intra * col(o_mk)) * v_row
            n_ref[s, h] = w_inter * n_old + w_intra * k_row
            m_row = jnp.where(head_lane == h, m_t, m_row)
        m_ref[row, :] = m_row

    gnorm = gnorm_ref[...]
    for h in range(HEADS):
        cols = slice(h * DV, (h + 1) * DV)
        y = _rms(og_ref[:, cols], gnorm) * _silu(pm_ref[:, C_GG + h * DV:C_GG + (h + 1) * DV])
        br_out_ref[:, cols] = y.astype(BF16)
        y = (_rms(om_ref[:, cols], mnorm_ref[:, cols])
             * _sigmoid(pm_ref[:, C_MO + h * DV:C_MO + (h + 1) * DV]))
        br_out_ref[:, BRANCH_W + h * DV:BRANCH_W + (h + 1) * DV] = y.astype(BF16)

    x = pm_ref[:, C_RX:C_RX + RG_WIDTH]
    xc = convb_ref[...] + convw_ref[3:4, :] * x
    for jj in range(RG_CONV - 1):
        xc = xc + convw_ref[jj:jj + 1, :] * buf0_ref[:, jj * RG_WIDTH:(jj + 1) * RG_WIDTH]
    buf_ref[:, 0:2 * RG_WIDTH] = buf0_ref[:, RG_WIDTH:3 * RG_WIDTH]
    buf_ref[:, 2 * RG_WIDTH:3 * RG_WIDTH] = x
    a, b = _rg_gates(xc, wr_ref, br_ref, wi_ref, bi_ref, lam_ref)
    hnew = a * h0_ref[...] + b
    h_ref[...] = hnew
    y = hnew * _gelu_tanh(pm_ref[:, C_RG:C_RG + RG_WIDTH])
    br_out_ref[:, 2 * BRANCH_W:3 * BRANCH_W] = y.astype(BF16)


def _mix_sample(pm, ps, st, layer, lw, *, bs):
    n = pm.shape[0]
    const = lambda shape: pl.BlockSpec(shape, lambda i: (0,) * len(shape))
    blk = lambda shape: pl.BlockSpec(shape, lambda i: (i,) + (0,) * (len(shape) - 1))
    lblk = lambda shape: pl.BlockSpec((None,) + shape,
                                      lambda i: (layer, i) + (0,) * (len(shape) - 1))
    s0, c0, n0, m0, h0, buf0 = st
    shapes = [(bs, HEADS, DK, DV), (bs, HEADS, DK, DV), (bs, HEADS, 1, DK),
              (bs, HEADS), (bs, RG_WIDTH), (bs, (RG_CONV - 1) * RG_WIDTH)]
    return pl.pallas_call(
        _mix_sample_kernel,
        grid=(n // bs,),
        in_specs=[blk((bs, MIX_COLS)), blk((bs, SMALL_COLS))] + [lblk(sh) for sh in shapes] + [
            const((SMALL_COLS, HEADS * DK)), const((1, HEADS * DK)), const((1, DV)),
            const((1, SMALL_COLS)), const((1, BRANCH_W)),
            const((RG_CONV, RG_WIDTH)), const((1, RG_WIDTH)),
            const((RG_WIDTH, RG_WIDTH)), const((1, RG_WIDTH)),
            const((RG_WIDTH, RG_WIDTH)), const((1, RG_WIDTH)), const((1, RG_WIDTH))],
        out_specs=[blk((bs, N_BRANCH * BRANCH_W))] + [blk(sh) for sh in shapes],
        out_shape=[jax.ShapeDtypeStruct((n, N_BRANCH * BRANCH_W), BF16)]
        + [jax.ShapeDtypeStruct(a.shape[1:], F32) for a in st],
        scratch_shapes=[pltpu.VMEM((bs, BRANCH_W), F32),
                        pltpu.VMEM((bs, BRANCH_W), F32)],
        compiler_params=pltpu.CompilerParams(
            dimension_semantics=("parallel",), vmem_limit_bytes=VMEM_LIMIT),
        name="mix_sample",
    )(pm, ps, s0, c0, n0, m0, h0, buf0,
      lw["wa2"], lw["ba"], lw["gnorm"], lw["bif"], lw["mnorm"], lw["convw"], lw["convb"],
      lw["wr"], lw["br"], lw["wi"], lw["bi"], lw["lam"])


def _merge_kernel(x_ref, br_ref, g0_ref, g1_ref, g2_ref, wb_ref, wo_ref, o_ref):
    merged = None
    for b, g_ref in enumerate((g0_ref, g1_ref, g2_ref)):
        z = _dot(br_ref[:, b * BRANCH_W:(b + 1) * BRANCH_W], wb_ref[b])
        term = _sigmoid(g_ref[...]) * z
        merged = term if merged is None else merged + term
    o_ref[...] = x_ref[...] + _dot(merged.astype(BF16), wo_ref[...])


def _merge(x, br, pm, wb, wo, *, tm):
    n, d = x.shape
    gate_blk0 = MIX_COLS // d
    gate_spec = lambda b: pl.BlockSpec((tm, d), lambda i: (i, gate_blk0 + b))
    return pl.pallas_call(
        _merge_kernel,
        grid=(n // tm,),
        in_specs=[pl.BlockSpec((tm, d), lambda i: (i, 0)),
                  pl.BlockSpec((tm, N_BRANCH * BRANCH_W), lambda i: (i, 0)),
                  gate_spec(0), gate_spec(1), gate_spec(2),
                  pl.BlockSpec((N_BRANCH, BRANCH_W, d), lambda i: (0, 0, 0)),
                  pl.BlockSpec((d, d), lambda i: (0, 0))],
        out_specs=pl.BlockSpec((tm, d), lambda i: (i, 0)),
        out_shape=jax.ShapeDtypeStruct((n, d), F32),
        compiler_params=pltpu.CompilerParams(
            dimension_semantics=("parallel",), vmem_limit_bytes=VMEM_LIMIT),
        name="merge",
    )(x, br, pm, pm, pm, wb, wo)


def _xattn_prompt_kernel(x_ref, g_ref, wq_ref, wo_ref, k_ref, v_ref, o_ref, kb_ref, vb_ref):
    @pl.when(pl.program_id(1) == 0)
    def _():
        kb_ref[...] = k_ref[...].astype(BF16)
        vb_ref[...] = v_ref[...].astype(BF16)

    x = x_ref[...]
    q = _dot(_rms(x, g_ref[...]).astype(BF16), wq_ref[...]).astype(BF16)
    heads = []
    for h in range(XA_HEADS):
        cols = slice(h * XA_HEAD_DIM, (h + 1) * XA_HEAD_DIM)
        s = _dot_nt(q[:, cols], kb_ref[:, cols]) * (XA_HEAD_DIM ** -0.5)
        p = jnp.exp(s - jnp.max(s, axis=-1, keepdims=True))
        p = p / jnp.sum(p, axis=-1, keepdims=True)
        heads.append(_dot(p.astype(BF16), vb_ref[:, cols]))
    o = jnp.concatenate(heads, axis=1).astype(BF16)
    o_ref[...] = x + _dot(o, wo_ref[...])


def _xattn_prompt(x, g, wq, wo, mem_k, mem_v, *, nb, t, tm):
    d = x.shape[1]
    nt = t // tm
    return pl.pallas_call(
        _xattn_prompt_kernel,
        grid=(nb, nt),
        in_specs=[pl.BlockSpec((tm, d), lambda b, j: (b * nt + j, 0)),
                  pl.BlockSpec((1, d), lambda b, j: (0, 0)),
                  pl.BlockSpec((d, d), lambda b, j: (0, 0)),
                  pl.BlockSpec((d, d), lambda b, j: (0, 0)),
                  pl.BlockSpec((None, N_MEM, d), lambda b, j: (b, 0, 0)),
                  pl.BlockSpec((None, N_MEM, d), lambda b, j: (b, 0, 0))],
        out_specs=pl.BlockSpec((tm, d), lambda b, j: (b * nt + j, 0)),
        out_shape=jax.ShapeDtypeStruct(x.shape, F32),
        scratch_shapes=[pltpu.VMEM((N_MEM, d), BF16), pltpu.VMEM((N_MEM, d), BF16)],
        compiler_params=pltpu.CompilerParams(
            dimension_semantics=("parallel", "arbitrary"), vmem_limit_bytes=VMEM_LIMIT),
        name="xattn_prompt",
    )(x, g.reshape(1, d), wq, wo, mem_k, mem_v)


def _xattn_sample_kernel(q_ref, k_ref, v_ref, o_ref):
    bs = q_ref.shape[0]
    for s in range(bs):
        q = q_ref[s]
        sc = jnp.sum(k_ref[s] * q[None], axis=-1, keepdims=True) * (XA_HEAD_DIM ** -0.5)
        p = jnp.exp(sc - jnp.max(sc, axis=0, keepdims=True))
        p = p / jnp.sum(p, axis=0, keepdims=True)
        o_ref[s] = jnp.sum(p * v_ref[s], axis=0)


def _xattn_sample(q, cache_k, cache_v, layer, *, bs):
    n = q.shape[0]
    blk = (None, bs, N_MEM, XA_HEADS, XA_HEAD_DIM)
    return pl.pallas_call(
        _xattn_sample_kernel,
        grid=(n // bs,),
        in_specs=[pl.BlockSpec((bs, XA_HEADS, XA_HEAD_DIM), lambda i: (i, 0, 0)),
                  pl.BlockSpec(blk, lambda i: (layer, i, 0, 0, 0)),
                  pl.BlockSpec(blk, lambda i: (layer, i, 0, 0, 0))],
        out_specs=pl.BlockSpec((bs, XA_HEADS, XA_HEAD_DIM), lambda i: (i, 0, 0)),
        out_shape=jax.ShapeDtypeStruct((n, XA_HEADS, XA_HEAD_DIM), F32),
        compiler_params=pltpu.CompilerParams(
            dimension_semantics=("parallel",), vmem_limit_bytes=VMEM_LIMIT),
        name="xattn_sample",
    )(q, cache_k, cache_v)


def _matmul_residual_kernel(x_ref, a_ref, w_ref, o_ref):
    o_ref[...] = x_ref[...] + _dot(a_ref[...].astype(BF16), w_ref[...])


def _matmul_residual(x, a, w):
    n, d = x.shape
    return pl.pallas_call(
        _matmul_residual_kernel,
        out_shape=jax.ShapeDtypeStruct((n, d), F32),
        compiler_params=pltpu.CompilerParams(vmem_limit_bytes=VMEM_LIMIT),
        name="matmul_residual",
    )(x, a, w)


def _ffn_kernel(x_ref, g_ref, wg_ref, wu_ref, wd_ref, gf_ref, o_ref, hid_ref, *, final_norm):
    x = x_ref[...]
    u = _rms(x, g_ref[...]).astype(BF16)
    for jt in range(D_FF // FFN_TILE):
        cols = slice(jt * FFN_TILE, (jt + 1) * FFN_TILE)
        hid = _silu(_dot(u, wg_ref[:, cols])) * _dot(u, wu_ref[:, cols])
        hid_ref[:, cols] = hid.astype(BF16)
    y = x + _dot(hid_ref[...], wd_ref[...])
    if final_norm:
        y = _rms(y, gf_ref[...])
    o_ref[...] = y


def _ffn(x, g, wg, wu, wd, gf, *, tm, final_norm):
    n, d = x.shape
    f = wg.shape[1]
    const = lambda shape: pl.BlockSpec(shape, lambda i: (0,) * len(shape),
                                       pipeline_mode=pl.Buffered(1))
    return pl.pallas_call(
        functools.partial(_ffn_kernel, final_norm=final_norm),
        grid=(n // tm,),
        in_specs=[pl.BlockSpec((tm, d), lambda i: (i, 0)),
                  const((1, d)), const((d, f)), const((d, f)), const((f, d)), const((1, d))],
        out_specs=pl.BlockSpec((tm, d), lambda i: (i, 0)),
        out_shape=jax.ShapeDtypeStruct((n, d), F32),
        scratch_shapes=[pltpu.VMEM((tm, f), BF16)],
        compiler_params=pltpu.CompilerParams(
            dimension_semantics=("parallel",), vmem_limit_bytes=VMEM_LIMIT),
        name="ffn",
    )(x, g.reshape(1, d), wg, wu, wd, gf.reshape(1, d))


def _split_w_in(w_in):
    widths = (HEADS * DK, HEADS * DK, HEADS * DV, HEADS * DV, GLA_RANK,
              HEADS * DK, HEADS * DK, HEADS * DV, HEADS * DV, HEADS, HEADS,
              RG_WIDTH, RG_WIDTH, GATE_COLS)
    parts, acc = [], 0
    for w in widths:
        parts.append(w_in[:, acc:acc + w])
        acc += w
    return parts


def _block_diag(w):
    eye = jnp.eye(RG_BLOCKS, dtype=w.dtype)
    return jnp.einsum("nde,nm->ndme", w, eye).reshape(RG_WIDTH, RG_WIDTH)


def _layer_weights(l, p):
    (g_q, g_k, g_v, g_g, g_a, m_q, m_k, m_v, m_o, m_i, m_f, r_x, r_g, gates) = _split_w_in(p["w_in"][l])
    d = D_MODEL
    w_main = jnp.concatenate([g_q, g_k, g_v, g_g, m_q, m_k, m_v, m_o, r_x, r_g, gates], axis=1)
    w_small = jnp.concatenate(
        [g_a, m_i, m_f, jnp.zeros((d, SMALL_COLS - GLA_RANK - 2 * HEADS), F32)], axis=1)
    wa2 = jnp.concatenate(
        [p["gla_w_a2"][l], jnp.zeros((SMALL_COLS - GLA_RANK, HEADS * DK), F32)], axis=0)
    bif = jnp.concatenate(
        [jnp.zeros((L_MI,), F32), p["mlstm_b_i"][l], p["mlstm_b_f"][l],
         jnp.zeros((SMALL_COLS - L_MF - HEADS,), F32)]).reshape(1, SMALL_COLS)
    return {
        "w_main": w_main.astype(BF16), "w_small": w_small.astype(BF16),
        "wa2": wa2.astype(BF16), "ba": p["gla_b_a"][l].reshape(1, -1),
        "gnorm": p["gla_norm"][l].reshape(1, DV), "bif": bif,
        "mnorm": p["mlstm_norm"][l].reshape(1, BRANCH_W),
        "convw": p["rg_conv_w"][l], "convb": p["rg_conv_b"][l].reshape(1, -1),
        "wr": _block_diag(p["rg_w_r"][l]).astype(BF16), "br": p["rg_b_r"][l].reshape(1, -1),
        "wi": _block_diag(p["rg_w_i"][l]).astype(BF16), "bi": p["rg_b_i"][l].reshape(1, -1),
        "lam": p["rg_lambda"][l].reshape(1, -1),
        "wb": p["w_branch"][l].astype(BF16), "wo": p["w_out"][l].astype(BF16),
        "xa_wq": p["xa_wq"][l].astype(BF16), "xa_wo": p["xa_wo"][l].astype(BF16),
        "xa_wkv": jnp.concatenate([p["xa_wk"][l], p["xa_wv"][l]], axis=1).astype(BF16),
        "wg": p["ffn_w_gate"][l].astype(BF16), "wu": p["ffn_w_up"][l].astype(BF16),
        "wd": p["ffn_w_down"][l].astype(BF16),
    }


def kernel(x_prompt, x_sample, mem_prompt, cache_mem_k, cache_mem_v, state_gla, state_mlstm_c, state_mlstm_n, state_mlstm_m, state_rglru_h, state_rglru_conv, norm_mix, w_in, gla_w_a2, gla_b_a, gla_norm, mlstm_b_i, mlstm_b_f, mlstm_norm, rg_conv_w, rg_conv_b, rg_w_r, rg_b_r, rg_w_i, rg_b_i, rg_lambda, w_branch, w_out, norm_xa, norm_mem, xa_wq, xa_wk, xa_wv, xa_wo, norm_ffn, ffn_w_gate, ffn_w_up, ffn_w_down, norm_final):
    p = {"w_in": w_in, "gla_w_a2": gla_w_a2, "gla_b_a": gla_b_a, "gla_norm": gla_norm,
         "mlstm_b_i": mlstm_b_i, "mlstm_b_f": mlstm_b_f, "mlstm_norm": mlstm_norm,
         "rg_conv_w": rg_conv_w, "rg_conv_b": rg_conv_b, "rg_w_r": rg_w_r, "rg_b_r": rg_b_r,
         "rg_w_i": rg_w_i, "rg_b_i": rg_b_i, "rg_lambda": rg_lambda, "w_branch": w_branch,
         "w_out": w_out, "xa_wq": xa_wq, "xa_wk": xa_wk, "xa_wv": xa_wv, "xa_wo": xa_wo,
         "ffn_w_gate": ffn_w_gate, "ffn_w_up": ffn_w_up, "ffn_w_down": ffn_w_down}
    nb, t, d = x_prompt.shape
    ns = x_sample.shape[0]
    xp = x_prompt.reshape(nb * t, d)
    xs = x_sample.reshape(ns, d)
    mem = mem_prompt.reshape(nb * N_MEM, d)

    st_sample = (state_gla, state_mlstm_c, state_mlstm_n.reshape(DEPTH, ns, HEADS, 1, DK),
                 state_mlstm_m, state_rglru_h,
                 state_rglru_conv.reshape(DEPTH, ns, (RG_CONV - 1) * RG_WIDTH))

    new_p = [[] for _ in range(8)]
    new_s = [[] for _ in range(6)]
    for l in range(DEPTH):
        lw = _layer_weights(l, p)
        last = l == DEPTH - 1

        kv = _norm_matmul(mem, norm_mem[l], lw["xa_wkv"], tm=512, tn=1024)
        k_p = kv[:, :d].reshape(nb, N_MEM, d)
        v_p = kv[:, d:].reshape(nb, N_MEM, d)

        xp, g_s, c_s, n_s, m_s, h_s, buf_s = _mix_prompt(xp, norm_mix[l], lw, nb=nb, t=t, tt=256)
        xp = _xattn_prompt(xp, norm_xa[l], lw["xa_wq"], lw["xa_wo"], k_p, v_p, nb=nb, t=t, tm=512)
        xp = _ffn(xp, norm_ffn[l], lw["wg"], lw["wu"], lw["wd"], norm_final,
                  tm=512, final_norm=last)
        for lst, val in zip(new_p, (k_p.reshape(nb, N_MEM, XA_HEADS, XA_HEAD_DIM),
                                    v_p.reshape(nb, N_MEM, XA_HEADS, XA_HEAD_DIM),
                                    g_s, c_s, n_s.reshape(nb, HEADS, DK), m_s.reshape(nb, HEADS),
                                    h_s.reshape(nb, RG_WIDTH), buf_s)):
            lst.append(val)

        pm, ps = _in_proj(xs, norm_mix[l], lw["w_main"], lw["w_small"], tm=ns, tn=1024)
        br, g_s, c_s, n_s, m_s, h_s, buf_s = _mix_sample(pm, ps, st_sample, l, lw, bs=8)
        xs = _merge(xs, br, pm, lw["wb"], lw["wo"], tm=ns)
        q = _norm_matmul(xs, norm_xa[l], lw["xa_wq"], tm=ns, tn=d)
        att = _xattn_sample(q.reshape(ns, XA_HEADS, XA_HEAD_DIM), cache_mem_k, cache_mem_v, l, bs=4)
        xs = _matmul_residual(xs, att.reshape(ns, d), lw["xa_wo"])
        xs = _ffn(xs, norm_ffn[l], lw["wg"], lw["wu"], lw["wd"], norm_final,
                  tm=ns, final_norm=last)
        for lst, val in zip(new_s, (g_s, c_s, n_s.reshape(ns, HEADS, DK), m_s, h_s,
                                    buf_s.reshape(ns, RG_CONV - 1, RG_WIDTH))):
            lst.append(val)

    y_prompt = xp.reshape(nb, t, d)
    y_sample = xs.reshape(ns, 1, d)
    outs_p = [jnp.stack(v, axis=0) for v in new_p]
    outs_s = [jnp.stack(v, axis=0) for v in new_s]
    return (y_prompt, y_sample, *outs_p, *outs_s)
```

```python
import functools
import math

import jax
import jax.numpy as jnp
from jax import lax
from jax.experimental import pallas as pl
from jax.experimental.pallas import tpu as pltpu

F32 = jnp.float32
BF16 = jnp.bfloat16

D_MODEL = 1024
DEPTH = 2
EPS = 1e-6
N_MEM = 256
XA_HEADS = 4
XA_HEAD_DIM = D_MODEL // XA_HEADS
N_BRANCH = 3
BRANCH_W = D_MODEL // 2
HEADS = 4
DK = 64
DV = 128
GLA_RANK = 16
GLA_GATE_NORM = 16.0
GLA_CHUNK = 32
MLSTM_CHUNK = 64
RG_WIDTH = BRANCH_W
RG_BLOCKS = 8
RG_BLOCK = RG_WIDTH // RG_BLOCKS
RG_CONV = 4
RG_C = 8.0
D_FF = 2816

C_GQ, C_GK, C_GV, C_GG = 0, 256, 512, 1024
C_MQ, C_MK, C_MV, C_MO = 1536, 1792, 2048, 2560
C_RX, C_RG = 3072, 3584
MIX_COLS = 4096
GATE_COLS = N_BRANCH * D_MODEL
MAIN_COLS = MIX_COLS + GATE_COLS
SMALL_COLS = 128
L_GA = 0
L_MI = 16
L_MF = 20

PROJ_TILE = 256
MERGE_TILE = 256
FFN_TILE = 256
VMEM_LIMIT = 48 * 1024 * 1024


def _softplus(x):
    return jnp.maximum(x, 0.0) + jnp.log1p(jnp.exp(-jnp.abs(x)))


def _log_sigmoid(x):
    return -_softplus(-x)


def _sigmoid(x):
    return 1.0 / (1.0 + jnp.exp(-x))


def _silu(x):
    return x * _sigmoid(x)


def _gelu_tanh(x):
    c = math.sqrt(2.0 / math.pi)
    return x * (0.5 * (1.0 + jnp.tanh(c * (x + 0.044715 * (x * x * x)))))


def _neg_expm1(x):
    return -jnp.tanh(0.5 * x) * (jnp.exp(x) + 1.0)


def _rms(x, g):
    ms = jnp.mean(x * x, axis=-1, keepdims=True)
    return x * lax.rsqrt(ms + EPS) * g


def _dot(a, b):
    return jnp.dot(a, b, preferred_element_type=F32)


def _dot_nt(a, b):
    return lax.dot_general(a, b, (((1,), (1,)), ((), ())), preferred_element_type=F32)


def _dot_tn(a, b):
    return lax.dot_general(a, b, (((0,), (0,)), ((), ())), preferred_element_type=F32)


def _dot_mask_exact(mask, x):
    hi = x.astype(BF16)
    rest = x - hi.astype(F32)
    mid = rest.astype(BF16)
    lo = (rest - mid.astype(F32)).astype(BF16)
    return _dot(mask, hi) + _dot(mask, mid) + _dot(mask, lo)


def _chunk_masks(n, chunk):
    shift = chunk.bit_length() - 1
    r = lax.broadcasted_iota(jnp.int32, (n, n), 0)
    c = lax.broadcasted_iota(jnp.int32, (n, n), 1)
    same = lax.shift_right_logical(r, shift) == lax.shift_right_logical(c, shift)
    tril = jnp.where(same & (c <= r), 1.0, 0.0).astype(F32)
    full = jnp.where(same, 1.0, 0.0).astype(F32)
    return tril, full


def _norm_matmul_kernel(x_ref, g_ref, w_ref, o_ref, u_ref):
    @pl.when(pl.program_id(1) == 0)
    def _():
        u_ref[...] = _rms(x_ref[...], g_ref[...]).astype(BF16)

    o_ref[...] = _dot(u_ref[...], w_ref[...]).astype(o_ref.dtype)


def _norm_matmul(x, g, w, layer, *, tm, tn):
    n, d = x.shape
    nout = w.shape[2]
    return pl.pallas_call(
        _norm_matmul_kernel,
        grid=(n // tm, nout // tn),
        in_specs=[pl.BlockSpec((tm, d), lambda i, j: (i, 0)),
                  pl.BlockSpec((1, d), lambda i, j: (0, 0)),
                  pl.BlockSpec((None, d, tn), lambda i, j: (layer, 0, j))],
        out_specs=pl.BlockSpec((None, tm, tn), lambda i, j: (j, i, 0)),
        out_shape=jax.ShapeDtypeStruct((nout // tn, n, tn), F32),
        scratch_shapes=[pltpu.VMEM((tm, d), BF16)],
        compiler_params=pltpu.CompilerParams(
            dimension_semantics=("parallel", "arbitrary"), vmem_limit_bytes=VMEM_LIMIT),
        name="norm_matmul",
    )(x, g.reshape(1, d), w)


def _in_proj_kernel(x_ref, g_ref, w_ref, ws_ref, o_ref, os_ref, u_ref):
    @pl.when(pl.program_id(1) == 0)
    def _():
        u = _rms(x_ref[...], g_ref[...]).astype(BF16)
        u_ref[...] = u
        os_ref[...] = _dot(u, ws_ref[...])

    o_ref[...] = _dot(u_ref[...], w_ref[...])


def _in_proj(x, g, w_main, w_small, layer, *, tm, tn):
    n, d = x.shape
    return pl.pallas_call(
        _in_proj_kernel,
        grid=(n // tm, MAIN_COLS // tn),
        in_specs=[pl.BlockSpec((tm, d), lambda i, j: (i, 0)),
                  pl.BlockSpec((1, d), lambda i, j: (0, 0)),
                  pl.BlockSpec((None, d, tn), lambda i, j: (layer, 0, j)),
                  pl.BlockSpec((None, d, SMALL_COLS), lambda i, j: (layer, 0, 0))],
        out_specs=[pl.BlockSpec((tm, tn), lambda i, j: (i, j)),
                   pl.BlockSpec((tm, SMALL_COLS), lambda i, j: (i, 0))],
        out_shape=[jax.ShapeDtypeStruct((n, MAIN_COLS), F32),
                   jax.ShapeDtypeStruct((n, SMALL_COLS), F32)],
        scratch_shapes=[pltpu.VMEM((tm, d), BF16)],
        compiler_params=pltpu.CompilerParams(
            dimension_semantics=("parallel", "arbitrary"), vmem_limit_bytes=VMEM_LIMIT),
        name="in_proj",
    )(x, g.reshape(1, d), w_main, w_small)


def _rg_gates(xc, wr_ref, br_ref, wi_ref, bi_ref, lam_ref):
    xcb = xc.astype(BF16)
    r = _sigmoid(_dot(xcb, wr_ref[...]) + br_ref[...])
    i = _sigmoid(_dot(xcb, wi_ref[...]) + bi_ref[...])
    log_a = (-RG_C * r) * _softplus(-lam_ref[...])
    a = jnp.exp(log_a)
    b = jnp.sqrt(_neg_expm1(2.0 * log_a)) * (i * xc)
    return a, b


def _project_stage(x_ref, g_ref, wmain_ref, wsmall_ref, pm_ref, ps_ref, xs_ref):
    x = x_ref[...]
    xs_ref[...] = x
    u = _rms(x, g_ref[...]).astype(BF16)
    ps_ref[...] = _dot(u, wsmall_ref[...])
    for cb in range(MAIN_COLS // PROJ_TILE):
        cols = slice(cb * PROJ_TILE, (cb + 1) * PROJ_TILE)
        pm_ref[:, cols] = _dot(u, wmain_ref[:, cols])


def _mixer_stage(pm_ref, ps_ref,
                 wa2_ref, ba_ref, gnorm_ref, bif_ref, mnorm_ref,
                 convw_ref, convb_ref, wr_ref, br_ref, wi_ref, bi_ref, lam_ref,
                 s_ref, c_ref, n_ref, m_ref, h_ref, buf_ref,
                 br_out_ref, sg_ref, og_ref, xpad_ref):
    tt = pm_ref.shape[0]
    ps = ps_ref[...]
    a_pre = _dot(ps.astype(BF16), wa2_ref[...]) + ba_ref[...]
    log_a = _log_sigmoid(a_pre) * (1.0 / GLA_GATE_NORM)
    tril32, full32 = _chunk_masks(tt, GLA_CHUNK)
    sums = _dot_mask_exact(jnp.concatenate([tril32, full32], axis=0).astype(BF16), log_a)
    b_cum = sums[:tt]
    b_end = sums[tt:]
    q = pm_ref[:, C_GQ:C_GQ + HEADS * DK] * (DK ** -0.5)
    k = pm_ref[:, C_GK:C_GK + HEADS * DK]
    qi_f = q * jnp.exp(b_cum)
    qi_all = qi_f.astype(BF16)
    ki_all = (k * jnp.exp(-b_cum)).astype(BF16)
    ke_t = (k * jnp.exp(b_end - b_cum)).T
    de_t = jnp.exp(b_end).T
    in_chunk = tril32 > 0.0
    n_chunks = tt // GLA_CHUNK
    cshift = GLA_CHUNK.bit_length() - 1
    chunk_of_lane = lax.shift_right_logical(lax.broadcasted_iota(jnp.int32, (DK, tt), 1), cshift)
    chunk_of_row = lax.shift_right_logical(lax.broadcasted_iota(jnp.int32, (tt, 2 * DK), 0), cshift)
    upper_half = (lax.broadcasted_iota(jnp.int32, (tt, 2 * DK), 1) >= DK).astype(jnp.int32)
    for h in range(HEADS):
        kcols = slice(h * DK, (h + 1) * DK)
        v = pm_ref[:, C_GV + h * DV:C_GV + (h + 1) * DV].astype(BF16)
        att = jnp.where(in_chunk, _dot_nt(qi_all[:, kcols], ki_all[:, kcols]), 0.0)
        o_intra = _dot(att.astype(BF16), v)
        ke_h = ke_t[kcols, :]
        ke_blk = jnp.concatenate(
            [jnp.where(chunk_of_lane == c, ke_h, 0.0) for c in range(n_chunks)], axis=0)
        upd = _dot(ke_blk.astype(BF16), v)
        s_cur = s_ref[h]
        s_start = []
        for c in range(n_chunks):
            s_start.append(s_cur)
            de_col = de_t[kcols, c * GLA_CHUNK:c * GLA_CHUNK + 1]
            s_cur = de_col * s_cur + upd[c * DK:(c + 1) * DK, :]
        s_ref[h] = s_cur
        qi_h = qi_f[:, kcols]
        qi_dup = jnp.concatenate([qi_h, qi_h], axis=1)
        q_blk = jnp.concatenate(
            [jnp.where(chunk_of_row == 2 * jj + upper_half, qi_dup, 0.0)
             for jj in range(n_chunks // 2)], axis=1)
        o_inter = _dot(q_blk.astype(BF16), jnp.concatenate(s_start, axis=0).astype(BF16))
        og_ref[:, h * DV:(h + 1) * DV] = o_intra + o_inter

    gnorm = gnorm_ref[...]
    for h in range(HEADS):
        cols = slice(h * DV, (h + 1) * DV)
        y = _rms(og_ref[:, cols], gnorm) * _silu(pm_ref[:, C_GG + h * DV:C_GG + (h + 1) * DV])
        br_out_ref[:, cols] = y.astype(BF16)

    pre = ps + bif_ref[...]
    rr = lax.broadcasted_iota(jnp.int32, (tt, tt), 0)
    cc = lax.broadcasted_iota(jnp.int32, (tt, tt), 1)
    causal = cc <= rr
    fcum = _dot_mask_exact(jnp.where(causal, 1.0, 0.0).astype(BF16), _log_sigmoid(pre))
    pre_t = pre.T
    fcum_t = fcum.T
    q_all = pm_ref[:, C_MQ:C_MQ + HEADS * DK] * (DK ** -0.5)
    k_all = pm_ref[:, C_MK:C_MK + HEADS * DK]
    for h in range(HEADS):
        kcols = slice(h * DK, (h + 1) * DK)
        fc_col = fcum[:, L_MF + h:L_MF + h + 1]
        ic_col = pre[:, L_MI + h:L_MI + h + 1]
        fc_row = fcum_t[L_MF + h:L_MF + h + 1, :]
        ic_row = pre_t[L_MI + h:L_MI + h + 1, :]
        m_prev = m_ref[:, h:h + 1]
        d = jnp.where(causal, fc_col + (ic_row - fc_row), -jnp.inf)
        inter = fc_col + m_prev
        m_t = jnp.maximum(inter, jnp.max(d, axis=1, keepdims=True))
        w_intra = jnp.exp(d - m_t)
        w_inter = jnp.exp(inter - m_t)
        qf = q_all[:, kcols]
        kf = k_all[:, kcols]
        qc = qf.astype(BF16)
        vc = pm_ref[:, C_MV + h * DV:C_MV + (h + 1) * DV].astype(BF16)
        s = _dot_nt(qc, kf.astype(BF16)) * w_intra
        c_old = c_ref[h]
        n_old = n_ref[h]
        num = w_inter * _dot(qc, c_old.astype(BF16)) + _dot(s.astype(BF16), vc)
        den = (w_inter * jnp.sum(qf * n_old, axis=1, keepdims=True)
               + jnp.sum(s, axis=1, keepdims=True))
        hh = num / jnp.maximum(jnp.abs(den), jnp.exp(-m_t))
        og_ref[:, h * DV:(h + 1) * DV] = hh
        f_end = fc_col[tt - 1:tt, :]
        m_end = m_t[tt - 1:tt, :]
        w_k = jnp.exp(f_end - fc_col + ic_col - m_end)
        w_c = jnp.exp(f_end + m_prev - m_end)
        kw = w_k * kf
        c_ref[h] = w_c * c_old + _dot_tn(kw.astype(BF16), vc)
        n_ref[h] = w_c * n_old + jnp.sum(kw, axis=0, keepdims=True)
        m_ref[:, h:h + 1] = m_end

    for h in range(HEADS):
        cols = slice(h * DV, (h + 1) * DV)
        y = (_rms(og_ref[:, cols], mnorm_ref[:, cols])
             * _sigmoid(pm_ref[:, C_MO + h * DV:C_MO + (h + 1) * DV]))
        br_out_ref[:, BRANCH_W + h * DV:BRANCH_W + (h + 1) * DV] = y.astype(BF16)

    x = pm_ref[:, C_RX:C_RX + RG_WIDTH]
    xpad_ref[pl.ds(8, tt), :] = x
    xc = (convb_ref[...] + convw_ref[3:4, :] * x
          + convw_ref[2:3, :] * xpad_ref[pl.ds(7, tt), :]
          + convw_ref[1:2, :] * xpad_ref[pl.ds(6, tt), :]
          + convw_ref[0:1, :] * xpad_ref[pl.ds(5, tt), :])
    buf_ref[...] = xpad_ref[pl.ds(tt + 8 - (RG_CONV - 1), RG_CONV - 1), :]
    xpad_ref[pl.ds(0, 8), :] = xpad_ref[pl.ds(tt, 8), :]
    a, b = _rg_gates(xc, wr_ref, br_ref, wi_ref, bi_ref, lam_ref)
    ridx = lax.broadcasted_iota(jnp.int32, (tt, RG_WIDTH), 0)
    sh = 1
    while sh < tt:
        keep = ridx >= sh
        a_sh = jnp.where(keep, pltpu.roll(a, sh, 0), 1.0)
        b_sh = jnp.where(keep, pltpu.roll(b, sh, 0), 0.0)
        b = a * b_sh + b
        a = a * a_sh
        sh *= 2
    hseq = a * h_ref[...] + b
    h_ref[...] = hseq[tt - 1:tt, :]
    y = hseq * _gelu_tanh(pm_ref[:, C_RG:C_RG + RG_WIDTH])
    br_out_ref[:, 2 * BRANCH_W:3 * BRANCH_W] = y.astype(BF16)

    for b in range(N_BRANCH):
        gcols = slice(b * D_MODEL, (b + 1) * D_MODEL)
        sg_ref[:, gcols] = _sigmoid(pm_ref[:, MIX_COLS + b * D_MODEL:MIX_COLS + (b + 1) * D_MODEL])


def _merge_stage(br_ref, sg_ref, xr_ref, wb_ref, wo_ref, xo_ref, mg_ref):
    n_tiles = D_MODEL // MERGE_TILE
    for jt in range(n_tiles):
        cols = slice(jt * MERGE_TILE, (jt + 1) * MERGE_TILE)
        merged = None
        for b in range(N_BRANCH):
            z = _dot(br_ref[:, b * BRANCH_W:(b + 1) * BRANCH_W], wb_ref[b, :, cols])
            term = sg_ref[:, b * D_MODEL + jt * MERGE_TILE:b * D_MODEL + (jt + 1) * MERGE_TILE] * z
            merged = term if merged is None else merged + term
        mg_ref[:, cols] = merged.astype(BF16)
    for jt in range(n_tiles):
        cols = slice(jt * MERGE_TILE, (jt + 1) * MERGE_TILE)
        xo_ref[:, cols] = xr_ref[:, cols] + _dot(mg_ref[...], wo_ref[:, cols])


N_MIXER_WEIGHTS = 12


def _mix_prompt_kernel(x_ref, g_ref, wmain_ref, wsmall_ref, *refs):
    mixer_w = refs[:N_MIXER_WEIGHTS]
    wb_ref, wo_ref = refs[N_MIXER_WEIGHTS:N_MIXER_WEIGHTS + 2]
    xo_ref, s_ref, c_ref, n_ref, m_ref, h_ref, buf_ref = refs[N_MIXER_WEIGHTS + 2:N_MIXER_WEIGHTS + 9]
    pm_ref, ps_ref, xs_ref, br_ref, sg_ref, mg_ref, og_ref, xpad_ref = refs[N_MIXER_WEIGHTS + 9:]

    @pl.when(pl.program_id(1) == 0)
    def _():
        for ref in (s_ref, c_ref, n_ref, m_ref, h_ref):
            ref[...] = jnp.zeros_like(ref)
        xpad_ref[pl.ds(0, 8), :] = jnp.zeros((8, RG_WIDTH), F32)

    _project_stage(x_ref, g_ref, wmain_ref, wsmall_ref, pm_ref, ps_ref, xs_ref)
    _mixer_stage(pm_ref, ps_ref, *mixer_w, s_ref, c_ref, n_ref, m_ref, h_ref, buf_ref,
                 br_ref, sg_ref, og_ref, xpad_ref)
    _merge_stage(br_ref, sg_ref, xs_ref, wb_ref, wo_ref, xo_ref, mg_ref)


def _mix_prompt(x, g, wts, lw, layer, *, nb, t, tt):
    d = x.shape[1]
    const = lambda shape: pl.BlockSpec(shape, lambda b, j: (0,) * len(shape),
                                       pipeline_mode=pl.Buffered(1))
    slab = lambda shape: pl.BlockSpec((None,) + shape, lambda b, j: (layer,) + (0,) * len(shape),
                                      pipeline_mode=pl.Buffered(1))
    nt = t // tt
    return pl.pallas_call(
        _mix_prompt_kernel,
        grid=(nb, nt),
        in_specs=[pl.BlockSpec((tt, d), lambda b, j: (b * nt + j, 0)),
                  const((1, d)), slab((d, MAIN_COLS)), slab((d, SMALL_COLS)),
                  slab((SMALL_COLS, HEADS * DK)), const((1, HEADS * DK)), const((1, DV)),
                  const((1, SMALL_COLS)), const((1, BRANCH_W)),
                  const((RG_CONV, RG_WIDTH)), const((1, RG_WIDTH)),
                  slab((RG_WIDTH, RG_WIDTH)), const((1, RG_WIDTH)),
                  slab((RG_WIDTH, RG_WIDTH)), const((1, RG_WIDTH)), const((1, RG_WIDTH)),
                  slab((N_BRANCH, BRANCH_W, d)), slab((d, d))],
        out_specs=[pl.BlockSpec((tt, d), lambda b, j: (b * nt + j, 0)),
                   pl.BlockSpec((None, HEADS, DK, DV), lambda b, j: (b, 0, 0, 0)),
                   pl.BlockSpec((None, HEADS, DK, DV), lambda b, j: (b, 0, 0, 0)),
                   pl.BlockSpec((None, HEADS, 1, DK), lambda b, j: (b, 0, 0, 0)),
                   pl.BlockSpec((None, 1, HEADS), lambda b, j: (b, 0, 0)),
                   pl.BlockSpec((None, 1, RG_WIDTH), lambda b, j: (b, 0, 0)),
                   pl.BlockSpec((None, RG_CONV - 1, RG_WIDTH), lambda b, j: (b, 0, 0))],
        out_shape=[jax.ShapeDtypeStruct((nb * t, d), F32),
                   jax.ShapeDtypeStruct((nb, HEADS, DK, DV), F32),
                   jax.ShapeDtypeStruct((nb, HEADS, DK, DV), F32),
                   jax.ShapeDtypeStruct((nb, HEADS, 1, DK), F32),
                   jax.ShapeDtypeStruct((nb, 1, HEADS), F32),
                   jax.ShapeDtypeStruct((nb, 1, RG_WIDTH), F32),
                   jax.ShapeDtypeStruct((nb, RG_CONV - 1, RG_WIDTH), F32)],
        scratch_shapes=[pltpu.VMEM((tt, MAIN_COLS), F32),
                        pltpu.VMEM((tt, SMALL_COLS), F32),
                        pltpu.VMEM((tt, d), F32),
                        pltpu.VMEM((tt, N_BRANCH * BRANCH_W), BF16),
                        pltpu.VMEM((tt, GATE_COLS), F32),
                        pltpu.VMEM((tt, d), BF16),
                        pltpu.VMEM((tt, BRANCH_W), F32),
                        pltpu.VMEM((tt + 8, RG_WIDTH), F32)],
        compiler_params=pltpu.CompilerParams(
            dimension_semantics=("parallel", "arbitrary"), vmem_limit_bytes=VMEM_LIMIT),
        name="mix_prompt",
    )(x, g.reshape(1, d), wts["w_main"], wts["w_small"],
      wts["wa2"], lw["ba"], lw["gnorm"], lw["bif"], lw["mnorm"], lw["convw"], lw["convb"],
      wts["wr"], lw["br"], wts["wi"], lw["bi"], lw["lam"], wts["wb"], wts["wo"])


def _mix_sample_kernel(pm_ref, ps_ref, s0_ref, c0_ref, n0_ref, m0_ref, h0_ref, buf0_ref,
                       wa2_ref, ba_ref, gnorm_ref, bif_ref, mnorm_ref,
                       convw_ref, convb_ref, wr_ref, br_ref, wi_ref, bi_ref, lam_ref,
                       br_out_ref, s_ref, c_ref, n_ref, m_ref, h_ref, buf_ref,
                       og_ref, om_ref):
    bs = pm_ref.shape[0]
    ps = ps_ref[...]
    a_pre = _dot(ps.astype(BF16), wa2_ref[...]) + ba_ref[...]
    a = jnp.exp(_log_sigmoid(a_pre) * (1.0 / GLA_GATE_NORM))
    a_hi = a.astype(BF16)
    a_r1 = a - a_hi.astype(F32)
    a_mid = a_r1.astype(BF16)
    a_lo = (a_r1 - a_mid.astype(F32)).astype(BF16)
    mq = pm_ref[:, C_MQ:C_MQ + HEADS * DK] * (DK ** -0.5)
    mk = pm_ref[:, C_MK:C_MK + HEADS * DK]
    kd = HEADS * DK
    cols_t = jnp.concatenate(
        [(pm_ref[:, C_GQ:C_GQ + kd] * (DK ** -0.5)).T, pm_ref[:, C_GK:C_GK + kd].T,
         mq.T, mk.T, a_hi.astype(F32).T, a_mid.astype(F32).T, a_lo.astype(F32).T],
        axis=0).astype(BF16)
    o_gq, o_gk, o_mq, o_mk, o_a0, o_a1, o_a2 = (i * kd for i in range(7))
    sample_id = lax.broadcasted_iota(jnp.int32, (bs, DV), 0)
    pre = ps + bif_ref[...]
    lf = _log_sigmoid(pre)
    m0_all = m0_ref[...]
    head_lane = lax.broadcasted_iota(jnp.int32, (1, HEADS), 1)

    for s in range(bs):
        row = slice(s, s + 1)
        bc = _dot(cols_t, jnp.where(sample_id == s, 1.0, 0.0).astype(BF16))
        m_row = m0_all[row, :]
        for h in range(HEADS):
            kcols = slice(h * DK, (h + 1) * DK)
            col = lambda off: bc[off + h * DK:off + (h + 1) * DK, :]
            v_row = pm_ref[row, C_GV + h * DV:C_GV + (h + 1) * DV]
            a_col = col(o_a0) + col(o_a1) + col(o_a2)
            s_new = a_col * s0_ref[s, h] + col(o_gk) * v_row
            s_ref[s, h] = s_new
            og_ref[row, h * DV:(h + 1) * DV] = jnp.sum(col(o_gq) * s_new,
                                                       axis=0, keepdims=True)
            ic = pre[row, L_MI + h:L_MI + h + 1]
            fc = lf[row, L_MF + h:L_MF + h + 1]
            m0 = m0_all[row, h:h + 1]
            inter = fc + m0
            m_t = jnp.maximum(inter, ic)
            w_intra = jnp.exp(ic - m_t)
            w_inter = jnp.exp(inter - m_t)
            q_row = mq[row, kcols]
            k_row = mk[row, kcols]
            v_row = pm_ref[row, C_MV + h * DV:C_MV + (h + 1) * DV]
            sc = jnp.sum(q_row * k_row, axis=1, keepdims=True) * w_intra
            c_old = c0_ref[s, h]
            n_old = n0_ref[s, h]
            qc = jnp.sum(col(o_mq) * c_old, axis=0, keepdims=True)
            num = w_inter * qc + sc * v_row
            den = w_inter * jnp.sum(q_row * n_old, axis=1, keepdims=True) + sc
            om_ref[row, h * DV:(h + 1) * DV] = num / jnp.maximum(jnp.abs(den), jnp.exp(-m_t))
            c_ref[s, h] = w_inter * c_old + (w_intra * col(o_mk)) * v_row
            n_ref[s, h] = w_inter * n_old + w_intra * k_row
            m_row = jnp.where(head_lane == h, m_t, m_row)
        m_ref[row, :] = m_row

    gnorm = gnorm_ref[...]
    for h in range(HEADS):
        cols = slice(h * DV, (h + 1) * DV)
        y = _rms(og_ref[:, cols], gnorm) * _silu(pm_ref[:, C_GG + h * DV:C_GG + (h + 1) * DV])
        br_out_ref[:, cols] = y.astype(BF16)
        y = (_rms(om_ref[:, cols], mnorm_ref[:, cols])
             * _sigmoid(pm_ref[:, C_MO + h * DV:C_MO + (h + 1) * DV]))
        br_out_ref[:, BRANCH_W + h * DV:BRANCH_W + (h + 1) * DV] = y.astype(BF16)

    x = pm_ref[:, C_RX:C_RX + RG_WIDTH]
    xc = convb_ref[...] + convw_ref[3:4, :] * x
    for jj in range(RG_CONV - 1):
        xc = xc + convw_ref[jj:jj + 1, :] * buf0_ref[:, jj * RG_WIDTH:(jj + 1) * RG_WIDTH]
    buf_ref[:, 0:2 * RG_WIDTH] = buf0_ref[:, RG_WIDTH:3 * RG_WIDTH]
    buf_ref[:, 2 * RG_WIDTH:3 * RG_WIDTH] = x
    a, b = _rg_gates(xc, wr_ref, br_ref, wi_ref, bi_ref, lam_ref)
    hnew = a * h0_ref[...] + b
    h_ref[...] = hnew
    y = hnew * _gelu_tanh(pm_ref[:, C_RG:C_RG + RG_WIDTH])
    br_out_ref[:, 2 * BRANCH_W:3 * BRANCH_W] = y.astype(BF16)


def _mix_sample(pm, ps, st, wts, lw, layer, *, bs):
    n = pm.shape[0]
    const = lambda shape: pl.BlockSpec(shape, lambda i: (0,) * len(shape))
    slab = lambda shape: pl.BlockSpec((None,) + shape, lambda i: (layer,) + (0,) * len(shape))
    blk = lambda shape: pl.BlockSpec(shape, lambda i: (i,) + (0,) * (len(shape) - 1))
    lblk = lambda shape: pl.BlockSpec((None,) + shape,
                                      lambda i: (layer, i) + (0,) * (len(shape) - 1))
    s0, c0, n0, m0, h0, buf0 = st
    shapes = [(bs, HEADS, DK, DV), (bs, HEADS, DK, DV), (bs, HEADS, 1, DK),
              (bs, HEADS), (bs, RG_WIDTH), (bs, (RG_CONV - 1) * RG_WIDTH)]
    return pl.pallas_call(
        _mix_sample_kernel,
        grid=(n // bs,),
        in_specs=[blk((bs, MIX_COLS)), blk((bs, SMALL_COLS))] + [lblk(sh) for sh in shapes] + [
            slab((SMALL_COLS, HEADS * DK)), const((1, HEADS * DK)), const((1, DV)),
            const((1, SMALL_COLS)), const((1, BRANCH_W)),
            const((RG_CONV, RG_WIDTH)), const((1, RG_WIDTH)),
            slab((RG_WIDTH, RG_WIDTH)), const((1, RG_WIDTH)),
            slab((RG_WIDTH, RG_WIDTH)), const((1, RG_WIDTH)), const((1, RG_WIDTH))],
        out_specs=[blk((bs, N_BRANCH * BRANCH_W))] + [blk(sh) for sh in shapes],
        out_shape=[jax.ShapeDtypeStruct((n, N_BRANCH * BRANCH_W), BF16)]
        + [jax.ShapeDtypeStruct(a.shape[1:], F32) for a in st],
        scratch_shapes=[pltpu.VMEM((bs, BRANCH_W), F32),
                        pltpu.VMEM((bs, BRANCH_W), F32)],
        compiler_params=pltpu.CompilerParams(
            dimension_semantics=("parallel",), vmem_limit_bytes=VMEM_LIMIT),
        name="mix_sample",
    )(pm, ps, s0, c0, n0, m0, h0, buf0,
      wts["wa2"], lw["ba"], lw["gnorm"], lw["bif"], lw["mnorm"], lw["convw"], lw["convb"],
      wts["wr"], lw["br"], wts["wi"], lw["bi"], lw["lam"])


def _merge_kernel(x_ref, br_ref, g0_ref, g1_ref, g2_ref, wb_ref, wo_ref, o_ref):
    merged = None
    for b, g_ref in enumerate((g0_ref, g1_ref, g2_ref)):
        z = _dot(br_ref[:, b * BRANCH_W:(b + 1) * BRANCH_W], wb_ref[b])
        term = _sigmoid(g_ref[...]) * z
        merged = term if merged is None else merged + term
    o_ref[...] = x_ref[...] + _dot(merged.astype(BF16), wo_ref[...])


def _merge(x, br, pm, wb, wo, layer, *, tm):
    n, d = x.shape
    gate_blk0 = MIX_COLS // d
    gate_spec = lambda b: pl.BlockSpec((tm, d), lambda i: (i, gate_blk0 + b))
    return pl.pallas_call(
        _merge_kernel,
        grid=(n // tm,),
        in_specs=[pl.BlockSpec((tm, d), lambda i: (i, 0)),
                  pl.BlockSpec((tm, N_BRANCH * BRANCH_W), lambda i: (i, 0)),
                  gate_spec(0), gate_spec(1), gate_spec(2),
                  pl.BlockSpec((None, N_BRANCH, BRANCH_W, d), lambda i: (layer, 0, 0, 0)),
                  pl.BlockSpec((None, d, d), lambda i: (layer, 0, 0))],
        out_specs=pl.BlockSpec((tm, d), lambda i: (i, 0)),
        out_shape=jax.ShapeDtypeStruct((n, d), F32),
        compiler_params=pltpu.CompilerParams(
            dimension_semantics=("parallel",), vmem_limit_bytes=VMEM_LIMIT),
        name="merge",
    )(x, br, pm, pm, pm, wb, wo)


def _xattn_ffn_prompt_kernel(x_ref, gx_ref, wq_ref, wo_ref, k_ref, v_ref,
                             gf_ref, wg_ref, wu_ref, wd_ref, gfin_ref,
                             o_ref, kb_ref, vb_ref, hid_ref, *, final_norm):
    @pl.when(pl.program_id(1) == 0)
    def _():
        kb_ref[...] = k_ref[...].astype(BF16)
        vb_ref[...] = v_ref[...].astype(BF16)

    x = x_ref[...]
    q = _dot(_rms(x, gx_ref[...]).astype(BF16), wq_ref[...]).astype(BF16)
    heads = []
    for h in range(XA_HEADS):
        cols = slice(h * XA_HEAD_DIM, (h + 1) * XA_HEAD_DIM)
        s = _dot_nt(q[:, cols], kb_ref[:, cols]) * (XA_HEAD_DIM ** -0.5)
        p = jnp.exp(s - jnp.max(s, axis=-1, keepdims=True))
        p = p / jnp.sum(p, axis=-1, keepdims=True)
        heads.append(_dot(p.astype(BF16), vb_ref[:, cols]))
    o = jnp.concatenate(heads, axis=1).astype(BF16)
    x = x + _dot(o, wo_ref[...])
    o_ref[...] = _ffn_block(x, gf_ref, wg_ref, wu_ref, wd_ref, gfin_ref, hid_ref, final_norm)


def _xattn_ffn_prompt(x, gx, wq, wo, mem_k, mem_v, gf, wg, wu, wd, gfin, layer, *,
                      nb, t, tm, final_norm):
    d = x.shape[1]
    f = wg.shape[2]
    nt = t // tm
    const = lambda shape: pl.BlockSpec(shape, lambda b, j: (0,) * len(shape),
                                       pipeline_mode=pl.Buffered(1))
    slab = lambda shape: pl.BlockSpec((None,) + shape, lambda b, j: (layer,) + (0,) * len(shape),
                                      pipeline_mode=pl.Buffered(1))
    return pl.pallas_call(
        functools.partial(_xattn_ffn_prompt_kernel, final_norm=final_norm),
        grid=(nb, nt),
        in_specs=[pl.BlockSpec((tm, d), lambda b, j: (b * nt + j, 0)),
                  const((1, d)), slab((d, d)), slab((d, d)),
                  pl.BlockSpec((None, N_MEM, d), lambda b, j: (b, 0, 0)),
                  pl.BlockSpec((None, N_MEM, d), lambda b, j: (b, 0, 0)),
                  const((1, d)), slab((d, f)), slab((d, f)), slab((f, d)), const((1, d))],
        out_specs=pl.BlockSpec((tm, d), lambda b, j: (b * nt + j, 0)),
        out_shape=jax.ShapeDtypeStruct(x.shape, F32),
        scratch_shapes=[pltpu.VMEM((N_MEM, d), BF16), pltpu.VMEM((N_MEM, d), BF16),
                        pltpu.VMEM((tm, f), BF16)],
        compiler_params=pltpu.CompilerParams(
            dimension_semantics=("parallel", "arbitrary"), vmem_limit_bytes=VMEM_LIMIT),
        name="xattn_ffn_prompt",
    )(x, gx.reshape(1, d), wq, wo, mem_k, mem_v, gf.reshape(1, d), wg, wu, wd, gfin.reshape(1, d))


def _xattn_sample_kernel(q_ref, k_ref, v_ref, o_ref):
    bs = q_ref.shape[0]
    for s in range(bs):
        q = q_ref[s]
        sc = jnp.sum(k_ref[s] * q[None], axis=-1, keepdims=True) * (XA_HEAD_DIM ** -0.5)
        p = jnp.exp(sc - jnp.max(sc, axis=0, keepdims=True))
        p = p / jnp.sum(p, axis=0, keepdims=True)
        o_ref[s] = jnp.sum(p * v_ref[s], axis=0)


def _xattn_sample(q, cache_k, cache_v, layer, *, bs):
    n = q.shape[0]
    blk = (None, bs, N_MEM, XA_HEADS, XA_HEAD_DIM)
    return pl.pallas_call(
        _xattn_sample_kernel,
        grid=(n // bs,),
        in_specs=[pl.BlockSpec((bs, XA_HEADS, XA_HEAD_DIM), lambda i: (i, 0, 0)),
                  pl.BlockSpec(blk, lambda i: (layer, i, 0, 0, 0)),
                  pl.BlockSpec(blk, lambda i: (layer, i, 0, 0, 0))],
        out_specs=pl.BlockSpec((bs, XA_HEADS, XA_HEAD_DIM), lambda i: (i, 0, 0)),
        out_shape=jax.ShapeDtypeStruct((n, XA_HEADS, XA_HEAD_DIM), F32),
        compiler_params=pltpu.CompilerParams(
            dimension_semantics=("parallel",), vmem_limit_bytes=VMEM_LIMIT),
        name="xattn_sample",
    )(q, cache_k, cache_v)


def _matmul_residual_kernel(x_ref, a_ref, w_ref, o_ref):
    o_ref[...] = x_ref[...] + _dot(a_ref[...].astype(BF16), w_ref[...])


def _matmul_residual(x, a, w, layer):
    n, d = x.shape
    return pl.pallas_call(
        _matmul_residual_kernel,
        grid=(1,),
        in_specs=[pl.BlockSpec((n, d), lambda i: (0, 0)),
                  pl.BlockSpec((n, d), lambda i: (0, 0)),
                  pl.BlockSpec((None, d, d), lambda i: (layer, 0, 0))],
        out_specs=pl.BlockSpec((n, d), lambda i: (0, 0)),
        out_shape=jax.ShapeDtypeStruct((n, d), F32),
        compiler_params=pltpu.CompilerParams(vmem_limit_bytes=VMEM_LIMIT),
        name="matmul_residual",
    )(x, a, w)


def _ffn_block(x, g_ref, wg_ref, wu_ref, wd_ref, gfin_ref, hid_ref, final_norm):
    u = _rms(x, g_ref[...]).astype(BF16)
    for jt in range(D_FF // FFN_TILE):
        cols = slice(jt * FFN_TILE, (jt + 1) * FFN_TILE)
        hid = _silu(_dot(u, wg_ref[:, cols])) * _dot(u, wu_ref[:, cols])
        hid_ref[:, cols] = hid.astype(BF16)
    y = x + _dot(hid_ref[...], wd_ref[...])
    if final_norm:
        y = _rms(y, gfin_ref[...])
    return y


def _ffn_kernel(x_ref, g_ref, wg_ref, wu_ref, wd_ref, gf_ref, o_ref, hid_ref, *, final_norm):
    o_ref[...] = _ffn_block(x_ref[...], g_ref, wg_ref, wu_ref, wd_ref, gf_ref, hid_ref, final_norm)


def _ffn(x, g, wg, wu, wd, gf, layer, *, tm, final_norm):
    n, d = x.shape
    f = wg.shape[2]
    const = lambda shape: pl.BlockSpec(shape, lambda i: (0,) * len(shape),
                                       pipeline_mode=pl.Buffered(1))
    slab = lambda shape: pl.BlockSpec((None,) + shape, lambda i: (layer,) + (0,) * len(shape),
                                      pipeline_mode=pl.Buffered(1))
    return pl.pallas_call(
        functools.partial(_ffn_kernel, final_norm=final_norm),
        grid=(n // tm,),
        in_specs=[pl.BlockSpec((tm, d), lambda i: (i, 0)),
                  const((1, d)), slab((d, f)), slab((d, f)), slab((f, d)), const((1, d))],
        out_specs=pl.BlockSpec((tm, d), lambda i: (i, 0)),
        out_shape=jax.ShapeDtypeStruct((n, d), F32),
        scratch_shapes=[pltpu.VMEM((tm, f), BF16)],
        compiler_params=pltpu.CompilerParams(
            dimension_semantics=("parallel",), vmem_limit_bytes=VMEM_LIMIT),
        name="ffn",
    )(x, g.reshape(1, d), wg, wu, wd, gf.reshape(1, d))


def _split_w_in(w_in):
    widths = (HEADS * DK, HEADS * DK, HEADS * DV, HEADS * DV, GLA_RANK,
              HEADS * DK, HEADS * DK, HEADS * DV, HEADS * DV, HEADS, HEADS,
              RG_WIDTH, RG_WIDTH, GATE_COLS)
    parts, acc = [], 0
    for w in widths:
        parts.append(w_in[..., acc:acc + w])
        acc += w
    return parts


def _block_diag(w):
    eye = jnp.eye(RG_BLOCKS, dtype=w.dtype)
    return jnp.einsum("lnde,nm->lndme", w, eye).reshape(DEPTH, RG_WIDTH, RG_WIDTH)


def _matmul_weights(p):
    (g_q, g_k, g_v, g_g, g_a, m_q, m_k, m_v, m_o, m_i, m_f, r_x, r_g, gates) = _split_w_in(p["w_in"])
    d = D_MODEL
    w_main = jnp.concatenate([g_q, g_k, g_v, g_g, m_q, m_k, m_v, m_o, r_x, r_g, gates], axis=2)
    w_small = jnp.concatenate(
        [g_a, m_i, m_f, jnp.zeros((DEPTH, d, SMALL_COLS - GLA_RANK - 2 * HEADS), F32)], axis=2)
    wa2 = jnp.concatenate(
        [p["gla_w_a2"], jnp.zeros((DEPTH, SMALL_COLS - GLA_RANK, HEADS * DK), F32)], axis=1)
    return {
        "w_main": w_main.astype(BF16), "w_small": w_small.astype(BF16), "wa2": wa2.astype(BF16),
        "wr": _block_diag(p["rg_w_r"]).astype(BF16), "wi": _block_diag(p["rg_w_i"]).astype(BF16),
        "wb": p["w_branch"].astype(BF16), "wo": p["w_out"].astype(BF16),
        "xa_wq": p["xa_wq"].astype(BF16), "xa_wo": p["xa_wo"].astype(BF16),
        "xa_wkv": jnp.concatenate([p["xa_wk"], p["xa_wv"]], axis=2).astype(BF16),
        "wg": p["ffn_w_gate"].astype(BF16), "wu": p["ffn_w_up"].astype(BF16),
        "wd": p["ffn_w_down"].astype(BF16),
    }


def _vector_params(l, p):
    bif = jnp.concatenate(
        [jnp.zeros((L_MI,), F32), p["mlstm_b_i"][l], p["mlstm_b_f"][l],
         jnp.zeros((SMALL_COLS - L_MF - HEADS,), F32)]).reshape(1, SMALL_COLS)
    return {
        "ba": p["gla_b_a"][l].reshape(1, -1), "gnorm": p["gla_norm"][l].reshape(1, DV),
        "bif": bif, "mnorm": p["mlstm_norm"][l].reshape(1, BRANCH_W),
        "convw": p["rg_conv_w"][l], "convb": p["rg_conv_b"][l].reshape(1, -1),
        "br": p["rg_b_r"][l].reshape(1, -1), "bi": p["rg_b_i"][l].reshape(1, -1),
        "lam": p["rg_lambda"][l].reshape(1, -1),
    }


def kernel(x_prompt, x_sample, mem_prompt, cache_mem_k, cache_mem_v, state_gla, state_mlstm_c, state_mlstm_n, state_mlstm_m, state_rglru_h, state_rglru_conv, norm_mix, w_in, gla_w_a2, gla_b_a, gla_norm, mlstm_b_i, mlstm_b_f, mlstm_norm, rg_conv_w, rg_conv_b, rg_w_r, rg_b_r, rg_w_i, rg_b_i, rg_lambda, w_branch, w_out, norm_xa, norm_mem, xa_wq, xa_wk, xa_wv, xa_wo, norm_ffn, ffn_w_gate, ffn_w_up, ffn_w_down, norm_final):
    p = {"w_in": w_in, "gla_w_a2": gla_w_a2, "gla_b_a": gla_b_a, "gla_norm": gla_norm,
         "mlstm_b_i": mlstm_b_i, "mlstm_b_f": mlstm_b_f, "mlstm_norm": mlstm_norm,
         "rg_conv_w": rg_conv_w, "rg_conv_b": rg_conv_b, "rg_w_r": rg_w_r, "rg_b_r": rg_b_r,
         "rg_w_i": rg_w_i, "rg_b_i": rg_b_i, "rg_lambda": rg_lambda, "w_branch": w_branch,
         "w_out": w_out, "xa_wq": xa_wq, "xa_wk": xa_wk, "xa_wv": xa_wv, "xa_wo": xa_wo,
         "ffn_w_gate": ffn_w_gate, "ffn_w_up": ffn_w_up, "ffn_w_down": ffn_w_down}
    nb, t, d = x_prompt.shape
    ns = x_sample.shape[0]
    xp = x_prompt.reshape(nb * t, d)
    xs = x_sample.reshape(ns, d)
    mem = mem_prompt.reshape(nb * N_MEM, d)
    wts = _matmul_weights(p)

    st_sample = (state_gla, state_mlstm_c, state_mlstm_n.reshape(DEPTH, ns, HEADS, 1, DK),
                 state_mlstm_m, state_rglru_h,
                 state_rglru_conv.reshape(DEPTH, ns, (RG_CONV - 1) * RG_WIDTH))

    new_p = [[] for _ in range(8)]
    new_s = [[] for _ in range(6)]
    for l in range(DEPTH):
        lw = _vector_params(l, p)
        last = l == DEPTH - 1

        kv = _norm_matmul(mem, norm_mem[l], wts["xa_wkv"], l, tm=512, tn=d)
        k_p = kv[0].reshape(nb, N_MEM, d)
        v_p = kv[1].reshape(nb, N_MEM, d)

        xp, g_s, c_s, n_s, m_s, h_s, buf_s = _mix_prompt(xp, norm_mix[l], wts, lw, l,
                                                         nb=nb, t=t, tt=256)
        xp = _xattn_ffn_prompt(xp, norm_xa[l], wts["xa_wq"], wts["xa_wo"], k_p, v_p,
                               norm_ffn[l], wts["wg"], wts["wu"], wts["wd"], norm_final, l,
                               nb=nb, t=t, tm=512, final_norm=last)
        for lst, val in zip(new_p, (k_p.reshape(nb, N_MEM, XA_HEADS, XA_HEAD_DIM),
                                    v_p.reshape(nb, N_MEM, XA_HEADS, XA_HEAD_DIM),
                                    g_s, c_s, n_s.reshape(nb, HEADS, DK), m_s.reshape(nb, HEADS),
                                    h_s.reshape(nb, RG_WIDTH), buf_s)):
            lst.append(val)

        pm, ps = _in_proj(xs, norm_mix[l], wts["w_main"], wts["w_small"], l, tm=ns, tn=1024)
        br, g_s, c_s, n_s, m_s, h_s, buf_s = _mix_sample(pm, ps, st_sample, wts, lw, l, bs=8)
        xs = _merge(xs, br, pm, wts["wb"], wts["wo"], l, tm=ns)
        q = _norm_matmul(xs, norm_xa[l], wts["xa_wq"], l, tm=ns, tn=d)
        att = _xattn_sample(q.reshape(ns, XA_HEADS, XA_HEAD_DIM), cache_mem_k, cache_mem_v, l, bs=4)
        xs = _matmul_residual(xs, att.reshape(ns, d), wts["xa_wo"], l)
        xs = _ffn(xs, norm_ffn[l], wts["wg"], wts["wu"], wts["wd"], norm_final, l,
                  tm=ns, final_norm=last)
        for lst, val in zip(new_s, (g_s, c_s, n_s.reshape(ns, HEADS, DK), m_s, h_s,
                                    buf_s.reshape(ns, RG_CONV - 1, RG_WIDTH))):
            lst.append(val)

    y_prompt = xp.reshape(nb, t, d)
    y_sample = xs.reshape(ns, 1, d)
    outs_p = [jnp.stack(v, axis=0) for v in new_p]
    outs_s = [jnp.stack(v, axis=0) for v in new_s]
    return (y_prompt, y_sample, *outs_p, *outs_s)
```

```python
import functools
import math

import jax
import jax.numpy as jnp
from jax import lax
from jax.experimental import pallas as pl
from jax.experimental.pallas import tpu as pltpu

F32 = jnp.float32
BF16 = jnp.bfloat16

D_MODEL = 1024
DEPTH = 2
EPS = 1e-6
N_MEM = 256
XA_HEADS = 4
XA_HEAD_DIM = D_MODEL // XA_HEADS
N_BRANCH = 3
BRANCH_W = D_MODEL // 2
HEADS = 4
DK = 64
DV = 128
GLA_RANK = 16
GLA_GATE_NORM = 16.0
GLA_CHUNK = 32
MLSTM_CHUNK = 64
RG_WIDTH = BRANCH_W
RG_BLOCKS = 8
RG_BLOCK = RG_WIDTH // RG_BLOCKS
RG_CONV = 4
RG_C = 8.0
D_FF = 2816

C_GQ, C_GK, C_GV, C_GG = 0, 256, 512, 1024
C_MQ, C_MK, C_MV, C_MO = 1536, 1792, 2048, 2560
C_RX, C_RG = 3072, 3584
MIX_COLS = 4096
GATE_COLS = N_BRANCH * D_MODEL
MAIN_COLS = MIX_COLS + GATE_COLS
SMALL_COLS = 128
L_GA = 0
L_MI = 16
L_MF = 20

PROJ_TILE = 256
MERGE_TILE = 256
FFN_TILE = 256
VMEM_LIMIT = 48 * 1024 * 1024


def _softplus(x):
    return jnp.maximum(x, 0.0) + jnp.log1p(jnp.exp(-jnp.abs(x)))


def _log_sigmoid(x):
    return -_softplus(-x)


def _sigmoid(x):
    return 1.0 / (1.0 + jnp.exp(-x))


def _silu(x):
    return x * _sigmoid(x)


def _gelu_tanh(x):
    c = math.sqrt(2.0 / math.pi)
    return x * (0.5 * (1.0 + jnp.tanh(c * (x + 0.044715 * (x * x * x)))))


def _neg_expm1(x):
    return -jnp.tanh(0.5 * x) * (jnp.exp(x) + 1.0)


def _rms(x, g):
    ms = jnp.mean(x * x, axis=-1, keepdims=True)
    return x * lax.rsqrt(ms + EPS) * g


def _dot(a, b):
    return jnp.dot(a, b, preferred_element_type=F32)


def _dot_nt(a, b):
    return lax.dot_general(a, b, (((1,), (1,)), ((), ())), preferred_element_type=F32)


def _dot_tn(a, b):
    return lax.dot_general(a, b, (((0,), (0,)), ((), ())), preferred_element_type=F32)


def _dot_mask_exact(mask, x):
    hi = x.astype(BF16)
    rest = x - hi.astype(F32)
    mid = rest.astype(BF16)
    lo = (rest - mid.astype(F32)).astype(BF16)
    return _dot(mask, hi) + _dot(mask, mid) + _dot(mask, lo)


def _chunk_masks(n, chunk):
    shift = chunk.bit_length() - 1
    r = lax.broadcasted_iota(jnp.int32, (n, n), 0)
    c = lax.broadcasted_iota(jnp.int32, (n, n), 1)
    same = lax.shift_right_logical(r, shift) == lax.shift_right_logical(c, shift)
    tril = jnp.where(same & (c <= r), 1.0, 0.0).astype(F32)
    full = jnp.where(same, 1.0, 0.0).astype(F32)
    return tril, full


def _norm_matmul_kernel(x_ref, g_ref, w_ref, *refs):
    out_refs, u_ref = refs[:-1], refs[-1]
    j = pl.program_id(1)

    @pl.when(j == 0)
    def _():
        u_ref[...] = _rms(x_ref[...], g_ref[...]).astype(BF16)

    res = _dot(u_ref[...], w_ref[...])
    for idx, o_ref in enumerate(out_refs):
        @pl.when(j == idx)
        def _(o_ref=o_ref):
            o_ref[...] = res


def _norm_matmul(x, g, w, layer, *, tm, tn):
    n, d = x.shape
    nout = w.shape[2]
    n_out = nout // tn
    return pl.pallas_call(
        _norm_matmul_kernel,
        grid=(n // tm, n_out),
        in_specs=[pl.BlockSpec((tm, d), lambda i, j: (i, 0)),
                  pl.BlockSpec((1, d), lambda i, j: (0, 0)),
                  pl.BlockSpec((None, d, tn), lambda i, j: (layer, 0, j))],
        out_specs=[pl.BlockSpec((tm, tn), lambda i, j: (i, 0))] * n_out,
        out_shape=[jax.ShapeDtypeStruct((n, tn), F32)] * n_out,
        scratch_shapes=[pltpu.VMEM((tm, d), BF16)],
        compiler_params=pltpu.CompilerParams(
            dimension_semantics=("parallel", "arbitrary"), vmem_limit_bytes=VMEM_LIMIT),
        name="norm_matmul",
    )(x, g.reshape(1, d), w)


def _in_proj_kernel(x_ref, g_ref, w_ref, ws_ref, o_ref, os_ref, u_ref):
    @pl.when(pl.program_id(1) == 0)
    def _():
        u = _rms(x_ref[...], g_ref[...]).astype(BF16)
        u_ref[...] = u
        os_ref[...] = _dot(u, ws_ref[...])

    o_ref[...] = _dot(u_ref[...], w_ref[...])


def _in_proj(x, g, w_main, w_small, layer, *, tm, tn):
    n, d = x.shape
    return pl.pallas_call(
        _in_proj_kernel,
        grid=(n // tm, MAIN_COLS // tn),
        in_specs=[pl.BlockSpec((tm, d), lambda i, j: (i, 0)),
                  pl.BlockSpec((1, d), lambda i, j: (0, 0)),
                  pl.BlockSpec((None, d, tn), lambda i, j: (layer, 0, j)),
                  pl.BlockSpec((None, d, SMALL_COLS), lambda i, j: (layer, 0, 0))],
        out_specs=[pl.BlockSpec((tm, tn), lambda i, j: (i, j)),
                   pl.BlockSpec((tm, SMALL_COLS), lambda i, j: (i, 0))],
        out_shape=[jax.ShapeDtypeStruct((n, MAIN_COLS), F32),
                   jax.ShapeDtypeStruct((n, SMALL_COLS), F32)],
        scratch_shapes=[pltpu.VMEM((tm, d), BF16)],
        compiler_params=pltpu.CompilerParams(
            dimension_semantics=("parallel", "arbitrary"), vmem_limit_bytes=VMEM_LIMIT),
        name="in_proj",
    )(x, g.reshape(1, d), w_main, w_small)


def _rg_gates(xc, wr_ref, br_ref, wi_ref, bi_ref, lam_ref, between=lambda: None):
    xcb = xc.astype(BF16)
    r = _sigmoid(_dot(xcb, wr_ref[...]) + br_ref[...])
    between()
    i = _sigmoid(_dot(xcb, wi_ref[...]) + bi_ref[...])
    between()
    log_a = (-RG_C * r) * _softplus(-lam_ref[...])
    a = jnp.exp(log_a)
    between()
    b = jnp.sqrt(_neg_expm1(2.0 * log_a)) * (i * xc)
    return a, b


RG_TILES = tuple(range(C_RX // PROJ_TILE, MIX_COLS // PROJ_TILE))
PROJ_TILE_ORDER = RG_TILES + tuple(
    cb for cb in range(MAIN_COLS // PROJ_TILE) if cb not in RG_TILES)


def _project_stage(x_ref, g_ref, wmain_ref, wsmall_ref, pm_ref, ps_ref, xs_ref):
    x = x_ref[...]
    xs_ref[...] = x
    u = _rms(x, g_ref[...]).astype(BF16)
    ps_ref[...] = _dot(u, wsmall_ref[...])
    for cb in PROJ_TILE_ORDER:
        cols = slice(cb * PROJ_TILE, (cb + 1) * PROJ_TILE)
        pm_ref[:, cols] = _dot(u, wmain_ref[:, cols])
        yield


def _mixer_stage(more_tiles, pm_ref, ps_ref,
                 wa2_ref, ba_ref, gnorm_ref, bif_ref, mnorm_ref,
                 convw_ref, convb_ref, wr_ref, br_ref, wi_ref, bi_ref, lam_ref,
                 s_ref, c_ref, n_ref, m_ref, h_ref, buf_ref,
                 br_out_ref, sg_ref, og_ref, xpad_ref):
    tt = pm_ref.shape[0]

    def more(n):
        for _ in range(n):
            next(more_tiles, None)

    x = pm_ref[:, C_RX:C_RX + RG_WIDTH]
    xpad_ref[pl.ds(8, tt), :] = x
    xc = (convb_ref[...] + convw_ref[3:4, :] * x
          + convw_ref[2:3, :] * xpad_ref[pl.ds(7, tt), :]
          + convw_ref[1:2, :] * xpad_ref[pl.ds(6, tt), :]
          + convw_ref[0:1, :] * xpad_ref[pl.ds(5, tt), :])
    buf_ref[...] = xpad_ref[pl.ds(tt + 8 - (RG_CONV - 1), RG_CONV - 1), :]
    xpad_ref[pl.ds(0, 8), :] = xpad_ref[pl.ds(tt, 8), :]
    more(2)
    a, b = _rg_gates(xc, wr_ref, br_ref, wi_ref, bi_ref, lam_ref, between=lambda: more(2))
    ridx = lax.broadcasted_iota(jnp.int32, (tt, RG_WIDTH), 0)
    sh = 1
    while sh < tt:
        more(2 if sh <= 8 else 1)
        keep = ridx >= sh
        a_sh = jnp.where(keep, pltpu.roll(a, sh, 0), 1.0)
        b_sh = jnp.where(keep, pltpu.roll(b, sh, 0), 0.0)
        b = a * b_sh + b
        a = a * a_sh
        sh *= 2
    hseq = a * h_ref[...] + b
    h_ref[...] = hseq[tt - 1:tt, :]
    more(2)
    y = hseq * _gelu_tanh(pm_ref[:, C_RG:C_RG + RG_WIDTH])
    br_out_ref[:, 2 * BRANCH_W:3 * BRANCH_W] = y.astype(BF16)
    for _ in more_tiles:
        pass

    ps = ps_ref[...]
    a_pre = _dot(ps.astype(BF16), wa2_ref[...]) + ba_ref[...]
    log_a = _log_sigmoid(a_pre) * (1.0 / GLA_GATE_NORM)
    tril32, full32 = _chunk_masks(tt, GLA_CHUNK)
    sums = _dot_mask_exact(jnp.concatenate([tril32, full32], axis=0).astype(BF16), log_a)
    b_cum = sums[:tt]
    b_end = sums[tt:]
    q = pm_ref[:, C_GQ:C_GQ + HEADS * DK] * (DK ** -0.5)
    k = pm_ref[:, C_GK:C_GK + HEADS * DK]
    qi_f = q * jnp.exp(b_cum)
    qi_all = qi_f.astype(BF16)
    ki_all = (k * jnp.exp(-b_cum)).astype(BF16)
    ke_t = (k * jnp.exp(b_end - b_cum)).T
    de_t = jnp.exp(b_end).T
    in_chunk = tril32 > 0.0
    n_chunks = tt // GLA_CHUNK
    cshift = GLA_CHUNK.bit_length() - 1
    chunk_of_lane = lax.shift_right_logical(lax.broadcasted_iota(jnp.int32, (DK, tt), 1), cshift)
    chunk_of_row = lax.shift_right_logical(lax.broadcasted_iota(jnp.int32, (tt, 2 * DK), 0), cshift)
    upper_half = (lax.broadcasted_iota(jnp.int32, (tt, 2 * DK), 1) >= DK).astype(jnp.int32)
    for h in range(HEADS):
        kcols = slice(h * DK, (h + 1) * DK)
        v = pm_ref[:, C_GV + h * DV:C_GV + (h + 1) * DV].astype(BF16)
        att = jnp.where(in_chunk, _dot_nt(qi_all[:, kcols], ki_all[:, kcols]), 0.0)
        o_intra = _dot(att.astype(BF16), v)
        ke_h = ke_t[kcols, :]
        ke_blk = jnp.concatenate(
            [jnp.where(chunk_of_lane == c, ke_h, 0.0) for c in range(n_chunks)], axis=0)
        upd = _dot(ke_blk.astype(BF16), v)
        s_cur = s_ref[h]
        s_start = []
        for c in range(n_chunks):
            s_start.append(s_cur)
            de_col = de_t[kcols, c * GLA_CHUNK:c * GLA_CHUNK + 1]
            s_cur = de_col * s_cur + upd[c * DK:(c + 1) * DK, :]
        s_ref[h] = s_cur
        qi_h = qi_f[:, kcols]
        qi_dup = jnp.concatenate([qi_h, qi_h], axis=1)
        q_blk = jnp.concatenate(
            [jnp.where(chunk_of_row == 2 * jj + upper_half, qi_dup, 0.0)
             for jj in range(n_chunks // 2)], axis=1)
        o_inter = _dot(q_blk.astype(BF16), jnp.concatenate(s_start, axis=0).astype(BF16))
        og_ref[:, h * DV:(h + 1) * DV] = o_intra + o_inter

    gnorm = gnorm_ref[...]
    for h in range(HEADS):
        cols = slice(h * DV, (h + 1) * DV)
        y = _rms(og_ref[:, cols], gnorm) * _silu(pm_ref[:, C_GG + h * DV:C_GG + (h + 1) * DV])
        br_out_ref[:, cols] = y.astype(BF16)

    pre = ps + bif_ref[...]
    rr = lax.broadcasted_iota(jnp.int32, (tt, tt), 0)
    cc = lax.broadcasted_iota(jnp.int32, (tt, tt), 1)
    causal = cc <= rr
    fcum = _dot_mask_exact(jnp.where(causal, 1.0, 0.0).astype(BF16), _log_sigmoid(pre))
    pre_t = pre.T
    fcum_t = fcum.T
    q_all = pm_ref[:, C_MQ:C_MQ + HEADS * DK] * (DK ** -0.5)
    k_all = pm_ref[:, C_MK:C_MK + HEADS * DK]
    for h in range(HEADS):
        kcols = slice(h * DK, (h + 1) * DK)
        fc_col = fcum[:, L_MF + h:L_MF + h + 1]
        ic_col = pre[:, L_MI + h:L_MI + h + 1]
        fc_row = fcum_t[L_MF + h:L_MF + h + 1, :]
        ic_row = pre_t[L_MI + h:L_MI + h + 1, :]
        m_prev = m_ref[:, h:h + 1]
        d = jnp.where(causal, fc_col + (ic_row - fc_row), -jnp.inf)
        inter = fc_col + m_prev
        m_t = jnp.maximum(inter, jnp.max(d, axis=1, keepdims=True))
        w_intra = jnp.exp(d - m_t)
        w_inter = jnp.exp(inter - m_t)
        qf = q_all[:, kcols]
        kf = k_all[:, kcols]
        qc = qf.astype(BF16)
        vc = pm_ref[:, C_MV + h * DV:C_MV + (h + 1) * DV].astype(BF16)
        s = _dot_nt(qc, kf.astype(BF16)) * w_intra
        c_old = c_ref[h]
        n_old = n_ref[h]
        num = w_inter * _dot(qc, c_old.astype(BF16)) + _dot(s.astype(BF16), vc)
        den = (w_inter * jnp.sum(qf * n_old, axis=1, keepdims=True)
               + jnp.sum(s, axis=1, keepdims=True))
        hh = num / jnp.maximum(jnp.abs(den), jnp.exp(-m_t))
        og_ref[:, h * DV:(h + 1) * DV] = hh
        f_end = fc_col[tt - 1:tt, :]
        m_end = m_t[tt - 1:tt, :]
        w_k = jnp.exp(f_end - fc_col + ic_col - m_end)
        w_c = jnp.exp(f_end + m_prev - m_end)
        kw = w_k * kf
        c_ref[h] = w_c * c_old + _dot_tn(kw.astype(BF16), vc)
        n_ref[h] = w_c * n_old + jnp.sum(kw, axis=0, keepdims=True)
        m_ref[:, h:h + 1] = m_end

    for h in range(HEADS):
        cols = slice(h * DV, (h + 1) * DV)
        y = (_rms(og_ref[:, cols], mnorm_ref[:, cols])
             * _sigmoid(pm_ref[:, C_MO + h * DV:C_MO + (h + 1) * DV]))
        br_out_ref[:, BRANCH_W + h * DV:BRANCH_W + (h + 1) * DV] = y.astype(BF16)

    for b in range(N_BRANCH):
        gcols = slice(b * D_MODEL, (b + 1) * D_MODEL)
        sg_ref[:, gcols] = _sigmoid(pm_ref[:, MIX_COLS + b * D_MODEL:MIX_COLS + (b + 1) * D_MODEL])


def _merge_stage(br_ref, sg_ref, xr_ref, wb_ref, wo_ref, xo_ref, mg_ref):
    n_tiles = D_MODEL // MERGE_TILE
    for jt in range(n_tiles):
        cols = slice(jt * MERGE_TILE, (jt + 1) * MERGE_TILE)
        merged = None
        for b in range(N_BRANCH):
            z = _dot(br_ref[:, b * BRANCH_W:(b + 1) * BRANCH_W], wb_ref[b, :, cols])
            term = sg_ref[:, b * D_MODEL + jt * MERGE_TILE:b * D_MODEL + (jt + 1) * MERGE_TILE] * z
            merged = term if merged is None else merged + term
        mg_ref[:, cols] = merged.astype(BF16)
    for jt in range(n_tiles):
        cols = slice(jt * MERGE_TILE, (jt + 1) * MERGE_TILE)
        xo_ref[:, cols] = xr_ref[:, cols] + _dot(mg_ref[...], wo_ref[:, cols])


N_MIXER_WEIGHTS = 12


def _mix_prompt_kernel(x_ref, g_ref, wmain_ref, wsmall_ref, *refs):
    mixer_w = refs[:N_MIXER_WEIGHTS]
    wb_ref, wo_ref = refs[N_MIXER_WEIGHTS:N_MIXER_WEIGHTS + 2]
    xo_ref, s_ref, c_ref, n_ref, m_ref, h_ref, buf_ref = refs[N_MIXER_WEIGHTS + 2:N_MIXER_WEIGHTS + 9]
    pm_ref, ps_ref, xs_ref, br_ref, sg_ref, mg_ref, og_ref, xpad_ref = refs[N_MIXER_WEIGHTS + 9:]

    @pl.when(pl.program_id(1) == 0)
    def _():
        for ref in (s_ref, c_ref, n_ref, m_ref, h_ref):
            ref[...] = jnp.zeros_like(ref)
        xpad_ref[pl.ds(0, 8), :] = jnp.zeros((8, RG_WIDTH), F32)

    tiles = _project_stage(x_ref, g_ref, wmain_ref, wsmall_ref, pm_ref, ps_ref, xs_ref)
    for _ in RG_TILES:
        next(tiles)
    _mixer_stage(tiles, pm_ref, ps_ref, *mixer_w, s_ref, c_ref, n_ref, m_ref, h_ref, buf_ref,
                 br_ref, sg_ref, og_ref, xpad_ref)
    _merge_stage(br_ref, sg_ref, xs_ref, wb_ref, wo_ref, xo_ref, mg_ref)


def _mix_prompt(x, g, wts, lw, layer, *, nb, t, tt):
    d = x.shape[1]
    const = lambda shape: pl.BlockSpec(shape, lambda b, j: (0,) * len(shape),
                                       pipeline_mode=pl.Buffered(1))
    slab = lambda shape: pl.BlockSpec((None,) + shape, lambda b, j: (layer,) + (0,) * len(shape),
                                      pipeline_mode=pl.Buffered(1))
    nt = t // tt
    return pl.pallas_call(
        _mix_prompt_kernel,
        grid=(nb, nt),
        in_specs=[pl.BlockSpec((tt, d), lambda b, j: (b * nt + j, 0)),
                  const((1, d)), slab((d, MAIN_COLS)), slab((d, SMALL_COLS)),
                  slab((SMALL_COLS, HEADS * DK)), const((1, HEADS * DK)), const((1, DV)),
                  const((1, SMALL_COLS)), const((1, BRANCH_W)),
                  const((RG_CONV, RG_WIDTH)), const((1, RG_WIDTH)),
                  slab((RG_WIDTH, RG_WIDTH)), const((1, RG_WIDTH)),
                  slab((RG_WIDTH, RG_WIDTH)), const((1, RG_WIDTH)), const((1, RG_WIDTH)),
                  slab((N_BRANCH, BRANCH_W, d)), slab((d, d))],
        out_specs=[pl.BlockSpec((tt, d), lambda b, j: (b * nt + j, 0)),
                   pl.BlockSpec((None, HEADS, DK, DV), lambda b, j: (b, 0, 0, 0)),
                   pl.BlockSpec((None, HEADS, DK, DV), lambda b, j: (b, 0, 0, 0)),
                   pl.BlockSpec((None, HEADS, 1, DK), lambda b, j: (b, 0, 0, 0)),
                   pl.BlockSpec((None, 1, HEADS), lambda b, j: (b, 0, 0)),
                   pl.BlockSpec((None, 1, RG_WIDTH), lambda b, j: (b, 0, 0)),
                   pl.BlockSpec((None, RG_CONV - 1, RG_WIDTH), lambda b, j: (b, 0, 0))],
        out_shape=[jax.ShapeDtypeStruct((nb * t, d), F32),
                   jax.ShapeDtypeStruct((nb, HEADS, DK, DV), F32),
                   jax.ShapeDtypeStruct((nb, HEADS, DK, DV), F32),
                   jax.ShapeDtypeStruct((nb, HEADS, 1, DK), F32),
                   jax.ShapeDtypeStruct((nb, 1, HEADS), F32),
                   jax.ShapeDtypeStruct((nb, 1, RG_WIDTH), F32),
                   jax.ShapeDtypeStruct((nb, RG_CONV - 1, RG_WIDTH), F32)],
        scratch_shapes=[pltpu.VMEM((tt, MAIN_COLS), F32),
                        pltpu.VMEM((tt, SMALL_COLS), F32),
                        pltpu.VMEM((tt, d), F32),
                        pltpu.VMEM((tt, N_BRANCH * BRANCH_W), BF16),
                        pltpu.VMEM((tt, GATE_COLS), F32),
                        pltpu.VMEM((tt, d), BF16),
                        pltpu.VMEM((tt, BRANCH_W), F32),
                        pltpu.VMEM((tt + 8, RG_WIDTH), F32)],
        compiler_params=pltpu.CompilerParams(
            dimension_semantics=("parallel", "arbitrary"), vmem_limit_bytes=VMEM_LIMIT),
        name="mix_prompt",
    )(x, g.reshape(1, d), wts["w_main"], wts["w_small"],
      wts["wa2"], lw["ba"], lw["gnorm"], lw["bif"], lw["mnorm"], lw["convw"], lw["convb"],
      wts["wr"], lw["br"], wts["wi"], lw["bi"], lw["lam"], wts["wb"], wts["wo"])


def _mix_sample_kernel(pm_ref, ps_ref, s0_ref, c0_ref, n0_ref, m0_ref, h0_ref, buf0_ref,
                       wa2_ref, ba_ref, gnorm_ref, bif_ref, mnorm_ref,
                       convw_ref, convb_ref, wr_ref, br_ref, wi_ref, bi_ref, lam_ref,
                       br_out_ref, s_ref, c_ref, n_ref, m_ref, h_ref, buf_ref,
                       og_ref, om_ref):
    bs = pm_ref.shape[0]
    ps = ps_ref[...]
    a_pre = _dot(ps.astype(BF16), wa2_ref[...]) + ba_ref[...]
    a = jnp.exp(_log_sigmoid(a_pre) * (1.0 / GLA_GATE_NORM))
    a_hi = a.astype(BF16)
    a_r1 = a - a_hi.astype(F32)
    a_mid = a_r1.astype(BF16)
    a_lo = (a_r1 - a_mid.astype(F32)).astype(BF16)
    mq = pm_ref[:, C_MQ:C_MQ + HEADS * DK] * (DK ** -0.5)
    mk = pm_ref[:, C_MK:C_MK + HEADS * DK]
    kd = HEADS * DK
    cols_t = jnp.concatenate(
        [(pm_ref[:, C_GQ:C_GQ + kd] * (DK ** -0.5)).T, pm_ref[:, C_GK:C_GK + kd].T,
         mq.T, mk.T, a_hi.astype(F32).T, a_mid.astype(F32).T, a_lo.astype(F32).T],
        axis=0).astype(BF16)
    o_gq, o_gk, o_mq, o_mk, o_a0, o_a1, o_a2 = (i * kd for i in range(7))
    sample_id = lax.broadcasted_iota(jnp.int32, (bs, DV), 0)
    pre = ps + bif_ref[...]
    lf = _log_sigmoid(pre)
    m0_all = m0_ref[...]
    head_lane = lax.broadcasted_iota(jnp.int32, (1, HEADS), 1)

    for s in range(bs):
        row = slice(s, s + 1)
        bc = _dot(cols_t, jnp.where(sample_id == s, 1.0, 0.0).astype(BF16))
        m_row = m0_all[row, :]
        for h in range(HEADS):
            kcols = slice(h * DK, (h + 1) * DK)
            col = lambda off: bc[off + h * DK:off + (h + 1) * DK, :]
            v_row = pm_ref[row, C_GV + h * DV:C_GV + (h + 1) * DV]
            a_col = col(o_a0) + col(o_a1) + col(o_a2)
            s_new = a_col * s0_ref[s, h] + col(o_gk) * v_row
            s_ref[s, h] = s_new
            og_ref[row, h * DV:(h + 1) * DV] = jnp.sum(col(o_gq) * s_new,
                                                       axis=0, keepdims=True)
            ic = pre[row, L_MI + h:L_MI + h + 1]
            fc = lf[row, L_MF + h:L_MF + h + 1]
            m0 = m0_all[row, h:h + 1]
            inter = fc + m0
            m_t = jnp.maximum(inter, ic)
            w_intra = jnp.exp(ic - m_t)
            w_inter = jnp.exp(inter - m_t)
            q_row = mq[row, kcols]
            k_row = mk[row, kcols]
            v_row = pm_ref[row, C_MV + h * DV:C_MV + (h + 1) * DV]
            sc = jnp.sum(q_row * k_row, axis=1, keepdims=True) * w_intra
            c_old = c0_ref[s, h]
            n_old = n0_ref[s, h]
            qc = jnp.sum(col(o_mq) * c_old, axis=0, keepdims=True)
            num = w_inter * qc + sc * v_row
            den = w_inter * jnp.sum(q_row * n_old, axis=1, keepdims=True) + sc
            om_ref[row, h * DV:(h + 1) * DV] = num / jnp.maximum(jnp.abs(den), jnp.exp(-m_t))
            c_ref[s, h] = w_inter * c_old + (w_intra * col(o_mk)) * v_row
            n_ref[s, h] = w_inter * n_old + w_intra * k_row
            m_row = jnp.where(head_lane == h, m_t, m_row)
        m_ref[row, :] = m_row

    gnorm = gnorm_ref[...]
    for h in range(HEADS):
        cols = slice(h * DV, (h + 1) * DV)
        y = _rms(og_ref[:, cols], gnorm) * _silu(pm_ref[:, C_GG + h * DV:C_GG + (h + 1) * DV])
        br_out_ref[:, cols] = y.astype(BF16)
        y = (_rms(om_ref[:, cols], mnorm_ref[:, cols])
             * _sigmoid(pm_ref[:, C_MO + h * DV:C_MO + (h + 1) * DV]))
        br_out_ref[:, BRANCH_W + h * DV:BRANCH_W + (h + 1) * DV] = y.astype(BF16)

    x = pm_ref[:, C_RX:C_RX + RG_WIDTH]
    xc = convb_ref[...] + convw_ref[3:4, :] * x
    for jj in range(RG_CONV - 1):
        xc = xc + convw_ref[jj:jj + 1, :] * buf0_ref[:, jj * RG_WIDTH:(jj + 1) * RG_WIDTH]
    buf_ref[:, 0:2 * RG_WIDTH] = buf0_ref[:, RG_WIDTH:3 * RG_WIDTH]
    buf_ref[:, 2 * RG_WIDTH:3 * RG_WIDTH] = x
    a, b = _rg_gates(xc, wr_ref, br_ref, wi_ref, bi_ref, lam_ref)
    hnew = a * h0_ref[...] + b
    h_ref[...] = hnew
    y = hnew * _gelu_tanh(pm_ref[:, C_RG:C_RG + RG_WIDTH])
    br_out_ref[:, 2 * BRANCH_W:3 * BRANCH_W] = y.astype(BF16)


def _mix_sample(pm, ps, st, wts, lw, layer, *, bs):
    n = pm.shape[0]
    const = lambda shape: pl.BlockSpec(shape, lambda i: (0,) * len(shape))
    slab = lambda shape: pl.BlockSpec((None,) + shape, lambda i: (layer,) + (0,) * len(shape))
    blk = lambda shape: pl.BlockSpec(shape, lambda i: (i,) + (0,) * (len(shape) - 1))
    lblk = lambda shape: pl.BlockSpec((None,) + shape,
                                      lambda i: (layer, i) + (0,) * (len(shape) - 1))
    s0, c0, n0, m0, h0, buf0 = st
    shapes = [(bs, HEADS, DK, DV), (bs, HEADS, DK, DV), (bs, HEADS, 1, DK),
              (bs, HEADS), (bs, RG_WIDTH), (bs, (RG_CONV - 1) * RG_WIDTH)]
    return pl.pallas_call(
        _mix_sample_kernel,
        grid=(n // bs,),
        in_specs=[blk((bs, MIX_COLS)), blk((bs, SMALL_COLS))] + [lblk(sh) for sh in shapes] + [
            slab((SMALL_COLS, HEADS * DK)), const((1, HEADS * DK)), const((1, DV)),
            const((1, SMALL_COLS)), const((1, BRANCH_W)),
            const((RG_CONV, RG_WIDTH)), const((1, RG_WIDTH)),
            slab((RG_WIDTH, RG_WIDTH)), const((1, RG_WIDTH)),
            slab((RG_WIDTH, RG_WIDTH)), const((1, RG_WIDTH)), const((1, RG_WIDTH))],
        out_specs=[blk((bs, N_BRANCH * BRANCH_W))] + [blk(sh) for sh in shapes],
        out_shape=[jax.ShapeDtypeStruct((n, N_BRANCH * BRANCH_W), BF16)]
        + [jax.ShapeDtypeStruct(a.shape[1:], F32) for a in st],
        scratch_shapes=[pltpu.VMEM((bs, BRANCH_W), F32),
                        pltpu.VMEM((bs, BRANCH_W), F32)],
        compiler_params=pltpu.CompilerParams(
            dimension_semantics=("parallel",), vmem_limit_bytes=VMEM_LIMIT),
        name="mix_sample",
    )(pm, ps, s0, c0, n0, m0, h0, buf0,
      wts["wa2"], lw["ba"], lw["gnorm"], lw["bif"], lw["mnorm"], lw["convw"], lw["convb"],
      wts["wr"], lw["br"], wts["wi"], lw["bi"], lw["lam"])


def _merge_kernel(x_ref, br_ref, g0_ref, g1_ref, g2_ref, wb_ref, wo_ref, o_ref):
    merged = None
    for b, g_ref in enumerate((g0_ref, g1_ref, g2_ref)):
        z = _dot(br_ref[:, b * BRANCH_W:(b + 1) * BRANCH_W], wb_ref[b])
        term = _sigmoid(g_ref[...]) * z
        merged = term if merged is None else merged + term
    o_ref[...] = x_ref[...] + _dot(merged.astype(BF16), wo_ref[...])


def _merge(x, br, pm, wb, wo, layer, *, tm):
    n, d = x.shape
    gate_blk0 = MIX_COLS // d
    gate_spec = lambda b: pl.BlockSpec((tm, d), lambda i: (i, gate_blk0 + b))
    return pl.pallas_call(
        _merge_kernel,
        grid=(n // tm,),
        in_specs=[pl.BlockSpec((tm, d), lambda i: (i, 0)),
                  pl.BlockSpec((tm, N_BRANCH * BRANCH_W), lambda i: (i, 0)),
                  gate_spec(0), gate_spec(1), gate_spec(2),
                  pl.BlockSpec((None, N_BRANCH, BRANCH_W, d), lambda i: (layer, 0, 0, 0)),
                  pl.BlockSpec((None, d, d), lambda i: (layer, 0, 0))],
        out_specs=pl.BlockSpec((tm, d), lambda i: (i, 0)),
        out_shape=jax.ShapeDtypeStruct((n, d), F32),
        compiler_params=pltpu.CompilerParams(
            dimension_semantics=("parallel",), vmem_limit_bytes=VMEM_LIMIT),
        name="merge",
    )(x, br, pm, pm, pm, wb, wo)


def _xattn_ffn_prompt_kernel(x_ref, gx_ref, wq_ref, wo_ref, k_ref, v_ref,
                             gf_ref, wg_ref, wu_ref, wd_ref, gfin_ref,
                             o_ref, kb_ref, vb_ref, hid_ref, *, final_norm):
    @pl.when(pl.program_id(1) == 0)
    def _():
        kb_ref[...] = k_ref[...].astype(BF16)
        vb_ref[...] = v_ref[...].astype(BF16)

    x = x_ref[...]
    q = _dot(_rms(x, gx_ref[...]).astype(BF16), wq_ref[...]).astype(BF16)
    heads = []
    for h in range(XA_HEADS):
        cols = slice(h * XA_HEAD_DIM, (h + 1) * XA_HEAD_DIM)
        s = _dot_nt(q[:, cols], kb_ref[:, cols]) * (XA_HEAD_DIM ** -0.5)
        p = jnp.exp(s - jnp.max(s, axis=-1, keepdims=True))
        p = p / jnp.sum(p, axis=-1, keepdims=True)
        heads.append(_dot(p.astype(BF16), vb_ref[:, cols]))
    o = jnp.concatenate(heads, axis=1).astype(BF16)
    x = x + _dot(o, wo_ref[...])
    o_ref[...] = _ffn_block(x, gf_ref, wg_ref, wu_ref, wd_ref, gfin_ref, hid_ref, final_norm)


def _xattn_ffn_prompt(x, gx, wq, wo, mem_k, mem_v, gf, wg, wu, wd, gfin, layer, *,
                      nb, t, tm, final_norm):
    d = x.shape[1]
    f = wg.shape[2]
    nt = t // tm
    const = lambda shape: pl.BlockSpec(shape, lambda b, j: (0,) * len(shape),
                                       pipeline_mode=pl.Buffered(1))
    slab = lambda shape: pl.BlockSpec((None,) + shape, lambda b, j: (layer,) + (0,) * len(shape),
                                      pipeline_mode=pl.Buffered(1))
    return pl.pallas_call(
        functools.partial(_xattn_ffn_prompt_kernel, final_norm=final_norm),
        grid=(nb, nt),
        in_specs=[pl.BlockSpec((tm, d), lambda b, j: (b * nt + j, 0)),
                  const((1, d)), slab((d, d)), slab((d, d)),
                  pl.BlockSpec((None, N_MEM, d), lambda b, j: (b, 0, 0)),
                  pl.BlockSpec((None, N_MEM, d), lambda b, j: (b, 0, 0)),
                  const((1, d)), slab((d, f)), slab((d, f)), slab((f, d)), const((1, d))],
        out_specs=pl.BlockSpec((tm, d), lambda b, j: (b * nt + j, 0)),
        out_shape=jax.ShapeDtypeStruct(x.shape, F32),
        scratch_shapes=[pltpu.VMEM((N_MEM, d), BF16), pltpu.VMEM((N_MEM, d), BF16),
                        pltpu.VMEM((tm, f), BF16)],
        compiler_params=pltpu.CompilerParams(
            dimension_semantics=("parallel", "arbitrary"), vmem_limit_bytes=VMEM_LIMIT),
        name="xattn_ffn_prompt",
    )(x, gx.reshape(1, d), wq, wo, mem_k, mem_v, gf.reshape(1, d), wg, wu, wd, gfin.reshape(1, d))


def _xattn_sample_kernel(q_ref, k_ref, v_ref, o_ref):
    bs = q_ref.shape[0]
    nrow = N_MEM * XA_HEADS
    row_head = lax.broadcasted_iota(jnp.int32, (XA_HEADS, nrow), 1) & (XA_HEADS - 1)
    own_head = row_head == lax.broadcasted_iota(jnp.int32, (XA_HEADS, nrow), 0)
    for s in range(bs):
        q = q_ref[s].astype(BF16)
        k2 = k_ref[s].reshape(nrow, XA_HEAD_DIM).astype(BF16)
        v2 = v_ref[s].reshape(nrow, XA_HEAD_DIM).astype(BF16)
        sc = jnp.where(own_head, _dot_nt(q, k2) * (XA_HEAD_DIM ** -0.5), -jnp.inf)
        p = jnp.exp(sc - jnp.max(sc, axis=1, keepdims=True))
        p = p / jnp.sum(p, axis=1, keepdims=True)
        o_ref[s] = _dot(p.astype(BF16), v2)


def _xattn_sample(q, cache_k, cache_v, layer, *, bs):
    n = q.shape[0]
    blk = (None, bs, N_MEM, XA_HEADS, XA_HEAD_DIM)
    return pl.pallas_call(
        _xattn_sample_kernel,
        grid=(n // bs,),
        in_specs=[pl.BlockSpec((bs, XA_HEADS, XA_HEAD_DIM), lambda i: (i, 0, 0)),
                  pl.BlockSpec(blk, lambda i: (layer, i, 0, 0, 0)),
                  pl.BlockSpec(blk, lambda i: (layer, i, 0, 0, 0))],
        out_specs=pl.BlockSpec((bs, XA_HEADS, XA_HEAD_DIM), lambda i: (i, 0, 0)),
        out_shape=jax.ShapeDtypeStruct((n, XA_HEADS, XA_HEAD_DIM), F32),
        compiler_params=pltpu.CompilerParams(
            dimension_semantics=("parallel",), vmem_limit_bytes=VMEM_LIMIT),
        name="xattn_sample",
    )(q, cache_k, cache_v)


def _matmul_residual_kernel(x_ref, a_ref, w_ref, o_ref):
    o_ref[...] = x_ref[...] + _dot(a_ref[...].astype(BF16), w_ref[...])


def _matmul_residual(x, a, w, layer):
    n, d = x.shape
    return pl.pallas_call(
        _matmul_residual_kernel,
        grid=(1,),
        in_specs=[pl.BlockSpec((n, d), lambda i: (0, 0)),
                  pl.BlockSpec((n, d), lambda i: (0, 0)),
                  pl.BlockSpec((None, d, d), lambda i: (layer, 0, 0))],
        out_specs=pl.BlockSpec((n, d), lambda i: (0, 0)),
        out_shape=jax.ShapeDtypeStruct((n, d), F32),
        compiler_params=pltpu.CompilerParams(vmem_limit_bytes=VMEM_LIMIT),
        name="matmul_residual",
    )(x, a, w)


def _ffn_block(x, g_ref, wg_ref, wu_ref, wd_ref, gfin_ref, hid_ref, final_norm):
    u = _rms(x, g_ref[...]).astype(BF16)
    for jt in range(D_FF // FFN_TILE):
        cols = slice(jt * FFN_TILE, (jt + 1) * FFN_TILE)
        hid = _silu(_dot(u, wg_ref[:, cols])) * _dot(u, wu_ref[:, cols])
        hid_ref[:, cols] = hid.astype(BF16)
    y = x + _dot(hid_ref[...], wd_ref[...])
    if final_norm:
        y = _rms(y, gfin_ref[...])
    return y


def _ffn_kernel(x_ref, g_ref, wg_ref, wu_ref, wd_ref, gf_ref, o_ref, hid_ref, *, final_norm):
    o_ref[...] = _ffn_block(x_ref[...], g_ref, wg_ref, wu_ref, wd_ref, gf_ref, hid_ref, final_norm)


def _ffn(x, g, wg, wu, wd, gf, layer, *, tm, final_norm):
    n, d = x.shape
    f = wg.shape[2]
    const = lambda shape: pl.BlockSpec(shape, lambda i: (0,) * len(shape),
                                       pipeline_mode=pl.Buffered(1))
    slab = lambda shape: pl.BlockSpec((None,) + shape, lambda i: (layer,) + (0,) * len(shape),
                                      pipeline_mode=pl.Buffered(1))
    return pl.pallas_call(
        functools.partial(_ffn_kernel, final_norm=final_norm),
        grid=(n // tm,),
        in_specs=[pl.BlockSpec((tm, d), lambda i: (i, 0)),
                  const((1, d)), slab((d, f)), slab((d, f)), slab((f, d)), const((1, d))],
        out_specs=pl.BlockSpec((tm, d), lambda i: (i, 0)),
        out_shape=jax.ShapeDtypeStruct((n, d), F32),
        scratch_shapes=[pltpu.VMEM((tm, f), BF16)],
        compiler_params=pltpu.CompilerParams(
            dimension_semantics=("parallel",), vmem_limit_bytes=VMEM_LIMIT),
        name="ffn",
    )(x, g.reshape(1, d), wg, wu, wd, gf.reshape(1, d))


def _split_w_in(w_in):
    widths = (HEADS * DK, HEADS * DK, HEADS * DV, HEADS * DV, GLA_RANK,
              HEADS * DK, HEADS * DK, HEADS * DV, HEADS * DV, HEADS, HEADS,
              RG_WIDTH, RG_WIDTH, GATE_COLS)
    parts, acc = [], 0
    for w in widths:
        parts.append(w_in[..., acc:acc + w])
        acc += w
    return parts


def _block_diag(w):
    eye = jnp.eye(RG_BLOCKS, dtype=w.dtype)
    return jnp.einsum("lnde,nm->lndme", w, eye).reshape(DEPTH, RG_WIDTH, RG_WIDTH)


def _matmul_weights(p):
    (g_q, g_k, g_v, g_g, g_a, m_q, m_k, m_v, m_o, m_i, m_f, r_x, r_g, gates) = _split_w_in(p["w_in"])
    d = D_MODEL
    w_main = jnp.concatenate([g_q, g_k, g_v, g_g, m_q, m_k, m_v, m_o, r_x, r_g, gates], axis=2)
    w_small = jnp.concatenate(
        [g_a, m_i, m_f, jnp.zeros((DEPTH, d, SMALL_COLS - GLA_RANK - 2 * HEADS), F32)], axis=2)
    wa2 = jnp.concatenate(
        [p["gla_w_a2"], jnp.zeros((DEPTH, SMALL_COLS - GLA_RANK, HEADS * DK), F32)], axis=1)
    return {
        "w_main": w_main.astype(BF16), "w_small": w_small.astype(BF16), "wa2": wa2.astype(BF16),
        "wr": _block_diag(p["rg_w_r"]).astype(BF16), "wi": _block_diag(p["rg_w_i"]).astype(BF16),
        "wb": p["w_branch"].astype(BF16), "wo": p["w_out"].astype(BF16),
        "xa_wq": p["xa_wq"].astype(BF16), "xa_wo": p["xa_wo"].astype(BF16),
        "xa_wkv": jnp.concatenate([p["xa_wk"], p["xa_wv"]], axis=2).astype(BF16),
        "wg": p["ffn_w_gate"].astype(BF16), "wu": p["ffn_w_up"].astype(BF16),
        "wd": p["ffn_w_down"].astype(BF16),
    }


def _vector_params(l, p):
    bif = jnp.concatenate(
        [jnp.zeros((L_MI,), F32), p["mlstm_b_i"][l], p["mlstm_b_f"][l],
         jnp.zeros((SMALL_COLS - L_MF - HEADS,), F32)]).reshape(1, SMALL_COLS)
    return {
        "ba": p["gla_b_a"][l].reshape(1, -1), "gnorm": p["gla_norm"][l].reshape(1, DV),
        "bif": bif, "mnorm": p["mlstm_norm"][l].reshape(1, BRANCH_W),
        "convw": p["rg_conv_w"][l], "convb": p["rg_conv_b"][l].reshape(1, -1),
        "br": p["rg_b_r"][l].reshape(1, -1), "bi": p["rg_b_i"][l].reshape(1, -1),
        "lam": p["rg_lambda"][l].reshape(1, -1),
    }


def kernel(x_prompt, x_sample, mem_prompt, cache_mem_k, cache_mem_v, state_gla, state_mlstm_c, state_mlstm_n, state_mlstm_m, state_rglru_h, state_rglru_conv, norm_mix, w_in, gla_w_a2, gla_b_a, gla_norm, mlstm_b_i, mlstm_b_f, mlstm_norm, rg_conv_w, rg_conv_b, rg_w_r, rg_b_r, rg_w_i, rg_b_i, rg_lambda, w_branch, w_out, norm_xa, norm_mem, xa_wq, xa_wk, xa_wv, xa_wo, norm_ffn, ffn_w_gate, ffn_w_up, ffn_w_down, norm_final):
    p = {"w_in": w_in, "gla_w_a2": gla_w_a2, "gla_b_a": gla_b_a, "gla_norm": gla_norm,
         "mlstm_b_i": mlstm_b_i, "mlstm_b_f": mlstm_b_f, "mlstm_norm": mlstm_norm,
         "rg_conv_w": rg_conv_w, "rg_conv_b": rg_conv_b, "rg_w_r": rg_w_r, "rg_b_r": rg_b_r,
         "rg_w_i": rg_w_i, "rg_b_i": rg_b_i, "rg_lambda": rg_lambda, "w_branch": w_branch,
         "w_out": w_out, "xa_wq": xa_wq, "xa_wk": xa_wk, "xa_wv": xa_wv, "xa_wo": xa_wo,
         "ffn_w_gate": ffn_w_gate, "ffn_w_up": ffn_w_up, "ffn_w_down": ffn_w_down}
    nb, t, d = x_prompt.shape
    ns = x_sample.shape[0]
    xp = x_prompt.reshape(nb * t, d)
    xs = x_sample.reshape(ns, d)
    mem = mem_prompt.reshape(nb * N_MEM, d)
    wts = _matmul_weights(p)

    st_sample = (state_gla, state_mlstm_c, state_mlstm_n.reshape(DEPTH, ns, HEADS, 1, DK),
                 state_mlstm_m, state_rglru_h,
                 state_rglru_conv.reshape(DEPTH, ns, (RG_CONV - 1) * RG_WIDTH))

    new_p = [[] for _ in range(8)]
    new_s = [[] for _ in range(6)]
    for l in range(DEPTH):
        lw = _vector_params(l, p)
        last = l == DEPTH - 1

        k_p, v_p = _norm_matmul(mem, norm_mem[l], wts["xa_wkv"], l, tm=512, tn=d)
        k_p = k_p.reshape(nb, N_MEM, d)
        v_p = v_p.reshape(nb, N_MEM, d)

        xp, g_s, c_s, n_s, m_s, h_s, buf_s = _mix_prompt(xp, norm_mix[l], wts, lw, l,
                                                         nb=nb, t=t, tt=256)
        xp = _xattn_ffn_prompt(xp, norm_xa[l], wts["xa_wq"], wts["xa_wo"], k_p, v_p,
                               norm_ffn[l], wts["wg"], wts["wu"], wts["wd"], norm_final, l,
                               nb=nb, t=t, tm=512, final_norm=last)
        for lst, val in zip(new_p, (k_p.reshape(nb, N_MEM, XA_HEADS, XA_HEAD_DIM),
                                    v_p.reshape(nb, N_MEM, XA_HEADS, XA_HEAD_DIM),
                                    g_s, c_s, n_s.reshape(nb, HEADS, DK), m_s.reshape(nb, HEADS),
                                    h_s.reshape(nb, RG_WIDTH), buf_s)):
            lst.append(val)

        pm, ps = _in_proj(xs, norm_mix[l], wts["w_main"], wts["w_small"], l, tm=ns, tn=1024)
        br, g_s, c_s, n_s, m_s, h_s, buf_s = _mix_sample(pm, ps, st_sample, wts, lw, l, bs=8)
        xs = _merge(xs, br, pm, wts["wb"], wts["wo"], l, tm=ns)
        (q,) = _norm_matmul(xs, norm_xa[l], wts["xa_wq"], l, tm=ns, tn=d)
        att = _xattn_sample(q.reshape(ns, XA_HEADS, XA_HEAD_DIM), cache_mem_k, cache_mem_v, l, bs=4)
        xs = _matmul_residual(xs, att.reshape(ns, d), wts["xa_wo"], l)
        xs = _ffn(xs, norm_ffn[l], wts["wg"], wts["wu"], wts["wd"], norm_final, l,
                  tm=ns, final_norm=last)
        for lst, val in zip(new_s, (g_s, c_s, n_s.reshape(ns, HEADS, DK), m_s, h_s,
                                    buf_s.reshape(ns, RG_CONV - 1, RG_WIDTH))):
            lst.append(val)

    y_prompt = xp.reshape(nb, t, d)
    y_sample = xs.reshape(ns, 1, d)
    outs_p = [jnp.stack(v, axis=0) for v in new_p]
    outs_s = [jnp.stack(v, axis=0) for v in new_s]
    return (y_prompt, y_sample, *outs_p, *outs_s)
```

```python
import functools
import math

import jax
import jax.numpy as jnp
from jax import lax
from jax.experimental import pallas as pl
from jax.experimental.pallas import tpu as pltpu

F32 = jnp.float32
BF16 = jnp.bfloat16

D_MODEL = 1024
DEPTH = 2
EPS = 1e-6
N_MEM = 256
XA_HEADS = 4
XA_HEAD_DIM = D_MODEL // XA_HEADS
N_BRANCH = 3
BRANCH_W = D_MODEL // 2
HEADS = 4
DK = 64
DV = 128
GLA_RANK = 16
GLA_GATE_NORM = 16.0
GLA_CHUNK = 32
MLSTM_CHUNK = 64
RG_WIDTH = BRANCH_W
RG_BLOCKS = 8
RG_BLOCK = RG_WIDTH // RG_BLOCKS
RG_CONV = 4
RG_C = 8.0
D_FF = 2816

C_GQ, C_GK, C_GV, C_GG = 0, 256, 512, 1024
C_MQ, C_MK, C_MV, C_MO = 1536, 1792, 2048, 2560
C_RX, C_RG = 3072, 3584
MIX_COLS = 4096
GATE_COLS = N_BRANCH * D_MODEL
MAIN_COLS = MIX_COLS + GATE_COLS
SMALL_COLS = 128
L_GA = 0
L_MI = 16
L_MF = 20

PROJ_TILE = 256
MERGE_TILE = 256
FFN_TILE = 256
VMEM_LIMIT = 48 * 1024 * 1024


def _softplus(x):
    return jnp.maximum(x, 0.0) + jnp.log1p(jnp.exp(-jnp.abs(x)))


def _log_sigmoid(x):
    return -_softplus(-x)


def _sigmoid(x):
    return 1.0 / (1.0 + jnp.exp(-x))


def _silu(x):
    return x * _sigmoid(x)


def _gelu_tanh(x):
    c = math.sqrt(2.0 / math.pi)
    return x * (0.5 * (1.0 + jnp.tanh(c * (x + 0.044715 * (x * x * x)))))


def _neg_expm1(x):
    return -jnp.tanh(0.5 * x) * (jnp.exp(x) + 1.0)


def _rms(x, g):
    ms = jnp.mean(x * x, axis=-1, keepdims=True)
    return x * lax.rsqrt(ms + EPS) * g


def _dot(a, b):
    return jnp.dot(a, b, preferred_element_type=F32)


def _dot_nt(a, b):
    return lax.dot_general(a, b, (((1,), (1,)), ((), ())), preferred_element_type=F32)


def _dot_tn(a, b):
    return lax.dot_general(a, b, (((0,), (0,)), ((), ())), preferred_element_type=F32)


def _dot_mask_exact(mask, x):
    hi = x.astype(BF16)
    rest = x - hi.astype(F32)
    mid = rest.astype(BF16)
    lo = (rest - mid.astype(F32)).astype(BF16)
    return _dot(mask, hi) + _dot(mask, mid) + _dot(mask, lo)


def _dot_mask_exact_rhs(x, mask):
    hi = x.astype(BF16)
    rest = x - hi.astype(F32)
    mid = rest.astype(BF16)
    lo = (rest - mid.astype(F32)).astype(BF16)
    return _dot(hi, mask) + _dot(mid, mask) + _dot(lo, mask)


def _chunk_masks(n, chunk):
    shift = chunk.bit_length() - 1
    r = lax.broadcasted_iota(jnp.int32, (n, n), 0)
    c = lax.broadcasted_iota(jnp.int32, (n, n), 1)
    same = lax.shift_right_logical(r, shift) == lax.shift_right_logical(c, shift)
    tril = jnp.where(same & (c <= r), 1.0, 0.0).astype(F32)
    full = jnp.where(same, 1.0, 0.0).astype(F32)
    return tril, full


def _norm_matmul_kernel(x_ref, g_ref, w_ref, *refs):
    out_refs, u_ref = refs[:-1], refs[-1]
    j = pl.program_id(1)

    @pl.when(j == 0)
    def _():
        u_ref[...] = _rms(x_ref[...], g_ref[...]).astype(BF16)

    res = _dot(u_ref[...], w_ref[...])
    for idx, o_ref in enumerate(out_refs):
        @pl.when(j == idx)
        def _(o_ref=o_ref):
            o_ref[...] = res


def _norm_matmul(x, g, w, layer, *, tm, tn):
    n, d = x.shape
    nout = w.shape[2]
    n_out = nout // tn
    return pl.pallas_call(
        _norm_matmul_kernel,
        grid=(n // tm, n_out),
        in_specs=[pl.BlockSpec((tm, d), lambda i, j: (i, 0)),
                  pl.BlockSpec((1, d), lambda i, j: (0, 0)),
                  pl.BlockSpec((None, d, tn), lambda i, j: (layer, 0, j))],
        out_specs=[pl.BlockSpec((tm, tn), lambda i, j: (i, 0))] * n_out,
        out_shape=[jax.ShapeDtypeStruct((n, tn), F32)] * n_out,
        scratch_shapes=[pltpu.VMEM((tm, d), BF16)],
        compiler_params=pltpu.CompilerParams(
            dimension_semantics=("parallel", "arbitrary"), vmem_limit_bytes=VMEM_LIMIT),
        name="norm_matmul",
    )(x, g.reshape(1, d), w)


def _in_proj_kernel(x_ref, g_ref, w_ref, ws_ref, o_ref, os_ref, u_ref):
    @pl.when(pl.program_id(1) == 0)
    def _():
        u = _rms(x_ref[...], g_ref[...]).astype(BF16)
        u_ref[...] = u
        os_ref[...] = _dot(u, ws_ref[...])

    o_ref[...] = _dot(u_ref[...], w_ref[...])


def _in_proj(x, g, w_main, w_small, layer, *, tm, tn):
    n, d = x.shape
    return pl.pallas_call(
        _in_proj_kernel,
        grid=(n // tm, MAIN_COLS // tn),
        in_specs=[pl.BlockSpec((tm, d), lambda i, j: (i, 0)),
                  pl.BlockSpec((1, d), lambda i, j: (0, 0)),
                  pl.BlockSpec((None, d, tn), lambda i, j: (layer, 0, j)),
                  pl.BlockSpec((None, d, SMALL_COLS), lambda i, j: (layer, 0, 0))],
        out_specs=[pl.BlockSpec((tm, tn), lambda i, j: (i, j)),
                   pl.BlockSpec((tm, SMALL_COLS), lambda i, j: (i, 0))],
        out_shape=[jax.ShapeDtypeStruct((n, MAIN_COLS), F32),
                   jax.ShapeDtypeStruct((n, SMALL_COLS), F32)],
        scratch_shapes=[pltpu.VMEM((tm, d), BF16)],
        compiler_params=pltpu.CompilerParams(
            dimension_semantics=("parallel", "arbitrary"), vmem_limit_bytes=VMEM_LIMIT),
        name="in_proj",
    )(x, g.reshape(1, d), w_main, w_small)


def _rg_gates(xc, wr_ref, br_ref, wi_ref, bi_ref, lam_ref):
    xcb = xc.astype(BF16)
    r = _sigmoid(_dot(xcb, wr_ref[...]) + br_ref[...])
    i = _sigmoid(_dot(xcb, wi_ref[...]) + bi_ref[...])
    log_a = (-RG_C * r) * _softplus(-lam_ref[...])
    a = jnp.exp(log_a)
    b = jnp.sqrt(_neg_expm1(2.0 * log_a)) * (i * xc)
    return a, b


def _project_stage(x_ref, g_ref, wmain_ref, wsmall_ref, pm_ref, ps_ref, xs_ref):
    x = x_ref[...]
    xs_ref[...] = x
    u = _rms(x, g_ref[...]).astype(BF16)
    ps_ref[...] = _dot(u, wsmall_ref[...])
    for cb in range(MAIN_COLS // PROJ_TILE):
        cols = slice(cb * PROJ_TILE, (cb + 1) * PROJ_TILE)
        pm_ref[:, cols] = _dot(u, wmain_ref[:, cols])


def _mixer_stage(pm_ref, ps_ref,
                 wa2_ref, ba_ref, gnorm_ref, bif_ref, mnorm_ref,
                 convw_ref, convb_ref, wr_ref, br_ref, wi_ref, bi_ref, lam_ref,
                 s_ref, c_ref, n_ref, m_ref, h_ref, buf_ref,
                 br_out_ref, sg_ref, og_ref, xpad_ref):
    tt = pm_ref.shape[0]

    x = pm_ref[:, C_RX:C_RX + RG_WIDTH]
    xpad_ref[pl.ds(8, tt), :] = x
    xc = (convb_ref[...] + convw_ref[3:4, :] * x
          + convw_ref[2:3, :] * xpad_ref[pl.ds(7, tt), :]
          + convw_ref[1:2, :] * xpad_ref[pl.ds(6, tt), :]
          + convw_ref[0:1, :] * xpad_ref[pl.ds(5, tt), :])
    buf_ref[...] = xpad_ref[pl.ds(tt + 8 - (RG_CONV - 1), RG_CONV - 1), :]
    xpad_ref[pl.ds(0, 8), :] = xpad_ref[pl.ds(tt, 8), :]
    a, b = _rg_gates(xc, wr_ref, br_ref, wi_ref, bi_ref, lam_ref)
    ridx = lax.broadcasted_iota(jnp.int32, (tt, RG_WIDTH), 0)
    sh = 1
    while sh < tt:
        keep = ridx >= sh
        a_sh = jnp.where(keep, pltpu.roll(a, sh, 0), 1.0)
        b_sh = jnp.where(keep, pltpu.roll(b, sh, 0), 0.0)
        b = a * b_sh + b
        a = a * a_sh
        sh *= 2
    hseq = a * h_ref[...] + b
    h_ref[...] = hseq[tt - 1:tt, :]
    y = hseq * _gelu_tanh(pm_ref[:, C_RG:C_RG + RG_WIDTH])
    br_out_ref[:, 2 * BRANCH_W:3 * BRANCH_W] = y.astype(BF16)

    ps = ps_ref[...]
    a_pre = _dot(ps.astype(BF16), wa2_ref[...]) + ba_ref[...]
    log_a = _log_sigmoid(a_pre) * (1.0 / GLA_GATE_NORM)
    tril32, full32 = _chunk_masks(tt, GLA_CHUNK)
    sums = _dot_mask_exact(jnp.concatenate([tril32, full32], axis=0).astype(BF16), log_a)
    b_cum = sums[:tt]
    b_end = sums[tt:]
    q = pm_ref[:, C_GQ:C_GQ + HEADS * DK] * (DK ** -0.5)
    k = pm_ref[:, C_GK:C_GK + HEADS * DK]
    qi_f = q * jnp.exp(b_cum)
    qi_all = qi_f.astype(BF16)
    ki_all = (k * jnp.exp(-b_cum)).astype(BF16)
    ke_t = (k * jnp.exp(b_end - b_cum)).T
    de_t = jnp.exp(b_end).T
    in_chunk = tril32 > 0.0
    n_chunks = tt // GLA_CHUNK
    cshift = GLA_CHUNK.bit_length() - 1
    chunk_of_lane = lax.shift_right_logical(lax.broadcasted_iota(jnp.int32, (DK, tt), 1), cshift)
    chunk_of_row = lax.shift_right_logical(lax.broadcasted_iota(jnp.int32, (tt, 2 * DK), 0), cshift)
    upper_half = (lax.broadcasted_iota(jnp.int32, (tt, 2 * DK), 1) >= DK).astype(jnp.int32)
    for h in range(HEADS):
        kcols = slice(h * DK, (h + 1) * DK)
        v = pm_ref[:, C_GV + h * DV:C_GV + (h + 1) * DV].astype(BF16)
        att = jnp.where(in_chunk, _dot_nt(qi_all[:, kcols], ki_all[:, kcols]), 0.0)
        o_intra = _dot(att.astype(BF16), v)
        ke_h = ke_t[kcols, :]
        ke_blk = jnp.concatenate(
            [jnp.where(chunk_of_lane == c, ke_h, 0.0) for c in range(n_chunks)], axis=0)
        upd = _dot(ke_blk.astype(BF16), v)
        s_cur = s_ref[h]
        s_start = []
        for c in range(n_chunks):
            s_start.append(s_cur)
            de_col = de_t[kcols, c * GLA_CHUNK:c * GLA_CHUNK + 1]
            s_cur = de_col * s_cur + upd[c * DK:(c + 1) * DK, :]
        s_ref[h] = s_cur
        qi_h = qi_f[:, kcols]
        qi_dup = jnp.concatenate([qi_h, qi_h], axis=1)
        q_blk = jnp.concatenate(
            [jnp.where(chunk_of_row == 2 * jj + upper_half, qi_dup, 0.0)
             for jj in range(n_chunks // 2)], axis=1)
        o_inter = _dot(q_blk.astype(BF16), jnp.concatenate(s_start, axis=0).astype(BF16))
        og_ref[:, h * DV:(h + 1) * DV] = o_intra + o_inter

    gnorm = gnorm_ref[...]
    for h in range(HEADS):
        cols = slice(h * DV, (h + 1) * DV)
        y = _rms(og_ref[:, cols], gnorm) * _silu(pm_ref[:, C_GG + h * DV:C_GG + (h + 1) * DV])
        br_out_ref[:, cols] = y.astype(BF16)

    pre = ps + bif_ref[...]
    rr = lax.broadcasted_iota(jnp.int32, (tt, tt), 0)
    cc = lax.broadcasted_iota(jnp.int32, (tt, tt), 1)
    causal = cc <= rr
    fcum = _dot_mask_exact(jnp.where(causal, 1.0, 0.0).astype(BF16), _log_sigmoid(pre))
    pre_t = pre.T
    fcum_t = fcum.T
    q_all = pm_ref[:, C_MQ:C_MQ + HEADS * DK] * (DK ** -0.5)
    k_all = pm_ref[:, C_MK:C_MK + HEADS * DK]
    for h in range(HEADS):
        kcols = slice(h * DK, (h + 1) * DK)
        fc_col = fcum[:, L_MF + h:L_MF + h + 1]
        ic_col = pre[:, L_MI + h:L_MI + h + 1]
        fc_row = fcum_t[L_MF + h:L_MF + h + 1, :]
        ic_row = pre_t[L_MI + h:L_MI + h + 1, :]
        m_prev = m_ref[:, h:h + 1]
        d = jnp.where(causal, fc_col + (ic_row - fc_row), -jnp.inf)
        inter = fc_col + m_prev
        m_t = jnp.maximum(inter, jnp.max(d, axis=1, keepdims=True))
        w_intra = jnp.exp(d - m_t)
        w_inter = jnp.exp(inter - m_t)
        qf = q_all[:, kcols]
        kf = k_all[:, kcols]
        qc = qf.astype(BF16)
        vc = pm_ref[:, C_MV + h * DV:C_MV + (h + 1) * DV].astype(BF16)
        s = _dot_nt(qc, kf.astype(BF16)) * w_intra
        c_old = c_ref[h]
        n_old = n_ref[h]
        num = w_inter * _dot(qc, c_old.astype(BF16)) + _dot(s.astype(BF16), vc)
        den = (w_inter * jnp.sum(qf * n_old, axis=1, keepdims=True)
               + jnp.sum(s, axis=1, keepdims=True))
        hh = num / jnp.maximum(jnp.abs(den), jnp.exp(-m_t))
        og_ref[:, h * DV:(h + 1) * DV] = hh
        f_end = fc_col[tt - 1:tt, :]
        m_end = m_t[tt - 1:tt, :]
        w_k = jnp.exp(f_end - fc_col + ic_col - m_end)
        w_c = jnp.exp(f_end + m_prev - m_end)
        kw = w_k * kf
        c_ref[h] = w_c * c_old + _dot_tn(kw.astype(BF16), vc)
        n_ref[h] = w_c * n_old + jnp.sum(kw, axis=0, keepdims=True)
        m_ref[:, h:h + 1] = m_end

    for h in range(HEADS):
        cols = slice(h * DV, (h + 1) * DV)
        y = (_rms(og_ref[:, cols], mnorm_ref[:, cols])
             * _sigmoid(pm_ref[:, C_MO + h * DV:C_MO + (h + 1) * DV]))
        br_out_ref[:, BRANCH_W + h * DV:BRANCH_W + (h + 1) * DV] = y.astype(BF16)

    for b in range(N_BRANCH):
        gcols = slice(b * D_MODEL, (b + 1) * D_MODEL)
        sg_ref[:, gcols] = _sigmoid(pm_ref[:, MIX_COLS + b * D_MODEL:MIX_COLS + (b + 1) * D_MODEL])


def _merge_stage(br_ref, sg_ref, xr_ref, wb_ref, wo_ref, xo_ref, mg_ref):
    n_tiles = D_MODEL // MERGE_TILE
    for jt in range(n_tiles):
        cols = slice(jt * MERGE_TILE, (jt + 1) * MERGE_TILE)
        merged = None
        for b in range(N_BRANCH):
            z = _dot(br_ref[:, b * BRANCH_W:(b + 1) * BRANCH_W], wb_ref[b, :, cols])
            term = sg_ref[:, b * D_MODEL + jt * MERGE_TILE:b * D_MODEL + (jt + 1) * MERGE_TILE] * z
            merged = term if merged is None else merged + term
        mg_ref[:, cols] = merged.astype(BF16)
    for jt in range(n_tiles):
        cols = slice(jt * MERGE_TILE, (jt + 1) * MERGE_TILE)
        xo_ref[:, cols] = xr_ref[:, cols] + _dot(mg_ref[...], wo_ref[:, cols])


N_MIXER_WEIGHTS = 12


def _mix_prompt_kernel(x_ref, g_ref, wmain_ref, wsmall_ref, *refs):
    mixer_w = refs[:N_MIXER_WEIGHTS]
    wb_ref, wo_ref = refs[N_MIXER_WEIGHTS:N_MIXER_WEIGHTS + 2]
    xo_ref, s_ref, c_ref, n_ref, m_ref, h_ref, buf_ref = refs[N_MIXER_WEIGHTS + 2:N_MIXER_WEIGHTS + 9]
    pm_ref, ps_ref, xs_ref, br_ref, sg_ref, mg_ref, og_ref, xpad_ref = refs[N_MIXER_WEIGHTS + 9:]

    @pl.when(pl.program_id(1) == 0)
    def _():
        for ref in (s_ref, c_ref, n_ref, m_ref, h_ref):
            ref[...] = jnp.zeros_like(ref)
        xpad_ref[pl.ds(0, 8), :] = jnp.zeros((8, RG_WIDTH), F32)

    _project_stage(x_ref, g_ref, wmain_ref, wsmall_ref, pm_ref, ps_ref, xs_ref)
    _mixer_stage(pm_ref, ps_ref, *mixer_w, s_ref, c_ref, n_ref, m_ref, h_ref, buf_ref,
                 br_ref, sg_ref, og_ref, xpad_ref)
    _merge_stage(br_ref, sg_ref, xs_ref, wb_ref, wo_ref, xo_ref, mg_ref)


def _mix_prompt(x, g, wts, lw, layer, *, nb, t, tt):
    d = x.shape[1]
    const = lambda shape: pl.BlockSpec(shape, lambda b, j: (0,) * len(shape),
                                       pipeline_mode=pl.Buffered(1))
    slab = lambda shape: pl.BlockSpec((None,) + shape, lambda b, j: (layer,) + (0,) * len(shape),
                                      pipeline_mode=pl.Buffered(1))
    nt = t // tt
    return pl.pallas_call(
        _mix_prompt_kernel,
        grid=(nb, nt),
        in_specs=[pl.BlockSpec((tt, d), lambda b, j: (b * nt + j, 0)),
                  const((1, d)), slab((d, MAIN_COLS)), slab((d, SMALL_COLS)),
                  slab((SMALL_COLS, HEADS * DK)), const((1, HEADS * DK)), const((1, DV)),
                  const((1, SMALL_COLS)), const((1, BRANCH_W)),
                  const((RG_CONV, RG_WIDTH)), const((1, RG_WIDTH)),
                  slab((RG_WIDTH, RG_WIDTH)), const((1, RG_WIDTH)),
                  slab((RG_WIDTH, RG_WIDTH)), const((1, RG_WIDTH)), const((1, RG_WIDTH)),
                  slab((N_BRANCH, BRANCH_W, d)), slab((d, d))],
        out_specs=[pl.BlockSpec((tt, d), lambda b, j: (b * nt + j, 0)),
                   pl.BlockSpec((None, HEADS, DK, DV), lambda b, j: (b, 0, 0, 0)),
                   pl.BlockSpec((None, HEADS, DK, DV), lambda b, j: (b, 0, 0, 0)),
                   pl.BlockSpec((None, HEADS, 1, DK), lambda b, j: (b, 0, 0, 0)),
                   pl.BlockSpec((None, 1, HEADS), lambda b, j: (b, 0, 0)),
                   pl.BlockSpec((None, 1, RG_WIDTH), lambda b, j: (b, 0, 0)),
                   pl.BlockSpec((None, RG_CONV - 1, RG_WIDTH), lambda b, j: (b, 0, 0))],
        out_shape=[jax.ShapeDtypeStruct((nb * t, d), F32),
                   jax.ShapeDtypeStruct((nb, HEADS, DK, DV), F32),
                   jax.ShapeDtypeStruct((nb, HEADS, DK, DV), F32),
                   jax.ShapeDtypeStruct((nb, HEADS, 1, DK), F32),
                   jax.ShapeDtypeStruct((nb, 1, HEADS), F32),
                   jax.ShapeDtypeStruct((nb, 1, RG_WIDTH), F32),
                   jax.ShapeDtypeStruct((nb, RG_CONV - 1, RG_WIDTH), F32)],
        scratch_shapes=[pltpu.VMEM((tt, MAIN_COLS), F32),
                        pltpu.VMEM((tt, SMALL_COLS), F32),
                        pltpu.VMEM((tt, d), F32),
                        pltpu.VMEM((tt, N_BRANCH * BRANCH_W), BF16),
                        pltpu.VMEM((tt, GATE_COLS), F32),
                        pltpu.VMEM((tt, d), BF16),
                        pltpu.VMEM((tt, BRANCH_W), F32),
                        pltpu.VMEM((tt + 8, RG_WIDTH), F32)],
        compiler_params=pltpu.CompilerParams(
            dimension_semantics=("parallel", "arbitrary"), vmem_limit_bytes=VMEM_LIMIT),
        name="mix_prompt",
    )(x, g.reshape(1, d), wts["w_main"], wts["w_small"],
      wts["wa2"], lw["ba"], lw["gnorm"], lw["bif"], lw["mnorm"], lw["convw"], lw["convb"],
      wts["wr"], lw["br"], wts["wi"], lw["bi"], lw["lam"], wts["wb"], wts["wo"])


def _mix_sample_kernel(pm_ref, ps_ref, s0_ref, c0_ref, n0_ref, m0_ref, h0_ref, buf0_ref,
                       wa2_ref, ba_ref, gnorm_ref, bif_ref, mnorm_ref,
                       convw_ref, convb_ref, wr_ref, br_ref, wi_ref, bi_ref, lam_ref,
                       br_out_ref, s_ref, c_ref, n_ref, m_ref, h_ref, buf_ref,
                       og_ref, om_ref):
    bs = pm_ref.shape[0]
    ps = ps_ref[...]
    a_pre = _dot(ps.astype(BF16), wa2_ref[...]) + ba_ref[...]
    a = jnp.exp(_log_sigmoid(a_pre) * (1.0 / GLA_GATE_NORM))
    a_hi = a.astype(BF16)
    a_r1 = a - a_hi.astype(F32)
    a_mid = a_r1.astype(BF16)
    a_lo = (a_r1 - a_mid.astype(F32)).astype(BF16)
    mq = pm_ref[:, C_MQ:C_MQ + HEADS * DK] * (DK ** -0.5)
    mk = pm_ref[:, C_MK:C_MK + HEADS * DK]
    kd = HEADS * DK

    pre = ps + bif_ref[...]
    f_log = pltpu.roll(_log_sigmoid(pre), SMALL_COLS - (L_MF - L_MI), 1)
    inter = f_log + m0_ref[...]
    m_t = jnp.maximum(inter, pre)
    lane = lax.broadcasted_iota(jnp.int32, (bs, SMALL_COLS), 1)
    is_head = (lane >= L_MI) & (lane < L_MI + HEADS)
    w_intra = jnp.where(is_head, jnp.exp(pre - m_t), 0.0)
    w_inter = jnp.where(is_head, jnp.exp(inter - m_t), 0.0)
    kd_row = lax.broadcasted_iota(jnp.int32, (kd, SMALL_COLS), 0)
    head_sum = jnp.where(
        lax.shift_right_logical(kd_row, DK.bit_length() - 1) + L_MI
        == lax.broadcasted_iota(jnp.int32, (kd, SMALL_COLS), 1), 1.0, 0.0).astype(BF16)
    sc = _dot_mask_exact_rhs(mq * mk, head_sum) * w_intra
    den = w_inter * _dot_mask_exact_rhs(mq * n0_ref[...], head_sum) + sc
    dmax = jnp.where(is_head, jnp.maximum(jnp.abs(den), jnp.exp(-m_t)), 1.0)
    m_ref[...] = m_t
    lane_head = lax.broadcasted_iota(jnp.int32, (SMALL_COLS, kd), 0) - L_MI
    head_spread = jnp.where(
        lane_head == lax.shift_right_logical(
            lax.broadcasted_iota(jnp.int32, (SMALL_COLS, kd), 1), DK.bit_length() - 1),
        1.0, 0.0).astype(BF16)
    n_ref[...] = (_dot_mask_exact_rhs(w_inter, head_spread) * n0_ref[...]
                  + _dot_mask_exact_rhs(w_intra, head_spread) * mk)

    scalars = jnp.concatenate([w_inter, w_intra, sc, dmax], axis=1)
    sc_hi = scalars.astype(BF16)
    sc_r1 = scalars - sc_hi.astype(F32)
    sc_mid = sc_r1.astype(BF16)
    sc_lo = (sc_r1 - sc_mid.astype(F32)).astype(BF16)
    n_scalar = 4

    def scalar_rows(piece):
        pt = piece.astype(F32).T
        return jnp.concatenate(
            [pt[i * SMALL_COLS + L_MI:i * SMALL_COLS + L_MI + 8, :] for i in range(n_scalar)],
            axis=0)

    cols_t = jnp.concatenate(
        [(pm_ref[:, C_GQ:C_GQ + kd] * (DK ** -0.5)).T, pm_ref[:, C_GK:C_GK + kd].T,
         mq.T, mk.T, a_hi.astype(F32).T, a_mid.astype(F32).T, a_lo.astype(F32).T,
         scalar_rows(sc_hi), scalar_rows(sc_mid), scalar_rows(sc_lo)],
        axis=0).astype(BF16)
    o_gq, o_gk, o_mq, o_mk, o_a0, o_a1, o_a2 = (i * kd for i in range(7))
    o_scalar = 7 * kd
    sample_id = lax.broadcasted_iota(jnp.int32, (bs, DV), 0)

    for s in range(bs):
        row = slice(s, s + 1)
        bc = _dot(cols_t, jnp.where(sample_id == s, 1.0, 0.0).astype(BF16))
        for h in range(HEADS):
            col = lambda off: bc[off + h * DK:off + (h + 1) * DK, :]

            def scalar(i):
                r0 = o_scalar + i * 8 + h
                return (bc[r0:r0 + 1, :] + bc[r0 + 8 * n_scalar:r0 + 8 * n_scalar + 1, :]
                        + bc[r0 + 16 * n_scalar:r0 + 16 * n_scalar + 1, :])

            v_row = pm_ref[row, C_GV + h * DV:C_GV + (h + 1) * DV]
            a_col = col(o_a0) + col(o_a1) + col(o_a2)
            s_new = a_col * s0_ref[s, h] + col(o_gk) * v_row
            s_ref[s, h] = s_new
            og_ref[row, h * DV:(h + 1) * DV] = jnp.sum(col(o_gq) * s_new,
                                                       axis=0, keepdims=True)
            v_row = pm_ref[row, C_MV + h * DV:C_MV + (h + 1) * DV]
            c_old = c0_ref[s, h]
            qc = jnp.sum(col(o_mq) * c_old, axis=0, keepdims=True)
            om_ref[row, h * DV:(h + 1) * DV] = (scalar(0) * qc + scalar(2) * v_row) / scalar(3)
            c_ref[s, h] = scalar(0) * c_old + (scalar(1) * col(o_mk)) * v_row

    gnorm = gnorm_ref[...]
    for h in range(HEADS):
        cols = slice(h * DV, (h + 1) * DV)
        y = _rms(og_ref[:, cols], gnorm) * _silu(pm_ref[:, C_GG + h * DV:C_GG + (h + 1) * DV])
        br_out_ref[:, cols] = y.astype(BF16)
        y = (_rms(om_ref[:, cols], mnorm_ref[:, cols])
             * _sigmoid(pm_ref[:, C_MO + h * DV:C_MO + (h + 1) * DV]))
        br_out_ref[:, BRANCH_W + h * DV:BRANCH_W + (h + 1) * DV] = y.astype(BF16)

    x = pm_ref[:, C_RX:C_RX + RG_WIDTH]
    xc = convb_ref[...] + convw_ref[3:4, :] * x
    for jj in range(RG_CONV - 1):
        xc = xc + convw_ref[jj:jj + 1, :] * buf0_ref[:, jj * RG_WIDTH:(jj + 1) * RG_WIDTH]
    buf_ref[:, 0:2 * RG_WIDTH] = buf0_ref[:, RG_WIDTH:3 * RG_WIDTH]
    buf_ref[:, 2 * RG_WIDTH:3 * RG_WIDTH] = x
    a, b = _rg_gates(xc, wr_ref, br_ref, wi_ref, bi_ref, lam_ref)
    hnew = a * h0_ref[...] + b
    h_ref[...] = hnew
    y = hnew * _gelu_tanh(pm_ref[:, C_RG:C_RG + RG_WIDTH])
    br_out_ref[:, 2 * BRANCH_W:3 * BRANCH_W] = y.astype(BF16)


def _mix_sample(pm, ps, st, wts, lw, layer, *, bs):
    n = pm.shape[0]
    const = lambda shape: pl.BlockSpec(shape, lambda i: (0,) * len(shape))
    slab = lambda shape: pl.BlockSpec((None,) + shape, lambda i: (layer,) + (0,) * len(shape))
    blk = lambda shape: pl.BlockSpec(shape, lambda i: (i,) + (0,) * (len(shape) - 1))
    lblk = lambda shape: pl.BlockSpec((None,) + shape,
                                      lambda i: (layer, i) + (0,) * (len(shape) - 1))
    s0, c0, n0, m0, h0, buf0 = st
    shapes = [(bs, HEADS, DK, DV), (bs, HEADS, DK, DV), (bs, HEADS * DK),
              (bs, SMALL_COLS), (bs, RG_WIDTH), (bs, (RG_CONV - 1) * RG_WIDTH)]
    return pl.pallas_call(
        _mix_sample_kernel,
        grid=(n // bs,),
        in_specs=[blk((bs, MIX_COLS)), blk((bs, SMALL_COLS))] + [lblk(sh) for sh in shapes] + [
            slab((SMALL_COLS, HEADS * DK)), const((1, HEADS * DK)), const((1, DV)),
            const((1, SMALL_COLS)), const((1, BRANCH_W)),
            const((RG_CONV, RG_WIDTH)), const((1, RG_WIDTH)),
            slab((RG_WIDTH, RG_WIDTH)), const((1, RG_WIDTH)),
            slab((RG_WIDTH, RG_WIDTH)), const((1, RG_WIDTH)), const((1, RG_WIDTH))],
        out_specs=[blk((bs, N_BRANCH * BRANCH_W))] + [blk(sh) for sh in shapes],
        out_shape=[jax.ShapeDtypeStruct((n, N_BRANCH * BRANCH_W), BF16)]
        + [jax.ShapeDtypeStruct(a.shape[1:], F32) for a in st],
        scratch_shapes=[pltpu.VMEM((bs, BRANCH_W), F32),
                        pltpu.VMEM((bs, BRANCH_W), F32)],
        compiler_params=pltpu.CompilerParams(
            dimension_semantics=("parallel",), vmem_limit_bytes=VMEM_LIMIT),
        name="mix_sample",
    )(pm, ps, s0, c0, n0, m0, h0, buf0,
      wts["wa2"], lw["ba"], lw["gnorm"], lw["bif"], lw["mnorm"], lw["convw"], lw["convb"],
      wts["wr"], lw["br"], wts["wi"], lw["bi"], lw["lam"])


def _merge_kernel(x_ref, br_ref, g0_ref, g1_ref, g2_ref, wb_ref, wo_ref, o_ref):
    merged = None
    for b, g_ref in enumerate((g0_ref, g1_ref, g2_ref)):
        z = _dot(br_ref[:, b * BRANCH_W:(b + 1) * BRANCH_W], wb_ref[b])
        term = _sigmoid(g_ref[...]) * z
        merged = term if merged is None else merged + term
    o_ref[...] = x_ref[...] + _dot(merged.astype(BF16), wo_ref[...])


def _merge(x, br, pm, wb, wo, layer, *, tm):
    n, d = x.shape
    gate_blk0 = MIX_COLS // d
    gate_spec = lambda b: pl.BlockSpec((tm, d), lambda i: (i, gate_blk0 + b))
    return pl.pallas_call(
        _merge_kernel,
        grid=(n // tm,),
        in_specs=[pl.BlockSpec((tm, d), lambda i: (i, 0)),
                  pl.BlockSpec((tm, N_BRANCH * BRANCH_W), lambda i: (i, 0)),
                  gate_spec(0), gate_spec(1), gate_spec(2),
                  pl.BlockSpec((None, N_BRANCH, BRANCH_W, d), lambda i: (layer, 0, 0, 0)),
                  pl.BlockSpec((None, d, d), lambda i: (layer, 0, 0))],
        out_specs=pl.BlockSpec((tm, d), lambda i: (i, 0)),
        out_shape=jax.ShapeDtypeStruct((n, d), F32),
        compiler_params=pltpu.CompilerParams(
            dimension_semantics=("parallel",), vmem_limit_bytes=VMEM_LIMIT),
        name="merge",
    )(x, br, pm, pm, pm, wb, wo)


def _xattn_ffn_prompt_kernel(x_ref, gx_ref, wq_ref, wo_ref, k_ref, v_ref,
                             gf_ref, wg_ref, wu_ref, wd_ref, gfin_ref,
                             o_ref, kb_ref, vb_ref, hid_ref, *, final_norm):
    @pl.when(pl.program_id(1) == 0)
    def _():
        kb_ref[...] = k_ref[...].astype(BF16)
        vb_ref[...] = v_ref[...].astype(BF16)

    x = x_ref[...]
    q = _dot(_rms(x, gx_ref[...]).astype(BF16), wq_ref[...]).astype(BF16)
    heads = []
    for h in range(XA_HEADS):
        cols = slice(h * XA_HEAD_DIM, (h + 1) * XA_HEAD_DIM)
        s = _dot_nt(q[:, cols], kb_ref[:, cols]) * (XA_HEAD_DIM ** -0.5)
        p = jnp.exp(s - jnp.max(s, axis=-1, keepdims=True))
        p = p / jnp.sum(p, axis=-1, keepdims=True)
        heads.append(_dot(p.astype(BF16), vb_ref[:, cols]))
    o = jnp.concatenate(heads, axis=1).astype(BF16)
    x = x + _dot(o, wo_ref[...])
    o_ref[...] = _ffn_block(x, gf_ref, wg_ref, wu_ref, wd_ref, gfin_ref, hid_ref, final_norm)


def _xattn_ffn_prompt(x, gx, wq, wo, mem_k, mem_v, gf, wg, wu, wd, gfin, layer, *,
                      nb, t, tm, final_norm):
    d = x.shape[1]
    f = wg.shape[2]
    nt = t // tm
    const = lambda shape: pl.BlockSpec(shape, lambda b, j: (0,) * len(shape),
                                       pipeline_mode=pl.Buffered(1))
    slab = lambda shape: pl.BlockSpec((None,) + shape, lambda b, j: (layer,) + (0,) * len(shape),
                                      pipeline_mode=pl.Buffered(1))
    return pl.pallas_call(
        functools.partial(_xattn_ffn_prompt_kernel, final_norm=final_norm),
        grid=(nb, nt),
        in_specs=[pl.BlockSpec((tm, d), lambda b, j: (b * nt + j, 0)),
                  const((1, d)), slab((d, d)), slab((d, d)),
                  pl.BlockSpec((None, N_MEM, d), lambda b, j: (b, 0, 0)),
                  pl.BlockSpec((None, N_MEM, d), lambda b, j: (b, 0, 0)),
                  const((1, d)), slab((d, f)), slab((d, f)), slab((f, d)), const((1, d))],
        out_specs=pl.BlockSpec((tm, d), lambda b, j: (b * nt + j, 0)),
        out_shape=jax.ShapeDtypeStruct(x.shape, F32),
        scratch_shapes=[pltpu.VMEM((N_MEM, d), BF16), pltpu.VMEM((N_MEM, d), BF16),
                        pltpu.VMEM((tm, f), BF16)],
        compiler_params=pltpu.CompilerParams(
            dimension_semantics=("parallel", "arbitrary"), vmem_limit_bytes=VMEM_LIMIT),
        name="xattn_ffn_prompt",
    )(x, gx.reshape(1, d), wq, wo, mem_k, mem_v, gf.reshape(1, d), wg, wu, wd, gfin.reshape(1, d))


def _xattn_sample_kernel(q_ref, k_ref, v_ref, o_ref):
    bs = q_ref.shape[0]
    nrow = N_MEM * XA_HEADS
    row_head = lax.broadcasted_iota(jnp.int32, (XA_HEADS, nrow), 1) & (XA_HEADS - 1)
    own_head = row_head == lax.broadcasted_iota(jnp.int32, (XA_HEADS, nrow), 0)
    for s in range(bs):
        q = q_ref[s].astype(BF16)
        k2 = k_ref[s].reshape(nrow, XA_HEAD_DIM).astype(BF16)
        v2 = v_ref[s].reshape(nrow, XA_HEAD_DIM).astype(BF16)
        sc = jnp.where(own_head, _dot_nt(q, k2) * (XA_HEAD_DIM ** -0.5), -jnp.inf)
        p = jnp.exp(sc - jnp.max(sc, axis=1, keepdims=True))
        p = p / jnp.sum(p, axis=1, keepdims=True)
        o_ref[s] = _dot(p.astype(BF16), v2)


def _xattn_sample(q, cache_k, cache_v, layer, *, bs):
    n = q.shape[0]
    blk = (None, bs, N_MEM, XA_HEADS, XA_HEAD_DIM)
    return pl.pallas_call(
        _xattn_sample_kernel,
        grid=(n // bs,),
        in_specs=[pl.BlockSpec((bs, XA_HEADS, XA_HEAD_DIM), lambda i: (i, 0, 0)),
                  pl.BlockSpec(blk, lambda i: (layer, i, 0, 0, 0)),
                  pl.BlockSpec(blk, lambda i: (layer, i, 0, 0, 0))],
        out_specs=pl.BlockSpec((bs, XA_HEADS, XA_HEAD_DIM), lambda i: (i, 0, 0)),
        out_shape=jax.ShapeDtypeStruct((n, XA_HEADS, XA_HEAD_DIM), F32),
        compiler_params=pltpu.CompilerParams(
            dimension_semantics=("parallel",), vmem_limit_bytes=VMEM_LIMIT),
        name="xattn_sample",
    )(q, cache_k, cache_v)


def _matmul_residual_kernel(x_ref, a_ref, w_ref, o_ref):
    o_ref[...] = x_ref[...] + _dot(a_ref[...].astype(BF16), w_ref[...])


def _matmul_residual(x, a, w, layer):
    n, d = x.shape
    return pl.pallas_call(
        _matmul_residual_kernel,
        grid=(1,),
        in_specs=[pl.BlockSpec((n, d), lambda i: (0, 0)),
                  pl.BlockSpec((n, d), lambda i: (0, 0)),
                  pl.BlockSpec((None, d, d), lambda i: (layer, 0, 0))],
        out_specs=pl.BlockSpec((n, d), lambda i: (0, 0)),
        out_shape=jax.ShapeDtypeStruct((n, d), F32),
        compiler_params=pltpu.CompilerParams(vmem_limit_bytes=VMEM_LIMIT),
        name="matmul_residual",
    )(x, a, w)


def _ffn_block(x, g_ref, wg_ref, wu_ref, wd_ref, gfin_ref, hid_ref, final_norm):
    u = _rms(x, g_ref[...]).astype(BF16)
    for jt in range(D_FF // FFN_TILE):
        cols = slice(jt * FFN_TILE, (jt + 1) * FFN_TILE)
        hid = _silu(_dot(u, wg_ref[:, cols])) * _dot(u, wu_ref[:, cols])
        hid_ref[:, cols] = hid.astype(BF16)
    y = x + _dot(hid_ref[...], wd_ref[...])
    if final_norm:
        y = _rms(y, gfin_ref[...])
    return y


def _ffn_kernel(x_ref, g_ref, wg_ref, wu_ref, wd_ref, gf_ref, o_ref, hid_ref, *, final_norm):
    o_ref[...] = _ffn_block(x_ref[...], g_ref, wg_ref, wu_ref, wd_ref, gf_ref, hid_ref, final_norm)


def _ffn(x, g, wg, wu, wd, gf, layer, *, tm, final_norm):
    n, d = x.shape
    f = wg.shape[2]
    const = lambda shape: pl.BlockSpec(shape, lambda i: (0,) * len(shape),
                                       pipeline_mode=pl.Buffered(1))
    slab = lambda shape: pl.BlockSpec((None,) + shape, lambda i: (layer,) + (0,) * len(shape),
                                      pipeline_mode=pl.Buffered(1))
    return pl.pallas_call(
        functools.partial(_ffn_kernel, final_norm=final_norm),
        grid=(n // tm,),
        in_specs=[pl.BlockSpec((tm, d), lambda i: (i, 0)),
                  const((1, d)), slab((d, f)), slab((d, f)), slab((f, d)), const((1, d))],
        out_specs=pl.BlockSpec((tm, d), lambda i: (i, 0)),
        out_shape=jax.ShapeDtypeStruct((n, d), F32),
        scratch_shapes=[pltpu.VMEM((tm, f), BF16)],
        compiler_params=pltpu.CompilerParams(
            dimension_semantics=("parallel",), vmem_limit_bytes=VMEM_LIMIT),
        name="ffn",
    )(x, g.reshape(1, d), wg, wu, wd, gf.reshape(1, d))


def _split_w_in(w_in):
    widths = (HEADS * DK, HEADS * DK, HEADS * DV, HEADS * DV, GLA_RANK,
              HEADS * DK, HEADS * DK, HEADS * DV, HEADS * DV, HEADS, HEADS,
              RG_WIDTH, RG_WIDTH, GATE_COLS)
    parts, acc = [], 0
    for w in widths:
        parts.append(w_in[..., acc:acc + w])
        acc += w
    return parts


def _block_diag(w):
    eye = jnp.eye(RG_BLOCKS, dtype=w.dtype)
    return jnp.einsum("lnde,nm->lndme", w, eye).reshape(DEPTH, RG_WIDTH, RG_WIDTH)


def _matmul_weights(p):
    (g_q, g_k, g_v, g_g, g_a, m_q, m_k, m_v, m_o, m_i, m_f, r_x, r_g, gates) = _split_w_in(p["w_in"])
    d = D_MODEL
    w_main = jnp.concatenate([g_q, g_k, g_v, g_g, m_q, m_k, m_v, m_o, r_x, r_g, gates], axis=2)
    w_small = jnp.concatenate(
        [g_a, m_i, m_f, jnp.zeros((DEPTH, d, SMALL_COLS - GLA_RANK - 2 * HEADS), F32)], axis=2)
    wa2 = jnp.concatenate(
        [p["gla_w_a2"], jnp.zeros((DEPTH, SMALL_COLS - GLA_RANK, HEADS * DK), F32)], axis=1)
    return {
        "w_main": w_main.astype(BF16), "w_small": w_small.astype(BF16), "wa2": wa2.astype(BF16),
        "wr": _block_diag(p["rg_w_r"]).astype(BF16), "wi": _block_diag(p["rg_w_i"]).astype(BF16),
        "wb": p["w_branch"].astype(BF16), "wo": p["w_out"].astype(BF16),
        "xa_wq": p["xa_wq"].astype(BF16), "xa_wo": p["xa_wo"].astype(BF16),
        "xa_wkv": jnp.concatenate([p["xa_wk"], p["xa_wv"]], axis=2).astype(BF16),
        "wg": p["ffn_w_gate"].astype(BF16), "wu": p["ffn_w_up"].astype(BF16),
        "wd": p["ffn_w_down"].astype(BF16),
    }


def _vector_params(l, p):
    bif = jnp.concatenate(
        [jnp.zeros((L_MI,), F32), p["mlstm_b_i"][l], p["mlstm_b_f"][l],
         jnp.zeros((SMALL_COLS - L_MF - HEADS,), F32)]).reshape(1, SMALL_COLS)
    return {
        "ba": p["gla_b_a"][l].reshape(1, -1), "gnorm": p["gla_norm"][l].reshape(1, DV),
        "bif": bif, "mnorm": p["mlstm_norm"][l].reshape(1, BRANCH_W),
        "convw": p["rg_conv_w"][l], "convb": p["rg_conv_b"][l].reshape(1, -1),
        "br": p["rg_b_r"][l].reshape(1, -1), "bi": p["rg_b_i"][l].reshape(1, -1),
        "lam": p["rg_lambda"][l].reshape(1, -1),
    }


def kernel(x_prompt, x_sample, mem_prompt, cache_mem_k, cache_mem_v, state_gla, state_mlstm_c, state_mlstm_n, state_mlstm_m, state_rglru_h, state_rglru_conv, norm_mix, w_in, gla_w_a2, gla_b_a, gla_norm, mlstm_b_i, mlstm_b_f, mlstm_norm, rg_conv_w, rg_conv_b, rg_w_r, rg_b_r, rg_w_i, rg_b_i, rg_lambda, w_branch, w_out, norm_xa, norm_mem, xa_wq, xa_wk, xa_wv, xa_wo, norm_ffn, ffn_w_gate, ffn_w_up, ffn_w_down, norm_final):
    p = {"w_in": w_in, "gla_w_a2": gla_w_a2, "gla_b_a": gla_b_a, "gla_norm": gla_norm,
         "mlstm_b_i": mlstm_b_i, "mlstm_b_f": mlstm_b_f, "mlstm_norm": mlstm_norm,
         "rg_conv_w": rg_conv_w, "rg_conv_b": rg_conv_b, "rg_w_r": rg_w_r, "rg_b_r": rg_b_r,
         "rg_w_i": rg_w_i, "rg_b_i": rg_b_i, "rg_lambda": rg_lambda, "w_branch": w_branch,
         "w_out": w_out, "xa_wq": xa_wq, "xa_wk": xa_wk, "xa_wv": xa_wv, "xa_wo": xa_wo,
         "ffn_w_gate": ffn_w_gate, "ffn_w_up": ffn_w_up, "ffn_w_down": ffn_w_down}
    nb, t, d = x_prompt.shape
    ns = x_sample.shape[0]
    xp = x_prompt.reshape(nb * t, d)
    xs = x_sample.reshape(ns, d)
    mem = mem_prompt.reshape(nb * N_MEM, d)
    wts = _matmul_weights(p)

    st_sample = (state_gla, state_mlstm_c, state_mlstm_n.reshape(DEPTH, ns, HEADS * DK),
                 jnp.pad(state_mlstm_m, ((0, 0), (0, 0), (L_MI, SMALL_COLS - L_MI - HEADS))),
                 state_rglru_h,
                 state_rglru_conv.reshape(DEPTH, ns, (RG_CONV - 1) * RG_WIDTH))

    new_p = [[] for _ in range(8)]
    new_s = [[] for _ in range(6)]
    for l in range(DEPTH):
        lw = _vector_params(l, p)
        last = l == DEPTH - 1

        k_p, v_p = _norm_matmul(mem, norm_mem[l], wts["xa_wkv"], l, tm=512, tn=d)
        k_p = k_p.reshape(nb, N_MEM, d)
        v_p = v_p.reshape(nb, N_MEM, d)

        xp, g_s, c_s, n_s, m_s, h_s, buf_s = _mix_prompt(xp, norm_mix[l], wts, lw, l,
                                                         nb=nb, t=t, tt=256)
        xp = _xattn_ffn_prompt(xp, norm_xa[l], wts["xa_wq"], wts["xa_wo"], k_p, v_p,
                               norm_ffn[l], wts["wg"], wts["wu"], wts["wd"], norm_final, l,
                               nb=nb, t=t, tm=512, final_norm=last)
        for lst, val in zip(new_p, (k_p.reshape(nb, N_MEM, XA_HEADS, XA_HEAD_DIM),
                                    v_p.reshape(nb, N_MEM, XA_HEADS, XA_HEAD_DIM),
                                    g_s, c_s, n_s.reshape(nb, HEADS, DK), m_s.reshape(nb, HEADS),
                                    h_s.reshape(nb, RG_WIDTH), buf_s)):
            lst.append(val)

        pm, ps = _in_proj(xs, norm_mix[l], wts["w_main"], wts["w_small"], l, tm=ns, tn=1024)
        br, g_s, c_s, n_s, m_s, h_s, buf_s = _mix_sample(pm, ps, st_sample, wts, lw, l, bs=8)
        xs = _merge(xs, br, pm, wts["wb"], wts["wo"], l, tm=ns)
        (q,) = _norm_matmul(xs, norm_xa[l], wts["xa_wq"], l, tm=ns, tn=d)
        att = _xattn_sample(q.reshape(ns, XA_HEADS, XA_HEAD_DIM), cache_mem_k, cache_mem_v, l, bs=4)
        xs = _matmul_residual(xs, att.reshape(ns, d), wts["xa_wo"], l)
        xs = _ffn(xs, norm_ffn[l], wts["wg"], wts["wu"], wts["wd"], norm_final, l,
                  tm=ns, final_norm=last)
        for lst, val in zip(new_s, (g_s, c_s, n_s.reshape(ns, HEADS, DK),
                                    m_s[:, L_MI:L_MI + HEADS], h_s,
                                    buf_s.reshape(ns, RG_CONV - 1, RG_WIDTH))):
            lst.append(val)

    y_prompt = xp.reshape(nb, t, d)
    y_sample = xs.reshape(ns, 1, d)
    outs_p = [jnp.stack(v, axis=0) for v in new_p]
    outs_s = [jnp.stack(v, axis=0) for v in new_s]
    return (y_prompt, y_sample, *outs_p, *outs_s)
```

```python
import functools
import math

import jax
import jax.numpy as jnp
from jax import lax
from jax.experimental import pallas as pl
from jax.experimental.pallas import tpu as pltpu

F32 = jnp.float32
BF16 = jnp.bfloat16

D_MODEL = 1024
DEPTH = 2
EPS = 1e-6
N_MEM = 256
XA_HEADS = 4
XA_HEAD_DIM = D_MODEL // XA_HEADS
N_BRANCH = 3
BRANCH_W = D_MODEL // 2
HEADS = 4
DK = 64
DV = 128
GLA_RANK = 16
GLA_GATE_NORM = 16.0
GLA_CHUNK = 32
MLSTM_CHUNK = 64
RG_WIDTH = BRANCH_W
RG_BLOCKS = 8
RG_BLOCK = RG_WIDTH // RG_BLOCKS
RG_CONV = 4
RG_C = 8.0
D_FF = 2816

C_GQ, C_GK, C_GV, C_GG = 0, 256, 512, 1024
C_MQ, C_MK, C_MV, C_MO = 1536, 1792, 2048, 2560
C_RX, C_RG = 3072, 3584
MIX_COLS = 4096
GATE_COLS = N_BRANCH * D_MODEL
MAIN_COLS = MIX_COLS + GATE_COLS
SMALL_COLS = 128
L_GA = 0
L_MI = 16
L_MF = 20

PROJ_TILE = 256
MERGE_TILE = 256
FFN_TILE = 256
VMEM_LIMIT = 48 * 1024 * 1024


def _softplus(x):
    return jnp.maximum(x, 0.0) + jnp.log1p(jnp.exp(-jnp.abs(x)))


def _log_sigmoid(x):
    return -_softplus(-x)


def _sigmoid(x):
    return 1.0 / (1.0 + jnp.exp(-x))


def _silu(x):
    return x * _sigmoid(x)


def _gelu_tanh(x):
    c = math.sqrt(2.0 / math.pi)
    return x * (0.5 * (1.0 + jnp.tanh(c * (x + 0.044715 * (x * x * x)))))


def _neg_expm1(x):
    return -jnp.tanh(0.5 * x) * (jnp.exp(x) + 1.0)


def _rms(x, g):
    ms = jnp.mean(x * x, axis=-1, keepdims=True)
    return x * lax.rsqrt(ms + EPS) * g


def _dot(a, b):
    return jnp.dot(a, b, preferred_element_type=F32)


def _dot_nt(a, b):
    return lax.dot_general(a, b, (((1,), (1,)), ((), ())), preferred_element_type=F32)


def _dot_tn(a, b):
    return lax.dot_general(a, b, (((0,), (0,)), ((), ())), preferred_element_type=F32)


def _dot_mask_exact(mask, x):
    hi = x.astype(BF16)
    rest = x - hi.astype(F32)
    mid = rest.astype(BF16)
    lo = (rest - mid.astype(F32)).astype(BF16)
    return _dot(mask, hi) + _dot(mask, mid) + _dot(mask, lo)


def _dot_mask_exact_rhs(x, mask):
    hi = x.astype(BF16)
    rest = x - hi.astype(F32)
    mid = rest.astype(BF16)
    lo = (rest - mid.astype(F32)).astype(BF16)
    return _dot(hi, mask) + _dot(mid, mask) + _dot(lo, mask)


def _chunk_masks(n, chunk):
    shift = chunk.bit_length() - 1
    r = lax.broadcasted_iota(jnp.int32, (n, n), 0)
    c = lax.broadcasted_iota(jnp.int32, (n, n), 1)
    same = lax.shift_right_logical(r, shift) == lax.shift_right_logical(c, shift)
    tril = jnp.where(same & (c <= r), 1.0, 0.0).astype(F32)
    full = jnp.where(same, 1.0, 0.0).astype(F32)
    return tril, full


def _norm_matmul_kernel(x_ref, g_ref, w_ref, *refs):
    out_refs, u_ref = refs[:-1], refs[-1]
    j = pl.program_id(1)

    @pl.when(j == 0)
    def _():
        u_ref[...] = _rms(x_ref[...], g_ref[...]).astype(BF16)

    res = _dot(u_ref[...], w_ref[...])
    for idx, o_ref in enumerate(out_refs):
        @pl.when(j == idx)
        def _(o_ref=o_ref):
            o_ref[...] = res


def _norm_matmul(x, g, w, layer, *, tm, tn):
    n, d = x.shape
    nout = w.shape[2]
    n_out = nout // tn
    return pl.pallas_call(
        _norm_matmul_kernel,
        grid=(n // tm, n_out),
        in_specs=[pl.BlockSpec((tm, d), lambda i, j: (i, 0)),
                  pl.BlockSpec((1, d), lambda i, j: (0, 0)),
                  pl.BlockSpec((None, d, tn), lambda i, j: (layer, 0, j))],
        out_specs=[pl.BlockSpec((tm, tn), lambda i, j: (i, 0))] * n_out,
        out_shape=[jax.ShapeDtypeStruct((n, tn), F32)] * n_out,
        scratch_shapes=[pltpu.VMEM((tm, d), BF16)],
        compiler_params=pltpu.CompilerParams(
            dimension_semantics=("parallel", "arbitrary"), vmem_limit_bytes=VMEM_LIMIT),
        name="norm_matmul",
    )(x, g.reshape(1, d), w)


def _in_proj_kernel(x_ref, g_ref, w_ref, ws_ref, o_ref, os_ref, u_ref):
    @pl.when(pl.program_id(1) == 0)
    def _():
        u = _rms(x_ref[...], g_ref[...]).astype(BF16)
        u_ref[...] = u
        os_ref[...] = _dot(u, ws_ref[...])

    o_ref[...] = _dot(u_ref[...], w_ref[...])


def _in_proj(x, g, w_main, w_small, layer, *, tm, tn):
    n, d = x.shape
    return pl.pallas_call(
        _in_proj_kernel,
        grid=(n // tm, MAIN_COLS // tn),
        in_specs=[pl.BlockSpec((tm, d), lambda i, j: (i, 0)),
                  pl.BlockSpec((1, d), lambda i, j: (0, 0)),
                  pl.BlockSpec((None, d, tn), lambda i, j: (layer, 0, j)),
                  pl.BlockSpec((None, d, SMALL_COLS), lambda i, j: (layer, 0, 0))],
        out_specs=[pl.BlockSpec((tm, tn), lambda i, j: (i, j)),
                   pl.BlockSpec((tm, SMALL_COLS), lambda i, j: (i, 0))],
        out_shape=[jax.ShapeDtypeStruct((n, MAIN_COLS), F32),
                   jax.ShapeDtypeStruct((n, SMALL_COLS), F32)],
        scratch_shapes=[pltpu.VMEM((tm, d), BF16)],
        compiler_params=pltpu.CompilerParams(
            dimension_semantics=("parallel", "arbitrary"), vmem_limit_bytes=VMEM_LIMIT),
        name="in_proj",
    )(x, g.reshape(1, d), w_main, w_small)


def _rg_gates(xc, wr_ref, br_ref, wi_ref, bi_ref, lam_ref):
    xcb = xc.astype(BF16)
    r = _sigmoid(_dot(xcb, wr_ref[...]) + br_ref[...])
    i = _sigmoid(_dot(xcb, wi_ref[...]) + bi_ref[...])
    log_a = (-RG_C * r) * _softplus(-lam_ref[...])
    a = jnp.exp(log_a)
    b = jnp.sqrt(_neg_expm1(2.0 * log_a)) * (i * xc)
    return a, b


def _projection_activation(col):
    if C_GG <= col < C_GG + HEADS * DV:
        return _silu
    if C_MO <= col < C_MO + HEADS * DV:
        return _sigmoid
    if C_RG <= col < C_RG + RG_WIDTH:
        return _gelu_tanh
    if col >= MIX_COLS:
        return _sigmoid
    return None


def _project_stage(x_ref, g_ref, wmain_ref, wsmall_ref, pm_ref, ps_ref, xs_ref, sg_ref):
    x = x_ref[...]
    xs_ref[...] = x
    u = _rms(x, g_ref[...]).astype(BF16)
    ps_ref[...] = _dot(u, wsmall_ref[...])
    for cb in range(MAIN_COLS // PROJ_TILE):
        c0 = cb * PROJ_TILE
        y = _dot(u, wmain_ref[:, c0:c0 + PROJ_TILE])
        act = _projection_activation(c0)
        if act is not None:
            y = act(y)
        if c0 >= MIX_COLS:
            sg_ref[:, c0 - MIX_COLS:c0 - MIX_COLS + PROJ_TILE] = y
        else:
            pm_ref[:, c0:c0 + PROJ_TILE] = y


def _mixer_stage(pm_ref, ps_ref,
                 wa2_ref, ba_ref, gnorm_ref, bif_ref, mnorm_ref,
                 convw_ref, convb_ref, wr_ref, br_ref, wi_ref, bi_ref, lam_ref,
                 s_ref, c_ref, n_ref, m_ref, h_ref, buf_ref,
                 br_out_ref, og_ref, xpad_ref):
    tt = pm_ref.shape[0]

    x = pm_ref[:, C_RX:C_RX + RG_WIDTH]
    xpad_ref[pl.ds(8, tt), :] = x
    xc = (convb_ref[...] + convw_ref[3:4, :] * x
          + convw_ref[2:3, :] * xpad_ref[pl.ds(7, tt), :]
          + convw_ref[1:2, :] * xpad_ref[pl.ds(6, tt), :]
          + convw_ref[0:1, :] * xpad_ref[pl.ds(5, tt), :])
    buf_ref[...] = xpad_ref[pl.ds(tt + 8 - (RG_CONV - 1), RG_CONV - 1), :]
    xpad_ref[pl.ds(0, 8), :] = xpad_ref[pl.ds(tt, 8), :]
    a, b = _rg_gates(xc, wr_ref, br_ref, wi_ref, bi_ref, lam_ref)
    ridx = lax.broadcasted_iota(jnp.int32, (tt, RG_WIDTH), 0)
    sh = 1
    while sh < tt:
        keep = ridx >= sh
        a_sh = jnp.where(keep, pltpu.roll(a, sh, 0), 1.0)
        b_sh = jnp.where(keep, pltpu.roll(b, sh, 0), 0.0)
        b = a * b_sh + b
        a = a * a_sh
        sh *= 2
    hseq = a * h_ref[...] + b
    h_ref[...] = hseq[tt - 1:tt, :]
    y = hseq * pm_ref[:, C_RG:C_RG + RG_WIDTH]
    br_out_ref[:, 2 * BRANCH_W:3 * BRANCH_W] = y.astype(BF16)

    ps = ps_ref[...]
    a_pre = _dot(ps.astype(BF16), wa2_ref[...]) + ba_ref[...]
    log_a = _log_sigmoid(a_pre) * (1.0 / GLA_GATE_NORM)
    tril32, full32 = _chunk_masks(tt, GLA_CHUNK)
    sums = _dot_mask_exact(jnp.concatenate([tril32, full32], axis=0).astype(BF16), log_a)
    b_cum = sums[:tt]
    b_end = sums[tt:]
    q = pm_ref[:, C_GQ:C_GQ + HEADS * DK] * (DK ** -0.5)
    k = pm_ref[:, C_GK:C_GK + HEADS * DK]
    qi_f = q * jnp.exp(b_cum)
    qi_all = qi_f.astype(BF16)
    ki_all = (k * jnp.exp(-b_cum)).astype(BF16)
    ke_t = (k * jnp.exp(b_end - b_cum)).T
    de_t = jnp.exp(b_end).T
    in_chunk = tril32 > 0.0
    n_chunks = tt // GLA_CHUNK
    cshift = GLA_CHUNK.bit_length() - 1
    chunk_of_lane = lax.shift_right_logical(lax.broadcasted_iota(jnp.int32, (DK, tt), 1), cshift)
    chunk_of_row = lax.shift_right_logical(lax.broadcasted_iota(jnp.int32, (tt, 2 * DK), 0), cshift)
    upper_half = (lax.broadcasted_iota(jnp.int32, (tt, 2 * DK), 1) >= DK).astype(jnp.int32)
    for h in range(HEADS):
        kcols = slice(h * DK, (h + 1) * DK)
        v = pm_ref[:, C_GV + h * DV:C_GV + (h + 1) * DV].astype(BF16)
        att = jnp.where(in_chunk, _dot_nt(qi_all[:, kcols], ki_all[:, kcols]), 0.0)
        o_intra = _dot(att.astype(BF16), v)
        ke_h = ke_t[kcols, :]
        ke_blk = jnp.concatenate(
            [jnp.where(chunk_of_lane == c, ke_h, 0.0) for c in range(n_chunks)], axis=0)
        upd = _dot(ke_blk.astype(BF16), v)
        s_cur = s_ref[h]
        s_start = []
        for c in range(n_chunks):
            s_start.append(s_cur)
            de_col = de_t[kcols, c * GLA_CHUNK:c * GLA_CHUNK + 1]
            s_cur = de_col * s_cur + upd[c * DK:(c + 1) * DK, :]
        s_ref[h] = s_cur
        qi_h = qi_f[:, kcols]
        qi_dup = jnp.concatenate([qi_h, qi_h], axis=1)
        q_blk = jnp.concatenate(
            [jnp.where(chunk_of_row == 2 * jj + upper_half, qi_dup, 0.0)
             for jj in range(n_chunks // 2)], axis=1)
        o_inter = _dot(q_blk.astype(BF16), jnp.concatenate(s_start, axis=0).astype(BF16))
        og_ref[:, h * DV:(h + 1) * DV] = o_intra + o_inter

    gnorm = gnorm_ref[...]
    for h in range(HEADS):
        cols = slice(h * DV, (h + 1) * DV)
        y = _rms(og_ref[:, cols], gnorm) * pm_ref[:, C_GG + h * DV:C_GG + (h + 1) * DV]
        br_out_ref[:, cols] = y.astype(BF16)

    pre = ps + bif_ref[...]
    rr = lax.broadcasted_iota(jnp.int32, (tt, tt), 0)
    cc = lax.broadcasted_iota(jnp.int32, (tt, tt), 1)
    causal = cc <= rr
    fcum = _dot_mask_exact(jnp.where(causal, 1.0, 0.0).astype(BF16), _log_sigmoid(pre))
    pre_t = pre.T
    fcum_t = fcum.T
    q_all = pm_ref[:, C_MQ:C_MQ + HEADS * DK] * (DK ** -0.5)
    k_all = pm_ref[:, C_MK:C_MK + HEADS * DK]
    for h in range(HEADS):
        kcols = slice(h * DK, (h + 1) * DK)
        fc_col = fcum[:, L_MF + h:L_MF + h + 1]
        ic_col = pre[:, L_MI + h:L_MI + h + 1]
        fc_row = fcum_t[L_MF + h:L_MF + h + 1, :]
        ic_row = pre_t[L_MI + h:L_MI + h + 1, :]
        m_prev = m_ref[:, h:h + 1]
        d = jnp.where(causal, fc_col + (ic_row - fc_row), -jnp.inf)
        inter = fc_col + m_prev
        m_t = jnp.maximum(inter, jnp.max(d, axis=1, keepdims=True))
        w_intra = jnp.exp(d - m_t)
        w_inter = jnp.exp(inter - m_t)
        qf = q_all[:, kcols]
        kf = k_all[:, kcols]
        qc = qf.astype(BF16)
        vc = pm_ref[:, C_MV + h * DV:C_MV + (h + 1) * DV].astype(BF16)
        s = _dot_nt(qc, kf.astype(BF16)) * w_intra
        c_old = c_ref[h]
        n_old = n_ref[h]
        num = w_inter * _dot(qc, c_old.astype(BF16)) + _dot(s.astype(BF16), vc)
        den = (w_inter * jnp.sum(qf * n_old, axis=1, keepdims=True)
               + jnp.sum(s, axis=1, keepdims=True))
        hh = num / jnp.maximum(jnp.abs(den), jnp.exp(-m_t))
        og_ref[:, h * DV:(h + 1) * DV] = hh
        f_end = fc_col[tt - 1:tt, :]
        m_end = m_t[tt - 1:tt, :]
        w_k = jnp.exp(f_end - fc_col + ic_col - m_end)
        w_c = jnp.exp(f_end + m_prev - m_end)
        kw = w_k * kf
        c_ref[h] = w_c * c_old + _dot_tn(kw.astype(BF16), vc)
        n_ref[h] = w_c * n_old + jnp.sum(kw, axis=0, keepdims=True)
        m_ref[:, h:h + 1] = m_end

    for h in range(HEADS):
        cols = slice(h * DV, (h + 1) * DV)
        y = (_rms(og_ref[:, cols], mnorm_ref[:, cols])
             * pm_ref[:, C_MO + h * DV:C_MO + (h + 1) * DV])
        br_out_ref[:, BRANCH_W + h * DV:BRANCH_W + (h + 1) * DV] = y.astype(BF16)


def _merge_stage(br_ref, sg_ref, xr_ref, wb_ref, wo_ref, xo_ref, mg_ref):
    n_tiles = D_MODEL // MERGE_TILE
    for jt in range(n_tiles):
        cols = slice(jt * MERGE_TILE, (jt + 1) * MERGE_TILE)
        merged = None
        for b in range(N_BRANCH):
            z = _dot(br_ref[:, b * BRANCH_W:(b + 1) * BRANCH_W], wb_ref[b, :, cols])
            term = sg_ref[:, b * D_MODEL + jt * MERGE_TILE:b * D_MODEL + (jt + 1) * MERGE_TILE] * z
            merged = term if merged is None else merged + term
        mg_ref[:, cols] = merged.astype(BF16)
    for jt in range(n_tiles):
        cols = slice(jt * MERGE_TILE, (jt + 1) * MERGE_TILE)
        xo_ref[:, cols] = xr_ref[:, cols] + _dot(mg_ref[...], wo_ref[:, cols])


N_MIXER_WEIGHTS = 12


def _mix_prompt_kernel(x_ref, g_ref, wmain_ref, wsmall_ref, *refs):
    mixer_w = refs[:N_MIXER_WEIGHTS]
    wb_ref, wo_ref = refs[N_MIXER_WEIGHTS:N_MIXER_WEIGHTS + 2]
    xo_ref, s_ref, c_ref, n_ref, m_ref, h_ref, buf_ref = refs[N_MIXER_WEIGHTS + 2:N_MIXER_WEIGHTS + 9]
    pm_ref, ps_ref, xs_ref, br_ref, sg_ref, mg_ref, og_ref, xpad_ref = refs[N_MIXER_WEIGHTS + 9:]

    @pl.when(pl.program_id(1) == 0)
    def _():
        for ref in (s_ref, c_ref, n_ref, m_ref, h_ref):
            ref[...] = jnp.zeros_like(ref)
        xpad_ref[pl.ds(0, 8), :] = jnp.zeros((8, RG_WIDTH), F32)

    _project_stage(x_ref, g_ref, wmain_ref, wsmall_ref, pm_ref, ps_ref, xs_ref, sg_ref)
    _mixer_stage(pm_ref, ps_ref, *mixer_w, s_ref, c_ref, n_ref, m_ref, h_ref, buf_ref,
                 br_ref, og_ref, xpad_ref)
    _merge_stage(br_ref, sg_ref, xs_ref, wb_ref, wo_ref, xo_ref, mg_ref)


def _mix_prompt(x, g, wts, lw, layer, *, nb, t, tt):
    d = x.shape[1]
    const = lambda shape: pl.BlockSpec(shape, lambda b, j: (0,) * len(shape),
                                       pipeline_mode=pl.Buffered(1))
    slab = lambda shape: pl.BlockSpec((None,) + shape, lambda b, j: (layer,) + (0,) * len(shape),
                                      pipeline_mode=pl.Buffered(1))
    nt = t // tt
    return pl.pallas_call(
        _mix_prompt_kernel,
        grid=(nb, nt),
        in_specs=[pl.BlockSpec((tt, d), lambda b, j: (b * nt + j, 0)),
                  const((1, d)), slab((d, MAIN_COLS)), slab((d, SMALL_COLS)),
                  slab((SMALL_COLS, HEADS * DK)), const((1, HEADS * DK)), const((1, DV)),
                  const((1, SMALL_COLS)), const((1, BRANCH_W)),
                  const((RG_CONV, RG_WIDTH)), const((1, RG_WIDTH)),
                  slab((RG_WIDTH, RG_WIDTH)), const((1, RG_WIDTH)),
                  slab((RG_WIDTH, RG_WIDTH)), const((1, RG_WIDTH)), const((1, RG_WIDTH)),
                  slab((N_BRANCH, BRANCH_W, d)), slab((d, d))],
        out_specs=[pl.BlockSpec((tt, d), lambda b, j: (b * nt + j, 0)),
                   pl.BlockSpec((None, HEADS, DK, DV), lambda b, j: (b, 0, 0, 0)),
                   pl.BlockSpec((None, HEADS, DK, DV), lambda b, j: (b, 0, 0, 0)),
                   pl.BlockSpec((None, HEADS, 1, DK), lambda b, j: (b, 0, 0, 0)),
                   pl.BlockSpec((None, 1, HEADS), lambda b, j: (b, 0, 0)),
                   pl.BlockSpec((None, 1, RG_WIDTH), lambda b, j: (b, 0, 0)),
                   pl.BlockSpec((None, RG_CONV - 1, RG_WIDTH), lambda b, j: (b, 0, 0))],
        out_shape=[jax.ShapeDtypeStruct((nb * t, d), F32),
                   jax.ShapeDtypeStruct((nb, HEADS, DK, DV), F32),
                   jax.ShapeDtypeStruct((nb, HEADS, DK, DV), F32),
                   jax.ShapeDtypeStruct((nb, HEADS, 1, DK), F32),
                   jax.ShapeDtypeStruct((nb, 1, HEADS), F32),
                   jax.ShapeDtypeStruct((nb, 1, RG_WIDTH), F32),
                   jax.ShapeDtypeStruct((nb, RG_CONV - 1, RG_WIDTH), F32)],
        scratch_shapes=[pltpu.VMEM((tt, MIX_COLS), F32),
                        pltpu.VMEM((tt, SMALL_COLS), F32),
                        pltpu.VMEM((tt, d), F32),
                        pltpu.VMEM((tt, N_BRANCH * BRANCH_W), BF16),
                        pltpu.VMEM((tt, GATE_COLS), F32),
                        pltpu.VMEM((tt, d), BF16),
                        pltpu.VMEM((tt, BRANCH_W), F32),
                        pltpu.VMEM((tt + 8, RG_WIDTH), F32)],
        compiler_params=pltpu.CompilerParams(
            dimension_semantics=("parallel", "arbitrary"), vmem_limit_bytes=VMEM_LIMIT),
        name="mix_prompt",
    )(x, g.reshape(1, d), wts["w_main"], wts["w_small"],
      wts["wa2"], lw["ba"], lw["gnorm"], lw["bif"], lw["mnorm"], lw["convw"], lw["convb"],
      wts["wr"], lw["br"], wts["wi"], lw["bi"], lw["lam"], wts["wb"], wts["wo"])


def _mix_sample_kernel(pm_ref, ps_ref, s0_ref, c0_ref, n0_ref, m0_ref, h0_ref, buf0_ref,
                       wa2_ref, ba_ref, gnorm_ref, bif_ref, mnorm_ref,
                       convw_ref, convb_ref, wr_ref, br_ref, wi_ref, bi_ref, lam_ref,
                       br_out_ref, s_ref, c_ref, n_ref, m_ref, h_ref, buf_ref,
                       og_ref, om_ref):
    bs = pm_ref.shape[0]
    ps = ps_ref[...]
    a_pre = _dot(ps.astype(BF16), wa2_ref[...]) + ba_ref[...]
    a = jnp.exp(_log_sigmoid(a_pre) * (1.0 / GLA_GATE_NORM))
    a_hi = a.astype(BF16)
    a_r1 = a - a_hi.astype(F32)
    a_mid = a_r1.astype(BF16)
    a_lo = (a_r1 - a_mid.astype(F32)).astype(BF16)
    mq = pm_ref[:, C_MQ:C_MQ + HEADS * DK] * (DK ** -0.5)
    mk = pm_ref[:, C_MK:C_MK + HEADS * DK]
    kd = HEADS * DK

    pre = ps + bif_ref[...]
    f_log = pltpu.roll(_log_sigmoid(pre), SMALL_COLS - (L_MF - L_MI), 1)
    inter = f_log + m0_ref[...]
    m_t = jnp.maximum(inter, pre)
    lane = lax.broadcasted_iota(jnp.int32, (bs, SMALL_COLS), 1)
    is_head = (lane >= L_MI) & (lane < L_MI + HEADS)
    w_intra = jnp.where(is_head, jnp.exp(pre - m_t), 0.0)
    w_inter = jnp.where(is_head, jnp.exp(inter - m_t), 0.0)
    kd_row = lax.broadcasted_iota(jnp.int32, (kd, SMALL_COLS), 0)
    head_sum = jnp.where(
        lax.shift_right_logical(kd_row, DK.bit_length() - 1) + L_MI
        == lax.broadcasted_iota(jnp.int32, (kd, SMALL_COLS), 1), 1.0, 0.0).astype(BF16)
    sc = _dot_mask_exact_rhs(mq * mk, head_sum) * w_intra
    den = w_inter * _dot_mask_exact_rhs(mq * n0_ref[...], head_sum) + sc
    dmax = jnp.where(is_head, jnp.maximum(jnp.abs(den), jnp.exp(-m_t)), 1.0)
    m_ref[...] = m_t
    lane_head = lax.broadcasted_iota(jnp.int32, (SMALL_COLS, kd), 0) - L_MI
    head_spread = jnp.where(
        lane_head == lax.shift_right_logical(
            lax.broadcasted_iota(jnp.int32, (SMALL_COLS, kd), 1), DK.bit_length() - 1),
        1.0, 0.0).astype(BF16)
    n_ref[...] = (_dot_mask_exact_rhs(w_inter, head_spread) * n0_ref[...]
                  + _dot_mask_exact_rhs(w_intra, head_spread) * mk)

    scalars = jnp.concatenate([w_inter, w_intra, sc, dmax], axis=1)
    sc_hi = scalars.astype(BF16)
    sc_r1 = scalars - sc_hi.astype(F32)
    sc_mid = sc_r1.astype(BF16)
    sc_lo = (sc_r1 - sc_mid.astype(F32)).astype(BF16)
    n_scalar = 4

    def scalar_rows(piece):
        pt = piece.astype(F32).T
        return jnp.concatenate(
            [pt[i * SMALL_COLS + L_MI:i * SMALL_COLS + L_MI + 8, :] for i in range(n_scalar)],
            axis=0)

    cols_t = jnp.concatenate(
        [(pm_ref[:, C_GQ:C_GQ + kd] * (DK ** -0.5)).T, pm_ref[:, C_GK:C_GK + kd].T,
         mq.T, mk.T, a_hi.astype(F32).T, a_mid.astype(F32).T, a_lo.astype(F32).T,
         scalar_rows(sc_hi), scalar_rows(sc_mid), scalar_rows(sc_lo)],
        axis=0).astype(BF16)
    o_gq, o_gk, o_mq, o_mk, o_a0, o_a1, o_a2 = (i * kd for i in range(7))
    o_scalar = 7 * kd
    sample_id = lax.broadcasted_iota(jnp.int32, (bs, DV), 0)

    for s in range(bs):
        row = slice(s, s + 1)
        bc = _dot(cols_t, jnp.where(sample_id == s, 1.0, 0.0).astype(BF16))
        for h in range(HEADS):
            col = lambda off: bc[off + h * DK:off + (h + 1) * DK, :]

            def scalar(i):
                r0 = o_scalar + i * 8 + h
                return (bc[r0:r0 + 1, :] + bc[r0 + 8 * n_scalar:r0 + 8 * n_scalar + 1, :]
                        + bc[r0 + 16 * n_scalar:r0 + 16 * n_scalar + 1, :])

            v_row = pm_ref[row, C_GV + h * DV:C_GV + (h + 1) * DV]
            a_col = col(o_a0) + col(o_a1) + col(o_a2)
            s_new = a_col * s0_ref[s, h] + col(o_gk) * v_row
            s_ref[s, h] = s_new
            og_ref[row, h * DV:(h + 1) * DV] = jnp.sum(col(o_gq) * s_new,
                                                       axis=0, keepdims=True)
            v_row = pm_ref[row, C_MV + h * DV:C_MV + (h + 1) * DV]
            c_old = c0_ref[s, h]
            qc = jnp.sum(col(o_mq) * c_old, axis=0, keepdims=True)
            om_ref[row, h * DV:(h + 1) * DV] = (scalar(0) * qc + scalar(2) * v_row) / scalar(3)
            c_ref[s, h] = scalar(0) * c_old + (scalar(1) * col(o_mk)) * v_row

    gnorm = gnorm_ref[...]
    for h in range(HEADS):
        cols = slice(h * DV, (h + 1) * DV)
        y = _rms(og_ref[:, cols], gnorm) * _silu(pm_ref[:, C_GG + h * DV:C_GG + (h + 1) * DV])
        br_out_ref[:, cols] = y.astype(BF16)
        y = (_rms(om_ref[:, cols], mnorm_ref[:, cols])
             * _sigmoid(pm_ref[:, C_MO + h * DV:C_MO + (h + 1) * DV]))
        br_out_ref[:, BRANCH_W + h * DV:BRANCH_W + (h + 1) * DV] = y.astype(BF16)

    x = pm_ref[:, C_RX:C_RX + RG_WIDTH]
    xc = convb_ref[...] + convw_ref[3:4, :] * x
    for jj in range(RG_CONV - 1):
        xc = xc + convw_ref[jj:jj + 1, :] * buf0_ref[:, jj * RG_WIDTH:(jj + 1) * RG_WIDTH]
    buf_ref[:, 0:2 * RG_WIDTH] = buf0_ref[:, RG_WIDTH:3 * RG_WIDTH]
    buf_ref[:, 2 * RG_WIDTH:3 * RG_WIDTH] = x
    a, b = _rg_gates(xc, wr_ref, br_ref, wi_ref, bi_ref, lam_ref)
    hnew = a * h0_ref[...] + b
    h_ref[...] = hnew
    y = hnew * _gelu_tanh(pm_ref[:, C_RG:C_RG + RG_WIDTH])
    br_out_ref[:, 2 * BRANCH_W:3 * BRANCH_W] = y.astype(BF16)


def _mix_sample(pm, ps, st, wts, lw, layer, *, bs):
    n = pm.shape[0]
    const = lambda shape: pl.BlockSpec(shape, lambda i: (0,) * len(shape))
    slab = lambda shape: pl.BlockSpec((None,) + shape, lambda i: (layer,) + (0,) * len(shape))
    blk = lambda shape: pl.BlockSpec(shape, lambda i: (i,) + (0,) * (len(shape) - 1))
    lblk = lambda shape: pl.BlockSpec((None,) + shape,
                                      lambda i: (layer, i) + (0,) * (len(shape) - 1))
    s0, c0, n0, m0, h0, buf0 = st
    shapes = [(bs, HEADS, DK, DV), (bs, HEADS, DK, DV), (bs, HEADS * DK),
              (bs, SMALL_COLS), (bs, RG_WIDTH), (bs, (RG_CONV - 1) * RG_WIDTH)]
    return pl.pallas_call(
        _mix_sample_kernel,
        grid=(n // bs,),
        in_specs=[blk((bs, MIX_COLS)), blk((bs, SMALL_COLS))] + [lblk(sh) for sh in shapes] + [
            slab((SMALL_COLS, HEADS * DK)), const((1, HEADS * DK)), const((1, DV)),
            const((1, SMALL_COLS)), const((1, BRANCH_W)),
            const((RG_CONV, RG_WIDTH)), const((1, RG_WIDTH)),
            slab((RG_WIDTH, RG_WIDTH)), const((1, RG_WIDTH)),
            slab((RG_WIDTH, RG_WIDTH)), const((1, RG_WIDTH)), const((1, RG_WIDTH))],
        out_specs=[blk((bs, N_BRANCH * BRANCH_W))] + [blk(sh) for sh in shapes],
        out_shape=[jax.ShapeDtypeStruct((n, N_BRANCH * BRANCH_W), BF16)]
        + [jax.ShapeDtypeStruct(a.shape[1:], F32) for a in st],
        scratch_shapes=[pltpu.VMEM((bs, BRANCH_W), F32),
                        pltpu.VMEM((bs, BRANCH_W), F32)],
        compiler_params=pltpu.CompilerParams(
            dimension_semantics=("parallel",), vmem_limit_bytes=VMEM_LIMIT),
        name="mix_sample",
    )(pm, ps, s0, c0, n0, m0, h0, buf0,
      wts["wa2"], lw["ba"], lw["gnorm"], lw["bif"], lw["mnorm"], lw["convw"], lw["convb"],
      wts["wr"], lw["br"], wts["wi"], lw["bi"], lw["lam"])


def _merge_kernel(x_ref, br_ref, g0_ref, g1_ref, g2_ref, wb_ref, wo_ref, o_ref):
    merged = None
    for b, g_ref in enumerate((g0_ref, g1_ref, g2_ref)):
        z = _dot(br_ref[:, b * BRANCH_W:(b + 1) * BRANCH_W], wb_ref[b])
        term = _sigmoid(g_ref[...]) * z
        merged = term if merged is None else merged + term
    o_ref[...] = x_ref[...] + _dot(merged.astype(BF16), wo_ref[...])


def _merge(x, br, pm, wb, wo, layer, *, tm):
    n, d = x.shape
    gate_blk0 = MIX_COLS // d
    gate_spec = lambda b: pl.BlockSpec((tm, d), lambda i: (i, gate_blk0 + b))
    return pl.pallas_call(
        _merge_kernel,
        grid=(n // tm,),
        in_specs=[pl.BlockSpec((tm, d), lambda i: (i, 0)),
                  pl.BlockSpec((tm, N_BRANCH * BRANCH_W), lambda i: (i, 0)),
                  gate_spec(0), gate_spec(1), gate_spec(2),
                  pl.BlockSpec((None, N_BRANCH, BRANCH_W, d), lambda i: (layer, 0, 0, 0)),
                  pl.BlockSpec((None, d, d), lambda i: (layer, 0, 0))],
        out_specs=pl.BlockSpec((tm, d), lambda i: (i, 0)),
        out_shape=jax.ShapeDtypeStruct((n, d), F32),
        compiler_params=pltpu.CompilerParams(
            dimension_semantics=("parallel",), vmem_limit_bytes=VMEM_LIMIT),
        name="merge",
    )(x, br, pm, pm, pm, wb, wo)


def _xattn_ffn_prompt_kernel(x_ref, gx_ref, wq_ref, wo_ref, k_ref, v_ref,
                             gf_ref, wg_ref, wu_ref, wd_ref, gfin_ref,
                             o_ref, kb_ref, vb_ref, hid_ref, *, final_norm):
    @pl.when(pl.program_id(1) == 0)
    def _():
        kb_ref[...] = k_ref[...].astype(BF16)
        vb_ref[...] = v_ref[...].astype(BF16)

    x = x_ref[...]
    q = _dot(_rms(x, gx_ref[...]).astype(BF16), wq_ref[...]).astype(BF16)
    heads = []
    for h in range(XA_HEADS):
        cols = slice(h * XA_HEAD_DIM, (h + 1) * XA_HEAD_DIM)
        s = _dot_nt(q[:, cols], kb_ref[:, cols]) * (XA_HEAD_DIM ** -0.5)
        p = jnp.exp(s - jnp.max(s, axis=-1, keepdims=True))
        p = p / jnp.sum(p, axis=-1, keepdims=True)
        heads.append(_dot(p.astype(BF16), vb_ref[:, cols]))
    o = jnp.concatenate(heads, axis=1).astype(BF16)
    x = x + _dot(o, wo_ref[...])
    o_ref[...] = _ffn_block(x, gf_ref, wg_ref, wu_ref, wd_ref, gfin_ref, hid_ref, final_norm)


def _xattn_ffn_prompt(x, gx, wq, wo, mem_k, mem_v, gf, wg, wu, wd, gfin, layer, *,
                      nb, t, tm, final_norm):
    d = x.shape[1]
    f = wg.shape[2]
    nt = t // tm
    const = lambda shape: pl.BlockSpec(shape, lambda b, j: (0,) * len(shape),
                                       pipeline_mode=pl.Buffered(1))
    slab = lambda shape: pl.BlockSpec((None,) + shape, lambda b, j: (layer,) + (0,) * len(shape),
                                      pipeline_mode=pl.Buffered(1))
    return pl.pallas_call(
        functools.partial(_xattn_ffn_prompt_kernel, final_norm=final_norm),
        grid=(nb, nt),
        in_specs=[pl.BlockSpec((tm, d), lambda b, j: (b * nt + j, 0)),
                  const((1, d)), slab((d, d)), slab((d, d)),
                  pl.BlockSpec((None, N_MEM, d), lambda b, j: (b, 0, 0)),
                  pl.BlockSpec((None, N_MEM, d), lambda b, j: (b, 0, 0)),
                  const((1, d)), slab((d, f)), slab((d, f)), slab((f, d)), const((1, d))],
        out_specs=pl.BlockSpec((tm, d), lambda b, j: (b * nt + j, 0)),
        out_shape=jax.ShapeDtypeStruct(x.shape, F32),
        scratch_shapes=[pltpu.VMEM((N_MEM, d), BF16), pltpu.VMEM((N_MEM, d), BF16),
                        pltpu.VMEM((tm, f), BF16)],
        compiler_params=pltpu.CompilerParams(
            dimension_semantics=("parallel", "arbitrary"), vmem_limit_bytes=VMEM_LIMIT),
        name="xattn_ffn_prompt",
    )(x, gx.reshape(1, d), wq, wo, mem_k, mem_v, gf.reshape(1, d), wg, wu, wd, gfin.reshape(1, d))


def _xattn_sample_kernel(q_ref, k_ref, v_ref, o_ref):
    bs = q_ref.shape[0]
    nrow = N_MEM * XA_HEADS
    row_head = lax.broadcasted_iota(jnp.int32, (XA_HEADS, nrow), 1) & (XA_HEADS - 1)
    own_head = row_head == lax.broadcasted_iota(jnp.int32, (XA_HEADS, nrow), 0)
    for s in range(bs):
        q = q_ref[s].astype(BF16)
        k2 = k_ref[s].reshape(nrow, XA_HEAD_DIM).astype(BF16)
        v2 = v_ref[s].reshape(nrow, XA_HEAD_DIM).astype(BF16)
        sc = jnp.where(own_head, _dot_nt(q, k2) * (XA_HEAD_DIM ** -0.5), -jnp.inf)
        p = jnp.exp(sc - jnp.max(sc, axis=1, keepdims=True))
        p = p / jnp.sum(p, axis=1, keepdims=True)
        o_ref[s] = _dot(p.astype(BF16), v2)


def _xattn_sample(q, cache_k, cache_v, layer, *, bs):
    n = q.shape[0]
    blk = (None, bs, N_MEM, XA_HEADS, XA_HEAD_DIM)
    return pl.pallas_call(
        _xattn_sample_kernel,
        grid=(n // bs,),
        in_specs=[pl.BlockSpec((bs, XA_HEADS, XA_HEAD_DIM), lambda i: (i, 0, 0)),
                  pl.BlockSpec(blk, lambda i: (layer, i, 0, 0, 0)),
                  pl.BlockSpec(blk, lambda i: (layer, i, 0, 0, 0))],
        out_specs=pl.BlockSpec((bs, XA_HEADS, XA_HEAD_DIM), lambda i: (i, 0, 0)),
        out_shape=jax.ShapeDtypeStruct((n, XA_HEADS, XA_HEAD_DIM), F32),
        compiler_params=pltpu.CompilerParams(
            dimension_semantics=("parallel",), vmem_limit_bytes=VMEM_LIMIT),
        name="xattn_sample",
    )(q, cache_k, cache_v)


def _matmul_residual_kernel(x_ref, a_ref, w_ref, o_ref):
    o_ref[...] = x_ref[...] + _dot(a_ref[...].astype(BF16), w_ref[...])


def _matmul_residual(x, a, w, layer):
    n, d = x.shape
    return pl.pallas_call(
        _matmul_residual_kernel,
        grid=(1,),
        in_specs=[pl.BlockSpec((n, d), lambda i: (0, 0)),
                  pl.BlockSpec((n, d), lambda i: (0, 0)),
                  pl.BlockSpec((None, d, d), lambda i: (layer, 0, 0))],
        out_specs=pl.BlockSpec((n, d), lambda i: (0, 0)),
        out_shape=jax.ShapeDtypeStruct((n, d), F32),
        compiler_params=pltpu.CompilerParams(vmem_limit_bytes=VMEM_LIMIT),
        name="matmul_residual",
    )(x, a, w)


def _ffn_block(x, g_ref, wg_ref, wu_ref, wd_ref, gfin_ref, hid_ref, final_norm):
    u = _rms(x, g_ref[...]).astype(BF16)
    for jt in range(D_FF // FFN_TILE):
        cols = slice(jt * FFN_TILE, (jt + 1) * FFN_TILE)
        hid = _silu(_dot(u, wg_ref[:, cols])) * _dot(u, wu_ref[:, cols])
        hid_ref[:, cols] = hid.astype(BF16)
    y = x + _dot(hid_ref[...], wd_ref[...])
    if final_norm:
        y = _rms(y, gfin_ref[...])
    return y


def _ffn_kernel(x_ref, g_ref, wg_ref, wu_ref, wd_ref, gf_ref, o_ref, hid_ref, *, final_norm):
    o_ref[...] = _ffn_block(x_ref[...], g_ref, wg_ref, wu_ref, wd_ref, gf_ref, hid_ref, final_norm)


def _ffn(x, g, wg, wu, wd, gf, layer, *, tm, final_norm):
    n, d = x.shape
    f = wg.shape[2]
    const = lambda shape: pl.BlockSpec(shape, lambda i: (0,) * len(shape),
                                       pipeline_mode=pl.Buffered(1))
    slab = lambda shape: pl.BlockSpec((None,) + shape, lambda i: (layer,) + (0,) * len(shape),
                                      pipeline_mode=pl.Buffered(1))
    return pl.pallas_call(
        functools.partial(_ffn_kernel, final_norm=final_norm),
        grid=(n // tm,),
        in_specs=[pl.BlockSpec((tm, d), lambda i: (i, 0)),
                  const((1, d)), slab((d, f)), slab((d, f)), slab((f, d)), const((1, d))],
        out_specs=pl.BlockSpec((tm, d), lambda i: (i, 0)),
        out_shape=jax.ShapeDtypeStruct((n, d), F32),
        scratch_shapes=[pltpu.VMEM((tm, f), BF16)],
        compiler_params=pltpu.CompilerParams(
            dimension_semantics=("parallel",), vmem_limit_bytes=VMEM_LIMIT),
        name="ffn",
    )(x, g.reshape(1, d), wg, wu, wd, gf.reshape(1, d))


def _split_w_in(w_in):
    widths = (HEADS * DK, HEADS * DK, HEADS * DV, HEADS * DV, GLA_RANK,
              HEADS * DK, HEADS * DK, HEADS * DV, HEADS * DV, HEADS, HEADS,
              RG_WIDTH, RG_WIDTH, GATE_COLS)
    parts, acc = [], 0
    for w in widths:
        parts.append(w_in[..., acc:acc + w])
        acc += w
    return parts


def _block_diag(w):
    eye = jnp.eye(RG_BLOCKS, dtype=w.dtype)
    return jnp.einsum("lnde,nm->lndme", w, eye).reshape(DEPTH, RG_WIDTH, RG_WIDTH)


def _matmul_weights(p):
    (g_q, g_k, g_v, g_g, g_a, m_q, m_k, m_v, m_o, m_i, m_f, r_x, r_g, gates) = _split_w_in(p["w_in"])
    d = D_MODEL
    w_main = jnp.concatenate([g_q, g_k, g_v, g_g, m_q, m_k, m_v, m_o, r_x, r_g, gates], axis=2)
    w_small = jnp.concatenate(
        [g_a, m_i, m_f, jnp.zeros((DEPTH, d, SMALL_COLS - GLA_RANK - 2 * HEADS), F32)], axis=2)
    wa2 = jnp.concatenate(
        [p["gla_w_a2"], jnp.zeros((DEPTH, SMALL_COLS - GLA_RANK, HEADS * DK), F32)], axis=1)
    return {
        "w_main": w_main.astype(BF16), "w_small": w_small.astype(BF16), "wa2": wa2.astype(BF16),
        "wr": _block_diag(p["rg_w_r"]).astype(BF16), "wi": _block_diag(p["rg_w_i"]).astype(BF16),
        "wb": p["w_branch"].astype(BF16), "wo": p["w_out"].astype(BF16),
        "xa_wq": p["xa_wq"].astype(BF16), "xa_wo": p["xa_wo"].astype(BF16),
        "xa_wkv": jnp.concatenate([p["xa_wk"], p["xa_wv"]], axis=2).astype(BF16),
        "wg": p["ffn_w_gate"].astype(BF16), "wu": p["ffn_w_up"].astype(BF16),
        "wd": p["ffn_w_down"].astype(BF16),
    }


def _vector_params(l, p):
    bif = jnp.concatenate(
        [jnp.zeros((L_MI,), F32), p["mlstm_b_i"][l], p["mlstm_b_f"][l],
         jnp.zeros((SMALL_COLS - L_MF - HEADS,), F32)]).reshape(1, SMALL_COLS)
    return {
        "ba": p["gla_b_a"][l].reshape(1, -1), "gnorm": p["gla_norm"][l].reshape(1, DV),
        "bif": bif, "mnorm": p["mlstm_norm"][l].reshape(1, BRANCH_W),
        "convw": p["rg_conv_w"][l], "convb": p["rg_conv_b"][l].reshape(1, -1),
        "br": p["rg_b_r"][l].reshape(1, -1), "bi": p["rg_b_i"][l].reshape(1, -1),
        "lam": p["rg_lambda"][l].reshape(1, -1),
    }


def kernel(x_prompt, x_sample, mem_prompt, cache_mem_k, cache_mem_v, state_gla, state_mlstm_c, state_mlstm_n, state_mlstm_m, state_rglru_h, state_rglru_conv, norm_mix, w_in, gla_w_a2, gla_b_a, gla_norm, mlstm_b_i, mlstm_b_f, mlstm_norm, rg_conv_w, rg_conv_b, rg_w_r, rg_b_r, rg_w_i, rg_b_i, rg_lambda, w_branch, w_out, norm_xa, norm_mem, xa_wq, xa_wk, xa_wv, xa_wo, norm_ffn, ffn_w_gate, ffn_w_up, ffn_w_down, norm_final):
    p = {"w_in": w_in, "gla_w_a2": gla_w_a2, "gla_b_a": gla_b_a, "gla_norm": gla_norm,
         "mlstm_b_i": mlstm_b_i, "mlstm_b_f": mlstm_b_f, "mlstm_norm": mlstm_norm,
         "rg_conv_w": rg_conv_w, "rg_conv_b": rg_conv_b, "rg_w_r": rg_w_r, "rg_b_r": rg_b_r,
         "rg_w_i": rg_w_i, "rg_b_i": rg_b_i, "rg_lambda": rg_lambda, "w_branch": w_branch,
         "w_out": w_out, "xa_wq": xa_wq, "xa_wk": xa_wk, "xa_wv": xa_wv, "xa_wo": xa_wo,
         "ffn_w_gate": ffn_w_gate, "ffn_w_up": ffn_w_up, "ffn_w_down": ffn_w_down}
    nb, t, d = x_prompt.shape
    ns = x_sample.shape[0]
    xp = x_prompt.reshape(nb * t, d)
    xs = x_sample.reshape(ns, d)
    mem = mem_prompt.reshape(nb * N_MEM, d)
    wts = _matmul_weights(p)

    st_sample = (state_gla, state_mlstm_c, state_mlstm_n.reshape(DEPTH, ns, HEADS * DK),
                 jnp.pad(state_mlstm_m, ((0, 0), (0, 0), (L_MI, SMALL_COLS - L_MI - HEADS))),
                 state_rglru_h,
                 state_rglru_conv.reshape(DEPTH, ns, (RG_CONV - 1) * RG_WIDTH))

    new_p = [[] for _ in range(8)]
    new_s = [[] for _ in range(6)]
    for l in range(DEPTH):
        lw = _vector_params(l, p)
        last = l == DEPTH - 1

        k_p, v_p = _norm_matmul(mem, norm_mem[l], wts["xa_wkv"], l, tm=512, tn=d)
        k_p = k_p.reshape(nb, N_MEM, d)
        v_p = v_p.reshape(nb, N_MEM, d)

        xp, g_s, c_s, n_s, m_s, h_s, buf_s = _mix_prompt(xp, norm_mix[l], wts, lw, l,
                                                         nb=nb, t=t, tt=256)
        xp = _xattn_ffn_prompt(xp, norm_xa[l], wts["xa_wq"], wts["xa_wo"], k_p, v_p,
                               norm_ffn[l], wts["wg"], wts["wu"], wts["wd"], norm_final, l,
                               nb=nb, t=t, tm=512, final_norm=last)
        for lst, val in zip(new_p, (k_p.reshape(nb, N_MEM, XA_HEADS, XA_HEAD_DIM),
                                    v_p.reshape(nb, N_MEM, XA_HEADS, XA_HEAD_DIM),
                                    g_s, c_s, n_s.reshape(nb, HEADS, DK), m_s.reshape(nb, HEADS),
                                    h_s.reshape(nb, RG_WIDTH), buf_s)):
            lst.append(val)

        pm, ps = _in_proj(xs, norm_mix[l], wts["w_main"], wts["w_small"], l, tm=ns, tn=1024)
        br, g_s, c_s, n_s, m_s, h_s, buf_s = _mix_sample(pm, ps, st_sample, wts, lw, l, bs=8)
        xs = _merge(xs, br, pm, wts["wb"], wts["wo"], l, tm=ns)
        (q,) = _norm_matmul(xs, norm_xa[l], wts["xa_wq"], l, tm=ns, tn=d)
        att = _xattn_sample(q.reshape(ns, XA_HEADS, XA_HEAD_DIM), cache_mem_k, cache_mem_v, l, bs=8)
        xs = _matmul_residual(xs, att.reshape(ns, d), wts["xa_wo"], l)
        xs = _ffn(xs, norm_ffn[l], wts["wg"], wts["wu"], wts["wd"], norm_final, l,
                  tm=ns, final_norm=last)
        for lst, val in zip(new_s, (g_s, c_s, n_s.reshape(ns, HEADS, DK),
                                    m_s[:, L_MI:L_MI + HEADS], h_s,
                                    buf_s.reshape(ns, RG_CONV - 1, RG_WIDTH))):
            lst.append(val)

    y_prompt = xp.reshape(nb, t, d)
    y_sample = xs.reshape(ns, 1, d)
    outs_p = [jnp.stack(v, axis=0) for v in new_p]
    outs_s = [jnp.stack(v, axis=0) for v in new_s]
    return (y_prompt, y_sample, *outs_p, *outs_s)
```

```python
import functools
import math

import jax
import jax.numpy as jnp
from jax import lax
from jax.experimental import pallas as pl
from jax.experimental.pallas import tpu as pltpu

F32 = jnp.float32
BF16 = jnp.bfloat16

D_MODEL = 1024
DEPTH = 2
EPS = 1e-6
N_MEM = 256
XA_HEADS = 4
XA_HEAD_DIM = D_MODEL // XA_HEADS
N_BRANCH = 3
BRANCH_W = D_MODEL // 2
HEADS = 4
DK = 64
DV = 128
GLA_RANK = 16
GLA_GATE_NORM = 16.0
GLA_CHUNK = 32
MLSTM_CHUNK = 64
RG_WIDTH = BRANCH_W
RG_BLOCKS = 8
RG_BLOCK = RG_WIDTH // RG_BLOCKS
RG_CONV = 4
RG_C = 8.0
D_FF = 2816

C_GQ, C_GK, C_GV, C_GG = 0, 256, 512, 1024
C_MQ, C_MK, C_MV, C_MO = 1536, 1792, 2048, 2560
C_RX, C_RG = 3072, 3584
MIX_COLS = 4096
GATE_COLS = N_BRANCH * D_MODEL
MAIN_COLS = MIX_COLS + GATE_COLS
SMALL_COLS = 128
L_GA = 0
L_MI = 16
L_MF = 20

PROJ_TILE = 256
MERGE_TILE = 256
FFN_TILE = 256
VMEM_LIMIT = 48 * 1024 * 1024


def _softplus(x):
    return jnp.maximum(x, 0.0) + jnp.log1p(jnp.exp(-jnp.abs(x)))


def _log_sigmoid(x):
    return -_softplus(-x)


def _sigmoid(x):
    return 1.0 / (1.0 + jnp.exp(-x))


def _silu(x):
    return x * _sigmoid(x)


def _gelu_tanh(x):
    c = math.sqrt(2.0 / math.pi)
    return x * (0.5 * (1.0 + jnp.tanh(c * (x + 0.044715 * (x * x * x)))))


def _neg_expm1(x):
    return -jnp.tanh(0.5 * x) * (jnp.exp(x) + 1.0)


def _rms(x, g):
    ms = jnp.mean(x * x, axis=-1, keepdims=True)
    return x * lax.rsqrt(ms + EPS) * g


def _dot(a, b):
    return jnp.dot(a, b, preferred_element_type=F32)


def _dot_nt(a, b):
    return lax.dot_general(a, b, (((1,), (1,)), ((), ())), preferred_element_type=F32)


def _dot_tn(a, b):
    return lax.dot_general(a, b, (((0,), (0,)), ((), ())), preferred_element_type=F32)


def _dot_mask_exact(mask, x):
    hi = x.astype(BF16)
    rest = x - hi.astype(F32)
    mid = rest.astype(BF16)
    lo = (rest - mid.astype(F32)).astype(BF16)
    return _dot(mask, hi) + _dot(mask, mid) + _dot(mask, lo)


def _dot_mask_exact_rhs(x, mask):
    hi = x.astype(BF16)
    rest = x - hi.astype(F32)
    mid = rest.astype(BF16)
    lo = (rest - mid.astype(F32)).astype(BF16)
    return _dot(hi, mask) + _dot(mid, mask) + _dot(lo, mask)


def _chunk_masks(n, chunk):
    shift = chunk.bit_length() - 1
    r = lax.broadcasted_iota(jnp.int32, (n, n), 0)
    c = lax.broadcasted_iota(jnp.int32, (n, n), 1)
    same = lax.shift_right_logical(r, shift) == lax.shift_right_logical(c, shift)
    tril = jnp.where(same & (c <= r), 1.0, 0.0).astype(F32)
    full = jnp.where(same, 1.0, 0.0).astype(F32)
    return tril, full


def _norm_matmul_kernel(x_ref, g_ref, w_ref, *refs):
    out_refs, u_ref = refs[:-1], refs[-1]
    j = pl.program_id(1)

    @pl.when(j == 0)
    def _():
        u_ref[...] = _rms(x_ref[...], g_ref[...]).astype(BF16)

    res = _dot(u_ref[...], w_ref[...])
    for idx, o_ref in enumerate(out_refs):
        @pl.when(j == idx)
        def _(o_ref=o_ref):
            o_ref[...] = res


def _norm_matmul(x, g, w, layer, *, tm, tn):
    n, d = x.shape
    nout = w.shape[2]
    n_out = nout // tn
    return pl.pallas_call(
        _norm_matmul_kernel,
        grid=(n // tm, n_out),
        in_specs=[pl.BlockSpec((tm, d), lambda i, j: (i, 0)),
                  pl.BlockSpec((1, d), lambda i, j: (0, 0)),
                  pl.BlockSpec((None, d, tn), lambda i, j: (layer, 0, j))],
        out_specs=[pl.BlockSpec((tm, tn), lambda i, j: (i, 0))] * n_out,
        out_shape=[jax.ShapeDtypeStruct((n, tn), F32)] * n_out,
        scratch_shapes=[pltpu.VMEM((tm, d), BF16)],
        compiler_params=pltpu.CompilerParams(
            dimension_semantics=("parallel", "arbitrary"), vmem_limit_bytes=VMEM_LIMIT),
        name="norm_matmul",
    )(x, g.reshape(1, d), w)


def _in_proj_kernel(x_ref, g_ref, w_ref, ws_ref, o_ref, os_ref, u_ref):
    @pl.when(pl.program_id(1) == 0)
    def _():
        u = _rms(x_ref[...], g_ref[...]).astype(BF16)
        u_ref[...] = u
        os_ref[...] = _dot(u, ws_ref[...])

    o_ref[...] = _dot(u_ref[...], w_ref[...])


def _in_proj(x, g, w_main, w_small, layer, *, tm, tn):
    n, d = x.shape
    return pl.pallas_call(
        _in_proj_kernel,
        grid=(n // tm, MAIN_COLS // tn),
        in_specs=[pl.BlockSpec((tm, d), lambda i, j: (i, 0)),
                  pl.BlockSpec((1, d), lambda i, j: (0, 0)),
                  pl.BlockSpec((None, d, tn), lambda i, j: (layer, 0, j)),
                  pl.BlockSpec((None, d, SMALL_COLS), lambda i, j: (layer, 0, 0))],
        out_specs=[pl.BlockSpec((tm, tn), lambda i, j: (i, j)),
                   pl.BlockSpec((tm, SMALL_COLS), lambda i, j: (i, 0))],
        out_shape=[jax.ShapeDtypeStruct((n, MAIN_COLS), F32),
                   jax.ShapeDtypeStruct((n, SMALL_COLS), F32)],
        scratch_shapes=[pltpu.VMEM((tm, d), BF16)],
        compiler_params=pltpu.CompilerParams(
            dimension_semantics=("parallel", "arbitrary"), vmem_limit_bytes=VMEM_LIMIT),
        name="in_proj",
    )(x, g.reshape(1, d), w_main, w_small)


def _rg_gates(xc, wr_ref, br_ref, wi_ref, bi_ref, lam_ref):
    xcb = xc.astype(BF16)
    r = _sigmoid(_dot(xcb, wr_ref[...]) + br_ref[...])
    i = _sigmoid(_dot(xcb, wi_ref[...]) + bi_ref[...])
    log_a = (-RG_C * r) * _softplus(-lam_ref[...])
    a = jnp.exp(log_a)
    b = jnp.sqrt(_neg_expm1(2.0 * log_a)) * (i * xc)
    return a, b


def _projection_activation(col):
    if C_GG <= col < C_GG + HEADS * DV:
        return _silu
    if C_MO <= col < C_MO + HEADS * DV:
        return _sigmoid
    if C_RG <= col < C_RG + RG_WIDTH:
        return _gelu_tanh
    if col >= MIX_COLS:
        return _sigmoid
    return None


def _project_stage(x_ref, g_ref, wmain_ref, wsmall_ref, pm_ref, ps_ref, xs_ref, sg_ref):
    x = x_ref[...]
    xs_ref[...] = x
    u = _rms(x, g_ref[...]).astype(BF16)
    ps_ref[...] = _dot(u, wsmall_ref[...])
    for cb in range(MAIN_COLS // PROJ_TILE):
        c0 = cb * PROJ_TILE
        y = _dot(u, wmain_ref[:, c0:c0 + PROJ_TILE])
        act = _projection_activation(c0)
        if act is not None:
            y = act(y)
        if c0 >= MIX_COLS:
            sg_ref[:, c0 - MIX_COLS:c0 - MIX_COLS + PROJ_TILE] = y
        else:
            pm_ref[:, c0:c0 + PROJ_TILE] = y


def _mixer_stage(pm_ref, ps_ref,
                 wa2_ref, ba_ref, gnorm_ref, bif_ref, mnorm_ref,
                 convw_ref, convb_ref, wr_ref, br_ref, wi_ref, bi_ref, lam_ref,
                 s_ref, c_ref, n_ref, m_ref, h_ref, buf_ref,
                 br_out_ref, og_ref, xpad_ref):
    tt = pm_ref.shape[0]

    x = pm_ref[:, C_RX:C_RX + RG_WIDTH]
    xpad_ref[pl.ds(8, tt), :] = x
    xc = (convb_ref[...] + convw_ref[3:4, :] * x
          + convw_ref[2:3, :] * xpad_ref[pl.ds(7, tt), :]
          + convw_ref[1:2, :] * xpad_ref[pl.ds(6, tt), :]
          + convw_ref[0:1, :] * xpad_ref[pl.ds(5, tt), :])
    buf_ref[...] = xpad_ref[pl.ds(tt + 8 - (RG_CONV - 1), RG_CONV - 1), :]
    xpad_ref[pl.ds(0, 8), :] = xpad_ref[pl.ds(tt, 8), :]
    a, b = _rg_gates(xc, wr_ref, br_ref, wi_ref, bi_ref, lam_ref)
    ridx = lax.broadcasted_iota(jnp.int32, (tt, RG_WIDTH), 0)
    sh = 1
    while sh < tt:
        keep = ridx >= sh
        a_sh = jnp.where(keep, pltpu.roll(a, sh, 0), 1.0)
        b_sh = jnp.where(keep, pltpu.roll(b, sh, 0), 0.0)
        b = a * b_sh + b
        a = a * a_sh
        sh *= 2
    hseq = a * h_ref[...] + b
    h_ref[...] = hseq[tt - 1:tt, :]
    y = hseq * pm_ref[:, C_RG:C_RG + RG_WIDTH]
    br_out_ref[:, 2 * BRANCH_W:3 * BRANCH_W] = y.astype(BF16)

    ps = ps_ref[...]
    a_pre = _dot(ps.astype(BF16), wa2_ref[...]) + ba_ref[...]
    log_a = _log_sigmoid(a_pre) * (1.0 / GLA_GATE_NORM)
    tril32, full32 = _chunk_masks(tt, GLA_CHUNK)
    sums = _dot_mask_exact(jnp.concatenate([tril32, full32], axis=0).astype(BF16), log_a)
    b_cum = sums[:tt]
    b_end = sums[tt:]
    q = pm_ref[:, C_GQ:C_GQ + HEADS * DK] * (DK ** -0.5)
    k = pm_ref[:, C_GK:C_GK + HEADS * DK]
    qi_f = q * jnp.exp(b_cum)
    qi_all = qi_f.astype(BF16)
    ki_all = (k * jnp.exp(-b_cum)).astype(BF16)
    ke_t = (k * jnp.exp(b_end - b_cum)).T
    de_t = jnp.exp(b_end).T
    in_chunk = tril32 > 0.0
    n_chunks = tt // GLA_CHUNK
    cshift = GLA_CHUNK.bit_length() - 1
    chunk_of_lane = lax.shift_right_logical(lax.broadcasted_iota(jnp.int32, (DK, tt), 1), cshift)
    chunk_of_row = lax.shift_right_logical(lax.broadcasted_iota(jnp.int32, (tt, 2 * DK), 0), cshift)
    upper_half = (lax.broadcasted_iota(jnp.int32, (tt, 2 * DK), 1) >= DK).astype(jnp.int32)
    for h in range(HEADS):
        kcols = slice(h * DK, (h + 1) * DK)
        v = pm_ref[:, C_GV + h * DV:C_GV + (h + 1) * DV].astype(BF16)
        att = jnp.where(in_chunk, _dot_nt(qi_all[:, kcols], ki_all[:, kcols]), 0.0)
        o_intra = _dot(att.astype(BF16), v)
        ke_h = ke_t[kcols, :]
        ke_blk = jnp.concatenate(
            [jnp.where(chunk_of_lane == c, ke_h, 0.0) for c in range(n_chunks)], axis=0)
        upd = _dot(ke_blk.astype(BF16), v)
        s_cur = s_ref[h]
        s_start = []
        for c in range(n_chunks):
            s_start.append(s_cur)
            de_col = de_t[kcols, c * GLA_CHUNK:c * GLA_CHUNK + 1]
            s_cur = de_col * s_cur + upd[c * DK:(c + 1) * DK, :]
        s_ref[h] = s_cur
        qi_h = qi_f[:, kcols]
        qi_dup = jnp.concatenate([qi_h, qi_h], axis=1)
        q_blk = jnp.concatenate(
            [jnp.where(chunk_of_row == 2 * jj + upper_half, qi_dup, 0.0)
             for jj in range(n_chunks // 2)], axis=1)
        o_inter = _dot(q_blk.astype(BF16), jnp.concatenate(s_start, axis=0).astype(BF16))
        og_ref[:, h * DV:(h + 1) * DV] = o_intra + o_inter

    gnorm = gnorm_ref[...]
    for h in range(HEADS):
        cols = slice(h * DV, (h + 1) * DV)
        y = _rms(og_ref[:, cols], gnorm) * pm_ref[:, C_GG + h * DV:C_GG + (h + 1) * DV]
        br_out_ref[:, cols] = y.astype(BF16)

    pre = ps + bif_ref[...]
    rr = lax.broadcasted_iota(jnp.int32, (tt, tt), 0)
    cc = lax.broadcasted_iota(jnp.int32, (tt, tt), 1)
    causal = cc <= rr
    fcum = _dot_mask_exact(jnp.where(causal, 1.0, 0.0).astype(BF16), _log_sigmoid(pre))
    pre_t = pre.T
    fcum_t = fcum.T
    q_all = pm_ref[:, C_MQ:C_MQ + HEADS * DK] * (DK ** -0.5)
    k_all = pm_ref[:, C_MK:C_MK + HEADS * DK]
    for h in range(HEADS):
        kcols = slice(h * DK, (h + 1) * DK)
        fc_col = fcum[:, L_MF + h:L_MF + h + 1]
        ic_col = pre[:, L_MI + h:L_MI + h + 1]
        fc_row = fcum_t[L_MF + h:L_MF + h + 1, :]
        ic_row = pre_t[L_MI + h:L_MI + h + 1, :]
        m_prev = m_ref[:, h:h + 1]
        d = jnp.where(causal, fc_col + (ic_row - fc_row), -jnp.inf)
        inter = fc_col + m_prev
        m_t = jnp.maximum(inter, jnp.max(d, axis=1, keepdims=True))
        w_intra = jnp.exp(d - m_t)
        w_inter = jnp.exp(inter - m_t)
        qf = q_all[:, kcols]
        kf = k_all[:, kcols]
        qc = qf.astype(BF16)
        vc = pm_ref[:, C_MV + h * DV:C_MV + (h + 1) * DV].astype(BF16)
        s = _dot_nt(qc, kf.astype(BF16)) * w_intra
        c_old = c_ref[h]
        n_old = n_ref[h]
        num = w_inter * _dot(qc, c_old.astype(BF16)) + _dot(s.astype(BF16), vc)
        den = (w_inter * jnp.sum(qf * n_old, axis=1, keepdims=True)
               + jnp.sum(s, axis=1, keepdims=True))
        hh = num / jnp.maximum(jnp.abs(den), jnp.exp(-m_t))
        og_ref[:, h * DV:(h + 1) * DV] = hh
        f_end = fc_col[tt - 1:tt, :]
        m_end = m_t[tt - 1:tt, :]
        w_k = jnp.exp(f_end - fc_col + ic_col - m_end)
        w_c = jnp.exp(f_end + m_prev - m_end)
        kw = w_k * kf
        c_ref[h] = w_c * c_old + _dot_tn(kw.astype(BF16), vc)
        n_ref[h] = w_c * n_old + jnp.sum(kw, axis=0, keepdims=True)
        m_ref[:, h:h + 1] = m_end

    for h in range(HEADS):
        cols = slice(h * DV, (h + 1) * DV)
        y = (_rms(og_ref[:, cols], mnorm_ref[:, cols])
             * pm_ref[:, C_MO + h * DV:C_MO + (h + 1) * DV])
        br_out_ref[:, BRANCH_W + h * DV:BRANCH_W + (h + 1) * DV] = y.astype(BF16)


def _merge_stage(br_ref, sg_ref, xr_ref, wb_ref, wo_ref, xo_ref, mg_ref):
    n_tiles = D_MODEL // MERGE_TILE
    for jt in range(n_tiles):
        cols = slice(jt * MERGE_TILE, (jt + 1) * MERGE_TILE)
        merged = None
        for b in range(N_BRANCH):
            z = _dot(br_ref[:, b * BRANCH_W:(b + 1) * BRANCH_W], wb_ref[b, :, cols])
            term = sg_ref[:, b * D_MODEL + jt * MERGE_TILE:b * D_MODEL + (jt + 1) * MERGE_TILE] * z
            merged = term if merged is None else merged + term
        mg_ref[:, cols] = merged.astype(BF16)
    for jt in range(n_tiles):
        cols = slice(jt * MERGE_TILE, (jt + 1) * MERGE_TILE)
        xo_ref[:, cols] = xr_ref[:, cols] + _dot(mg_ref[...], wo_ref[:, cols])


N_MIXER_WEIGHTS = 12


def _mix_prompt_kernel(x_ref, g_ref, wmain_ref, wsmall_ref, *refs):
    mixer_w = refs[:N_MIXER_WEIGHTS]
    wb_ref, wo_ref = refs[N_MIXER_WEIGHTS:N_MIXER_WEIGHTS + 2]
    xo_ref, s_ref, c_ref, n_ref, m_ref, h_ref, buf_ref = refs[N_MIXER_WEIGHTS + 2:N_MIXER_WEIGHTS + 9]
    pm_ref, ps_ref, xs_ref, br_ref, sg_ref, mg_ref, og_ref, xpad_ref = refs[N_MIXER_WEIGHTS + 9:]

    @pl.when(pl.program_id(1) == 0)
    def _():
        for ref in (s_ref, c_ref, n_ref, m_ref, h_ref):
            ref[...] = jnp.zeros_like(ref)
        xpad_ref[pl.ds(0, 8), :] = jnp.zeros((8, RG_WIDTH), F32)

    _project_stage(x_ref, g_ref, wmain_ref, wsmall_ref, pm_ref, ps_ref, xs_ref, sg_ref)
    _mixer_stage(pm_ref, ps_ref, *mixer_w, s_ref, c_ref, n_ref, m_ref, h_ref, buf_ref,
                 br_ref, og_ref, xpad_ref)
    _merge_stage(br_ref, sg_ref, xs_ref, wb_ref, wo_ref, xo_ref, mg_ref)


def _mix_prompt(x, g, wts, lw, layer, *, nb, t, tt):
    d = x.shape[1]
    const = lambda shape: pl.BlockSpec(shape, lambda b, j: (0,) * len(shape),
                                       pipeline_mode=pl.Buffered(1))
    slab = lambda shape: pl.BlockSpec((None,) + shape, lambda b, j: (layer,) + (0,) * len(shape),
                                      pipeline_mode=pl.Buffered(1))
    nt = t // tt
    return pl.pallas_call(
        _mix_prompt_kernel,
        grid=(nb, nt),
        in_specs=[pl.BlockSpec((tt, d), lambda b, j: (b * nt + j, 0)),
                  const((1, d)), slab((d, MAIN_COLS)), slab((d, SMALL_COLS)),
                  slab((SMALL_COLS, HEADS * DK)), const((1, HEADS * DK)), const((1, DV)),
                  const((1, SMALL_COLS)), const((1, BRANCH_W)),
                  const((RG_CONV, RG_WIDTH)), const((1, RG_WIDTH)),
                  slab((RG_WIDTH, RG_WIDTH)), const((1, RG_WIDTH)),
                  slab((RG_WIDTH, RG_WIDTH)), const((1, RG_WIDTH)), const((1, RG_WIDTH)),
                  slab((N_BRANCH, BRANCH_W, d)), slab((d, d))],
        out_specs=[pl.BlockSpec((tt, d), lambda b, j: (b * nt + j, 0)),
                   pl.BlockSpec((None, HEADS, DK, DV), lambda b, j: (b, 0, 0, 0)),
                   pl.BlockSpec((None, HEADS, DK, DV), lambda b, j: (b, 0, 0, 0)),
                   pl.BlockSpec((None, HEADS, 1, DK), lambda b, j: (b, 0, 0, 0)),
                   pl.BlockSpec((None, 1, HEADS), lambda b, j: (b, 0, 0)),
                   pl.BlockSpec((None, 1, RG_WIDTH), lambda b, j: (b, 0, 0)),
                   pl.BlockSpec((None, RG_CONV - 1, RG_WIDTH), lambda b, j: (b, 0, 0))],
        out_shape=[jax.ShapeDtypeStruct((nb * t, d), F32),
                   jax.ShapeDtypeStruct((nb, HEADS, DK, DV), F32),
                   jax.ShapeDtypeStruct((nb, HEADS, DK, DV), F32),
                   jax.ShapeDtypeStruct((nb, HEADS, 1, DK), F32),
                   jax.ShapeDtypeStruct((nb, 1, HEADS), F32),
                   jax.ShapeDtypeStruct((nb, 1, RG_WIDTH), F32),
                   jax.ShapeDtypeStruct((nb, RG_CONV - 1, RG_WIDTH), F32)],
        scratch_shapes=[pltpu.VMEM((tt, MIX_COLS), F32),
                        pltpu.VMEM((tt, SMALL_COLS), F32),
                        pltpu.VMEM((tt, d), F32),
                        pltpu.VMEM((tt, N_BRANCH * BRANCH_W), BF16),
                        pltpu.VMEM((tt, GATE_COLS), F32),
                        pltpu.VMEM((tt, d), BF16),
                        pltpu.VMEM((tt, BRANCH_W), F32),
                        pltpu.VMEM((tt + 8, RG_WIDTH), F32)],
        compiler_params=pltpu.CompilerParams(
            dimension_semantics=("parallel", "arbitrary"), vmem_limit_bytes=VMEM_LIMIT),
        name="mix_prompt",
    )(x, g.reshape(1, d), wts["w_main"], wts["w_small"],
      wts["wa2"], lw["ba"], lw["gnorm"], lw["bif"], lw["mnorm"], lw["convw"], lw["convb"],
      wts["wr"], lw["br"], wts["wi"], lw["bi"], lw["lam"], wts["wb"], wts["wo"])


def _mix_sample_kernel(pm_ref, ps_ref, s0_ref, c0_ref, n0_ref, m0_ref, h0_ref, buf0_ref,
                       wa2_ref, ba_ref, gnorm_ref, bif_ref, mnorm_ref,
                       convw_ref, convb_ref, wr_ref, br_ref, wi_ref, bi_ref, lam_ref,
                       br_out_ref, s_ref, c_ref, n_ref, m_ref, h_ref, buf_ref,
                       og_ref, om_ref):
    bs = pm_ref.shape[0]
    ps = ps_ref[...]
    a_pre = _dot(ps.astype(BF16), wa2_ref[...]) + ba_ref[...]
    a = jnp.exp(_log_sigmoid(a_pre) * (1.0 / GLA_GATE_NORM))
    a_hi = a.astype(BF16)
    a_r1 = a - a_hi.astype(F32)
    a_mid = a_r1.astype(BF16)
    a_lo = (a_r1 - a_mid.astype(F32)).astype(BF16)
    mq = pm_ref[:, C_MQ:C_MQ + HEADS * DK] * (DK ** -0.5)
    mk = pm_ref[:, C_MK:C_MK + HEADS * DK]
    kd = HEADS * DK

    pre = ps + bif_ref[...]
    f_log = pltpu.roll(_log_sigmoid(pre), SMALL_COLS - (L_MF - L_MI), 1)
    inter = f_log + m0_ref[...]
    m_t = jnp.maximum(inter, pre)
    lane = lax.broadcasted_iota(jnp.int32, (bs, SMALL_COLS), 1)
    is_head = (lane >= L_MI) & (lane < L_MI + HEADS)
    w_intra = jnp.where(is_head, jnp.exp(pre - m_t), 0.0)
    w_inter = jnp.where(is_head, jnp.exp(inter - m_t), 0.0)
    kd_row = lax.broadcasted_iota(jnp.int32, (kd, SMALL_COLS), 0)
    head_sum = jnp.where(
        lax.shift_right_logical(kd_row, DK.bit_length() - 1) + L_MI
        == lax.broadcasted_iota(jnp.int32, (kd, SMALL_COLS), 1), 1.0, 0.0).astype(BF16)
    sc = _dot_mask_exact_rhs(mq * mk, head_sum) * w_intra
    den = w_inter * _dot_mask_exact_rhs(mq * n0_ref[...], head_sum) + sc
    dmax = jnp.where(is_head, jnp.maximum(jnp.abs(den), jnp.exp(-m_t)), 1.0)
    m_ref[...] = m_t
    lane_head = lax.broadcasted_iota(jnp.int32, (SMALL_COLS, kd), 0) - L_MI
    head_spread = jnp.where(
        lane_head == lax.shift_right_logical(
            lax.broadcasted_iota(jnp.int32, (SMALL_COLS, kd), 1), DK.bit_length() - 1),
        1.0, 0.0).astype(BF16)
    n_ref[...] = (_dot_mask_exact_rhs(w_inter, head_spread) * n0_ref[...]
                  + _dot_mask_exact_rhs(w_intra, head_spread) * mk)

    scalars = jnp.concatenate([w_inter, w_intra, sc, dmax], axis=1)
    sc_hi = scalars.astype(BF16)
    sc_r1 = scalars - sc_hi.astype(F32)
    sc_mid = sc_r1.astype(BF16)
    sc_lo = (sc_r1 - sc_mid.astype(F32)).astype(BF16)
    n_scalar = 4

    def scalar_rows(piece):
        pt = piece.astype(F32).T
        return jnp.concatenate(
            [pt[i * SMALL_COLS + L_MI:i * SMALL_COLS + L_MI + 8, :] for i in range(n_scalar)],
            axis=0)

    cols_t = jnp.concatenate(
        [(pm_ref[:, C_GQ:C_GQ + kd] * (DK ** -0.5)).T, pm_ref[:, C_GK:C_GK + kd].T,
         mq.T, mk.T, a_hi.astype(F32).T, a_mid.astype(F32).T, a_lo.astype(F32).T,
         scalar_rows(sc_hi), scalar_rows(sc_mid), scalar_rows(sc_lo)],
        axis=0).astype(BF16)
    o_gq, o_gk, o_mq, o_mk, o_a0, o_a1, o_a2 = (i * kd for i in range(7))
    o_scalar = 7 * kd
    sample_id = lax.broadcasted_iota(jnp.int32, (bs, DV), 0)

    for s in range(bs):
        row = slice(s, s + 1)
        bc = _dot(cols_t, jnp.where(sample_id == s, 1.0, 0.0).astype(BF16))
        for h in range(HEADS):
            col = lambda off: bc[off + h * DK:off + (h + 1) * DK, :]

            def scalar(i):
                r0 = o_scalar + i * 8 + h
                return (bc[r0:r0 + 1, :] + bc[r0 + 8 * n_scalar:r0 + 8 * n_scalar + 1, :]
                        + bc[r0 + 16 * n_scalar:r0 + 16 * n_scalar + 1, :])

            v_row = pm_ref[row, C_GV + h * DV:C_GV + (h + 1) * DV]
            a_col = col(o_a0) + col(o_a1) + col(o_a2)
            s_new = a_col * s0_ref[s, h] + col(o_gk) * v_row
            s_ref[s, h] = s_new
            og_ref[row, h * DV:(h + 1) * DV] = jnp.sum(col(o_gq) * s_new,
                                                       axis=0, keepdims=True)
            v_row = pm_ref[row, C_MV + h * DV:C_MV + (h + 1) * DV]
            c_old = c0_ref[s, h]
            qc = jnp.sum(col(o_mq) * c_old, axis=0, keepdims=True)
            om_ref[row, h * DV:(h + 1) * DV] = (scalar(0) * qc + scalar(2) * v_row) / scalar(3)
            c_ref[s, h] = scalar(0) * c_old + (scalar(1) * col(o_mk)) * v_row

    gnorm = gnorm_ref[...]
    for h in range(HEADS):
        cols = slice(h * DV, (h + 1) * DV)
        y = _rms(og_ref[:, cols], gnorm) * _silu(pm_ref[:, C_GG + h * DV:C_GG + (h + 1) * DV])
        br_out_ref[:, cols] = y.astype(BF16)
        y = (_rms(om_ref[:, cols], mnorm_ref[:, cols])
             * _sigmoid(pm_ref[:, C_MO + h * DV:C_MO + (h + 1) * DV]))
        br_out_ref[:, BRANCH_W + h * DV:BRANCH_W + (h + 1) * DV] = y.astype(BF16)

    x = pm_ref[:, C_RX:C_RX + RG_WIDTH]
    xc = convb_ref[...] + convw_ref[3:4, :] * x
    for jj in range(RG_CONV - 1):
        xc = xc + convw_ref[jj:jj + 1, :] * buf0_ref[:, jj * RG_WIDTH:(jj + 1) * RG_WIDTH]
    buf_ref[:, 0:2 * RG_WIDTH] = buf0_ref[:, RG_WIDTH:3 * RG_WIDTH]
    buf_ref[:, 2 * RG_WIDTH:3 * RG_WIDTH] = x
    a, b = _rg_gates(xc, wr_ref, br_ref, wi_ref, bi_ref, lam_ref)
    hnew = a * h0_ref[...] + b
    h_ref[...] = hnew
    y = hnew * _gelu_tanh(pm_ref[:, C_RG:C_RG + RG_WIDTH])
    br_out_ref[:, 2 * BRANCH_W:3 * BRANCH_W] = y.astype(BF16)


def _mix_sample(pm, ps, st, wts, lw, layer, *, bs):
    n = pm.shape[0]
    const = lambda shape: pl.BlockSpec(shape, lambda i: (0,) * len(shape))
    slab = lambda shape: pl.BlockSpec((None,) + shape, lambda i: (layer,) + (0,) * len(shape))
    blk = lambda shape: pl.BlockSpec(shape, lambda i: (i,) + (0,) * (len(shape) - 1))
    lblk = lambda shape: pl.BlockSpec((None,) + shape,
                                      lambda i: (layer, i) + (0,) * (len(shape) - 1))
    s0, c0, n0, m0, h0, buf0 = st
    shapes = [(bs, HEADS, DK, DV), (bs, HEADS, DK, DV), (bs, HEADS * DK),
              (bs, SMALL_COLS), (bs, RG_WIDTH), (bs, (RG_CONV - 1) * RG_WIDTH)]
    return pl.pallas_call(
        _mix_sample_kernel,
        grid=(n // bs,),
        in_specs=[blk((bs, MIX_COLS)), blk((bs, SMALL_COLS))] + [lblk(sh) for sh in shapes] + [
            slab((SMALL_COLS, HEADS * DK)), const((1, HEADS * DK)), const((1, DV)),
            const((1, SMALL_COLS)), const((1, BRANCH_W)),
            const((RG_CONV, RG_WIDTH)), const((1, RG_WIDTH)),
            slab((RG_WIDTH, RG_WIDTH)), const((1, RG_WIDTH)),
            slab((RG_WIDTH, RG_WIDTH)), const((1, RG_WIDTH)), const((1, RG_WIDTH))],
        out_specs=[blk((bs, N_BRANCH * BRANCH_W))] + [blk(sh) for sh in shapes],
        out_shape=[jax.ShapeDtypeStruct((n, N_BRANCH * BRANCH_W), BF16)]
        + [jax.ShapeDtypeStruct(a.shape[1:], F32) for a in st],
        scratch_shapes=[pltpu.VMEM((bs, BRANCH_W), F32),
                        pltpu.VMEM((bs, BRANCH_W), F32)],
        compiler_params=pltpu.CompilerParams(
            dimension_semantics=("parallel",), vmem_limit_bytes=VMEM_LIMIT),
        name="mix_sample",
    )(pm, ps, s0, c0, n0, m0, h0, buf0,
      wts["wa2"], lw["ba"], lw["gnorm"], lw["bif"], lw["mnorm"], lw["convw"], lw["convb"],
      wts["wr"], lw["br"], wts["wi"], lw["bi"], lw["lam"])


def _merge_kernel(x_ref, br_ref, g0_ref, g1_ref, g2_ref, wb_ref, wo_ref, o_ref):
    merged = None
    for b, g_ref in enumerate((g0_ref, g1_ref, g2_ref)):
        z = _dot(br_ref[:, b * BRANCH_W:(b + 1) * BRANCH_W], wb_ref[b])
        term = _sigmoid(g_ref[...]) * z
        merged = term if merged is None else merged + term
    o_ref[...] = x_ref[...] + _dot(merged.astype(BF16), wo_ref[...])


def _merge(x, br, pm, wb, wo, layer, *, tm):
    n, d = x.shape
    gate_blk0 = MIX_COLS // d
    gate_spec = lambda b: pl.BlockSpec((tm, d), lambda i: (i, gate_blk0 + b))
    return pl.pallas_call(
        _merge_kernel,
        grid=(n // tm,),
        in_specs=[pl.BlockSpec((tm, d), lambda i: (i, 0)),
                  pl.BlockSpec((tm, N_BRANCH * BRANCH_W), lambda i: (i, 0)),
                  gate_spec(0), gate_spec(1), gate_spec(2),
                  pl.BlockSpec((None, N_BRANCH, BRANCH_W, d), lambda i: (layer, 0, 0, 0)),
                  pl.BlockSpec((None, d, d), lambda i: (layer, 0, 0))],
        out_specs=pl.BlockSpec((tm, d), lambda i: (i, 0)),
        out_shape=jax.ShapeDtypeStruct((n, d), F32),
        compiler_params=pltpu.CompilerParams(
            dimension_semantics=("parallel",), vmem_limit_bytes=VMEM_LIMIT),
        name="merge",
    )(x, br, pm, pm, pm, wb, wo)


def _xattn_ffn_prompt_kernel(x_ref, gx_ref, wq_ref, wo_ref, k_ref, v_ref,
                             gf_ref, wg_ref, wu_ref, wd_ref, gfin_ref,
                             o_ref, kb_ref, vb_ref, hid_ref, *, final_norm):
    @pl.when(pl.program_id(1) == 0)
    def _():
        kb_ref[...] = k_ref[...].astype(BF16)
        vb_ref[...] = v_ref[...].astype(BF16)

    x = x_ref[...]
    q = _dot(_rms(x, gx_ref[...]).astype(BF16), wq_ref[...]).astype(BF16)
    heads = []
    for h in range(XA_HEADS):
        cols = slice(h * XA_HEAD_DIM, (h + 1) * XA_HEAD_DIM)
        s = _dot_nt(q[:, cols], kb_ref[:, cols]) * (XA_HEAD_DIM ** -0.5)
        p = jnp.exp(s - jnp.max(s, axis=-1, keepdims=True))
        p = p / jnp.sum(p, axis=-1, keepdims=True)
        heads.append(_dot(p.astype(BF16), vb_ref[:, cols]))
    o = jnp.concatenate(heads, axis=1).astype(BF16)
    x = x + _dot(o, wo_ref[...])
    o_ref[...] = _ffn_block(x, gf_ref, wg_ref, wu_ref, wd_ref, gfin_ref, hid_ref, final_norm)


def _xattn_ffn_prompt(x, gx, wq, wo, mem_k, mem_v, gf, wg, wu, wd, gfin, layer, *,
                      nb, t, tm, final_norm):
    d = x.shape[1]
    f = wg.shape[2]
    nt = t // tm
    const = lambda shape: pl.BlockSpec(shape, lambda b, j: (0,) * len(shape),
                                       pipeline_mode=pl.Buffered(1))
    slab = lambda shape: pl.BlockSpec((None,) + shape, lambda b, j: (layer,) + (0,) * len(shape),
                                      pipeline_mode=pl.Buffered(1))
    return pl.pallas_call(
        functools.partial(_xattn_ffn_prompt_kernel, final_norm=final_norm),
        grid=(nb, nt),
        in_specs=[pl.BlockSpec((tm, d), lambda b, j: (b * nt + j, 0)),
                  const((1, d)), slab((d, d)), slab((d, d)),
                  pl.BlockSpec((None, N_MEM, d), lambda b, j: (b, 0, 0)),
                  pl.BlockSpec((None, N_MEM, d), lambda b, j: (b, 0, 0)),
                  const((1, d)), slab((d, f)), slab((d, f)), slab((f, d)), const((1, d))],
        out_specs=pl.BlockSpec((tm, d), lambda b, j: (b * nt + j, 0)),
        out_shape=jax.ShapeDtypeStruct(x.shape, F32),
        scratch_shapes=[pltpu.VMEM((N_MEM, d), BF16), pltpu.VMEM((N_MEM, d), BF16),
                        pltpu.VMEM((tm, f), BF16)],
        compiler_params=pltpu.CompilerParams(
            dimension_semantics=("parallel", "arbitrary"), vmem_limit_bytes=VMEM_LIMIT),
        name="xattn_ffn_prompt",
    )(x, gx.reshape(1, d), wq, wo, mem_k, mem_v, gf.reshape(1, d), wg, wu, wd, gfin.reshape(1, d))


def _xattn_sample_kernel(q_ref, k_ref, v_ref, o_ref):
    bs = q_ref.shape[0]
    nrow = N_MEM * XA_HEADS
    row_head = lax.broadcasted_iota(jnp.int32, (XA_HEADS, nrow), 1) & (XA_HEADS - 1)
    own_head = row_head == lax.broadcasted_iota(jnp.int32, (XA_HEADS, nrow), 0)
    for s in range(bs):
        q = q_ref[s].astype(BF16)
        k2 = k_ref[s].reshape(nrow, XA_HEAD_DIM).astype(BF16)
        v2 = v_ref[s].reshape(nrow, XA_HEAD_DIM).astype(BF16)
        sc = jnp.where(own_head, _dot_nt(q, k2) * (XA_HEAD_DIM ** -0.5), -jnp.inf)
        p = jnp.exp(sc - jnp.max(sc, axis=1, keepdims=True))
        p = p / jnp.sum(p, axis=1, keepdims=True)
        o_ref[s] = _dot(p.astype(BF16), v2)


def _xattn_sample(q, cache_k, cache_v, layer, *, bs):
    n = q.shape[0]
    blk = (None, bs, N_MEM, XA_HEADS, XA_HEAD_DIM)
    return pl.pallas_call(
        _xattn_sample_kernel,
        grid=(n // bs,),
        in_specs=[pl.BlockSpec((bs, XA_HEADS, XA_HEAD_DIM), lambda i: (i, 0, 0)),
                  pl.BlockSpec(blk, lambda i: (layer, i, 0, 0, 0)),
                  pl.BlockSpec(blk, lambda i: (layer, i, 0, 0, 0))],
        out_specs=pl.BlockSpec((bs, XA_HEADS, XA_HEAD_DIM), lambda i: (i, 0, 0)),
        out_shape=jax.ShapeDtypeStruct((n, XA_HEADS, XA_HEAD_DIM), F32),
        compiler_params=pltpu.CompilerParams(
            dimension_semantics=("parallel",), vmem_limit_bytes=VMEM_LIMIT),
        name="xattn_sample",
    )(q, cache_k, cache_v)


def _matmul_residual_kernel(x_ref, a_ref, w_ref, o_ref):
    o_ref[...] = x_ref[...] + _dot(a_ref[...].astype(BF16), w_ref[...])


def _matmul_residual(x, a, w, layer):
    n, d = x.shape
    return pl.pallas_call(
        _matmul_residual_kernel,
        grid=(1,),
        in_specs=[pl.BlockSpec((n, d), lambda i: (0, 0)),
                  pl.BlockSpec((n, d), lambda i: (0, 0)),
                  pl.BlockSpec((None, d, d), lambda i: (layer, 0, 0))],
        out_specs=pl.BlockSpec((n, d), lambda i: (0, 0)),
        out_shape=jax.ShapeDtypeStruct((n, d), F32),
        compiler_params=pltpu.CompilerParams(vmem_limit_bytes=VMEM_LIMIT),
        name="matmul_residual",
    )(x, a, w)


def _ffn_block(x, g_ref, wg_ref, wu_ref, wd_ref, gfin_ref, hid_ref, final_norm):
    u = _rms(x, g_ref[...]).astype(BF16)
    for jt in range(D_FF // FFN_TILE):
        cols = slice(jt * FFN_TILE, (jt + 1) * FFN_TILE)
        hid = _silu(_dot(u, wg_ref[:, cols])) * _dot(u, wu_ref[:, cols])
        hid_ref[:, cols] = hid.astype(BF16)
    y = x + _dot(hid_ref[...], wd_ref[...])
    if final_norm:
        y = _rms(y, gfin_ref[...])
    return y


def _ffn_kernel(x_ref, g_ref, wg_ref, wu_ref, wd_ref, gf_ref, o_ref, hid_ref, *, final_norm):
    o_ref[...] = _ffn_block(x_ref[...], g_ref, wg_ref, wu_ref, wd_ref, gf_ref, hid_ref, final_norm)


def _ffn(x, g, wg, wu, wd, gf, layer, *, tm, final_norm):
    n, d = x.shape
    f = wg.shape[2]
    const = lambda shape: pl.BlockSpec(shape, lambda i: (0,) * len(shape),
                                       pipeline_mode=pl.Buffered(1))
    slab = lambda shape: pl.BlockSpec((None,) + shape, lambda i: (layer,) + (0,) * len(shape),
                                      pipeline_mode=pl.Buffered(1))
    return pl.pallas_call(
        functools.partial(_ffn_kernel, final_norm=final_norm),
        grid=(n // tm,),
        in_specs=[pl.BlockSpec((tm, d), lambda i: (i, 0)),
                  const((1, d)), slab((d, f)), slab((d, f)), slab((f, d)), const((1, d))],
        out_specs=pl.BlockSpec((tm, d), lambda i: (i, 0)),
        out_shape=jax.ShapeDtypeStruct((n, d), F32),
        scratch_shapes=[pltpu.VMEM((tm, f), BF16)],
        compiler_params=pltpu.CompilerParams(
            dimension_semantics=("parallel",), vmem_limit_bytes=VMEM_LIMIT),
        name="ffn",
    )(x, g.reshape(1, d), wg, wu, wd, gf.reshape(1, d))


def _split_w_in(w_in):
    widths = (HEADS * DK, HEADS * DK, HEADS * DV, HEADS * DV, GLA_RANK,
              HEADS * DK, HEADS * DK, HEADS * DV, HEADS * DV, HEADS, HEADS,
              RG_WIDTH, RG_WIDTH, GATE_COLS)
    parts, acc = [], 0
    for w in widths:
        parts.append(w_in[..., acc:acc + w])
        acc += w
    return parts


def _block_diag(w):
    eye = jnp.eye(RG_BLOCKS, dtype=w.dtype)
    return jnp.einsum("lnde,nm->lndme", w, eye).reshape(DEPTH, RG_WIDTH, RG_WIDTH)


def _w_in_relayout_kernel(w_ref, main_ref, small_ref):
    (g_q, g_k, g_v, g_g, g_a, m_q, m_k, m_v, m_o, m_i, m_f, r_x, r_g, gates) = _split_w_in(w_ref[...])
    main = jnp.concatenate([g_q, g_k, g_v, g_g, m_q, m_k, m_v, m_o, r_x, r_g, gates], axis=1)
    main_ref[...] = main.astype(BF16)
    pad = jnp.zeros((w_ref.shape[0], SMALL_COLS - GLA_RANK - 2 * HEADS), F32)
    small_ref[...] = jnp.concatenate([g_a, m_i, m_f, pad], axis=1).astype(BF16)


def _w_in_relayout(w_in, *, tr):
    depth, d, cols = w_in.shape
    return pl.pallas_call(
        _w_in_relayout_kernel,
        grid=(depth, d // tr),
        in_specs=[pl.BlockSpec((None, tr, cols), lambda l, i: (l, i, 0))],
        out_specs=[pl.BlockSpec((None, tr, MAIN_COLS), lambda l, i: (l, i, 0)),
                   pl.BlockSpec((None, tr, SMALL_COLS), lambda l, i: (l, i, 0))],
        out_shape=[jax.ShapeDtypeStruct((depth, d, MAIN_COLS), BF16),
                   jax.ShapeDtypeStruct((depth, d, SMALL_COLS), BF16)],
        compiler_params=pltpu.CompilerParams(
            dimension_semantics=("parallel", "parallel"), vmem_limit_bytes=VMEM_LIMIT),
        name="w_in_relayout",
    )(w_in)


def _matmul_weights(p):
    w_main, w_small = _w_in_relayout(p["w_in"], tr=128)
    wa2 = jnp.concatenate(
        [p["gla_w_a2"], jnp.zeros((DEPTH, SMALL_COLS - GLA_RANK, HEADS * DK), F32)], axis=1)
    return {
        "w_main": w_main, "w_small": w_small, "wa2": wa2.astype(BF16),
        "wr": _block_diag(p["rg_w_r"]).astype(BF16), "wi": _block_diag(p["rg_w_i"]).astype(BF16),
        "wb": p["w_branch"].astype(BF16), "wo": p["w_out"].astype(BF16),
        "xa_wq": p["xa_wq"].astype(BF16), "xa_wo": p["xa_wo"].astype(BF16),
        "xa_wkv": jnp.concatenate([p["xa_wk"], p["xa_wv"]], axis=2).astype(BF16),
        "wg": p["ffn_w_gate"].astype(BF16), "wu": p["ffn_w_up"].astype(BF16),
        "wd": p["ffn_w_down"].astype(BF16),
    }


def _vector_params(l, p):
    bif = jnp.concatenate(
        [jnp.zeros((L_MI,), F32), p["mlstm_b_i"][l], p["mlstm_b_f"][l],
         jnp.zeros((SMALL_COLS - L_MF - HEADS,), F32)]).reshape(1, SMALL_COLS)
    return {
        "ba": p["gla_b_a"][l].reshape(1, -1), "gnorm": p["gla_norm"][l].reshape(1, DV),
        "bif": bif, "mnorm": p["mlstm_norm"][l].reshape(1, BRANCH_W),
        "convw": p["rg_conv_w"][l], "convb": p["rg_conv_b"][l].reshape(1, -1),
        "br": p["rg_b_r"][l].reshape(1, -1), "bi": p["rg_b_i"][l].reshape(1, -1),
        "lam": p["rg_lambda"][l].reshape(1, -1),
    }


def kernel(x_prompt, x_sample, mem_prompt, cache_mem_k, cache_mem_v, state_gla, state_mlstm_c, state_mlstm_n, state_mlstm_m, state_rglru_h, state_rglru_conv, norm_mix, w_in, gla_w_a2, gla_b_a, gla_norm, mlstm_b_i, mlstm_b_f, mlstm_norm, rg_conv_w, rg_conv_b, rg_w_r, rg_b_r, rg_w_i, rg_b_i, rg_lambda, w_branch, w_out, norm_xa, norm_mem, xa_wq, xa_wk, xa_wv, xa_wo, norm_ffn, ffn_w_gate, ffn_w_up, ffn_w_down, norm_final):
    p = {"w_in": w_in, "gla_w_a2": gla_w_a2, "gla_b_a": gla_b_a, "gla_norm": gla_norm,
         "mlstm_b_i": mlstm_b_i, "mlstm_b_f": mlstm_b_f, "mlstm_norm": mlstm_norm,
         "rg_conv_w": rg_conv_w, "rg_conv_b": rg_conv_b, "rg_w_r": rg_w_r, "rg_b_r": rg_b_r,
         "rg_w_i": rg_w_i, "rg_b_i": rg_b_i, "rg_lambda": rg_lambda, "w_branch": w_branch,
         "w_out": w_out, "xa_wq": xa_wq, "xa_wk": xa_wk, "xa_wv": xa_wv, "xa_wo": xa_wo,
         "ffn_w_gate": ffn_w_gate, "ffn_w_up": ffn_w_up, "ffn_w_down": ffn_w_down}
    nb, t, d = x_prompt.shape
    ns = x_sample.shape[0]
    xp = x_prompt.reshape(nb * t, d)
    xs = x_sample.reshape(ns, d)
    mem = mem_prompt.reshape(nb * N_MEM, d)
    wts = _matmul_weights(p)

    st_sample = (state_gla, state_mlstm_c, state_mlstm_n.reshape(DEPTH, ns, HEADS * DK),
                 jnp.pad(state_mlstm_m, ((0, 0), (0, 0), (L_MI, SMALL_COLS - L_MI - HEADS))),
                 state_rglru_h,
                 state_rglru_conv.reshape(DEPTH, ns, (RG_CONV - 1) * RG_WIDTH))

    new_p = [[] for _ in range(8)]
    new_s = [[] for _ in range(6)]
    for l in range(DEPTH):
        lw = _vector_params(l, p)
        last = l == DEPTH - 1

        k_p, v_p = _norm_matmul(mem, norm_mem[l], wts["xa_wkv"], l, tm=512, tn=d)
        k_p = k_p.reshape(nb, N_MEM, d)
        v_p = v_p.reshape(nb, N_MEM, d)

        xp, g_s, c_s, n_s, m_s, h_s, buf_s = _mix_prompt(xp, norm_mix[l], wts, lw, l,
                                                         nb=nb, t=t, tt=256)
        xp = _xattn_ffn_prompt(xp, norm_xa[l], wts["xa_wq"], wts["xa_wo"], k_p, v_p,
                               norm_ffn[l], wts["wg"], wts["wu"], wts["wd"], norm_final, l,
                               nb=nb, t=t, tm=512, final_norm=last)
        for lst, val in zip(new_p, (k_p.reshape(nb, N_MEM, XA_HEADS, XA_HEAD_DIM),
                                    v_p.reshape(nb, N_MEM, XA_HEADS, XA_HEAD_DIM),
                                    g_s, c_s, n_s.reshape(nb, HEADS, DK), m_s.reshape(nb, HEADS),
                                    h_s.reshape(nb, RG_WIDTH), buf_s)):
            lst.append(val)

        pm, ps = _in_proj(xs, norm_mix[l], wts["w_main"], wts["w_small"], l, tm=ns, tn=1024)
        br, g_s, c_s, n_s, m_s, h_s, buf_s = _mix_sample(pm, ps, st_sample, wts, lw, l, bs=8)
        xs = _merge(xs, br, pm, wts["wb"], wts["wo"], l, tm=ns)
        (q,) = _norm_matmul(xs, norm_xa[l], wts["xa_wq"], l, tm=ns, tn=d)
        att = _xattn_sample(q.reshape(ns, XA_HEADS, XA_HEAD_DIM), cache_mem_k, cache_mem_v, l, bs=8)
        xs = _matmul_residual(xs, att.reshape(ns, d), wts["xa_wo"], l)
        xs = _ffn(xs, norm_ffn[l], wts["wg"], wts["wu"], wts["wd"], norm_final, l,
                  tm=ns, final_norm=last)
        for lst, val in zip(new_s, (g_s, c_s, n_s.reshape(ns, HEADS, DK),
                                    m_s[:, L_MI:L_MI + HEADS], h_s,
                                    buf_s.reshape(ns, RG_CONV - 1, RG_WIDTH))):
            lst.append(val)

    y_prompt = xp.reshape(nb, t, d)
    y_sample = xs.reshape(ns, 1, d)
    outs_p = [jnp.stack(v, axis=0) for v in new_p]
    outs_s = [jnp.stack(v, axis=0) for v in new_s]
    return (y_prompt, y_sample, *outs_p, *outs_s)
```

```python
import functools
import math

import jax
import jax.numpy as jnp
from jax import lax
from jax.experimental import pallas as pl
from jax.experimental.pallas import tpu as pltpu

F32 = jnp.float32
BF16 = jnp.bfloat16

D_MODEL = 1024
DEPTH = 2
EPS = 1e-6
N_MEM = 256
XA_HEADS = 4
XA_HEAD_DIM = D_MODEL // XA_HEADS
N_BRANCH = 3
BRANCH_W = D_MODEL // 2
HEADS = 4
DK = 64
DV = 128
GLA_RANK = 16
GLA_GATE_NORM = 16.0
GLA_CHUNK = 32
RG_WIDTH = BRANCH_W
RG_BLOCKS = 8
RG_CONV = 4
RG_C = 8.0
D_FF = 2816

C_GQ, C_GK, C_GV, C_GG = 0, 256, 512, 1024
C_MQ, C_MK, C_MV, C_MO = 1536, 1792, 2048, 2560
C_RX, C_RG = 3072, 3584
MIX_COLS = 4096
GATE_COLS = N_BRANCH * D_MODEL
MAIN_COLS = MIX_COLS + GATE_COLS
SMALL_COLS = 128
L_MI = 16
L_MF = 20

MXU_WIDTH = 256
PROJ_TILE = MXU_WIDTH
MERGE_TILE = MXU_WIDTH
FFN_TILE = MXU_WIDTH
MIX_TIME_BLOCK = 256
PROMPT_ROWS = 512
SAMPLE_BLOCK = 8
IN_PROJ_COLS = 1024
VMEM_LIMIT = 48 * 1024 * 1024


def _softplus(x):
    return jnp.maximum(x, 0.0) + jnp.log1p(jnp.exp(-jnp.abs(x)))


def _log_sigmoid(x):
    return -_softplus(-x)


def _sigmoid(x):
    return 1.0 / (1.0 + jnp.exp(-x))


def _silu(x):
    return x * _sigmoid(x)


def _gelu_tanh(x):
    c = math.sqrt(2.0 / math.pi)
    return x * (0.5 * (1.0 + jnp.tanh(c * (x + 0.044715 * (x * x * x)))))


def _neg_expm1(x):
    return -jnp.tanh(0.5 * x) * (jnp.exp(x) + 1.0)


def _rms(x, g):
    ms = jnp.mean(x * x, axis=-1, keepdims=True)
    return x * lax.rsqrt(ms + EPS) * g


def _dot(a, b):
    return jnp.dot(a, b, preferred_element_type=F32)


def _dot_nt(a, b):
    return lax.dot_general(a, b, (((1,), (1,)), ((), ())), preferred_element_type=F32)


def _dot_tn(a, b):
    return lax.dot_general(a, b, (((0,), (0,)), ((), ())), preferred_element_type=F32)


def _dot_mask_exact(mask, x):
    hi = x.astype(BF16)
    rest = x - hi.astype(F32)
    mid = rest.astype(BF16)
    lo = (rest - mid.astype(F32)).astype(BF16)
    return _dot(mask, hi) + _dot(mask, mid) + _dot(mask, lo)


def _dot_mask_exact_rhs(x, mask):
    hi = x.astype(BF16)
    rest = x - hi.astype(F32)
    mid = rest.astype(BF16)
    lo = (rest - mid.astype(F32)).astype(BF16)
    return _dot(hi, mask) + _dot(mid, mask) + _dot(lo, mask)


def _chunk_masks(n, chunk):
    shift = chunk.bit_length() - 1
    r = lax.broadcasted_iota(jnp.int32, (n, n), 0)
    c = lax.broadcasted_iota(jnp.int32, (n, n), 1)
    same = lax.shift_right_logical(r, shift) == lax.shift_right_logical(c, shift)
    tril = jnp.where(same & (c <= r), 1.0, 0.0).astype(F32)
    full = jnp.where(same, 1.0, 0.0).astype(F32)
    return tril, full


def _norm_matmul_kernel(x_ref, g_ref, w_ref, *refs):
    out_refs, u_ref = refs[:-1], refs[-1]
    j = pl.program_id(1)

    @pl.when(j == 0)
    def _():
        u_ref[...] = _rms(x_ref[...], g_ref[...]).astype(BF16)

    res = _dot(u_ref[...], w_ref[...])
    for idx, o_ref in enumerate(out_refs):
        @pl.when(j == idx)
        def _(o_ref=o_ref):
            o_ref[...] = res


def _norm_matmul(x, g, w, layer, *, tm, tn):
    n, d = x.shape
    nout = w.shape[2]
    n_out = nout // tn
    return pl.pallas_call(
        _norm_matmul_kernel,
        grid=(n // tm, n_out),
        in_specs=[pl.BlockSpec((tm, d), lambda i, j: (i, 0)),
                  pl.BlockSpec((1, d), lambda i, j: (0, 0)),
                  pl.BlockSpec((None, d, tn), lambda i, j: (layer, 0, j))],
        out_specs=[pl.BlockSpec((tm, tn), lambda i, j: (i, 0))] * n_out,
        out_shape=[jax.ShapeDtypeStruct((n, tn), F32)] * n_out,
        scratch_shapes=[pltpu.VMEM((tm, d), BF16)],
        compiler_params=pltpu.CompilerParams(
            dimension_semantics=("parallel", "arbitrary"), vmem_limit_bytes=VMEM_LIMIT),
        name="norm_matmul",
    )(x, g.reshape(1, d), w)


def _in_proj_kernel(x_ref, g_ref, w_ref, ws_ref, o_ref, os_ref, u_ref):
    @pl.when(pl.program_id(1) == 0)
    def _():
        u = _rms(x_ref[...], g_ref[...]).astype(BF16)
        u_ref[...] = u
        os_ref[...] = _dot(u, ws_ref[...])

    o_ref[...] = _dot(u_ref[...], w_ref[...])


def _in_proj(x, g, w_main, w_small, layer, *, tm, tn):
    n, d = x.shape
    return pl.pallas_call(
        _in_proj_kernel,
        grid=(n // tm, MAIN_COLS // tn),
        in_specs=[pl.BlockSpec((tm, d), lambda i, j: (i, 0)),
                  pl.BlockSpec((1, d), lambda i, j: (0, 0)),
                  pl.BlockSpec((None, d, tn), lambda i, j: (layer, 0, j)),
                  pl.BlockSpec((None, d, SMALL_COLS), lambda i, j: (layer, 0, 0))],
        out_specs=[pl.BlockSpec((tm, tn), lambda i, j: (i, j)),
                   pl.BlockSpec((tm, SMALL_COLS), lambda i, j: (i, 0))],
        out_shape=[jax.ShapeDtypeStruct((n, MAIN_COLS), F32),
                   jax.ShapeDtypeStruct((n, SMALL_COLS), F32)],
        scratch_shapes=[pltpu.VMEM((tm, d), BF16)],
        compiler_params=pltpu.CompilerParams(
            dimension_semantics=("parallel", "arbitrary"), vmem_limit_bytes=VMEM_LIMIT),
        name="in_proj",
    )(x, g.reshape(1, d), w_main, w_small)


def _rg_gates(xc, wr_ref, br_ref, wi_ref, bi_ref, lam_ref):
    xcb = xc.astype(BF16)
    r = _sigmoid(_dot(xcb, wr_ref[...]) + br_ref[...])
    i = _sigmoid(_dot(xcb, wi_ref[...]) + bi_ref[...])
    log_a = (-RG_C * r) * _softplus(-lam_ref[...])
    a = jnp.exp(log_a)
    b = jnp.sqrt(_neg_expm1(2.0 * log_a)) * (i * xc)
    return a, b


def _projection_activation(col):
    if C_GG <= col < C_GG + HEADS * DV:
        return _silu
    if C_MO <= col < C_MO + HEADS * DV:
        return _sigmoid
    if C_RG <= col < C_RG + RG_WIDTH:
        return _gelu_tanh
    if col >= MIX_COLS:
        return _sigmoid
    return None


def _project_stage(x_ref, g_ref, wmain_ref, wsmall_ref, pm_ref, ps_ref, xs_ref, sg_ref):
    x = x_ref[...]
    xs_ref[...] = x
    u = _rms(x, g_ref[...]).astype(BF16)
    ps_ref[...] = _dot(u, wsmall_ref[...])
    for cb in range(MAIN_COLS // PROJ_TILE):
        c0 = cb * PROJ_TILE
        y = _dot(u, wmain_ref[:, c0:c0 + PROJ_TILE])
        act = _projection_activation(c0)
        if act is not None:
            y = act(y)
        if c0 >= MIX_COLS:
            sg_ref[:, c0 - MIX_COLS:c0 - MIX_COLS + PROJ_TILE] = y
        else:
            pm_ref[:, c0:c0 + PROJ_TILE] = y


def _mixer_stage(pm_ref, ps_ref,
                 wa2_ref, ba_ref, gnorm_ref, bif_ref, mnorm_ref,
                 convw_ref, convb_ref, wr_ref, br_ref, wi_ref, bi_ref, lam_ref,
                 s_ref, c_ref, n_ref, m_ref, h_ref, buf_ref,
                 br_out_ref, og_ref, xpad_ref):
    tt = pm_ref.shape[0]

    x = pm_ref[:, C_RX:C_RX + RG_WIDTH]
    xpad_ref[pl.ds(8, tt), :] = x
    xc = (convb_ref[...] + convw_ref[3:4, :] * x
          + convw_ref[2:3, :] * xpad_ref[pl.ds(7, tt), :]
          + convw_ref[1:2, :] * xpad_ref[pl.ds(6, tt), :]
          + convw_ref[0:1, :] * xpad_ref[pl.ds(5, tt), :])
    buf_ref[...] = xpad_ref[pl.ds(tt + 8 - (RG_CONV - 1), RG_CONV - 1), :]
    xpad_ref[pl.ds(0, 8), :] = xpad_ref[pl.ds(tt, 8), :]
    a, b = _rg_gates(xc, wr_ref, br_ref, wi_ref, bi_ref, lam_ref)
    ridx = lax.broadcasted_iota(jnp.int32, (tt, RG_WIDTH), 0)
    sh = 1
    while sh < tt:
        keep = ridx >= sh
        a_sh = jnp.where(keep, pltpu.roll(a, sh, 0), 1.0)
        b_sh = jnp.where(keep, pltpu.roll(b, sh, 0), 0.0)
        b = a * b_sh + b
        a = a * a_sh
        sh *= 2
    hseq = a * h_ref[...] + b
    h_ref[...] = hseq[tt - 1:tt, :]
    y = hseq * pm_ref[:, C_RG:C_RG + RG_WIDTH]
    br_out_ref[:, 2 * BRANCH_W:3 * BRANCH_W] = y.astype(BF16)

    ps = ps_ref[...]
    a_pre = _dot(ps.astype(BF16), wa2_ref[...]) + ba_ref[...]
    log_a = _log_sigmoid(a_pre) * (1.0 / GLA_GATE_NORM)
    tril32, full32 = _chunk_masks(tt, GLA_CHUNK)
    sums = _dot_mask_exact(jnp.concatenate([tril32, full32], axis=0).astype(BF16), log_a)
    b_cum = sums[:tt]
    b_end = sums[tt:]
    q = pm_ref[:, C_GQ:C_GQ + HEADS * DK] * (DK ** -0.5)
    k = pm_ref[:, C_GK:C_GK + HEADS * DK]
    qi_f = q * jnp.exp(b_cum)
    qi_all = qi_f.astype(BF16)
    ki_all = (k * jnp.exp(-b_cum)).astype(BF16)
    ke_t = (k * jnp.exp(b_end - b_cum)).T
    de_t = jnp.exp(b_end).T
    in_chunk = tril32 > 0.0
    n_chunks = tt // GLA_CHUNK
    cshift = GLA_CHUNK.bit_length() - 1
    chunk_of_lane = lax.shift_right_logical(lax.broadcasted_iota(jnp.int32, (DK, tt), 1), cshift)
    chunk_of_row = lax.shift_right_logical(lax.broadcasted_iota(jnp.int32, (tt, 2 * DK), 0), cshift)
    upper_half = (lax.broadcasted_iota(jnp.int32, (tt, 2 * DK), 1) >= DK).astype(jnp.int32)
    for h in range(HEADS):
        kcols = slice(h * DK, (h + 1) * DK)
        v = pm_ref[:, C_GV + h * DV:C_GV + (h + 1) * DV].astype(BF16)
        att = jnp.where(in_chunk, _dot_nt(qi_all[:, kcols], ki_all[:, kcols]), 0.0)
        o_intra = _dot(att.astype(BF16), v)
        ke_h = ke_t[kcols, :]
        ke_blk = jnp.concatenate(
            [jnp.where(chunk_of_lane == c, ke_h, 0.0) for c in range(n_chunks)], axis=0)
        upd = _dot(ke_blk.astype(BF16), v)
        s_cur = s_ref[h]
        s_start = []
        for c in range(n_chunks):
            s_start.append(s_cur)
            de_col = de_t[kcols, c * GLA_CHUNK:c * GLA_CHUNK + 1]
            s_cur = de_col * s_cur + upd[c * DK:(c + 1) * DK, :]
        s_ref[h] = s_cur
        qi_h = qi_f[:, kcols]
        qi_dup = jnp.concatenate([qi_h, qi_h], axis=1)
        q_blk = jnp.concatenate(
            [jnp.where(chunk_of_row == 2 * jj + upper_half, qi_dup, 0.0)
             for jj in range(n_chunks // 2)], axis=1)
        o_inter = _dot(q_blk.astype(BF16), jnp.concatenate(s_start, axis=0).astype(BF16))
        og_ref[:, h * DV:(h + 1) * DV] = o_intra + o_inter

    gnorm = gnorm_ref[...]
    for h in range(HEADS):
        cols = slice(h * DV, (h + 1) * DV)
        y = _rms(og_ref[:, cols], gnorm) * pm_ref[:, C_GG + h * DV:C_GG + (h + 1) * DV]
        br_out_ref[:, cols] = y.astype(BF16)

    pre = ps + bif_ref[...]
    rr = lax.broadcasted_iota(jnp.int32, (tt, tt), 0)
    cc = lax.broadcasted_iota(jnp.int32, (tt, tt), 1)
    causal = cc <= rr
    fcum = _dot_mask_exact(jnp.where(causal, 1.0, 0.0).astype(BF16), _log_sigmoid(pre))
    pre_t = pre.T
    fcum_t = fcum.T
    q_all = pm_ref[:, C_MQ:C_MQ + HEADS * DK] * (DK ** -0.5)
    k_all = pm_ref[:, C_MK:C_MK + HEADS * DK]
    for h in range(HEADS):
        kcols = slice(h * DK, (h + 1) * DK)
        fc_col = fcum[:, L_MF + h:L_MF + h + 1]
        ic_col = pre[:, L_MI + h:L_MI + h + 1]
        fc_row = fcum_t[L_MF + h:L_MF + h + 1, :]
        ic_row = pre_t[L_MI + h:L_MI + h + 1, :]
        m_prev = m_ref[:, h:h + 1]
        d = jnp.where(causal, fc_col + (ic_row - fc_row), -jnp.inf)
        inter = fc_col + m_prev
        m_t = jnp.maximum(inter, jnp.max(d, axis=1, keepdims=True))
        w_intra = jnp.exp(d - m_t)
        w_inter = jnp.exp(inter - m_t)
        qf = q_all[:, kcols]
        kf = k_all[:, kcols]
        qc = qf.astype(BF16)
        vc = pm_ref[:, C_MV + h * DV:C_MV + (h + 1) * DV].astype(BF16)
        s = _dot_nt(qc, kf.astype(BF16)) * w_intra
        c_old = c_ref[h]
        n_old = n_ref[h]
        num = w_inter * _dot(qc, c_old.astype(BF16)) + _dot(s.astype(BF16), vc)
        den = (w_inter * jnp.sum(qf * n_old, axis=1, keepdims=True)
               + jnp.sum(s, axis=1, keepdims=True))
        hh = num / jnp.maximum(jnp.abs(den), jnp.exp(-m_t))
        og_ref[:, h * DV:(h + 1) * DV] = hh
        f_end = fc_col[tt - 1:tt, :]
        m_end = m_t[tt - 1:tt, :]
        w_k = jnp.exp(f_end - fc_col + ic_col - m_end)
        w_c = jnp.exp(f_end + m_prev - m_end)
        kw = w_k * kf
        c_ref[h] = w_c * c_old + _dot_tn(kw.astype(BF16), vc)
        n_ref[h] = w_c * n_old + jnp.sum(kw, axis=0, keepdims=True)
        m_ref[:, h:h + 1] = m_end

    for h in range(HEADS):
        cols = slice(h * DV, (h + 1) * DV)
        y = (_rms(og_ref[:, cols], mnorm_ref[:, cols])
             * pm_ref[:, C_MO + h * DV:C_MO + (h + 1) * DV])
        br_out_ref[:, BRANCH_W + h * DV:BRANCH_W + (h + 1) * DV] = y.astype(BF16)


def _merge_stage(br_ref, sg_ref, xr_ref, wb_ref, wo_ref, xo_ref, mg_ref):
    n_tiles = D_MODEL // MERGE_TILE
    for jt in range(n_tiles):
        cols = slice(jt * MERGE_TILE, (jt + 1) * MERGE_TILE)
        merged = None
        for b in range(N_BRANCH):
            z = _dot(br_ref[:, b * BRANCH_W:(b + 1) * BRANCH_W], wb_ref[b, :, cols])
            term = sg_ref[:, b * D_MODEL + jt * MERGE_TILE:b * D_MODEL + (jt + 1) * MERGE_TILE] * z
            merged = term if merged is None else merged + term
        mg_ref[:, cols] = merged.astype(BF16)
    for jt in range(n_tiles):
        cols = slice(jt * MERGE_TILE, (jt + 1) * MERGE_TILE)
        xo_ref[:, cols] = xr_ref[:, cols] + _dot(mg_ref[...], wo_ref[:, cols])


N_MIXER_WEIGHTS = 12


def _mix_prompt_kernel(x_ref, g_ref, wmain_ref, wsmall_ref, *refs):
    mixer_w = refs[:N_MIXER_WEIGHTS]
    wb_ref, wo_ref = refs[N_MIXER_WEIGHTS:N_MIXER_WEIGHTS + 2]
    xo_ref, s_ref, c_ref, n_ref, m_ref, h_ref, buf_ref = refs[N_MIXER_WEIGHTS + 2:N_MIXER_WEIGHTS + 9]
    pm_ref, ps_ref, xs_ref, br_ref, sg_ref, mg_ref, og_ref, xpad_ref = refs[N_MIXER_WEIGHTS + 9:]

    @pl.when(pl.program_id(1) == 0)
    def _():
        for ref in (s_ref, c_ref, n_ref, m_ref, h_ref):
            ref[...] = jnp.zeros_like(ref)
        xpad_ref[pl.ds(0, 8), :] = jnp.zeros((8, RG_WIDTH), F32)

    _project_stage(x_ref, g_ref, wmain_ref, wsmall_ref, pm_ref, ps_ref, xs_ref, sg_ref)
    _mixer_stage(pm_ref, ps_ref, *mixer_w, s_ref, c_ref, n_ref, m_ref, h_ref, buf_ref,
                 br_ref, og_ref, xpad_ref)
    _merge_stage(br_ref, sg_ref, xs_ref, wb_ref, wo_ref, xo_ref, mg_ref)


def _mix_prompt(x, g, wts, lw, layer, *, nb, t, tt):
    d = x.shape[1]
    const = lambda shape: pl.BlockSpec(shape, lambda b, j: (0,) * len(shape),
                                       pipeline_mode=pl.Buffered(1))
    slab = lambda shape: pl.BlockSpec((None,) + shape, lambda b, j: (layer,) + (0,) * len(shape),
                                      pipeline_mode=pl.Buffered(1))
    nt = t // tt
    return pl.pallas_call(
        _mix_prompt_kernel,
        grid=(nb, nt),
        in_specs=[pl.BlockSpec((tt, d), lambda b, j: (b * nt + j, 0)),
                  const((1, d)), slab((d, MAIN_COLS)), slab((d, SMALL_COLS)),
                  slab((SMALL_COLS, HEADS * DK)), const((1, HEADS * DK)), const((1, DV)),
                  const((1, SMALL_COLS)), const((1, BRANCH_W)),
                  const((RG_CONV, RG_WIDTH)), const((1, RG_WIDTH)),
                  slab((RG_WIDTH, RG_WIDTH)), const((1, RG_WIDTH)),
                  slab((RG_WIDTH, RG_WIDTH)), const((1, RG_WIDTH)), const((1, RG_WIDTH)),
                  slab((N_BRANCH, BRANCH_W, d)), slab((d, d))],
        out_specs=[pl.BlockSpec((tt, d), lambda b, j: (b * nt + j, 0)),
                   pl.BlockSpec((None, HEADS, DK, DV), lambda b, j: (b, 0, 0, 0)),
                   pl.BlockSpec((None, HEADS, DK, DV), lambda b, j: (b, 0, 0, 0)),
                   pl.BlockSpec((None, HEADS, 1, DK), lambda b, j: (b, 0, 0, 0)),
                   pl.BlockSpec((None, 1, HEADS), lambda b, j: (b, 0, 0)),
                   pl.BlockSpec((None, 1, RG_WIDTH), lambda b, j: (b, 0, 0)),
                   pl.BlockSpec((None, RG_CONV - 1, RG_WIDTH), lambda b, j: (b, 0, 0))],
        out_shape=[jax.ShapeDtypeStruct((nb * t, d), F32),
                   jax.ShapeDtypeStruct((nb, HEADS, DK, DV), F32),
                   jax.ShapeDtypeStruct((nb, HEADS, DK, DV), F32),
                   jax.ShapeDtypeStruct((nb, HEADS, 1, DK), F32),
                   jax.ShapeDtypeStruct((nb, 1, HEADS), F32),
                   jax.ShapeDtypeStruct((nb, 1, RG_WIDTH), F32),
                   jax.ShapeDtypeStruct((nb, RG_CONV - 1, RG_WIDTH), F32)],
        scratch_shapes=[pltpu.VMEM((tt, MIX_COLS), F32),
                        pltpu.VMEM((tt, SMALL_COLS), F32),
                        pltpu.VMEM((tt, d), F32),
                        pltpu.VMEM((tt, N_BRANCH * BRANCH_W), BF16),
                        pltpu.VMEM((tt, GATE_COLS), F32),
                        pltpu.VMEM((tt, d), BF16),
                        pltpu.VMEM((tt, BRANCH_W), F32),
                        pltpu.VMEM((tt + 8, RG_WIDTH), F32)],
        compiler_params=pltpu.CompilerParams(
            dimension_semantics=("parallel", "arbitrary"), vmem_limit_bytes=VMEM_LIMIT),
        name="mix_prompt",
    )(x, g.reshape(1, d), wts["w_main"], wts["w_small"],
      wts["wa2"], lw["ba"], lw["gnorm"], lw["bif"], lw["mnorm"], lw["convw"], lw["convb"],
      wts["wr"], lw["br"], wts["wi"], lw["bi"], lw["lam"], wts["wb"], wts["wo"])


def _mix_sample_kernel(pm_ref, ps_ref, s0_ref, c0_ref, n0_ref, m0_ref, h0_ref, buf0_ref,
                       wa2_ref, ba_ref, gnorm_ref, bif_ref, mnorm_ref,
                       convw_ref, convb_ref, wr_ref, br_ref, wi_ref, bi_ref, lam_ref,
                       br_out_ref, s_ref, c_ref, n_ref, m_ref, h_ref, buf_ref,
                       og_ref, om_ref):
    bs = pm_ref.shape[0]
    ps = ps_ref[...]
    a_pre = _dot(ps.astype(BF16), wa2_ref[...]) + ba_ref[...]
    a = jnp.exp(_log_sigmoid(a_pre) * (1.0 / GLA_GATE_NORM))
    a_hi = a.astype(BF16)
    a_r1 = a - a_hi.astype(F32)
    a_mid = a_r1.astype(BF16)
    a_lo = (a_r1 - a_mid.astype(F32)).astype(BF16)
    mq = pm_ref[:, C_MQ:C_MQ + HEADS * DK] * (DK ** -0.5)
    mk = pm_ref[:, C_MK:C_MK + HEADS * DK]
    kd = HEADS * DK

    pre = ps + bif_ref[...]
    f_log = pltpu.roll(_log_sigmoid(pre), SMALL_COLS - (L_MF - L_MI), 1)
    inter = f_log + m0_ref[...]
    m_t = jnp.maximum(inter, pre)
    lane = lax.broadcasted_iota(jnp.int32, (bs, SMALL_COLS), 1)
    is_head = (lane >= L_MI) & (lane < L_MI + HEADS)
    w_intra = jnp.where(is_head, jnp.exp(pre - m_t), 0.0)
    w_inter = jnp.where(is_head, jnp.exp(inter - m_t), 0.0)
    kd_row = lax.broadcasted_iota(jnp.int32, (kd, SMALL_COLS), 0)
    head_sum = jnp.where(
        lax.shift_right_logical(kd_row, DK.bit_length() - 1) + L_MI
        == lax.broadcasted_iota(jnp.int32, (kd, SMALL_COLS), 1), 1.0, 0.0).astype(BF16)
    sc = _dot_mask_exact_rhs(mq * mk, head_sum) * w_intra
    den = w_inter * _dot_mask_exact_rhs(mq * n0_ref[...], head_sum) + sc
    dmax = jnp.where(is_head, jnp.maximum(jnp.abs(den), jnp.exp(-m_t)), 1.0)
    m_ref[...] = m_t
    lane_head = lax.broadcasted_iota(jnp.int32, (SMALL_COLS, kd), 0) - L_MI
    head_spread = jnp.where(
        lane_head == lax.shift_right_logical(
            lax.broadcasted_iota(jnp.int32, (SMALL_COLS, kd), 1), DK.bit_length() - 1),
        1.0, 0.0).astype(BF16)
    n_ref[...] = (_dot_mask_exact_rhs(w_inter, head_spread) * n0_ref[...]
                  + _dot_mask_exact_rhs(w_intra, head_spread) * mk)

    scalars = jnp.concatenate([w_inter, w_intra, sc, dmax], axis=1)
    sc_hi = scalars.astype(BF16)
    sc_r1 = scalars - sc_hi.astype(F32)
    sc_mid = sc_r1.astype(BF16)
    sc_lo = (sc_r1 - sc_mid.astype(F32)).astype(BF16)
    n_scalar = 4

    def scalar_rows(piece):
        pt = piece.astype(F32).T
        return jnp.concatenate(
            [pt[i * SMALL_COLS + L_MI:i * SMALL_COLS + L_MI + 8, :] for i in range(n_scalar)],
            axis=0)

    cols_t = jnp.concatenate(
        [(pm_ref[:, C_GQ:C_GQ + kd] * (DK ** -0.5)).T, pm_ref[:, C_GK:C_GK + kd].T,
         mq.T, mk.T, a_hi.astype(F32).T, a_mid.astype(F32).T, a_lo.astype(F32).T,
         scalar_rows(sc_hi), scalar_rows(sc_mid), scalar_rows(sc_lo)],
        axis=0).astype(BF16)
    o_gq, o_gk, o_mq, o_mk, o_a0, o_a1, o_a2 = (i * kd for i in range(7))
    o_scalar = 7 * kd
    sample_id = lax.broadcasted_iota(jnp.int32, (bs, DV), 0)

    for s in range(bs):
        row = slice(s, s + 1)
        bc = _dot(cols_t, jnp.where(sample_id == s, 1.0, 0.0).astype(BF16))
        for h in range(HEADS):
            col = lambda off: bc[off + h * DK:off + (h + 1) * DK, :]

            def scalar(i):
                r0 = o_scalar + i * 8 + h
                return (bc[r0:r0 + 1, :] + bc[r0 + 8 * n_scalar:r0 + 8 * n_scalar + 1, :]
                        + bc[r0 + 16 * n_scalar:r0 + 16 * n_scalar + 1, :])

            v_row = pm_ref[row, C_GV + h * DV:C_GV + (h + 1) * DV]
            a_col = col(o_a0) + col(o_a1) + col(o_a2)
            s_new = a_col * s0_ref[s, h] + col(o_gk) * v_row
            s_ref[s, h] = s_new
            og_ref[row, h * DV:(h + 1) * DV] = jnp.sum(col(o_gq) * s_new,
                                                       axis=0, keepdims=True)
            v_row = pm_ref[row, C_MV + h * DV:C_MV + (h + 1) * DV]
            c_old = c0_ref[s, h]
            qc = jnp.sum(col(o_mq) * c_old, axis=0, keepdims=True)
            om_ref[row, h * DV:(h + 1) * DV] = (scalar(0) * qc + scalar(2) * v_row) / scalar(3)
            c_ref[s, h] = scalar(0) * c_old + (scalar(1) * col(o_mk)) * v_row

    gnorm = gnorm_ref[...]
    for h in range(HEADS):
        cols = slice(h * DV, (h + 1) * DV)
        y = _rms(og_ref[:, cols], gnorm) * _silu(pm_ref[:, C_GG + h * DV:C_GG + (h + 1) * DV])
        br_out_ref[:, cols] = y.astype(BF16)
        y = (_rms(om_ref[:, cols], mnorm_ref[:, cols])
             * _sigmoid(pm_ref[:, C_MO + h * DV:C_MO + (h + 1) * DV]))
        br_out_ref[:, BRANCH_W + h * DV:BRANCH_W + (h + 1) * DV] = y.astype(BF16)

    x = pm_ref[:, C_RX:C_RX + RG_WIDTH]
    xc = convb_ref[...] + convw_ref[3:4, :] * x
    for jj in range(RG_CONV - 1):
        xc = xc + convw_ref[jj:jj + 1, :] * buf0_ref[:, jj * RG_WIDTH:(jj + 1) * RG_WIDTH]
    buf_ref[:, 0:2 * RG_WIDTH] = buf0_ref[:, RG_WIDTH:3 * RG_WIDTH]
    buf_ref[:, 2 * RG_WIDTH:3 * RG_WIDTH] = x
    a, b = _rg_gates(xc, wr_ref, br_ref, wi_ref, bi_ref, lam_ref)
    hnew = a * h0_ref[...] + b
    h_ref[...] = hnew
    y = hnew * _gelu_tanh(pm_ref[:, C_RG:C_RG + RG_WIDTH])
    br_out_ref[:, 2 * BRANCH_W:3 * BRANCH_W] = y.astype(BF16)


def _mix_sample(pm, ps, st, wts, lw, layer, *, bs):
    n = pm.shape[0]
    const = lambda shape: pl.BlockSpec(shape, lambda i: (0,) * len(shape))
    slab = lambda shape: pl.BlockSpec((None,) + shape, lambda i: (layer,) + (0,) * len(shape))
    blk = lambda shape: pl.BlockSpec(shape, lambda i: (i,) + (0,) * (len(shape) - 1))
    lblk = lambda shape: pl.BlockSpec((None,) + shape,
                                      lambda i: (layer, i) + (0,) * (len(shape) - 1))
    s0, c0, n0, m0, h0, buf0 = st
    shapes = [(bs, HEADS, DK, DV), (bs, HEADS, DK, DV), (bs, HEADS * DK),
              (bs, SMALL_COLS), (bs, RG_WIDTH), (bs, (RG_CONV - 1) * RG_WIDTH)]
    return pl.pallas_call(
        _mix_sample_kernel,
        grid=(n // bs,),
        in_specs=[blk((bs, MIX_COLS)), blk((bs, SMALL_COLS))] + [lblk(sh) for sh in shapes] + [
            slab((SMALL_COLS, HEADS * DK)), const((1, HEADS * DK)), const((1, DV)),
            const((1, SMALL_COLS)), const((1, BRANCH_W)),
            const((RG_CONV, RG_WIDTH)), const((1, RG_WIDTH)),
            slab((RG_WIDTH, RG_WIDTH)), const((1, RG_WIDTH)),
            slab((RG_WIDTH, RG_WIDTH)), const((1, RG_WIDTH)), const((1, RG_WIDTH))],
        out_specs=[blk((bs, N_BRANCH * BRANCH_W))] + [blk(sh) for sh in shapes],
        out_shape=[jax.ShapeDtypeStruct((n, N_BRANCH * BRANCH_W), BF16)]
        + [jax.ShapeDtypeStruct(a.shape[1:], F32) for a in st],
        scratch_shapes=[pltpu.VMEM((bs, BRANCH_W), F32),
                        pltpu.VMEM((bs, BRANCH_W), F32)],
        compiler_params=pltpu.CompilerParams(
            dimension_semantics=("parallel",), vmem_limit_bytes=VMEM_LIMIT),
        name="mix_sample",
    )(pm, ps, s0, c0, n0, m0, h0, buf0,
      wts["wa2"], lw["ba"], lw["gnorm"], lw["bif"], lw["mnorm"], lw["convw"], lw["convb"],
      wts["wr"], lw["br"], wts["wi"], lw["bi"], lw["lam"])


def _merge_kernel(x_ref, br_ref, g0_ref, g1_ref, g2_ref, wb_ref, wo_ref, o_ref):
    merged = None
    for b, g_ref in enumerate((g0_ref, g1_ref, g2_ref)):
        z = _dot(br_ref[:, b * BRANCH_W:(b + 1) * BRANCH_W], wb_ref[b])
        term = _sigmoid(g_ref[...]) * z
        merged = term if merged is None else merged + term
    o_ref[...] = x_ref[...] + _dot(merged.astype(BF16), wo_ref[...])


def _merge(x, br, pm, wb, wo, layer, *, tm):
    n, d = x.shape
    gate_blk0 = MIX_COLS // d
    gate_spec = lambda b: pl.BlockSpec((tm, d), lambda i: (i, gate_blk0 + b))
    return pl.pallas_call(
        _merge_kernel,
        grid=(n // tm,),
        in_specs=[pl.BlockSpec((tm, d), lambda i: (i, 0)),
                  pl.BlockSpec((tm, N_BRANCH * BRANCH_W), lambda i: (i, 0)),
                  gate_spec(0), gate_spec(1), gate_spec(2),
                  pl.BlockSpec((None, N_BRANCH, BRANCH_W, d), lambda i: (layer, 0, 0, 0)),
                  pl.BlockSpec((None, d, d), lambda i: (layer, 0, 0))],
        out_specs=pl.BlockSpec((tm, d), lambda i: (i, 0)),
        out_shape=jax.ShapeDtypeStruct((n, d), F32),
        compiler_params=pltpu.CompilerParams(
            dimension_semantics=("parallel",), vmem_limit_bytes=VMEM_LIMIT),
        name="merge",
    )(x, br, pm, pm, pm, wb, wo)


def _xattn_ffn_prompt_kernel(x_ref, gx_ref, wq_ref, wo_ref, k_ref, v_ref,
                             gf_ref, wg_ref, wu_ref, wd_ref, gfin_ref,
                             o_ref, kb_ref, vb_ref, hid_ref, *, final_norm):
    @pl.when(pl.program_id(1) == 0)
    def _():
        kb_ref[...] = k_ref[...].astype(BF16)
        vb_ref[...] = v_ref[...].astype(BF16)

    x = x_ref[...]
    q = _dot(_rms(x, gx_ref[...]).astype(BF16), wq_ref[...]).astype(BF16)
    heads = []
    for h in range(XA_HEADS):
        cols = slice(h * XA_HEAD_DIM, (h + 1) * XA_HEAD_DIM)
        s = _dot_nt(q[:, cols], kb_ref[:, cols]) * (XA_HEAD_DIM ** -0.5)
        p = jnp.exp(s - jnp.max(s, axis=-1, keepdims=True))
        p = p / jnp.sum(p, axis=-1, keepdims=True)
        heads.append(_dot(p.astype(BF16), vb_ref[:, cols]))
    o = jnp.concatenate(heads, axis=1).astype(BF16)
    x = x + _dot(o, wo_ref[...])
    o_ref[...] = _ffn_block(x, gf_ref, wg_ref, wu_ref, wd_ref, gfin_ref, hid_ref, final_norm)


def _xattn_ffn_prompt(x, gx, wq, wo, mem_k, mem_v, gf, wg, wu, wd, gfin, layer, *,
                      nb, t, tm, final_norm):
    d = x.shape[1]
    f = wg.shape[2]
    nt = t // tm
    const = lambda shape: pl.BlockSpec(shape, lambda b, j: (0,) * len(shape),
                                       pipeline_mode=pl.Buffered(1))
    slab = lambda shape: pl.BlockSpec((None,) + shape, lambda b, j: (layer,) + (0,) * len(shape),
                                      pipeline_mode=pl.Buffered(1))
    return pl.pallas_call(
        functools.partial(_xattn_ffn_prompt_kernel, final_norm=final_norm),
        grid=(nb, nt),
        in_specs=[pl.BlockSpec((tm, d), lambda b, j: (b * nt + j, 0)),
                  const((1, d)), slab((d, d)), slab((d, d)),
                  pl.BlockSpec((None, N_MEM, d), lambda b, j: (b, 0, 0)),
                  pl.BlockSpec((None, N_MEM, d), lambda b, j: (b, 0, 0)),
                  const((1, d)), slab((d, f)), slab((d, f)), slab((f, d)), const((1, d))],
        out_specs=pl.BlockSpec((tm, d), lambda b, j: (b * nt + j, 0)),
        out_shape=jax.ShapeDtypeStruct(x.shape, F32),
        scratch_shapes=[pltpu.VMEM((N_MEM, d), BF16), pltpu.VMEM((N_MEM, d), BF16),
                        pltpu.VMEM((tm, f), BF16)],
        compiler_params=pltpu.CompilerParams(
            dimension_semantics=("parallel", "arbitrary"), vmem_limit_bytes=VMEM_LIMIT),
        name="xattn_ffn_prompt",
    )(x, gx.reshape(1, d), wq, wo, mem_k, mem_v, gf.reshape(1, d), wg, wu, wd, gfin.reshape(1, d))


def _xattn_sample_kernel(q_ref, k_ref, v_ref, o_ref):
    bs = q_ref.shape[0]
    nrow = N_MEM * XA_HEADS
    row_head = lax.broadcasted_iota(jnp.int32, (XA_HEADS, nrow), 1) & (XA_HEADS - 1)
    own_head = row_head == lax.broadcasted_iota(jnp.int32, (XA_HEADS, nrow), 0)
    for s in range(bs):
        q = q_ref[s].astype(BF16)
        k2 = k_ref[s].reshape(nrow, XA_HEAD_DIM).astype(BF16)
        v2 = v_ref[s].reshape(nrow, XA_HEAD_DIM).astype(BF16)
        sc = jnp.where(own_head, _dot_nt(q, k2) * (XA_HEAD_DIM ** -0.5), -jnp.inf)
        p = jnp.exp(sc - jnp.max(sc, axis=1, keepdims=True))
        p = p / jnp.sum(p, axis=1, keepdims=True)
        o_ref[s] = _dot(p.astype(BF16), v2)


def _xattn_sample(q, cache_k, cache_v, layer, *, bs):
    n = q.shape[0]
    blk = (None, bs, N_MEM, XA_HEADS, XA_HEAD_DIM)
    return pl.pallas_call(
        _xattn_sample_kernel,
        grid=(n // bs,),
        in_specs=[pl.BlockSpec((bs, XA_HEADS, XA_HEAD_DIM), lambda i: (i, 0, 0)),
                  pl.BlockSpec(blk, lambda i: (layer, i, 0, 0, 0)),
                  pl.BlockSpec(blk, lambda i: (layer, i, 0, 0, 0))],
        out_specs=pl.BlockSpec((bs, XA_HEADS, XA_HEAD_DIM), lambda i: (i, 0, 0)),
        out_shape=jax.ShapeDtypeStruct((n, XA_HEADS, XA_HEAD_DIM), F32),
        compiler_params=pltpu.CompilerParams(
            dimension_semantics=("parallel",), vmem_limit_bytes=VMEM_LIMIT),
        name="xattn_sample",
    )(q, cache_k, cache_v)


def _matmul_residual_kernel(x_ref, a_ref, w_ref, o_ref):
    o_ref[...] = x_ref[...] + _dot(a_ref[...].astype(BF16), w_ref[...])


def _matmul_residual(x, a, w, layer):
    n, d = x.shape
    return pl.pallas_call(
        _matmul_residual_kernel,
        grid=(1,),
        in_specs=[pl.BlockSpec((n, d), lambda i: (0, 0)),
                  pl.BlockSpec((n, d), lambda i: (0, 0)),
                  pl.BlockSpec((None, d, d), lambda i: (layer, 0, 0))],
        out_specs=pl.BlockSpec((n, d), lambda i: (0, 0)),
        out_shape=jax.ShapeDtypeStruct((n, d), F32),
        compiler_params=pltpu.CompilerParams(vmem_limit_bytes=VMEM_LIMIT),
        name="matmul_residual",
    )(x, a, w)


def _ffn_block(x, g_ref, wg_ref, wu_ref, wd_ref, gfin_ref, hid_ref, final_norm):
    u = _rms(x, g_ref[...]).astype(BF16)
    for jt in range(D_FF // FFN_TILE):
        cols = slice(jt * FFN_TILE, (jt + 1) * FFN_TILE)
        hid = _silu(_dot(u, wg_ref[:, cols])) * _dot(u, wu_ref[:, cols])
        hid_ref[:, cols] = hid.astype(BF16)
    y = x + _dot(hid_ref[...], wd_ref[...])
    if final_norm:
        y = _rms(y, gfin_ref[...])
    return y


def _ffn_kernel(x_ref, g_ref, wg_ref, wu_ref, wd_ref, gf_ref, o_ref, hid_ref, *, final_norm):
    o_ref[...] = _ffn_block(x_ref[...], g_ref, wg_ref, wu_ref, wd_ref, gf_ref, hid_ref, final_norm)


def _ffn(x, g, wg, wu, wd, gf, layer, *, tm, final_norm):
    n, d = x.shape
    f = wg.shape[2]
    const = lambda shape: pl.BlockSpec(shape, lambda i: (0,) * len(shape),
                                       pipeline_mode=pl.Buffered(1))
    slab = lambda shape: pl.BlockSpec((None,) + shape, lambda i: (layer,) + (0,) * len(shape),
                                      pipeline_mode=pl.Buffered(1))
    return pl.pallas_call(
        functools.partial(_ffn_kernel, final_norm=final_norm),
        grid=(n // tm,),
        in_specs=[pl.BlockSpec((tm, d), lambda i: (i, 0)),
                  const((1, d)), slab((d, f)), slab((d, f)), slab((f, d)), const((1, d))],
        out_specs=pl.BlockSpec((tm, d), lambda i: (i, 0)),
        out_shape=jax.ShapeDtypeStruct((n, d), F32),
        scratch_shapes=[pltpu.VMEM((tm, f), BF16)],
        compiler_params=pltpu.CompilerParams(
            dimension_semantics=("parallel",), vmem_limit_bytes=VMEM_LIMIT),
        name="ffn",
    )(x, g.reshape(1, d), wg, wu, wd, gf.reshape(1, d))


def _split_w_in(w_in):
    widths = (HEADS * DK, HEADS * DK, HEADS * DV, HEADS * DV, GLA_RANK,
              HEADS * DK, HEADS * DK, HEADS * DV, HEADS * DV, HEADS, HEADS,
              RG_WIDTH, RG_WIDTH, GATE_COLS)
    parts, acc = [], 0
    for w in widths:
        parts.append(w_in[..., acc:acc + w])
        acc += w
    return parts


def _block_diag(w):
    eye = jnp.eye(RG_BLOCKS, dtype=w.dtype)
    return jnp.einsum("lnde,nm->lndme", w, eye).reshape(DEPTH, RG_WIDTH, RG_WIDTH)


def _matmul_weights(p):
    (g_q, g_k, g_v, g_g, g_a, m_q, m_k, m_v, m_o, m_i, m_f, r_x, r_g, gates) = _split_w_in(p["w_in"])
    d = D_MODEL
    w_main = jnp.concatenate([g_q, g_k, g_v, g_g, m_q, m_k, m_v, m_o, r_x, r_g, gates], axis=2)
    w_small = jnp.concatenate(
        [g_a, m_i, m_f, jnp.zeros((DEPTH, d, SMALL_COLS - GLA_RANK - 2 * HEADS), F32)], axis=2)
    wa2 = jnp.concatenate(
        [p["gla_w_a2"], jnp.zeros((DEPTH, SMALL_COLS - GLA_RANK, HEADS * DK), F32)], axis=1)
    return {
        "w_main": w_main.astype(BF16), "w_small": w_small.astype(BF16), "wa2": wa2.astype(BF16),
        "wr": _block_diag(p["rg_w_r"]).astype(BF16), "wi": _block_diag(p["rg_w_i"]).astype(BF16),
        "wb": p["w_branch"].astype(BF16), "wo": p["w_out"].astype(BF16),
        "xa_wq": p["xa_wq"].astype(BF16), "xa_wo": p["xa_wo"].astype(BF16),
        "xa_wkv": jnp.concatenate([p["xa_wk"], p["xa_wv"]], axis=2).astype(BF16),
        "wg": p["ffn_w_gate"].astype(BF16), "wu": p["ffn_w_up"].astype(BF16),
        "wd": p["ffn_w_down"].astype(BF16),
    }


def _vector_params(l, p):
    bif = jnp.concatenate(
        [jnp.zeros((L_MI,), F32), p["mlstm_b_i"][l], p["mlstm_b_f"][l],
         jnp.zeros((SMALL_COLS - L_MF - HEADS,), F32)]).reshape(1, SMALL_COLS)
    return {
        "ba": p["gla_b_a"][l].reshape(1, -1), "gnorm": p["gla_norm"][l].reshape(1, DV),
        "bif": bif, "mnorm": p["mlstm_norm"][l].reshape(1, BRANCH_W),
        "convw": p["rg_conv_w"][l], "convb": p["rg_conv_b"][l].reshape(1, -1),
        "br": p["rg_b_r"][l].reshape(1, -1), "bi": p["rg_b_i"][l].reshape(1, -1),
        "lam": p["rg_lambda"][l].reshape(1, -1),
    }


def kernel(x_prompt, x_sample, mem_prompt, cache_mem_k, cache_mem_v, state_gla, state_mlstm_c, state_mlstm_n, state_mlstm_m, state_rglru_h, state_rglru_conv, norm_mix, w_in, gla_w_a2, gla_b_a, gla_norm, mlstm_b_i, mlstm_b_f, mlstm_norm, rg_conv_w, rg_conv_b, rg_w_r, rg_b_r, rg_w_i, rg_b_i, rg_lambda, w_branch, w_out, norm_xa, norm_mem, xa_wq, xa_wk, xa_wv, xa_wo, norm_ffn, ffn_w_gate, ffn_w_up, ffn_w_down, norm_final):
    p = {"w_in": w_in, "gla_w_a2": gla_w_a2, "gla_b_a": gla_b_a, "gla_norm": gla_norm,
         "mlstm_b_i": mlstm_b_i, "mlstm_b_f": mlstm_b_f, "mlstm_norm": mlstm_norm,
         "rg_conv_w": rg_conv_w, "rg_conv_b": rg_conv_b, "rg_w_r": rg_w_r, "rg_b_r": rg_b_r,
         "rg_w_i": rg_w_i, "rg_b_i": rg_b_i, "rg_lambda": rg_lambda, "w_branch": w_branch,
         "w_out": w_out, "xa_wq": xa_wq, "xa_wk": xa_wk, "xa_wv": xa_wv, "xa_wo": xa_wo,
         "ffn_w_gate": ffn_w_gate, "ffn_w_up": ffn_w_up, "ffn_w_down": ffn_w_down}
    nb, t, d = x_prompt.shape
    ns = x_sample.shape[0]
    xp = x_prompt.reshape(nb * t, d)
    xs = x_sample.reshape(ns, d)
    mem = mem_prompt.reshape(nb * N_MEM, d)
    wts = _matmul_weights(p)

    st_sample = (state_gla, state_mlstm_c, state_mlstm_n.reshape(DEPTH, ns, HEADS * DK),
                 jnp.pad(state_mlstm_m, ((0, 0), (0, 0), (L_MI, SMALL_COLS - L_MI - HEADS))),
                 state_rglru_h,
                 state_rglru_conv.reshape(DEPTH, ns, (RG_CONV - 1) * RG_WIDTH))

    new_p = [[] for _ in range(8)]
    new_s = [[] for _ in range(6)]
    for l in range(DEPTH):
        lw = _vector_params(l, p)
        last = l == DEPTH - 1

        k_p, v_p = _norm_matmul(mem, norm_mem[l], wts["xa_wkv"], l, tm=PROMPT_ROWS, tn=d)
        k_p = k_p.reshape(nb, N_MEM, d)
        v_p = v_p.reshape(nb, N_MEM, d)

        xp, g_s, c_s, n_s, m_s, h_s, buf_s = _mix_prompt(xp, norm_mix[l], wts, lw, l,
                                                         nb=nb, t=t, tt=MIX_TIME_BLOCK)
        xp = _xattn_ffn_prompt(xp, norm_xa[l], wts["xa_wq"], wts["xa_wo"], k_p, v_p,
                               norm_ffn[l], wts["wg"], wts["wu"], wts["wd"], norm_final, l,
                               nb=nb, t=t, tm=PROMPT_ROWS, final_norm=last)
        for lst, val in zip(new_p, (k_p.reshape(nb, N_MEM, XA_HEADS, XA_HEAD_DIM),
                                    v_p.reshape(nb, N_MEM, XA_HEADS, XA_HEAD_DIM),
                                    g_s, c_s, n_s.reshape(nb, HEADS, DK), m_s.reshape(nb, HEADS),
                                    h_s.reshape(nb, RG_WIDTH), buf_s)):
            lst.append(val)

        pm, ps = _in_proj(xs, norm_mix[l], wts["w_main"], wts["w_small"], l,
                          tm=ns, tn=IN_PROJ_COLS)
        br, g_s, c_s, n_s, m_s, h_s, buf_s = _mix_sample(pm, ps, st_sample, wts, lw, l,
                                                         bs=SAMPLE_BLOCK)
        xs = _merge(xs, br, pm, wts["wb"], wts["wo"], l, tm=ns)
        (q,) = _norm_matmul(xs, norm_xa[l], wts["xa_wq"], l, tm=ns, tn=d)
        att = _xattn_sample(q.reshape(ns, XA_HEADS, XA_HEAD_DIM), cache_mem_k, cache_mem_v, l,
                            bs=SAMPLE_BLOCK)
        xs = _matmul_residual(xs, att.reshape(ns, d), wts["xa_wo"], l)
        xs = _ffn(xs, norm_ffn[l], wts["wg"], wts["wu"], wts["wd"], norm_final, l,
                  tm=ns, final_norm=last)
        for lst, val in zip(new_s, (g_s, c_s, n_s.reshape(ns, HEADS, DK),
                                    m_s[:, L_MI:L_MI + HEADS], h_s,
                                    buf_s.reshape(ns, RG_CONV - 1, RG_WIDTH))):
            lst.append(val)

    y_prompt = xp.reshape(nb, t, d)
    y_sample = xs.reshape(ns, 1, d)
    outs_p = [jnp.stack(v, axis=0) for v in new_p]
    outs_s = [jnp.stack(v, axis=0) for v in new_s]
    return (y_prompt, y_sample, *outs_p, *outs_s)
```

```python
import functools
import math

import jax
import jax.numpy as jnp
from jax import lax
from jax.experimental import pallas as pl
from jax.experimental.pallas import tpu as pltpu

F32 = jnp.float32
BF16 = jnp.bfloat16

D_MODEL = 1024
DEPTH = 2
EPS = 1e-6
N_MEM = 256
XA_HEADS = 4
XA_HEAD_DIM = D_MODEL // XA_HEADS
N_BRANCH = 3
BRANCH_W = D_MODEL // 2
HEADS = 4
DK = 64
DV = 128
GLA_RANK = 16
GLA_GATE_NORM = 16.0
GLA_CHUNK = 32
RG_WIDTH = BRANCH_W
RG_BLOCKS = 8
RG_CONV = 4
RG_C = 8.0
D_FF = 2816

C_GQ, C_GK, C_GV, C_GG = 0, 256, 512, 1024
C_MQ, C_MK, C_MV, C_MO = 1536, 1792, 2048, 2560
C_RX, C_RG = 3072, 3584
MIX_COLS = 4096
GATE_COLS = N_BRANCH * D_MODEL
MAIN_COLS = MIX_COLS + GATE_COLS
SMALL_COLS = 128
L_MI = 16
L_MF = 20

MXU_WIDTH = 256
PROJ_TILE = 2 * MXU_WIDTH
MERGE_TILE = 2 * MXU_WIDTH
FFN_TILE = MXU_WIDTH
MIX_TIME_BLOCK = 256
PROMPT_ROWS = 512
SAMPLE_BLOCK = 8
IN_PROJ_COLS = 1024
VMEM_LIMIT = 48 * 1024 * 1024


def _softplus(x):
    return jnp.maximum(x, 0.0) + jnp.log1p(jnp.exp(-jnp.abs(x)))


def _log_sigmoid(x):
    return -_softplus(-x)


def _sigmoid(x):
    return 1.0 / (1.0 + jnp.exp(-x))


def _silu(x):
    return x * _sigmoid(x)


def _gelu_tanh(x):
    c = math.sqrt(2.0 / math.pi)
    return x * (0.5 * (1.0 + jnp.tanh(c * (x + 0.044715 * (x * x * x)))))


def _neg_expm1(x):
    return -jnp.tanh(0.5 * x) * (jnp.exp(x) + 1.0)


def _rms(x, g):
    ms = jnp.mean(x * x, axis=-1, keepdims=True)
    return x * lax.rsqrt(ms + EPS) * g


def _dot(a, b):
    return jnp.dot(a, b, preferred_element_type=F32)


def _dot_nt(a, b):
    return lax.dot_general(a, b, (((1,), (1,)), ((), ())), preferred_element_type=F32)


def _dot_tn(a, b):
    return lax.dot_general(a, b, (((0,), (0,)), ((), ())), preferred_element_type=F32)


def _dot_mask_exact(mask, x):
    hi = x.astype(BF16)
    rest = x - hi.astype(F32)
    mid = rest.astype(BF16)
    lo = (rest - mid.astype(F32)).astype(BF16)
    return _dot(mask, hi) + _dot(mask, mid) + _dot(mask, lo)


def _dot_mask_exact_rhs(x, mask):
    hi = x.astype(BF16)
    rest = x - hi.astype(F32)
    mid = rest.astype(BF16)
    lo = (rest - mid.astype(F32)).astype(BF16)
    return _dot(hi, mask) + _dot(mid, mask) + _dot(lo, mask)


def _chunk_masks(n, chunk):
    shift = chunk.bit_length() - 1
    r = lax.broadcasted_iota(jnp.int32, (n, n), 0)
    c = lax.broadcasted_iota(jnp.int32, (n, n), 1)
    same = lax.shift_right_logical(r, shift) == lax.shift_right_logical(c, shift)
    tril = jnp.where(same & (c <= r), 1.0, 0.0).astype(F32)
    full = jnp.where(same, 1.0, 0.0).astype(F32)
    return tril, full


def _norm_matmul_kernel(x_ref, g_ref, w_ref, *refs):
    out_refs, u_ref = refs[:-1], refs[-1]
    j = pl.program_id(1)

    @pl.when(j == 0)
    def _():
        u_ref[...] = _rms(x_ref[...], g_ref[...]).astype(BF16)

    res = _dot(u_ref[...], w_ref[...])
    for idx, o_ref in enumerate(out_refs):
        @pl.when(j == idx)
        def _(o_ref=o_ref):
            o_ref[...] = res


def _norm_matmul(x, g, w, layer, *, tm, tn):
    n, d = x.shape
    nout = w.shape[2]
    n_out = nout // tn
    return pl.pallas_call(
        _norm_matmul_kernel,
        grid=(n // tm, n_out),
        in_specs=[pl.BlockSpec((tm, d), lambda i, j: (i, 0)),
                  pl.BlockSpec((1, d), lambda i, j: (0, 0)),
                  pl.BlockSpec((None, d, tn), lambda i, j: (layer, 0, j))],
        out_specs=[pl.BlockSpec((tm, tn), lambda i, j: (i, 0))] * n_out,
        out_shape=[jax.ShapeDtypeStruct((n, tn), F32)] * n_out,
        scratch_shapes=[pltpu.VMEM((tm, d), BF16)],
        compiler_params=pltpu.CompilerParams(
            dimension_semantics=("parallel", "arbitrary"), vmem_limit_bytes=VMEM_LIMIT),
        name="norm_matmul",
    )(x, g.reshape(1, d), w)


def _in_proj_kernel(x_ref, g_ref, w_ref, ws_ref, o_ref, os_ref, u_ref):
    @pl.when(pl.program_id(1) == 0)
    def _():
        u = _rms(x_ref[...], g_ref[...]).astype(BF16)
        u_ref[...] = u
        os_ref[...] = _dot(u, ws_ref[...])

    o_ref[...] = _dot(u_ref[...], w_ref[...])


def _in_proj(x, g, w_main, w_small, layer, *, tm, tn):
    n, d = x.shape
    return pl.pallas_call(
        _in_proj_kernel,
        grid=(n // tm, MAIN_COLS // tn),
        in_specs=[pl.BlockSpec((tm, d), lambda i, j: (i, 0)),
                  pl.BlockSpec((1, d), lambda i, j: (0, 0)),
                  pl.BlockSpec((None, d, tn), lambda i, j: (layer, 0, j)),
                  pl.BlockSpec((None, d, SMALL_COLS), lambda i, j: (layer, 0, 0))],
        out_specs=[pl.BlockSpec((tm, tn), lambda i, j: (i, j)),
                   pl.BlockSpec((tm, SMALL_COLS), lambda i, j: (i, 0))],
        out_shape=[jax.ShapeDtypeStruct((n, MAIN_COLS), F32),
                   jax.ShapeDtypeStruct((n, SMALL_COLS), F32)],
        scratch_shapes=[pltpu.VMEM((tm, d), BF16)],
        compiler_params=pltpu.CompilerParams(
            dimension_semantics=("parallel", "arbitrary"), vmem_limit_bytes=VMEM_LIMIT),
        name="in_proj",
    )(x, g.reshape(1, d), w_main, w_small)


def _rg_gates(xc, wr_ref, br_ref, wi_ref, bi_ref, lam_ref):
    xcb = xc.astype(BF16)
    r = _sigmoid(_dot(xcb, wr_ref[...]) + br_ref[...])
    i = _sigmoid(_dot(xcb, wi_ref[...]) + bi_ref[...])
    log_a = (-RG_C * r) * _softplus(-lam_ref[...])
    a = jnp.exp(log_a)
    b = jnp.sqrt(_neg_expm1(2.0 * log_a)) * (i * xc)
    return a, b


def _projection_activation(col):
    if C_GG <= col < C_GG + HEADS * DV:
        return _silu
    if C_MO <= col < C_MO + HEADS * DV:
        return _sigmoid
    if C_RG <= col < C_RG + RG_WIDTH:
        return _gelu_tanh
    if col >= MIX_COLS:
        return _sigmoid
    return None


def _project_stage(x_ref, g_ref, wmain_ref, wsmall_ref, pm_ref, ps_ref, xs_ref, sg_ref):
    x = x_ref[...]
    xs_ref[...] = x
    u = _rms(x, g_ref[...]).astype(BF16)
    ps_ref[...] = _dot(u, wsmall_ref[...])
    for cb in range(MAIN_COLS // PROJ_TILE):
        c0 = cb * PROJ_TILE
        y = _dot(u, wmain_ref[:, c0:c0 + PROJ_TILE])
        act = _projection_activation(c0)
        if act is not None:
            y = act(y)
        if c0 >= MIX_COLS:
            sg_ref[:, c0 - MIX_COLS:c0 - MIX_COLS + PROJ_TILE] = y
        else:
            pm_ref[:, c0:c0 + PROJ_TILE] = y


def _mixer_stage(pm_ref, ps_ref,
                 wa2_ref, ba_ref, gnorm_ref, bif_ref, mnorm_ref,
                 convw_ref, convb_ref, wr_ref, br_ref, wi_ref, bi_ref, lam_ref,
                 s_ref, c_ref, n_ref, m_ref, h_ref, buf_ref,
                 br_out_ref, og_ref, xpad_ref):
    tt = pm_ref.shape[0]

    x = pm_ref[:, C_RX:C_RX + RG_WIDTH]
    xpad_ref[pl.ds(8, tt), :] = x
    xc = (convb_ref[...] + convw_ref[3:4, :] * x
          + convw_ref[2:3, :] * xpad_ref[pl.ds(7, tt), :]
          + convw_ref[1:2, :] * xpad_ref[pl.ds(6, tt), :]
          + convw_ref[0:1, :] * xpad_ref[pl.ds(5, tt), :])
    buf_ref[...] = xpad_ref[pl.ds(tt + 8 - (RG_CONV - 1), RG_CONV - 1), :]
    xpad_ref[pl.ds(0, 8), :] = xpad_ref[pl.ds(tt, 8), :]
    a, b = _rg_gates(xc, wr_ref, br_ref, wi_ref, bi_ref, lam_ref)
    ridx = lax.broadcasted_iota(jnp.int32, (tt, RG_WIDTH), 0)
    sh = 1
    while sh < tt:
        keep = ridx >= sh
        a_sh = jnp.where(keep, pltpu.roll(a, sh, 0), 1.0)
        b_sh = jnp.where(keep, pltpu.roll(b, sh, 0), 0.0)
        b = a * b_sh + b
        a = a * a_sh
        sh *= 2
    hseq = a * h_ref[...] + b
    h_ref[...] = hseq[tt - 1:tt, :]
    y = hseq * pm_ref[:, C_RG:C_RG + RG_WIDTH]
    br_out_ref[:, 2 * BRANCH_W:3 * BRANCH_W] = y.astype(BF16)

    ps = ps_ref[...]
    a_pre = _dot(ps.astype(BF16), wa2_ref[...]) + ba_ref[...]
    log_a = _log_sigmoid(a_pre) * (1.0 / GLA_GATE_NORM)
    tril32, full32 = _chunk_masks(tt, GLA_CHUNK)
    sums = _dot_mask_exact(jnp.concatenate([tril32, full32], axis=0).astype(BF16), log_a)
    b_cum = sums[:tt]
    b_end = sums[tt:]
    q = pm_ref[:, C_GQ:C_GQ + HEADS * DK] * (DK ** -0.5)
    k = pm_ref[:, C_GK:C_GK + HEADS * DK]
    qi_f = q * jnp.exp(b_cum)
    qi_all = qi_f.astype(BF16)
    ki_all = (k * jnp.exp(-b_cum)).astype(BF16)
    ke_t = (k * jnp.exp(b_end - b_cum)).T
    de_t = jnp.exp(b_end).T
    in_chunk = tril32 > 0.0
    n_chunks = tt // GLA_CHUNK
    cshift = GLA_CHUNK.bit_length() - 1
    chunk_of_lane = lax.shift_right_logical(lax.broadcasted_iota(jnp.int32, (DK, tt), 1), cshift)
    chunk_of_row = lax.shift_right_logical(lax.broadcasted_iota(jnp.int32, (tt, 2 * DK), 0), cshift)
    upper_half = (lax.broadcasted_iota(jnp.int32, (tt, 2 * DK), 1) >= DK).astype(jnp.int32)
    for h in range(HEADS):
        kcols = slice(h * DK, (h + 1) * DK)
        v = pm_ref[:, C_GV + h * DV:C_GV + (h + 1) * DV].astype(BF16)
        att = jnp.where(in_chunk, _dot_nt(qi_all[:, kcols], ki_all[:, kcols]), 0.0)
        o_intra = _dot(att.astype(BF16), v)
        ke_h = ke_t[kcols, :]
        ke_blk = jnp.concatenate(
            [jnp.where(chunk_of_lane == c, ke_h, 0.0) for c in range(n_chunks)], axis=0)
        upd = _dot(ke_blk.astype(BF16), v)
        s_cur = s_ref[h]
        s_start = []
        for c in range(n_chunks):
            s_start.append(s_cur)
            de_col = de_t[kcols, c * GLA_CHUNK:c * GLA_CHUNK + 1]
            s_cur = de_col * s_cur + upd[c * DK:(c + 1) * DK, :]
        s_ref[h] = s_cur
        qi_h = qi_f[:, kcols]
        qi_dup = jnp.concatenate([qi_h, qi_h], axis=1)
        q_blk = jnp.concatenate(
            [jnp.where(chunk_of_row == 2 * jj + upper_half, qi_dup, 0.0)
             for jj in range(n_chunks // 2)], axis=1)
        o_inter = _dot(q_blk.astype(BF16), jnp.concatenate(s_start, axis=0).astype(BF16))
        og_ref[:, h * DV:(h + 1) * DV] = o_intra + o_inter

    gnorm = gnorm_ref[...]
    for h in range(HEADS):
        cols = slice(h * DV, (h + 1) * DV)
        y = _rms(og_ref[:, cols], gnorm) * pm_ref[:, C_GG + h * DV:C_GG + (h + 1) * DV]
        br_out_ref[:, cols] = y.astype(BF16)

    pre = ps + bif_ref[...]
    rr = lax.broadcasted_iota(jnp.int32, (tt, tt), 0)
    cc = lax.broadcasted_iota(jnp.int32, (tt, tt), 1)
    causal = cc <= rr
    fcum = _dot_mask_exact(jnp.where(causal, 1.0, 0.0).astype(BF16), _log_sigmoid(pre))
    pre_t = pre.T
    fcum_t = fcum.T
    q_all = pm_ref[:, C_MQ:C_MQ + HEADS * DK] * (DK ** -0.5)
    k_all = pm_ref[:, C_MK:C_MK + HEADS * DK]
    for h in range(HEADS):
        kcols = slice(h * DK, (h + 1) * DK)
        fc_col = fcum[:, L_MF + h:L_MF + h + 1]
        ic_col = pre[:, L_MI + h:L_MI + h + 1]
        fc_row = fcum_t[L_MF + h:L_MF + h + 1, :]
        ic_row = pre_t[L_MI + h:L_MI + h + 1, :]
        m_prev = m_ref[:, h:h + 1]
        d = jnp.where(causal, fc_col + (ic_row - fc_row), -jnp.inf)
        inter = fc_col + m_prev
        m_t = jnp.maximum(inter, jnp.max(d, axis=1, keepdims=True))
        w_intra = jnp.exp(d - m_t)
        w_inter = jnp.exp(inter - m_t)
        qf = q_all[:, kcols]
        kf = k_all[:, kcols]
        qc = qf.astype(BF16)
        vc = pm_ref[:, C_MV + h * DV:C_MV + (h + 1) * DV].astype(BF16)
        s = _dot_nt(qc, kf.astype(BF16)) * w_intra
        c_old = c_ref[h]
        n_old = n_ref[h]
        num = w_inter * _dot(qc, c_old.astype(BF16)) + _dot(s.astype(BF16), vc)
        den = (w_inter * jnp.sum(qf * n_old, axis=1, keepdims=True)
               + jnp.sum(s, axis=1, keepdims=True))
        hh = num / jnp.maximum(jnp.abs(den), jnp.exp(-m_t))
        og_ref[:, h * DV:(h + 1) * DV] = hh
        f_end = fc_col[tt - 1:tt, :]
        m_end = m_t[tt - 1:tt, :]
        w_k = jnp.exp(f_end - fc_col + ic_col - m_end)
        w_c = jnp.exp(f_end + m_prev - m_end)
        kw = w_k * kf
        c_ref[h] = w_c * c_old + _dot_tn(kw.astype(BF16), vc)
        n_ref[h] = w_c * n_old + jnp.sum(kw, axis=0, keepdims=True)
        m_ref[:, h:h + 1] = m_end

    for h in range(HEADS):
        cols = slice(h * DV, (h + 1) * DV)
        y = (_rms(og_ref[:, cols], mnorm_ref[:, cols])
             * pm_ref[:, C_MO + h * DV:C_MO + (h + 1) * DV])
        br_out_ref[:, BRANCH_W + h * DV:BRANCH_W + (h + 1) * DV] = y.astype(BF16)


def _merge_stage(br_ref, sg_ref, xr_ref, wb_ref, wo_ref, xo_ref, mg_ref):
    n_tiles = D_MODEL // MERGE_TILE
    for jt in range(n_tiles):
        cols = slice(jt * MERGE_TILE, (jt + 1) * MERGE_TILE)
        merged = None
        for b in range(N_BRANCH):
            z = _dot(br_ref[:, b * BRANCH_W:(b + 1) * BRANCH_W], wb_ref[b, :, cols])
            term = sg_ref[:, b * D_MODEL + jt * MERGE_TILE:b * D_MODEL + (jt + 1) * MERGE_TILE] * z
            merged = term if merged is None else merged + term
        mg_ref[:, cols] = merged.astype(BF16)
    for jt in range(n_tiles):
        cols = slice(jt * MERGE_TILE, (jt + 1) * MERGE_TILE)
        xo_ref[:, cols] = xr_ref[:, cols] + _dot(mg_ref[...], wo_ref[:, cols])


N_MIXER_WEIGHTS = 12


def _mix_prompt_kernel(x_ref, g_ref, wmain_ref, wsmall_ref, *refs):
    mixer_w = refs[:N_MIXER_WEIGHTS]
    wb_ref, wo_ref = refs[N_MIXER_WEIGHTS:N_MIXER_WEIGHTS + 2]
    xo_ref, s_ref, c_ref, n_ref, m_ref, h_ref, buf_ref = refs[N_MIXER_WEIGHTS + 2:N_MIXER_WEIGHTS + 9]
    pm_ref, ps_ref, xs_ref, br_ref, sg_ref, mg_ref, og_ref, xpad_ref = refs[N_MIXER_WEIGHTS + 9:]

    @pl.when(pl.program_id(1) == 0)
    def _():
        for ref in (s_ref, c_ref, n_ref, m_ref, h_ref):
            ref[...] = jnp.zeros_like(ref)
        xpad_ref[pl.ds(0, 8), :] = jnp.zeros((8, RG_WIDTH), F32)

    _project_stage(x_ref, g_ref, wmain_ref, wsmall_ref, pm_ref, ps_ref, xs_ref, sg_ref)
    _mixer_stage(pm_ref, ps_ref, *mixer_w, s_ref, c_ref, n_ref, m_ref, h_ref, buf_ref,
                 br_ref, og_ref, xpad_ref)
    _merge_stage(br_ref, sg_ref, xs_ref, wb_ref, wo_ref, xo_ref, mg_ref)


def _mix_prompt(x, g, wts, lw, layer, *, nb, t, tt):
    d = x.shape[1]
    const = lambda shape: pl.BlockSpec(shape, lambda b, j: (0,) * len(shape),
                                       pipeline_mode=pl.Buffered(1))
    slab = lambda shape: pl.BlockSpec((None,) + shape, lambda b, j: (layer,) + (0,) * len(shape),
                                      pipeline_mode=pl.Buffered(1))
    nt = t // tt
    return pl.pallas_call(
        _mix_prompt_kernel,
        grid=(nb, nt),
        in_specs=[pl.BlockSpec((tt, d), lambda b, j: (b * nt + j, 0)),
                  const((1, d)), slab((d, MAIN_COLS)), slab((d, SMALL_COLS)),
                  slab((SMALL_COLS, HEADS * DK)), const((1, HEADS * DK)), const((1, DV)),
                  const((1, SMALL_COLS)), const((1, BRANCH_W)),
                  const((RG_CONV, RG_WIDTH)), const((1, RG_WIDTH)),
                  slab((RG_WIDTH, RG_WIDTH)), const((1, RG_WIDTH)),
                  slab((RG_WIDTH, RG_WIDTH)), const((1, RG_WIDTH)), const((1, RG_WIDTH)),
                  slab((N_BRANCH, BRANCH_W, d)), slab((d, d))],
        out_specs=[pl.BlockSpec((tt, d), lambda b, j: (b * nt + j, 0)),
                   pl.BlockSpec((None, HEADS, DK, DV), lambda b, j: (b, 0, 0, 0)),
                   pl.BlockSpec((None, HEADS, DK, DV), lambda b, j: (b, 0, 0, 0)),
                   pl.BlockSpec((None, HEADS, 1, DK), lambda b, j: (b, 0, 0, 0)),
                   pl.BlockSpec((None, 1, HEADS), lambda b, j: (b, 0, 0)),
                   pl.BlockSpec((None, 1, RG_WIDTH), lambda b, j: (b, 0, 0)),
                   pl.BlockSpec((None, RG_CONV - 1, RG_WIDTH), lambda b, j: (b, 0, 0))],
        out_shape=[jax.ShapeDtypeStruct((nb * t, d), F32),
                   jax.ShapeDtypeStruct((nb, HEADS, DK, DV), F32),
                   jax.ShapeDtypeStruct((nb, HEADS, DK, DV), F32),
                   jax.ShapeDtypeStruct((nb, HEADS, 1, DK), F32),
                   jax.ShapeDtypeStruct((nb, 1, HEADS), F32),
                   jax.ShapeDtypeStruct((nb, 1, RG_WIDTH), F32),
                   jax.ShapeDtypeStruct((nb, RG_CONV - 1, RG_WIDTH), F32)],
        scratch_shapes=[pltpu.VMEM((tt, MIX_COLS), F32),
                        pltpu.VMEM((tt, SMALL_COLS), F32),
                        pltpu.VMEM((tt, d), F32),
                        pltpu.VMEM((tt, N_BRANCH * BRANCH_W), BF16),
                        pltpu.VMEM((tt, GATE_COLS), F32),
                        pltpu.VMEM((tt, d), BF16),
                        pltpu.VMEM((tt, BRANCH_W), F32),
                        pltpu.VMEM((tt + 8, RG_WIDTH), F32)],
        compiler_params=pltpu.CompilerParams(
            dimension_semantics=("parallel", "arbitrary"), vmem_limit_bytes=VMEM_LIMIT),
        name="mix_prompt",
    )(x, g.reshape(1, d), wts["w_main"], wts["w_small"],
      wts["wa2"], lw["ba"], lw["gnorm"], lw["bif"], lw["mnorm"], lw["convw"], lw["convb"],
      wts["wr"], lw["br"], wts["wi"], lw["bi"], lw["lam"], wts["wb"], wts["wo"])


def _mix_sample_kernel(pm_ref, ps_ref, s0_ref, c0_ref, n0_ref, m0_ref, h0_ref, buf0_ref,
                       wa2_ref, ba_ref, gnorm_ref, bif_ref, mnorm_ref,
                       convw_ref, convb_ref, wr_ref, br_ref, wi_ref, bi_ref, lam_ref,
                       br_out_ref, s_ref, c_ref, n_ref, m_ref, h_ref, buf_ref,
                       og_ref, om_ref):
    bs = pm_ref.shape[0]
    ps = ps_ref[...]
    a_pre = _dot(ps.astype(BF16), wa2_ref[...]) + ba_ref[...]
    a = jnp.exp(_log_sigmoid(a_pre) * (1.0 / GLA_GATE_NORM))
    a_hi = a.astype(BF16)
    a_r1 = a - a_hi.astype(F32)
    a_mid = a_r1.astype(BF16)
    a_lo = (a_r1 - a_mid.astype(F32)).astype(BF16)
    mq = pm_ref[:, C_MQ:C_MQ + HEADS * DK] * (DK ** -0.5)
    mk = pm_ref[:, C_MK:C_MK + HEADS * DK]
    kd = HEADS * DK

    pre = ps + bif_ref[...]
    f_log = pltpu.roll(_log_sigmoid(pre), SMALL_COLS - (L_MF - L_MI), 1)
    inter = f_log + m0_ref[...]
    m_t = jnp.maximum(inter, pre)
    lane = lax.broadcasted_iota(jnp.int32, (bs, SMALL_COLS), 1)
    is_head = (lane >= L_MI) & (lane < L_MI + HEADS)
    w_intra = jnp.where(is_head, jnp.exp(pre - m_t), 0.0)
    w_inter = jnp.where(is_head, jnp.exp(inter - m_t), 0.0)
    kd_row = lax.broadcasted_iota(jnp.int32, (kd, SMALL_COLS), 0)
    head_sum = jnp.where(
        lax.shift_right_logical(kd_row, DK.bit_length() - 1) + L_MI
        == lax.broadcasted_iota(jnp.int32, (kd, SMALL_COLS), 1), 1.0, 0.0).astype(BF16)
    sc = _dot_mask_exact_rhs(mq * mk, head_sum) * w_intra
    den = w_inter * _dot_mask_exact_rhs(mq * n0_ref[...], head_sum) + sc
    dmax = jnp.where(is_head, jnp.maximum(jnp.abs(den), jnp.exp(-m_t)), 1.0)
    m_ref[...] = m_t
    lane_head = lax.broadcasted_iota(jnp.int32, (SMALL_COLS, kd), 0) - L_MI
    head_spread = jnp.where(
        lane_head == lax.shift_right_logical(
            lax.broadcasted_iota(jnp.int32, (SMALL_COLS, kd), 1), DK.bit_length() - 1),
        1.0, 0.0).astype(BF16)
    n_ref[...] = (_dot_mask_exact_rhs(w_inter, head_spread) * n0_ref[...]
                  + _dot_mask_exact_rhs(w_intra, head_spread) * mk)

    scalars = jnp.concatenate([w_inter, w_intra, sc, dmax], axis=1)
    sc_hi = scalars.astype(BF16)
    sc_r1 = scalars - sc_hi.astype(F32)
    sc_mid = sc_r1.astype(BF16)
    sc_lo = (sc_r1 - sc_mid.astype(F32)).astype(BF16)
    n_scalar = 4

    def scalar_rows(piece):
        pt = piece.astype(F32).T
        return jnp.concatenate(
            [pt[i * SMALL_COLS + L_MI:i * SMALL_COLS + L_MI + 8, :] for i in range(n_scalar)],
            axis=0)

    cols_t = jnp.concatenate(
        [(pm_ref[:, C_GQ:C_GQ + kd] * (DK ** -0.5)).T, pm_ref[:, C_GK:C_GK + kd].T,
         mq.T, mk.T, a_hi.astype(F32).T, a_mid.astype(F32).T, a_lo.astype(F32).T,
         scalar_rows(sc_hi), scalar_rows(sc_mid), scalar_rows(sc_lo)],
        axis=0).astype(BF16)
    o_gq, o_gk, o_mq, o_mk, o_a0, o_a1, o_a2 = (i * kd for i in range(7))
    o_scalar = 7 * kd
    sample_id = lax.broadcasted_iota(jnp.int32, (bs, DV), 0)

    for s in range(bs):
        row = slice(s, s + 1)
        bc = _dot(cols_t, jnp.where(sample_id == s, 1.0, 0.0).astype(BF16))
        for h in range(HEADS):
            col = lambda off: bc[off + h * DK:off + (h + 1) * DK, :]

            def scalar(i):
                r0 = o_scalar + i * 8 + h
                return (bc[r0:r0 + 1, :] + bc[r0 + 8 * n_scalar:r0 + 8 * n_scalar + 1, :]
                        + bc[r0 + 16 * n_scalar:r0 + 16 * n_scalar + 1, :])

            v_row = pm_ref[row, C_GV + h * DV:C_GV + (h + 1) * DV]
            a_col = col(o_a0) + col(o_a1) + col(o_a2)
            s_new = a_col * s0_ref[s, h] + col(o_gk) * v_row
            s_ref[s, h] = s_new
            og_ref[row, h * DV:(h + 1) * DV] = jnp.sum(col(o_gq) * s_new,
                                                       axis=0, keepdims=True)
            v_row = pm_ref[row, C_MV + h * DV:C_MV + (h + 1) * DV]
            c_old = c0_ref[s, h]
            qc = jnp.sum(col(o_mq) * c_old, axis=0, keepdims=True)
            om_ref[row, h * DV:(h + 1) * DV] = (scalar(0) * qc + scalar(2) * v_row) / scalar(3)
            c_ref[s, h] = scalar(0) * c_old + (scalar(1) * col(o_mk)) * v_row

    gnorm = gnorm_ref[...]
    for h in range(HEADS):
        cols = slice(h * DV, (h + 1) * DV)
        y = _rms(og_ref[:, cols], gnorm) * _silu(pm_ref[:, C_GG + h * DV:C_GG + (h + 1) * DV])
        br_out_ref[:, cols] = y.astype(BF16)
        y = (_rms(om_ref[:, cols], mnorm_ref[:, cols])
             * _sigmoid(pm_ref[:, C_MO + h * DV:C_MO + (h + 1) * DV]))
        br_out_ref[:, BRANCH_W + h * DV:BRANCH_W + (h + 1) * DV] = y.astype(BF16)

    x = pm_ref[:, C_RX:C_RX + RG_WIDTH]
    xc = convb_ref[...] + convw_ref[3:4, :] * x
    for jj in range(RG_CONV - 1):
        xc = xc + convw_ref[jj:jj + 1, :] * buf0_ref[:, jj * RG_WIDTH:(jj + 1) * RG_WIDTH]
    buf_ref[:, 0:2 * RG_WIDTH] = buf0_ref[:, RG_WIDTH:3 * RG_WIDTH]
    buf_ref[:, 2 * RG_WIDTH:3 * RG_WIDTH] = x
    a, b = _rg_gates(xc, wr_ref, br_ref, wi_ref, bi_ref, lam_ref)
    hnew = a * h0_ref[...] + b
    h_ref[...] = hnew
    y = hnew * _gelu_tanh(pm_ref[:, C_RG:C_RG + RG_WIDTH])
    br_out_ref[:, 2 * BRANCH_W:3 * BRANCH_W] = y.astype(BF16)


def _mix_sample(pm, ps, st, wts, lw, layer, *, bs):
    n = pm.shape[0]
    const = lambda shape: pl.BlockSpec(shape, lambda i: (0,) * len(shape))
    slab = lambda shape: pl.BlockSpec((None,) + shape, lambda i: (layer,) + (0,) * len(shape))
    blk = lambda shape: pl.BlockSpec(shape, lambda i: (i,) + (0,) * (len(shape) - 1))
    lblk = lambda shape: pl.BlockSpec((None,) + shape,
                                      lambda i: (layer, i) + (0,) * (len(shape) - 1))
    s0, c0, n0, m0, h0, buf0 = st
    shapes = [(bs, HEADS, DK, DV), (bs, HEADS, DK, DV), (bs, HEADS * DK),
              (bs, SMALL_COLS), (bs, RG_WIDTH), (bs, (RG_CONV - 1) * RG_WIDTH)]
    return pl.pallas_call(
        _mix_sample_kernel,
        grid=(n // bs,),
        in_specs=[blk((bs, MIX_COLS)), blk((bs, SMALL_COLS))] + [lblk(sh) for sh in shapes] + [
            slab((SMALL_COLS, HEADS * DK)), const((1, HEADS * DK)), const((1, DV)),
            const((1, SMALL_COLS)), const((1, BRANCH_W)),
            const((RG_CONV, RG_WIDTH)), const((1, RG_WIDTH)),
            slab((RG_WIDTH, RG_WIDTH)), const((1, RG_WIDTH)),
            slab((RG_WIDTH, RG_WIDTH)), const((1, RG_WIDTH)), const((1, RG_WIDTH))],
        out_specs=[blk((bs, N_BRANCH * BRANCH_W))] + [blk(sh) for sh in shapes],
        out_shape=[jax.ShapeDtypeStruct((n, N_BRANCH * BRANCH_W), BF16)]
        + [jax.ShapeDtypeStruct(a.shape[1:], F32) for a in st],
        scratch_shapes=[pltpu.VMEM((bs, BRANCH_W), F32),
                        pltpu.VMEM((bs, BRANCH_W), F32)],
        compiler_params=pltpu.CompilerParams(
            dimension_semantics=("parallel",), vmem_limit_bytes=VMEM_LIMIT),
        name="mix_sample",
    )(pm, ps, s0, c0, n0, m0, h0, buf0,
      wts["wa2"], lw["ba"], lw["gnorm"], lw["bif"], lw["mnorm"], lw["convw"], lw["convb"],
      wts["wr"], lw["br"], wts["wi"], lw["bi"], lw["lam"])


def _merge_kernel(x_ref, br_ref, g0_ref, g1_ref, g2_ref, wb_ref, wo_ref, o_ref):
    merged = None
    for b, g_ref in enumerate((g0_ref, g1_ref, g2_ref)):
        z = _dot(br_ref[:, b * BRANCH_W:(b + 1) * BRANCH_W], wb_ref[b])
        term = _sigmoid(g_ref[...]) * z
        merged = term if merged is None else merged + term
    o_ref[...] = x_ref[...] + _dot(merged.astype(BF16), wo_ref[...])


def _merge(x, br, pm, wb, wo, layer, *, tm):
    n, d = x.shape
    gate_blk0 = MIX_COLS // d
    gate_spec = lambda b: pl.BlockSpec((tm, d), lambda i: (i, gate_blk0 + b))
    return pl.pallas_call(
        _merge_kernel,
        grid=(n // tm,),
        in_specs=[pl.BlockSpec((tm, d), lambda i: (i, 0)),
                  pl.BlockSpec((tm, N_BRANCH * BRANCH_W), lambda i: (i, 0)),
                  gate_spec(0), gate_spec(1), gate_spec(2),
                  pl.BlockSpec((None, N_BRANCH, BRANCH_W, d), lambda i: (layer, 0, 0, 0)),
                  pl.BlockSpec((None, d, d), lambda i: (layer, 0, 0))],
        out_specs=pl.BlockSpec((tm, d), lambda i: (i, 0)),
        out_shape=jax.ShapeDtypeStruct((n, d), F32),
        compiler_params=pltpu.CompilerParams(
            dimension_semantics=("parallel",), vmem_limit_bytes=VMEM_LIMIT),
        name="merge",
    )(x, br, pm, pm, pm, wb, wo)


def _xattn_ffn_prompt_kernel(x_ref, gx_ref, wq_ref, wo_ref, k_ref, v_ref,
                             gf_ref, wg_ref, wu_ref, wd_ref, gfin_ref,
                             o_ref, kb_ref, vb_ref, hid_ref, *, final_norm):
    @pl.when(pl.program_id(1) == 0)
    def _():
        kb_ref[...] = k_ref[...].astype(BF16)
        vb_ref[...] = v_ref[...].astype(BF16)

    x = x_ref[...]
    q = _dot(_rms(x, gx_ref[...]).astype(BF16), wq_ref[...]).astype(BF16)
    heads = []
    for h in range(XA_HEADS):
        cols = slice(h * XA_HEAD_DIM, (h + 1) * XA_HEAD_DIM)
        s = _dot_nt(q[:, cols], kb_ref[:, cols]) * (XA_HEAD_DIM ** -0.5)
        p = jnp.exp(s - jnp.max(s, axis=-1, keepdims=True))
        p = p / jnp.sum(p, axis=-1, keepdims=True)
        heads.append(_dot(p.astype(BF16), vb_ref[:, cols]))
    o = jnp.concatenate(heads, axis=1).astype(BF16)
    x = x + _dot(o, wo_ref[...])
    o_ref[...] = _ffn_block(x, gf_ref, wg_ref, wu_ref, wd_ref, gfin_ref, hid_ref, final_norm)


def _xattn_ffn_prompt(x, gx, wq, wo, mem_k, mem_v, gf, wg, wu, wd, gfin, layer, *,
                      nb, t, tm, final_norm):
    d = x.shape[1]
    f = wg.shape[2]
    nt = t // tm
    const = lambda shape: pl.BlockSpec(shape, lambda b, j: (0,) * len(shape),
                                       pipeline_mode=pl.Buffered(1))
    slab = lambda shape: pl.BlockSpec((None,) + shape, lambda b, j: (layer,) + (0,) * len(shape),
                                      pipeline_mode=pl.Buffered(1))
    return pl.pallas_call(
        functools.partial(_xattn_ffn_prompt_kernel, final_norm=final_norm),
        grid=(nb, nt),
        in_specs=[pl.BlockSpec((tm, d), lambda b, j: (b * nt + j, 0)),
                  const((1, d)), slab((d, d)), slab((d, d)),
                  pl.BlockSpec((None, N_MEM, d), lambda b, j: (b, 0, 0)),
                  pl.BlockSpec((None, N_MEM, d), lambda b, j: (b, 0, 0)),
                  const((1, d)), slab((d, f)), slab((d, f)), slab((f, d)), const((1, d))],
        out_specs=pl.BlockSpec((tm, d), lambda b, j: (b * nt + j, 0)),
        out_shape=jax.ShapeDtypeStruct(x.shape, F32),
        scratch_shapes=[pltpu.VMEM((N_MEM, d), BF16), pltpu.VMEM((N_MEM, d), BF16),
                        pltpu.VMEM((tm, f), BF16)],
        compiler_params=pltpu.CompilerParams(
            dimension_semantics=("parallel", "arbitrary"), vmem_limit_bytes=VMEM_LIMIT),
        name="xattn_ffn_prompt",
    )(x, gx.reshape(1, d), wq, wo, mem_k, mem_v, gf.reshape(1, d), wg, wu, wd, gfin.reshape(1, d))


def _xattn_sample_kernel(q_ref, k_ref, v_ref, o_ref):
    bs = q_ref.shape[0]
    nrow = N_MEM * XA_HEADS
    row_head = lax.broadcasted_iota(jnp.int32, (XA_HEADS, nrow), 1) & (XA_HEADS - 1)
    own_head = row_head == lax.broadcasted_iota(jnp.int32, (XA_HEADS, nrow), 0)
    for s in range(bs):
        q = q_ref[s].astype(BF16)
        k2 = k_ref[s].reshape(nrow, XA_HEAD_DIM).astype(BF16)
        v2 = v_ref[s].reshape(nrow, XA_HEAD_DIM).astype(BF16)
        sc = jnp.where(own_head, _dot_nt(q, k2) * (XA_HEAD_DIM ** -0.5), -jnp.inf)
        p = jnp.exp(sc - jnp.max(sc, axis=1, keepdims=True))
        p = p / jnp.sum(p, axis=1, keepdims=True)
        o_ref[s] = _dot(p.astype(BF16), v2)


def _xattn_sample(q, cache_k, cache_v, layer, *, bs):
    n = q.shape[0]
    blk = (None, bs, N_MEM, XA_HEADS, XA_HEAD_DIM)
    return pl.pallas_call(
        _xattn_sample_kernel,
        grid=(n // bs,),
        in_specs=[pl.BlockSpec((bs, XA_HEADS, XA_HEAD_DIM), lambda i: (i, 0, 0)),
                  pl.BlockSpec(blk, lambda i: (layer, i, 0, 0, 0)),
                  pl.BlockSpec(blk, lambda i: (layer, i, 0, 0, 0))],
        out_specs=pl.BlockSpec((bs, XA_HEADS, XA_HEAD_DIM), lambda i: (i, 0, 0)),
        out_shape=jax.ShapeDtypeStruct((n, XA_HEADS, XA_HEAD_DIM), F32),
        compiler_params=pltpu.CompilerParams(
            dimension_semantics=("parallel",), vmem_limit_bytes=VMEM_LIMIT),
        name="xattn_sample",
    )(q, cache_k, cache_v)


def _matmul_residual_kernel(x_ref, a_ref, w_ref, o_ref):
    o_ref[...] = x_ref[...] + _dot(a_ref[...].astype(BF16), w_ref[...])


def _matmul_residual(x, a, w, layer):
    n, d = x.shape
    return pl.pallas_call(
        _matmul_residual_kernel,
        grid=(1,),
        in_specs=[pl.BlockSpec((n, d), lambda i: (0, 0)),
                  pl.BlockSpec((n, d), lambda i: (0, 0)),
                  pl.BlockSpec((None, d, d), lambda i: (layer, 0, 0))],
        out_specs=pl.BlockSpec((n, d), lambda i: (0, 0)),
        out_shape=jax.ShapeDtypeStruct((n, d), F32),
        compiler_params=pltpu.CompilerParams(vmem_limit_bytes=VMEM_LIMIT),
        name="matmul_residual",
    )(x, a, w)


def _ffn_block(x, g_ref, wg_ref, wu_ref, wd_ref, gfin_ref, hid_ref, final_norm):
    u = _rms(x, g_ref[...]).astype(BF16)
    for jt in range(D_FF // FFN_TILE):
        cols = slice(jt * FFN_TILE, (jt + 1) * FFN_TILE)
        hid = _silu(_dot(u, wg_ref[:, cols])) * _dot(u, wu_ref[:, cols])
        hid_ref[:, cols] = hid.astype(BF16)
    y = x + _dot(hid_ref[...], wd_ref[...])
    if final_norm:
        y = _rms(y, gfin_ref[...])
    return y


def _ffn_kernel(x_ref, g_ref, wg_ref, wu_ref, wd_ref, gf_ref, o_ref, hid_ref, *, final_norm):
    o_ref[...] = _ffn_block(x_ref[...], g_ref, wg_ref, wu_ref, wd_ref, gf_ref, hid_ref, final_norm)


def _ffn(x, g, wg, wu, wd, gf, layer, *, tm, final_norm):
    n, d = x.shape
    f = wg.shape[2]
    const = lambda shape: pl.BlockSpec(shape, lambda i: (0,) * len(shape),
                                       pipeline_mode=pl.Buffered(1))
    slab = lambda shape: pl.BlockSpec((None,) + shape, lambda i: (layer,) + (0,) * len(shape),
                                      pipeline_mode=pl.Buffered(1))
    return pl.pallas_call(
        functools.partial(_ffn_kernel, final_norm=final_norm),
        grid=(n // tm,),
        in_specs=[pl.BlockSpec((tm, d), lambda i: (i, 0)),
                  const((1, d)), slab((d, f)), slab((d, f)), slab((f, d)), const((1, d))],
        out_specs=pl.BlockSpec((tm, d), lambda i: (i, 0)),
        out_shape=jax.ShapeDtypeStruct((n, d), F32),
        scratch_shapes=[pltpu.VMEM((tm, f), BF16)],
        compiler_params=pltpu.CompilerParams(
            dimension_semantics=("parallel",), vmem_limit_bytes=VMEM_LIMIT),
        name="ffn",
    )(x, g.reshape(1, d), wg, wu, wd, gf.reshape(1, d))


def _split_w_in(w_in):
    widths = (HEADS * DK, HEADS * DK, HEADS * DV, HEADS * DV, GLA_RANK,
              HEADS * DK, HEADS * DK, HEADS * DV, HEADS * DV, HEADS, HEADS,
              RG_WIDTH, RG_WIDTH, GATE_COLS)
    parts, acc = [], 0
    for w in widths:
        parts.append(w_in[..., acc:acc + w])
        acc += w
    return parts


def _block_diag(w):
    eye = jnp.eye(RG_BLOCKS, dtype=w.dtype)
    return jnp.einsum("lnde,nm->lndme", w, eye).reshape(DEPTH, RG_WIDTH, RG_WIDTH)


def _matmul_weights(p):
    (g_q, g_k, g_v, g_g, g_a, m_q, m_k, m_v, m_o, m_i, m_f, r_x, r_g, gates) = _split_w_in(p["w_in"])
    d = D_MODEL
    w_main = jnp.concatenate([g_q, g_k, g_v, g_g, m_q, m_k, m_v, m_o, r_x, r_g, gates], axis=2)
    w_small = jnp.concatenate(
        [g_a, m_i, m_f, jnp.zeros((DEPTH, d, SMALL_COLS - GLA_RANK - 2 * HEADS), F32)], axis=2)
    wa2 = jnp.concatenate(
        [p["gla_w_a2"], jnp.zeros((DEPTH, SMALL_COLS - GLA_RANK, HEADS * DK), F32)], axis=1)
    return {
        "w_main": w_main.astype(BF16), "w_small": w_small.astype(BF16), "wa2": wa2.astype(BF16),
        "wr": _block_diag(p["rg_w_r"]).astype(BF16), "wi": _block_diag(p["rg_w_i"]).astype(BF16),
        "wb": p["w_branch"].astype(BF16), "wo": p["w_out"].astype(BF16),
        "xa_wq": p["xa_wq"].astype(BF16), "xa_wo": p["xa_wo"].astype(BF16),
        "xa_wkv": jnp.concatenate([p["xa_wk"], p["xa_wv"]], axis=2).astype(BF16),
        "wg": p["ffn_w_gate"].astype(BF16), "wu": p["ffn_w_up"].astype(BF16),
        "wd": p["ffn_w_down"].astype(BF16),
    }


def _vector_params(l, p):
    bif = jnp.concatenate(
        [jnp.zeros((L_MI,), F32), p["mlstm_b_i"][l], p["mlstm_b_f"][l],
         jnp.zeros((SMALL_COLS - L_MF - HEADS,), F32)]).reshape(1, SMALL_COLS)
    return {
        "ba": p["gla_b_a"][l].reshape(1, -1), "gnorm": p["gla_norm"][l].reshape(1, DV),
        "bif": bif, "mnorm": p["mlstm_norm"][l].reshape(1, BRANCH_W),
        "convw": p["rg_conv_w"][l], "convb": p["rg_conv_b"][l].reshape(1, -1),
        "br": p["rg_b_r"][l].reshape(1, -1), "bi": p["rg_b_i"][l].reshape(1, -1),
        "lam": p["rg_lambda"][l].reshape(1, -1),
    }


def kernel(x_prompt, x_sample, mem_prompt, cache_mem_k, cache_mem_v, state_gla, state_mlstm_c, state_mlstm_n, state_mlstm_m, state_rglru_h, state_rglru_conv, norm_mix, w_in, gla_w_a2, gla_b_a, gla_norm, mlstm_b_i, mlstm_b_f, mlstm_norm, rg_conv_w, rg_conv_b, rg_w_r, rg_b_r, rg_w_i, rg_b_i, rg_lambda, w_branch, w_out, norm_xa, norm_mem, xa_wq, xa_wk, xa_wv, xa_wo, norm_ffn, ffn_w_gate, ffn_w_up, ffn_w_down, norm_final):
    p = {"w_in": w_in, "gla_w_a2": gla_w_a2, "gla_b_a": gla_b_a, "gla_norm": gla_norm,
         "mlstm_b_i": mlstm_b_i, "mlstm_b_f": mlstm_b_f, "mlstm_norm": mlstm_norm,
         "rg_conv_w": rg_conv_w, "rg_conv_b": rg_conv_b, "rg_w_r": rg_w_r, "rg_b_r": rg_b_r,
         "rg_w_i": rg_w_i, "rg_b_i": rg_b_i, "rg_lambda": rg_lambda, "w_branch": w_branch,
         "w_out": w_out, "xa_wq": xa_wq, "xa_wk": xa_wk, "xa_wv": xa_wv, "xa_wo": xa_wo,
         "ffn_w_gate": ffn_w_gate, "ffn_w_up": ffn_w_up, "ffn_w_down": ffn_w_down}
    nb, t, d = x_prompt.shape
    ns = x_sample.shape[0]
    xp = x_prompt.reshape(nb * t, d)
    xs = x_sample.reshape(ns, d)
    mem = mem_prompt.reshape(nb * N_MEM, d)
    wts = _matmul_weights(p)

    st_sample = (state_gla, state_mlstm_c, state_mlstm_n.reshape(DEPTH, ns, HEADS * DK),
                 jnp.pad(state_mlstm_m, ((0, 0), (0, 0), (L_MI, SMALL_COLS - L_MI - HEADS))),
                 state_rglru_h,
                 state_rglru_conv.reshape(DEPTH, ns, (RG_CONV - 1) * RG_WIDTH))

    new_p = [[] for _ in range(8)]
    new_s = [[] for _ in range(6)]
    for l in range(DEPTH):
        lw = _vector_params(l, p)
        last = l == DEPTH - 1

        k_p, v_p = _norm_matmul(mem, norm_mem[l], wts["xa_wkv"], l, tm=PROMPT_ROWS, tn=d)
        k_p = k_p.reshape(nb, N_MEM, d)
        v_p = v_p.reshape(nb, N_MEM, d)

        xp, g_s, c_s, n_s, m_s, h_s, buf_s = _mix_prompt(xp, norm_mix[l], wts, lw, l,
                                                         nb=nb, t=t, tt=MIX_TIME_BLOCK)
        xp = _xattn_ffn_prompt(xp, norm_xa[l], wts["xa_wq"], wts["xa_wo"], k_p, v_p,
                               norm_ffn[l], wts["wg"], wts["wu"], wts["wd"], norm_final, l,
                               nb=nb, t=t, tm=PROMPT_ROWS, final_norm=last)
        for lst, val in zip(new_p, (k_p.reshape(nb, N_MEM, XA_HEADS, XA_HEAD_DIM),
                                    v_p.reshape(nb, N_MEM, XA_HEADS, XA_HEAD_DIM),
                                    g_s, c_s, n_s.reshape(nb, HEADS, DK), m_s.reshape(nb, HEADS),
                                    h_s.reshape(nb, RG_WIDTH), buf_s)):
            lst.append(val)

        pm, ps = _in_proj(xs, norm_mix[l], wts["w_main"], wts["w_small"], l,
                          tm=ns, tn=IN_PROJ_COLS)
        br, g_s, c_s, n_s, m_s, h_s, buf_s = _mix_sample(pm, ps, st_sample, wts, lw, l,
                                                         bs=SAMPLE_BLOCK)
        xs = _merge(xs, br, pm, wts["wb"], wts["wo"], l, tm=ns)
        (q,) = _norm_matmul(xs, norm_xa[l], wts["xa_wq"], l, tm=ns, tn=d)
        att = _xattn_sample(q.reshape(ns, XA_HEADS, XA_HEAD_DIM), cache_mem_k, cache_mem_v, l,
                            bs=SAMPLE_BLOCK)
        xs = _matmul_residual(xs, att.reshape(ns, d), wts["xa_wo"], l)
        xs = _ffn(xs, norm_ffn[l], wts["wg"], wts["wu"], wts["wd"], norm_final, l,
                  tm=ns, final_norm=last)
        for lst, val in zip(new_s, (g_s, c_s, n_s.reshape(ns, HEADS, DK),
                                    m_s[:, L_MI:L_MI + HEADS], h_s,
                                    buf_s.reshape(ns, RG_CONV - 1, RG_WIDTH))):
            lst.append(val)

    y_prompt = xp.reshape(nb, t, d)
    y_sample = xs.reshape(ns, 1, d)
    outs_p = [jnp.stack(v, axis=0) for v in new_p]
    outs_s = [jnp.stack(v, axis=0) for v in new_s]
    return (y_prompt, y_sample, *outs_p, *outs_s)
```

```python
import functools
import math

import jax
import jax.numpy as jnp
from jax import lax
from jax.experimental import pallas as pl
from jax.experimental.pallas import tpu as pltpu

F32 = jnp.float32
BF16 = jnp.bfloat16

D_MODEL = 1024
DEPTH = 2
EPS = 1e-6
N_MEM = 256
XA_HEADS = 4
XA_HEAD_DIM = D_MODEL // XA_HEADS
N_BRANCH = 3
BRANCH_W = D_MODEL // 2
HEADS = 4
DK = 64
DV = 128
GLA_RANK = 16
GLA_GATE_NORM = 16.0
GLA_CHUNK = 32
RG_WIDTH = BRANCH_W
RG_BLOCKS = 8
RG_CONV = 4
RG_C = 8.0
D_FF = 2816

C_GQ, C_GK, C_GV, C_GG = 0, 256, 512, 1024
C_MQ, C_MK, C_MV, C_MO = 1536, 1792, 2048, 2560
C_RX, C_RG = 3072, 3584
MIX_COLS = 4096
GATE_COLS = N_BRANCH * D_MODEL
MAIN_COLS = MIX_COLS + GATE_COLS
SMALL_COLS = 128
L_MI = 16
L_MF = 20

MXU_WIDTH = 256
PROJ_TILE = 2 * MXU_WIDTH
MERGE_TILE = 2 * MXU_WIDTH
FFN_TILE = MXU_WIDTH
MIX_TIME_BLOCK = 256
PROMPT_ROWS = 512
SAMPLE_BLOCK = 8
IN_PROJ_COLS = 1024
VMEM_LIMIT = 48 * 1024 * 1024
assert all(c % PROJ_TILE == 0 for c in (C_GG, C_MQ, C_MO, C_RX, C_RG, MIX_COLS, MAIN_COLS))
assert D_MODEL % MERGE_TILE == 0 and D_FF % FFN_TILE == 0


def _softplus(x):
    return jnp.maximum(x, 0.0) + jnp.log1p(jnp.exp(-jnp.abs(x)))


def _log_sigmoid(x):
    return -_softplus(-x)


def _sigmoid(x):
    return 1.0 / (1.0 + jnp.exp(-x))


def _silu(x):
    return x * _sigmoid(x)


def _gelu_tanh(x):
    c = math.sqrt(2.0 / math.pi)
    return x * (0.5 * (1.0 + jnp.tanh(c * (x + 0.044715 * (x * x * x)))))


def _neg_expm1(x):
    return -jnp.tanh(0.5 * x) * (jnp.exp(x) + 1.0)


def _rms(x, g):
    ms = jnp.mean(x * x, axis=-1, keepdims=True)
    return x * lax.rsqrt(ms + EPS) * g


def _dot(a, b):
    return jnp.dot(a, b, preferred_element_type=F32)


def _dot_nt(a, b):
    return lax.dot_general(a, b, (((1,), (1,)), ((), ())), preferred_element_type=F32)


def _dot_tn(a, b):
    return lax.dot_general(a, b, (((0,), (0,)), ((), ())), preferred_element_type=F32)


def _dot_mask_exact(mask, x):
    hi = x.astype(BF16)
    rest = x - hi.astype(F32)
    mid = rest.astype(BF16)
    lo = (rest - mid.astype(F32)).astype(BF16)
    return _dot(mask, hi) + _dot(mask, mid) + _dot(mask, lo)


def _dot_mask_exact_rhs(x, mask):
    hi = x.astype(BF16)
    rest = x - hi.astype(F32)
    mid = rest.astype(BF16)
    lo = (rest - mid.astype(F32)).astype(BF16)
    return _dot(hi, mask) + _dot(mid, mask) + _dot(lo, mask)


def _chunk_masks(n, chunk):
    shift = chunk.bit_length() - 1
    r = lax.broadcasted_iota(jnp.int32, (n, n), 0)
    c = lax.broadcasted_iota(jnp.int32, (n, n), 1)
    same = lax.shift_right_logical(r, shift) == lax.shift_right_logical(c, shift)
    tril = jnp.where(same & (c <= r), 1.0, 0.0).astype(F32)
    full = jnp.where(same, 1.0, 0.0).astype(F32)
    return tril, full


def _norm_matmul_kernel(x_ref, g_ref, w_ref, *refs):
    out_refs, u_ref = refs[:-1], refs[-1]
    j = pl.program_id(1)

    @pl.when(j == 0)
    def _():
        u_ref[...] = _rms(x_ref[...], g_ref[...]).astype(BF16)

    res = _dot(u_ref[...], w_ref[...])
    for idx, o_ref in enumerate(out_refs):
        @pl.when(j == idx)
        def _(o_ref=o_ref):
            o_ref[...] = res


def _norm_matmul(x, g, w, layer, *, tm, tn):
    n, d = x.shape
    nout = w.shape[2]
    n_out = nout // tn
    return pl.pallas_call(
        _norm_matmul_kernel,
        grid=(n // tm, n_out),
        in_specs=[pl.BlockSpec((tm, d), lambda i, j: (i, 0)),
                  pl.BlockSpec((1, d), lambda i, j: (0, 0)),
                  pl.BlockSpec((None, d, tn), lambda i, j: (layer, 0, j))],
        out_specs=[pl.BlockSpec((tm, tn), lambda i, j: (i, 0))] * n_out,
        out_shape=[jax.ShapeDtypeStruct((n, tn), F32)] * n_out,
        scratch_shapes=[pltpu.VMEM((tm, d), BF16)],
        compiler_params=pltpu.CompilerParams(
            dimension_semantics=("parallel", "arbitrary"), vmem_limit_bytes=VMEM_LIMIT),
        name="norm_matmul",
    )(x, g.reshape(1, d), w)


def _in_proj_kernel(x_ref, g_ref, w_ref, ws_ref, o_ref, os_ref, u_ref):
    @pl.when(pl.program_id(1) == 0)
    def _():
        u = _rms(x_ref[...], g_ref[...]).astype(BF16)
        u_ref[...] = u
        os_ref[...] = _dot(u, ws_ref[...])

    o_ref[...] = _dot(u_ref[...], w_ref[...])


def _in_proj(x, g, w_main, w_small, layer, *, tm, tn):
    n, d = x.shape
    return pl.pallas_call(
        _in_proj_kernel,
        grid=(n // tm, MAIN_COLS // tn),
        in_specs=[pl.BlockSpec((tm, d), lambda i, j: (i, 0)),
                  pl.BlockSpec((1, d), lambda i, j: (0, 0)),
                  pl.BlockSpec((None, d, tn), lambda i, j: (layer, 0, j)),
                  pl.BlockSpec((None, d, SMALL_COLS), lambda i, j: (layer, 0, 0))],
        out_specs=[pl.BlockSpec((tm, tn), lambda i, j: (i, j)),
                   pl.BlockSpec((tm, SMALL_COLS), lambda i, j: (i, 0))],
        out_shape=[jax.ShapeDtypeStruct((n, MAIN_COLS), F32),
                   jax.ShapeDtypeStruct((n, SMALL_COLS), F32)],
        scratch_shapes=[pltpu.VMEM((tm, d), BF16)],
        compiler_params=pltpu.CompilerParams(
            dimension_semantics=("parallel", "arbitrary"), vmem_limit_bytes=VMEM_LIMIT),
        name="in_proj",
    )(x, g.reshape(1, d), w_main, w_small)


def _rg_gates(xc, wr_ref, br_ref, wi_ref, bi_ref, lam_ref):
    xcb = xc.astype(BF16)
    r = _sigmoid(_dot(xcb, wr_ref[...]) + br_ref[...])
    i = _sigmoid(_dot(xcb, wi_ref[...]) + bi_ref[...])
    log_a = (-RG_C * r) * _softplus(-lam_ref[...])
    a = jnp.exp(log_a)
    b = jnp.sqrt(_neg_expm1(2.0 * log_a)) * (i * xc)
    return a, b


def _projection_activation(col):
    if C_GG <= col < C_GG + HEADS * DV:
        return _silu
    if C_MO <= col < C_MO + HEADS * DV:
        return _sigmoid
    if C_RG <= col < C_RG + RG_WIDTH:
        return _gelu_tanh
    if col >= MIX_COLS:
        return _sigmoid
    return None


def _project_stage(x_ref, g_ref, wmain_ref, wsmall_ref, pm_ref, ps_ref, xs_ref, sg_ref):
    x = x_ref[...]
    xs_ref[...] = x
    u = _rms(x, g_ref[...]).astype(BF16)
    ps_ref[...] = _dot(u, wsmall_ref[...])
    for cb in range(MAIN_COLS // PROJ_TILE):
        c0 = cb * PROJ_TILE
        y = _dot(u, wmain_ref[:, c0:c0 + PROJ_TILE])
        act = _projection_activation(c0)
        if act is not None:
            y = act(y)
        if c0 >= MIX_COLS:
            sg_ref[:, c0 - MIX_COLS:c0 - MIX_COLS + PROJ_TILE] = y
        else:
            pm_ref[:, c0:c0 + PROJ_TILE] = y


def _mixer_stage(pm_ref, ps_ref,
                 wa2_ref, ba_ref, gnorm_ref, bif_ref, mnorm_ref,
                 convw_ref, convb_ref, wr_ref, br_ref, wi_ref, bi_ref, lam_ref,
                 s_ref, c_ref, n_ref, m_ref, h_ref, buf_ref,
                 br_out_ref, og_ref, xpad_ref):
    tt = pm_ref.shape[0]

    x = pm_ref[:, C_RX:C_RX + RG_WIDTH]
    xpad_ref[pl.ds(8, tt), :] = x
    xc = (convb_ref[...] + convw_ref[3:4, :] * x
          + convw_ref[2:3, :] * xpad_ref[pl.ds(7, tt), :]
          + convw_ref[1:2, :] * xpad_ref[pl.ds(6, tt), :]
          + convw_ref[0:1, :] * xpad_ref[pl.ds(5, tt), :])
    buf_ref[...] = xpad_ref[pl.ds(tt + 8 - (RG_CONV - 1), RG_CONV - 1), :]
    xpad_ref[pl.ds(0, 8), :] = xpad_ref[pl.ds(tt, 8), :]
    a, b = _rg_gates(xc, wr_ref, br_ref, wi_ref, bi_ref, lam_ref)
    ridx = lax.broadcasted_iota(jnp.int32, (tt, RG_WIDTH), 0)
    sh = 1
    while sh < tt:
        keep = ridx >= sh
        a_sh = jnp.where(keep, pltpu.roll(a, sh, 0), 1.0)
        b_sh = jnp.where(keep, pltpu.roll(b, sh, 0), 0.0)
        b = a * b_sh + b
        a = a * a_sh
        sh *= 2
    hseq = a * h_ref[...] + b
    h_ref[...] = hseq[tt - 1:tt, :]
    y = hseq * pm_ref[:, C_RG:C_RG + RG_WIDTH]
    br_out_ref[:, 2 * BRANCH_W:3 * BRANCH_W] = y.astype(BF16)

    ps = ps_ref[...]
    a_pre = _dot(ps.astype(BF16), wa2_ref[...]) + ba_ref[...]
    log_a = _log_sigmoid(a_pre) * (1.0 / GLA_GATE_NORM)
    tril32, full32 = _chunk_masks(tt, GLA_CHUNK)
    sums = _dot_mask_exact(jnp.concatenate([tril32, full32], axis=0).astype(BF16), log_a)
    b_cum = sums[:tt]
    b_end = sums[tt:]
    q = pm_ref[:, C_GQ:C_GQ + HEADS * DK] * (DK ** -0.5)
    k = pm_ref[:, C_GK:C_GK + HEADS * DK]
    qi_f = q * jnp.exp(b_cum)
    qi_all = qi_f.astype(BF16)
    ki_all = (k * jnp.exp(-b_cum)).astype(BF16)
    ke_t = (k * jnp.exp(b_end - b_cum)).T
    de_t = jnp.exp(b_end).T
    in_chunk = tril32 > 0.0
    n_chunks = tt // GLA_CHUNK
    cshift = GLA_CHUNK.bit_length() - 1
    chunk_of_lane = lax.shift_right_logical(lax.broadcasted_iota(jnp.int32, (DK, tt), 1), cshift)
    chunk_of_row = lax.shift_right_logical(lax.broadcasted_iota(jnp.int32, (tt, 2 * DK), 0), cshift)
    upper_half = (lax.broadcasted_iota(jnp.int32, (tt, 2 * DK), 1) >= DK).astype(jnp.int32)
    for h in range(HEADS):
        kcols = slice(h * DK, (h + 1) * DK)
        v = pm_ref[:, C_GV + h * DV:C_GV + (h + 1) * DV].astype(BF16)
        att = jnp.where(in_chunk, _dot_nt(qi_all[:, kcols], ki_all[:, kcols]), 0.0)
        o_intra = _dot(att.astype(BF16), v)
        ke_h = ke_t[kcols, :]
        ke_blk = jnp.concatenate(
            [jnp.where(chunk_of_lane == c, ke_h, 0.0) for c in range(n_chunks)], axis=0)
        upd = _dot(ke_blk.astype(BF16), v)
        s_cur = s_ref[h]
        s_start = []
        for c in range(n_chunks):
            s_start.append(s_cur)
            de_col = de_t[kcols, c * GLA_CHUNK:c * GLA_CHUNK + 1]
            s_cur = de_col * s_cur + upd[c * DK:(c + 1) * DK, :]
        s_ref[h] = s_cur
        qi_h = qi_f[:, kcols]
        qi_dup = jnp.concatenate([qi_h, qi_h], axis=1)
        q_blk = jnp.concatenate(
            [jnp.where(chunk_of_row == 2 * jj + upper_half, qi_dup, 0.0)
             for jj in range(n_chunks // 2)], axis=1)
        o_inter = _dot(q_blk.astype(BF16), jnp.concatenate(s_start, axis=0).astype(BF16))
        og_ref[:, h * DV:(h + 1) * DV] = o_intra + o_inter

    gnorm = gnorm_ref[...]
    for h in range(HEADS):
        cols = slice(h * DV, (h + 1) * DV)
        y = _rms(og_ref[:, cols], gnorm) * pm_ref[:, C_GG + h * DV:C_GG + (h + 1) * DV]
        br_out_ref[:, cols] = y.astype(BF16)

    pre = ps + bif_ref[...]
    rr = lax.broadcasted_iota(jnp.int32, (tt, tt), 0)
    cc = lax.broadcasted_iota(jnp.int32, (tt, tt), 1)
    causal = cc <= rr
    fcum = _dot_mask_exact(jnp.where(causal, 1.0, 0.0).astype(BF16), _log_sigmoid(pre))
    pre_t = pre.T
    fcum_t = fcum.T
    q_all = pm_ref[:, C_MQ:C_MQ + HEADS * DK] * (DK ** -0.5)
    k_all = pm_ref[:, C_MK:C_MK + HEADS * DK]
    for h in range(HEADS):
        kcols = slice(h * DK, (h + 1) * DK)
        fc_col = fcum[:, L_MF + h:L_MF + h + 1]
        ic_col = pre[:, L_MI + h:L_MI + h + 1]
        fc_row = fcum_t[L_MF + h:L_MF + h + 1, :]
        ic_row = pre_t[L_MI + h:L_MI + h + 1, :]
        m_prev = m_ref[:, h:h + 1]
        d = jnp.where(causal, fc_col + (ic_row - fc_row), -jnp.inf)
        inter = fc_col + m_prev
        m_t = jnp.maximum(inter, jnp.max(d, axis=1, keepdims=True))
        w_intra = jnp.exp(d - m_t)
        w_inter = jnp.exp(inter - m_t)
        qf = q_all[:, kcols]
        kf = k_all[:, kcols]
        qc = qf.astype(BF16)
        vc = pm_ref[:, C_MV + h * DV:C_MV + (h + 1) * DV].astype(BF16)
        s = _dot_nt(qc, kf.astype(BF16)) * w_intra
        c_old = c_ref[h]
        n_old = n_ref[h]
        num = w_inter * _dot(qc, c_old.astype(BF16)) + _dot(s.astype(BF16), vc)
        den = (w_inter * jnp.sum(qf * n_old, axis=1, keepdims=True)
               + jnp.sum(s, axis=1, keepdims=True))
        hh = num / jnp.maximum(jnp.abs(den), jnp.exp(-m_t))
        og_ref[:, h * DV:(h + 1) * DV] = hh
        f_end = fc_col[tt - 1:tt, :]
        m_end = m_t[tt - 1:tt, :]
        w_k = jnp.exp(f_end - fc_col + ic_col - m_end)
        w_c = jnp.exp(f_end + m_prev - m_end)
        kw = w_k * kf
        c_ref[h] = w_c * c_old + _dot_tn(kw.astype(BF16), vc)
        n_ref[h] = w_c * n_old + jnp.sum(kw, axis=0, keepdims=True)
        m_ref[:, h:h + 1] = m_end

    for h in range(HEADS):
        cols = slice(h * DV, (h + 1) * DV)
        y = (_rms(og_ref[:, cols], mnorm_ref[:, cols])
             * pm_ref[:, C_MO + h * DV:C_MO + (h + 1) * DV])
        br_out_ref[:, BRANCH_W + h * DV:BRANCH_W + (h + 1) * DV] = y.astype(BF16)


def _merge_stage(br_ref, sg_ref, xr_ref, wb_ref, wo_ref, xo_ref, mg_ref):
    n_tiles = D_MODEL // MERGE_TILE
    for jt in range(n_tiles):
        cols = slice(jt * MERGE_TILE, (jt + 1) * MERGE_TILE)
        merged = None
        for b in range(N_BRANCH):
            z = _dot(br_ref[:, b * BRANCH_W:(b + 1) * BRANCH_W], wb_ref[b, :, cols])
            term = sg_ref[:, b * D_MODEL + jt * MERGE_TILE:b * D_MODEL + (jt + 1) * MERGE_TILE] * z
            merged = term if merged is None else merged + term
        mg_ref[:, cols] = merged.astype(BF16)
    for jt in range(n_tiles):
        cols = slice(jt * MERGE_TILE, (jt + 1) * MERGE_TILE)
        xo_ref[:, cols] = xr_ref[:, cols] + _dot(mg_ref[...], wo_ref[:, cols])


N_MIXER_WEIGHTS = 12


def _mix_prompt_kernel(x_ref, g_ref, wmain_ref, wsmall_ref, *refs):
    mixer_w = refs[:N_MIXER_WEIGHTS]
    wb_ref, wo_ref = refs[N_MIXER_WEIGHTS:N_MIXER_WEIGHTS + 2]
    xo_ref, s_ref, c_ref, n_ref, m_ref, h_ref, buf_ref = refs[N_MIXER_WEIGHTS + 2:N_MIXER_WEIGHTS + 9]
    pm_ref, ps_ref, xs_ref, br_ref, sg_ref, mg_ref, og_ref, xpad_ref = refs[N_MIXER_WEIGHTS + 9:]

    @pl.when(pl.program_id(1) == 0)
    def _():
        for ref in (s_ref, c_ref, n_ref, m_ref, h_ref):
            ref[...] = jnp.zeros_like(ref)
        xpad_ref[pl.ds(0, 8), :] = jnp.zeros((8, RG_WIDTH), F32)

    _project_stage(x_ref, g_ref, wmain_ref, wsmall_ref, pm_ref, ps_ref, xs_ref, sg_ref)
    _mixer_stage(pm_ref, ps_ref, *mixer_w, s_ref, c_ref, n_ref, m_ref, h_ref, buf_ref,
                 br_ref, og_ref, xpad_ref)
    _merge_stage(br_ref, sg_ref, xs_ref, wb_ref, wo_ref, xo_ref, mg_ref)


def _mix_prompt(x, g, wts, lw, layer, *, nb, t, tt):
    d = x.shape[1]
    const = lambda shape: pl.BlockSpec(shape, lambda b, j: (0,) * len(shape),
                                       pipeline_mode=pl.Buffered(1))
    slab = lambda shape: pl.BlockSpec((None,) + shape, lambda b, j: (layer,) + (0,) * len(shape),
                                      pipeline_mode=pl.Buffered(1))
    nt = t // tt
    return pl.pallas_call(
        _mix_prompt_kernel,
        grid=(nb, nt),
        in_specs=[pl.BlockSpec((tt, d), lambda b, j: (b * nt + j, 0)),
                  const((1, d)), slab((d, MAIN_COLS)), slab((d, SMALL_COLS)),
                  slab((SMALL_COLS, HEADS * DK)), const((1, HEADS * DK)), const((1, DV)),
                  const((1, SMALL_COLS)), const((1, BRANCH_W)),
                  const((RG_CONV, RG_WIDTH)), const((1, RG_WIDTH)),
                  slab((RG_WIDTH, RG_WIDTH)), const((1, RG_WIDTH)),
                  slab((RG_WIDTH, RG_WIDTH)), const((1, RG_WIDTH)), const((1, RG_WIDTH)),
                  slab((N_BRANCH, BRANCH_W, d)), slab((d, d))],
        out_specs=[pl.BlockSpec((tt, d), lambda b, j: (b * nt + j, 0)),
                   pl.BlockSpec((None, HEADS, DK, DV), lambda b, j: (b, 0, 0, 0)),
                   pl.BlockSpec((None, HEADS, DK, DV), lambda b, j: (b, 0, 0, 0)),
                   pl.BlockSpec((None, HEADS, 1, DK), lambda b, j: (b, 0, 0, 0)),
                   pl.BlockSpec((None, 1, HEADS), lambda b, j: (b, 0, 0)),
                   pl.BlockSpec((None, 1, RG_WIDTH), lambda b, j: (b, 0, 0)),
                   pl.BlockSpec((None, RG_CONV - 1, RG_WIDTH), lambda b, j: (b, 0, 0))],
        out_shape=[jax.ShapeDtypeStruct((nb * t, d), F32),
                   jax.ShapeDtypeStruct((nb, HEADS, DK, DV), F32),
                   jax.ShapeDtypeStruct((nb, HEADS, DK, DV), F32),
                   jax.ShapeDtypeStruct((nb, HEADS, 1, DK), F32),
                   jax.ShapeDtypeStruct((nb, 1, HEADS), F32),
                   jax.ShapeDtypeStruct((nb, 1, RG_WIDTH), F32),
                   jax.ShapeDtypeStruct((nb, RG_CONV - 1, RG_WIDTH), F32)],
        scratch_shapes=[pltpu.VMEM((tt, MIX_COLS), F32),
                        pltpu.VMEM((tt, SMALL_COLS), F32),
                        pltpu.VMEM((tt, d), F32),
                        pltpu.VMEM((tt, N_BRANCH * BRANCH_W), BF16),
                        pltpu.VMEM((tt, GATE_COLS), F32),
                        pltpu.VMEM((tt, d), BF16),
                        pltpu.VMEM((tt, BRANCH_W), F32),
                        pltpu.VMEM((tt + 8, RG_WIDTH), F32)],
        compiler_params=pltpu.CompilerParams(
            dimension_semantics=("parallel", "arbitrary"), vmem_limit_bytes=VMEM_LIMIT),
        name="mix_prompt",
    )(x, g.reshape(1, d), wts["w_main"], wts["w_small"],
      wts["wa2"], lw["ba"], lw["gnorm"], lw["bif"], lw["mnorm"], lw["convw"], lw["convb"],
      wts["wr"], lw["br"], wts["wi"], lw["bi"], lw["lam"], wts["wb"], wts["wo"])


def _mix_sample_kernel(pm_ref, ps_ref, s0_ref, c0_ref, n0_ref, m0_ref, h0_ref, buf0_ref,
                       wa2_ref, ba_ref, gnorm_ref, bif_ref, mnorm_ref,
                       convw_ref, convb_ref, wr_ref, br_ref, wi_ref, bi_ref, lam_ref,
                       br_out_ref, s_ref, c_ref, n_ref, m_ref, h_ref, buf_ref,
                       og_ref, om_ref):
    bs = pm_ref.shape[0]
    ps = ps_ref[...]
    a_pre = _dot(ps.astype(BF16), wa2_ref[...]) + ba_ref[...]
    a = jnp.exp(_log_sigmoid(a_pre) * (1.0 / GLA_GATE_NORM))
    a_hi = a.astype(BF16)
    a_r1 = a - a_hi.astype(F32)
    a_mid = a_r1.astype(BF16)
    a_lo = (a_r1 - a_mid.astype(F32)).astype(BF16)
    mq = pm_ref[:, C_MQ:C_MQ + HEADS * DK] * (DK ** -0.5)
    mk = pm_ref[:, C_MK:C_MK + HEADS * DK]
    kd = HEADS * DK

    pre = ps + bif_ref[...]
    f_log = pltpu.roll(_log_sigmoid(pre), SMALL_COLS - (L_MF - L_MI), 1)
    inter = f_log + m0_ref[...]
    m_t = jnp.maximum(inter, pre)
    lane = lax.broadcasted_iota(jnp.int32, (bs, SMALL_COLS), 1)
    is_head = (lane >= L_MI) & (lane < L_MI + HEADS)
    w_intra = jnp.where(is_head, jnp.exp(pre - m_t), 0.0)
    w_inter = jnp.where(is_head, jnp.exp(inter - m_t), 0.0)
    kd_row = lax.broadcasted_iota(jnp.int32, (kd, SMALL_COLS), 0)
    head_sum = jnp.where(
        lax.shift_right_logical(kd_row, DK.bit_length() - 1) + L_MI
        == lax.broadcasted_iota(jnp.int32, (kd, SMALL_COLS), 1), 1.0, 0.0).astype(BF16)
    sc = _dot_mask_exact_rhs(mq * mk, head_sum) * w_intra
    den = w_inter * _dot_mask_exact_rhs(mq * n0_ref[...], head_sum) + sc
    dmax = jnp.where(is_head, jnp.maximum(jnp.abs(den), jnp.exp(-m_t)), 1.0)
    m_ref[...] = m_t
    lane_head = lax.broadcasted_iota(jnp.int32, (SMALL_COLS, kd), 0) - L_MI
    head_spread = jnp.where(
        lane_head == lax.shift_right_logical(
            lax.broadcasted_iota(jnp.int32, (SMALL_COLS, kd), 1), DK.bit_length() - 1),
        1.0, 0.0).astype(BF16)
    n_ref[...] = (_dot_mask_exact_rhs(w_inter, head_spread) * n0_ref[...]
                  + _dot_mask_exact_rhs(w_intra, head_spread) * mk)

    scalars = jnp.concatenate([w_inter, w_intra, sc, dmax], axis=1)
    sc_hi = scalars.astype(BF16)
    sc_r1 = scalars - sc_hi.astype(F32)
    sc_mid = sc_r1.astype(BF16)
    sc_lo = (sc_r1 - sc_mid.astype(F32)).astype(BF16)
    n_scalar = 4

    def scalar_rows(piece):
        pt = piece.astype(F32).T
        return jnp.concatenate(
            [pt[i * SMALL_COLS + L_MI:i * SMALL_COLS + L_MI + 8, :] for i in range(n_scalar)],
            axis=0)

    cols_t = jnp.concatenate(
        [(pm_ref[:, C_GQ:C_GQ + kd] * (DK ** -0.5)).T, pm_ref[:, C_GK:C_GK + kd].T,
         mq.T, mk.T, a_hi.astype(F32).T, a_mid.astype(F32).T, a_lo.astype(F32).T,
         scalar_rows(sc_hi), scalar_rows(sc_mid), scalar_rows(sc_lo)],
        axis=0).astype(BF16)
    o_gq, o_gk, o_mq, o_mk, o_a0, o_a1, o_a2 = (i * kd for i in range(7))
    o_scalar = 7 * kd
    sample_id = lax.broadcasted_iota(jnp.int32, (bs, DV), 0)

    for s in range(bs):
        row = slice(s, s + 1)
        bc = _dot(cols_t, jnp.where(sample_id == s, 1.0, 0.0).astype(BF16))
        for h in range(HEADS):
            col = lambda off: bc[off + h * DK:off + (h + 1) * DK, :]

            def scalar(i):
                r0 = o_scalar + i * 8 + h
                return (bc[r0:r0 + 1, :] + bc[r0 + 8 * n_scalar:r0 + 8 * n_scalar + 1, :]
                        + bc[r0 + 16 * n_scalar:r0 + 16 * n_scalar + 1, :])

            v_row = pm_ref[row, C_GV + h * DV:C_GV + (h + 1) * DV]
            a_col = col(o_a0) + col(o_a1) + col(o_a2)
            s_new = a_col * s0_ref[s, h] + col(o_gk) * v_row
            s_ref[s, h] = s_new
            og_ref[row, h * DV:(h + 1) * DV] = jnp.sum(col(o_gq) * s_new,
                                                       axis=0, keepdims=True)
            v_row = pm_ref[row, C_MV + h * DV:C_MV + (h + 1) * DV]
            c_old = c0_ref[s, h]
            qc = jnp.sum(col(o_mq) * c_old, axis=0, keepdims=True)
            om_ref[row, h * DV:(h + 1) * DV] = (scalar(0) * qc + scalar(2) * v_row) / scalar(3)
            c_ref[s, h] = scalar(0) * c_old + (scalar(1) * col(o_mk)) * v_row

    gnorm = gnorm_ref[...]
    for h in range(HEADS):
        cols = slice(h * DV, (h + 1) * DV)
        y = _rms(og_ref[:, cols], gnorm) * _silu(pm_ref[:, C_GG + h * DV:C_GG + (h + 1) * DV])
        br_out_ref[:, cols] = y.astype(BF16)
        y = (_rms(om_ref[:, cols], mnorm_ref[:, cols])
             * _sigmoid(pm_ref[:, C_MO + h * DV:C_MO + (h + 1) * DV]))
        br_out_ref[:, BRANCH_W + h * DV:BRANCH_W + (h + 1) * DV] = y.astype(BF16)

    x = pm_ref[:, C_RX:C_RX + RG_WIDTH]
    xc = convb_ref[...] + convw_ref[3:4, :] * x
    for jj in range(RG_CONV - 1):
        xc = xc + convw_ref[jj:jj + 1, :] * buf0_ref[:, jj * RG_WIDTH:(jj + 1) * RG_WIDTH]
    buf_ref[:, 0:2 * RG_WIDTH] = buf0_ref[:, RG_WIDTH:3 * RG_WIDTH]
    buf_ref[:, 2 * RG_WIDTH:3 * RG_WIDTH] = x
    a, b = _rg_gates(xc, wr_ref, br_ref, wi_ref, bi_ref, lam_ref)
    hnew = a * h0_ref[...] + b
    h_ref[...] = hnew
    y = hnew * _gelu_tanh(pm_ref[:, C_RG:C_RG + RG_WIDTH])
    br_out_ref[:, 2 * BRANCH_W:3 * BRANCH_W] = y.astype(BF16)


def _mix_sample(pm, ps, st, wts, lw, layer, *, bs):
    n = pm.shape[0]
    const = lambda shape: pl.BlockSpec(shape, lambda i: (0,) * len(shape))
    slab = lambda shape: pl.BlockSpec((None,) + shape, lambda i: (layer,) + (0,) * len(shape))
    blk = lambda shape: pl.BlockSpec(shape, lambda i: (i,) + (0,) * (len(shape) - 1))
    lblk = lambda shape: pl.BlockSpec((None,) + shape,
                                      lambda i: (layer, i) + (0,) * (len(shape) - 1))
    s0, c0, n0, m0, h0, buf0 = st
    shapes = [(bs, HEADS, DK, DV), (bs, HEADS, DK, DV), (bs, HEADS * DK),
              (bs, SMALL_COLS), (bs, RG_WIDTH), (bs, (RG_CONV - 1) * RG_WIDTH)]
    return pl.pallas_call(
        _mix_sample_kernel,
        grid=(n // bs,),
        in_specs=[blk((bs, MIX_COLS)), blk((bs, SMALL_COLS))] + [lblk(sh) for sh in shapes] + [
            slab((SMALL_COLS, HEADS * DK)), const((1, HEADS * DK)), const((1, DV)),
            const((1, SMALL_COLS)), const((1, BRANCH_W)),
            const((RG_CONV, RG_WIDTH)), const((1, RG_WIDTH)),
            slab((RG_WIDTH, RG_WIDTH)), const((1, RG_WIDTH)),
            slab((RG_WIDTH, RG_WIDTH)), const((1, RG_WIDTH)), const((1, RG_WIDTH))],
        out_specs=[blk((bs, N_BRANCH * BRANCH_W))] + [blk(sh) for sh in shapes],
        out_shape=[jax.ShapeDtypeStruct((n, N_BRANCH * BRANCH_W), BF16)]
        + [jax.ShapeDtypeStruct(a.shape[1:], F32) for a in st],
        scratch_shapes=[pltpu.VMEM((bs, BRANCH_W), F32),
                        pltpu.VMEM((bs, BRANCH_W), F32)],
        compiler_params=pltpu.CompilerParams(
            dimension_semantics=("parallel",), vmem_limit_bytes=VMEM_LIMIT),
        name="mix_sample",
    )(pm, ps, s0, c0, n0, m0, h0, buf0,
      wts["wa2"], lw["ba"], lw["gnorm"], lw["bif"], lw["mnorm"], lw["convw"], lw["convb"],
      wts["wr"], lw["br"], wts["wi"], lw["bi"], lw["lam"])


def _merge_kernel(x_ref, br_ref, g0_ref, g1_ref, g2_ref, wb_ref, wo_ref, o_ref):
    merged = None
    for b, g_ref in enumerate((g0_ref, g1_ref, g2_ref)):
        z = _dot(br_ref[:, b * BRANCH_W:(b + 1) * BRANCH_W], wb_ref[b])
        term = _sigmoid(g_ref[...]) * z
        merged = term if merged is None else merged + term
    o_ref[...] = x_ref[...] + _dot(merged.astype(BF16), wo_ref[...])


def _merge(x, br, pm, wb, wo, layer, *, tm):
    n, d = x.shape
    gate_blk0 = MIX_COLS // d
    gate_spec = lambda b: pl.BlockSpec((tm, d), lambda i: (i, gate_blk0 + b))
    return pl.pallas_call(
        _merge_kernel,
        grid=(n // tm,),
        in_specs=[pl.BlockSpec((tm, d), lambda i: (i, 0)),
                  pl.BlockSpec((tm, N_BRANCH * BRANCH_W), lambda i: (i, 0)),
                  gate_spec(0), gate_spec(1), gate_spec(2),
                  pl.BlockSpec((None, N_BRANCH, BRANCH_W, d), lambda i: (layer, 0, 0, 0)),
                  pl.BlockSpec((None, d, d), lambda i: (layer, 0, 0))],
        out_specs=pl.BlockSpec((tm, d), lambda i: (i, 0)),
        out_shape=jax.ShapeDtypeStruct((n, d), F32),
        compiler_params=pltpu.CompilerParams(
            dimension_semantics=("parallel",), vmem_limit_bytes=VMEM_LIMIT),
        name="merge",
    )(x, br, pm, pm, pm, wb, wo)


def _xattn_ffn_prompt_kernel(x_ref, gx_ref, wq_ref, wo_ref, k_ref, v_ref,
                             gf_ref, wg_ref, wu_ref, wd_ref, gfin_ref,
                             o_ref, kb_ref, vb_ref, hid_ref, *, final_norm):
    @pl.when(pl.program_id(1) == 0)
    def _():
        kb_ref[...] = k_ref[...].astype(BF16)
        vb_ref[...] = v_ref[...].astype(BF16)

    x = x_ref[...]
    q = _dot(_rms(x, gx_ref[...]).astype(BF16), wq_ref[...]).astype(BF16)
    heads = []
    for h in range(XA_HEADS):
        cols = slice(h * XA_HEAD_DIM, (h + 1) * XA_HEAD_DIM)
        s = _dot_nt(q[:, cols], kb_ref[:, cols]) * (XA_HEAD_DIM ** -0.5)
        p = jnp.exp(s - jnp.max(s, axis=-1, keepdims=True))
        p = p / jnp.sum(p, axis=-1, keepdims=True)
        heads.append(_dot(p.astype(BF16), vb_ref[:, cols]))
    o = jnp.concatenate(heads, axis=1).astype(BF16)
    x = x + _dot(o, wo_ref[...])
    o_ref[...] = _ffn_block(x, gf_ref, wg_ref, wu_ref, wd_ref, gfin_ref, hid_ref, final_norm)


def _xattn_ffn_prompt(x, gx, wq, wo, mem_k, mem_v, gf, wg, wu, wd, gfin, layer, *,
                      nb, t, tm, final_norm):
    d = x.shape[1]
    f = wg.shape[2]
    nt = t // tm
    const = lambda shape: pl.BlockSpec(shape, lambda b, j: (0,) * len(shape),
                                       pipeline_mode=pl.Buffered(1))
    slab = lambda shape: pl.BlockSpec((None,) + shape, lambda b, j: (layer,) + (0,) * len(shape),
                                      pipeline_mode=pl.Buffered(1))
    return pl.pallas_call(
        functools.partial(_xattn_ffn_prompt_kernel, final_norm=final_norm),
        grid=(nb, nt),
        in_specs=[pl.BlockSpec((tm, d), lambda b, j: (b * nt + j, 0)),
                  const((1, d)), slab((d, d)), slab((d, d)),
                  pl.BlockSpec((None, N_MEM, d), lambda b, j: (b, 0, 0)),
                  pl.BlockSpec((None, N_MEM, d), lambda b, j: (b, 0, 0)),
                  const((1, d)), slab((d, f)), slab((d, f)), slab((f, d)), const((1, d))],
        out_specs=pl.BlockSpec((tm, d), lambda b, j: (b * nt + j, 0)),
        out_shape=jax.ShapeDtypeStruct(x.shape, F32),
        scratch_shapes=[pltpu.VMEM((N_MEM, d), BF16), pltpu.VMEM((N_MEM, d), BF16),
                        pltpu.VMEM((tm, f), BF16)],
        compiler_params=pltpu.CompilerParams(
            dimension_semantics=("parallel", "arbitrary"), vmem_limit_bytes=VMEM_LIMIT),
        name="xattn_ffn_prompt",
    )(x, gx.reshape(1, d), wq, wo, mem_k, mem_v, gf.reshape(1, d), wg, wu, wd, gfin.reshape(1, d))


def _xattn_sample_kernel(q_ref, k_ref, v_ref, o_ref):
    bs = q_ref.shape[0]
    nrow = N_MEM * XA_HEADS
    row_head = lax.broadcasted_iota(jnp.int32, (XA_HEADS, nrow), 1) & (XA_HEADS - 1)
    own_head = row_head == lax.broadcasted_iota(jnp.int32, (XA_HEADS, nrow), 0)
    for s in range(bs):
        q = q_ref[s].astype(BF16)
        k2 = k_ref[s].reshape(nrow, XA_HEAD_DIM).astype(BF16)
        v2 = v_ref[s].reshape(nrow, XA_HEAD_DIM).astype(BF16)
        sc = jnp.where(own_head, _dot_nt(q, k2) * (XA_HEAD_DIM ** -0.5), -jnp.inf)
        p = jnp.exp(sc - jnp.max(sc, axis=1, keepdims=True))
        p = p / jnp.sum(p, axis=1, keepdims=True)
        o_ref[s] = _dot(p.astype(BF16), v2)


def _xattn_sample(q, cache_k, cache_v, layer, *, bs):
    n = q.shape[0]
    blk = (None, bs, N_MEM, XA_HEADS, XA_HEAD_DIM)
    return pl.pallas_call(
        _xattn_sample_kernel,
        grid=(n // bs,),
        in_specs=[pl.BlockSpec((bs, XA_HEADS, XA_HEAD_DIM), lambda i: (i, 0, 0)),
                  pl.BlockSpec(blk, lambda i: (layer, i, 0, 0, 0)),
                  pl.BlockSpec(blk, lambda i: (layer, i, 0, 0, 0))],
        out_specs=pl.BlockSpec((bs, XA_HEADS, XA_HEAD_DIM), lambda i: (i, 0, 0)),
        out_shape=jax.ShapeDtypeStruct((n, XA_HEADS, XA_HEAD_DIM), F32),
        compiler_params=pltpu.CompilerParams(
            dimension_semantics=("parallel",), vmem_limit_bytes=VMEM_LIMIT),
        name="xattn_sample",
    )(q, cache_k, cache_v)


def _matmul_residual_kernel(x_ref, a_ref, w_ref, o_ref):
    o_ref[...] = x_ref[...] + _dot(a_ref[...].astype(BF16), w_ref[...])


def _matmul_residual(x, a, w, layer):
    n, d = x.shape
    return pl.pallas_call(
        _matmul_residual_kernel,
        grid=(1,),
        in_specs=[pl.BlockSpec((n, d), lambda i: (0, 0)),
                  pl.BlockSpec((n, d), lambda i: (0, 0)),
                  pl.BlockSpec((None, d, d), lambda i: (layer, 0, 0))],
        out_specs=pl.BlockSpec((n, d), lambda i: (0, 0)),
        out_shape=jax.ShapeDtypeStruct((n, d), F32),
        compiler_params=pltpu.CompilerParams(vmem_limit_bytes=VMEM_LIMIT),
        name="matmul_residual",
    )(x, a, w)


def _ffn_block(x, g_ref, wg_ref, wu_ref, wd_ref, gfin_ref, hid_ref, final_norm):
    u = _rms(x, g_ref[...]).astype(BF16)
    for jt in range(D_FF // FFN_TILE):
        cols = slice(jt * FFN_TILE, (jt + 1) * FFN_TILE)
        hid = _silu(_dot(u, wg_ref[:, cols])) * _dot(u, wu_ref[:, cols])
        hid_ref[:, cols] = hid.astype(BF16)
    y = x + _dot(hid_ref[...], wd_ref[...])
    if final_norm:
        y = _rms(y, gfin_ref[...])
    return y


def _ffn_kernel(x_ref, g_ref, wg_ref, wu_ref, wd_ref, gf_ref, o_ref, hid_ref, *, final_norm):
    o_ref[...] = _ffn_block(x_ref[...], g_ref, wg_ref, wu_ref, wd_ref, gf_ref, hid_ref, final_norm)


def _ffn(x, g, wg, wu, wd, gf, layer, *, tm, final_norm):
    n, d = x.shape
    f = wg.shape[2]
    const = lambda shape: pl.BlockSpec(shape, lambda i: (0,) * len(shape),
                                       pipeline_mode=pl.Buffered(1))
    slab = lambda shape: pl.BlockSpec((None,) + shape, lambda i: (layer,) + (0,) * len(shape),
                                      pipeline_mode=pl.Buffered(1))
    return pl.pallas_call(
        functools.partial(_ffn_kernel, final_norm=final_norm),
        grid=(n // tm,),
        in_specs=[pl.BlockSpec((tm, d), lambda i: (i, 0)),
                  const((1, d)), slab((d, f)), slab((d, f)), slab((f, d)), const((1, d))],
        out_specs=pl.BlockSpec((tm, d), lambda i: (i, 0)),
        out_shape=jax.ShapeDtypeStruct((n, d), F32),
        scratch_shapes=[pltpu.VMEM((tm, f), BF16)],
        compiler_params=pltpu.CompilerParams(
            dimension_semantics=("parallel",), vmem_limit_bytes=VMEM_LIMIT),
        name="ffn",
    )(x, g.reshape(1, d), wg, wu, wd, gf.reshape(1, d))


def _split_w_in(w_in):
    widths = (HEADS * DK, HEADS * DK, HEADS * DV, HEADS * DV, GLA_RANK,
              HEADS * DK, HEADS * DK, HEADS * DV, HEADS * DV, HEADS, HEADS,
              RG_WIDTH, RG_WIDTH, GATE_COLS)
    parts, acc = [], 0
    for w in widths:
        parts.append(w_in[..., acc:acc + w])
        acc += w
    return parts


def _block_diag(w):
    eye = jnp.eye(RG_BLOCKS, dtype=w.dtype)
    return jnp.einsum("lnde,nm->lndme", w, eye).reshape(DEPTH, RG_WIDTH, RG_WIDTH)


def _matmul_weights(p):
    (g_q, g_k, g_v, g_g, g_a, m_q, m_k, m_v, m_o, m_i, m_f, r_x, r_g, gates) = _split_w_in(p["w_in"])
    d = D_MODEL
    w_main = jnp.concatenate([g_q, g_k, g_v, g_g, m_q, m_k, m_v, m_o, r_x, r_g, gates], axis=2)
    w_small = jnp.concatenate(
        [g_a, m_i, m_f, jnp.zeros((DEPTH, d, SMALL_COLS - GLA_RANK - 2 * HEADS), F32)], axis=2)
    wa2 = jnp.concatenate(
        [p["gla_w_a2"], jnp.zeros((DEPTH, SMALL_COLS - GLA_RANK, HEADS * DK), F32)], axis=1)
    return {
        "w_main": w_main.astype(BF16), "w_small": w_small.astype(BF16), "wa2": wa2.astype(BF16),
        "wr": _block_diag(p["rg_w_r"]).astype(BF16), "wi": _block_diag(p["rg_w_i"]).astype(BF16),
        "wb": p["w_branch"].astype(BF16), "wo": p["w_out"].astype(BF16),
        "xa_wq": p["xa_wq"].astype(BF16), "xa_wo": p["xa_wo"].astype(BF16),
        "xa_wkv": jnp.concatenate([p["xa_wk"], p["xa_wv"]], axis=2).astype(BF16),
        "wg": p["ffn_w_gate"].astype(BF16), "wu": p["ffn_w_up"].astype(BF16),
        "wd": p["ffn_w_down"].astype(BF16),
    }


def _vector_params(l, p):
    bif = jnp.concatenate(
        [jnp.zeros((L_MI,), F32), p["mlstm_b_i"][l], p["mlstm_b_f"][l],
         jnp.zeros((SMALL_COLS - L_MF - HEADS,), F32)]).reshape(1, SMALL_COLS)
    return {
        "ba": p["gla_b_a"][l].reshape(1, -1), "gnorm": p["gla_norm"][l].reshape(1, DV),
        "bif": bif, "mnorm": p["mlstm_norm"][l].reshape(1, BRANCH_W),
        "convw": p["rg_conv_w"][l], "convb": p["rg_conv_b"][l].reshape(1, -1),
        "br": p["rg_b_r"][l].reshape(1, -1), "bi": p["rg_b_i"][l].reshape(1, -1),
        "lam": p["rg_lambda"][l].reshape(1, -1),
    }


def kernel(x_prompt, x_sample, mem_prompt, cache_mem_k, cache_mem_v, state_gla, state_mlstm_c, state_mlstm_n, state_mlstm_m, state_rglru_h, state_rglru_conv, norm_mix, w_in, gla_w_a2, gla_b_a, gla_norm, mlstm_b_i, mlstm_b_f, mlstm_norm, rg_conv_w, rg_conv_b, rg_w_r, rg_b_r, rg_w_i, rg_b_i, rg_lambda, w_branch, w_out, norm_xa, norm_mem, xa_wq, xa_wk, xa_wv, xa_wo, norm_ffn, ffn_w_gate, ffn_w_up, ffn_w_down, norm_final):
    p = {"w_in": w_in, "gla_w_a2": gla_w_a2, "gla_b_a": gla_b_a, "gla_norm": gla_norm,
         "mlstm_b_i": mlstm_b_i, "mlstm_b_f": mlstm_b_f, "mlstm_norm": mlstm_norm,
         "rg_conv_w": rg_conv_w, "rg_conv_b": rg_conv_b, "rg_w_r": rg_w_r, "rg_b_r": rg_b_r,
         "rg_w_i": rg_w_i, "rg_b_i": rg_b_i, "rg_lambda": rg_lambda, "w_branch": w_branch,
         "w_out": w_out, "xa_wq": xa_wq, "xa_wk": xa_wk, "xa_wv": xa_wv, "xa_wo": xa_wo,
         "ffn_w_gate": ffn_w_gate, "ffn_w_up": ffn_w_up, "ffn_w_down": ffn_w_down}
    nb, t, d = x_prompt.shape
    ns = x_sample.shape[0]
    xp = x_prompt.reshape(nb * t, d)
    xs = x_sample.reshape(ns, d)
    mem = mem_prompt.reshape(nb * N_MEM, d)
    wts = _matmul_weights(p)

    st_sample = (state_gla, state_mlstm_c, state_mlstm_n.reshape(DEPTH, ns, HEADS * DK),
                 jnp.pad(state_mlstm_m, ((0, 0), (0, 0), (L_MI, SMALL_COLS - L_MI - HEADS))),
                 state_rglru_h,
                 state_rglru_conv.reshape(DEPTH, ns, (RG_CONV - 1) * RG_WIDTH))

    new_p = [[] for _ in range(8)]
    new_s = [[] for _ in range(6)]
    for l in range(DEPTH):
        lw = _vector_params(l, p)
        last = l == DEPTH - 1

        k_p, v_p = _norm_matmul(mem, norm_mem[l], wts["xa_wkv"], l, tm=PROMPT_ROWS, tn=d)
        k_p = k_p.reshape(nb, N_MEM, d)
        v_p = v_p.reshape(nb, N_MEM, d)

        xp, g_s, c_s, n_s, m_s, h_s, buf_s = _mix_prompt(xp, norm_mix[l], wts, lw, l,
                                                         nb=nb, t=t, tt=MIX_TIME_BLOCK)
        xp = _xattn_ffn_prompt(xp, norm_xa[l], wts["xa_wq"], wts["xa_wo"], k_p, v_p,
                               norm_ffn[l], wts["wg"], wts["wu"], wts["wd"], norm_final, l,
                               nb=nb, t=t, tm=PROMPT_ROWS, final_norm=last)
        for lst, val in zip(new_p, (k_p.reshape(nb, N_MEM, XA_HEADS, XA_HEAD_DIM),
                                    v_p.reshape(nb, N_MEM, XA_HEADS, XA_HEAD_DIM),
                                    g_s, c_s, n_s.reshape(nb, HEADS, DK), m_s.reshape(nb, HEADS),
                                    h_s.reshape(nb, RG_WIDTH), buf_s)):
            lst.append(val)

        pm, ps = _in_proj(xs, norm_mix[l], wts["w_main"], wts["w_small"], l,
                          tm=ns, tn=IN_PROJ_COLS)
        br, g_s, c_s, n_s, m_s, h_s, buf_s = _mix_sample(pm, ps, st_sample, wts, lw, l,
                                                         bs=SAMPLE_BLOCK)
        xs = _merge(xs, br, pm, wts["wb"], wts["wo"], l, tm=ns)
        (q,) = _norm_matmul(xs, norm_xa[l], wts["xa_wq"], l, tm=ns, tn=d)
        att = _xattn_sample(q.reshape(ns, XA_HEADS, XA_HEAD_DIM), cache_mem_k, cache_mem_v, l,
                            bs=SAMPLE_BLOCK)
        xs = _matmul_residual(xs, att.reshape(ns, d), wts["xa_wo"], l)
        xs = _ffn(xs, norm_ffn[l], wts["wg"], wts["wu"], wts["wd"], norm_final, l,
                  tm=ns, final_norm=last)
        for lst, val in zip(new_s, (g_s, c_s, n_s.reshape(ns, HEADS, DK),
                                    m_s[:, L_MI:L_MI + HEADS], h_s,
                                    buf_s.reshape(ns, RG_CONV - 1, RG_WIDTH))):
            lst.append(val)

    y_prompt = xp.reshape(nb, t, d)
    y_sample = xs.reshape(ns, 1, d)
    outs_p = [jnp.stack(v, axis=0) for v in new_p]
    outs_s = [jnp.stack(v, axis=0) for v in new_s]
    return (y_prompt, y_sample, *outs_p, *outs_s)
```

```python
import functools
import math

import jax
import jax.numpy as jnp
from jax import lax
from jax.experimental import pallas as pl
from jax.experimental.pallas import tpu as pltpu

F32 = jnp.float32
BF16 = jnp.bfloat16

D_MODEL = 1024
DEPTH = 2
EPS = 1e-6
N_MEM = 256
XA_HEADS = 4
XA_HEAD_DIM = D_MODEL // XA_HEADS
N_BRANCH = 3
BRANCH_W = D_MODEL // 2
HEADS = 4
DK = 64
DV = 128
GLA_RANK = 16
GLA_GATE_NORM = 16.0
GLA_CHUNK = 32
RG_WIDTH = BRANCH_W
RG_BLOCKS = 8
RG_CONV = 4
RG_C = 8.0
D_FF = 2816

C_GQ, C_GK, C_GV, C_GG = 0, 256, 512, 1024
C_MQ, C_MK, C_MV, C_MO = 1536, 1792, 2048, 2560
C_RX, C_RG = 3072, 3584
MIX_COLS = 4096
GATE_COLS = N_BRANCH * D_MODEL
MAIN_COLS = MIX_COLS + GATE_COLS
SMALL_COLS = 128
L_MI = 16
L_MF = 20

MXU_WIDTH = 256
PROJ_TILE = 2 * MXU_WIDTH
MERGE_TILE = 2 * MXU_WIDTH
FFN_TILE = MXU_WIDTH
MLSTM_BANDS = 2
MIX_TIME_BLOCK = 256
PROMPT_ROWS = 512
SAMPLE_BLOCK = 8
IN_PROJ_COLS = 1024
VMEM_LIMIT = 48 * 1024 * 1024
assert all(c % PROJ_TILE == 0 for c in (C_GG, C_MQ, C_MO, C_RX, C_RG, MIX_COLS, MAIN_COLS))
assert D_MODEL % MERGE_TILE == 0 and D_FF % FFN_TILE == 0


def _softplus(x):
    return jnp.maximum(x, 0.0) + jnp.log1p(jnp.exp(-jnp.abs(x)))


def _log_sigmoid(x):
    return -_softplus(-x)


def _sigmoid(x):
    return 1.0 / (1.0 + jnp.exp(-x))


def _silu(x):
    return x * _sigmoid(x)


def _gelu_tanh(x):
    c = math.sqrt(2.0 / math.pi)
    return x * (0.5 * (1.0 + jnp.tanh(c * (x + 0.044715 * (x * x * x)))))


def _neg_expm1(x):
    return -jnp.tanh(0.5 * x) * (jnp.exp(x) + 1.0)


def _rms(x, g):
    ms = jnp.mean(x * x, axis=-1, keepdims=True)
    return x * lax.rsqrt(ms + EPS) * g


def _dot(a, b):
    return jnp.dot(a, b, preferred_element_type=F32)


def _dot_nt(a, b):
    return lax.dot_general(a, b, (((1,), (1,)), ((), ())), preferred_element_type=F32)


def _dot_tn(a, b):
    return lax.dot_general(a, b, (((0,), (0,)), ((), ())), preferred_element_type=F32)


def _dot_mask_exact(mask, x):
    hi = x.astype(BF16)
    rest = x - hi.astype(F32)
    mid = rest.astype(BF16)
    lo = (rest - mid.astype(F32)).astype(BF16)
    return _dot(mask, hi) + _dot(mask, mid) + _dot(mask, lo)


def _dot_mask_exact_rhs(x, mask):
    hi = x.astype(BF16)
    rest = x - hi.astype(F32)
    mid = rest.astype(BF16)
    lo = (rest - mid.astype(F32)).astype(BF16)
    return _dot(hi, mask) + _dot(mid, mask) + _dot(lo, mask)


def _chunk_masks(n, chunk):
    shift = chunk.bit_length() - 1
    r = lax.broadcasted_iota(jnp.int32, (n, n), 0)
    c = lax.broadcasted_iota(jnp.int32, (n, n), 1)
    same = lax.shift_right_logical(r, shift) == lax.shift_right_logical(c, shift)
    tril = jnp.where(same & (c <= r), 1.0, 0.0).astype(F32)
    full = jnp.where(same, 1.0, 0.0).astype(F32)
    return tril, full


def _norm_matmul_kernel(x_ref, g_ref, w_ref, *refs):
    out_refs, u_ref = refs[:-1], refs[-1]
    j = pl.program_id(1)

    @pl.when(j == 0)
    def _():
        u_ref[...] = _rms(x_ref[...], g_ref[...]).astype(BF16)

    res = _dot(u_ref[...], w_ref[...])
    for idx, o_ref in enumerate(out_refs):
        @pl.when(j == idx)
        def _(o_ref=o_ref):
            o_ref[...] = res


def _norm_matmul(x, g, w, layer, *, tm, tn):
    n, d = x.shape
    nout = w.shape[2]
    n_out = nout // tn
    return pl.pallas_call(
        _norm_matmul_kernel,
        grid=(n // tm, n_out),
        in_specs=[pl.BlockSpec((tm, d), lambda i, j: (i, 0)),
                  pl.BlockSpec((1, d), lambda i, j: (0, 0)),
                  pl.BlockSpec((None, d, tn), lambda i, j: (layer, 0, j))],
        out_specs=[pl.BlockSpec((tm, tn), lambda i, j: (i, 0))] * n_out,
        out_shape=[jax.ShapeDtypeStruct((n, tn), F32)] * n_out,
        scratch_shapes=[pltpu.VMEM((tm, d), BF16)],
        compiler_params=pltpu.CompilerParams(
            dimension_semantics=("parallel", "arbitrary"), vmem_limit_bytes=VMEM_LIMIT),
        name="norm_matmul",
    )(x, g.reshape(1, d), w)


def _in_proj_kernel(x_ref, g_ref, w_ref, ws_ref, o_ref, os_ref, u_ref):
    @pl.when(pl.program_id(1) == 0)
    def _():
        u = _rms(x_ref[...], g_ref[...]).astype(BF16)
        u_ref[...] = u
        os_ref[...] = _dot(u, ws_ref[...])

    o_ref[...] = _dot(u_ref[...], w_ref[...])


def _in_proj(x, g, w_main, w_small, layer, *, tm, tn):
    n, d = x.shape
    return pl.pallas_call(
        _in_proj_kernel,
        grid=(n // tm, MAIN_COLS // tn),
        in_specs=[pl.BlockSpec((tm, d), lambda i, j: (i, 0)),
                  pl.BlockSpec((1, d), lambda i, j: (0, 0)),
                  pl.BlockSpec((None, d, tn), lambda i, j: (layer, 0, j)),
                  pl.BlockSpec((None, d, SMALL_COLS), lambda i, j: (layer, 0, 0))],
        out_specs=[pl.BlockSpec((tm, tn), lambda i, j: (i, j)),
                   pl.BlockSpec((tm, SMALL_COLS), lambda i, j: (i, 0))],
        out_shape=[jax.ShapeDtypeStruct((n, MAIN_COLS), F32),
                   jax.ShapeDtypeStruct((n, SMALL_COLS), F32)],
        scratch_shapes=[pltpu.VMEM((tm, d), BF16)],
        compiler_params=pltpu.CompilerParams(
            dimension_semantics=("parallel", "arbitrary"), vmem_limit_bytes=VMEM_LIMIT),
        name="in_proj",
    )(x, g.reshape(1, d), w_main, w_small)


def _rg_gates(xc, wr_ref, br_ref, wi_ref, bi_ref, lam_ref):
    xcb = xc.astype(BF16)
    r = _sigmoid(_dot(xcb, wr_ref[...]) + br_ref[...])
    i = _sigmoid(_dot(xcb, wi_ref[...]) + bi_ref[...])
    log_a = (-RG_C * r) * _softplus(-lam_ref[...])
    a = jnp.exp(log_a)
    b = jnp.sqrt(_neg_expm1(2.0 * log_a)) * (i * xc)
    return a, b


def _projection_activation(col):
    if C_GG <= col < C_GG + HEADS * DV:
        return _silu
    if C_MO <= col < C_MO + HEADS * DV:
        return _sigmoid
    if C_RG <= col < C_RG + RG_WIDTH:
        return _gelu_tanh
    if col >= MIX_COLS:
        return _sigmoid
    return None


def _project_stage(x_ref, g_ref, wmain_ref, wsmall_ref, pm_ref, ps_ref, xs_ref, sg_ref):
    x = x_ref[...]
    xs_ref[...] = x
    u = _rms(x, g_ref[...]).astype(BF16)
    ps_ref[...] = _dot(u, wsmall_ref[...])
    for cb in range(MAIN_COLS // PROJ_TILE):
        c0 = cb * PROJ_TILE
        y = _dot(u, wmain_ref[:, c0:c0 + PROJ_TILE])
        act = _projection_activation(c0)
        if act is not None:
            y = act(y)
        if c0 >= MIX_COLS:
            sg_ref[:, c0 - MIX_COLS:c0 - MIX_COLS + PROJ_TILE] = y
        else:
            pm_ref[:, c0:c0 + PROJ_TILE] = y


def _mixer_stage(pm_ref, ps_ref,
                 wa2_ref, ba_ref, gnorm_ref, bif_ref, mnorm_ref,
                 convw_ref, convb_ref, wr_ref, br_ref, wi_ref, bi_ref, lam_ref,
                 s_ref, c_ref, n_ref, m_ref, h_ref, buf_ref,
                 br_out_ref, og_ref, xpad_ref):
    tt = pm_ref.shape[0]

    x = pm_ref[:, C_RX:C_RX + RG_WIDTH]
    xpad_ref[pl.ds(8, tt), :] = x
    xc = (convb_ref[...] + convw_ref[3:4, :] * x
          + convw_ref[2:3, :] * xpad_ref[pl.ds(7, tt), :]
          + convw_ref[1:2, :] * xpad_ref[pl.ds(6, tt), :]
          + convw_ref[0:1, :] * xpad_ref[pl.ds(5, tt), :])
    buf_ref[...] = xpad_ref[pl.ds(tt + 8 - (RG_CONV - 1), RG_CONV - 1), :]
    xpad_ref[pl.ds(0, 8), :] = xpad_ref[pl.ds(tt, 8), :]
    a, b = _rg_gates(xc, wr_ref, br_ref, wi_ref, bi_ref, lam_ref)
    ridx = lax.broadcasted_iota(jnp.int32, (tt, RG_WIDTH), 0)
    sh = 1
    while sh < tt:
        keep = ridx >= sh
        a_sh = jnp.where(keep, pltpu.roll(a, sh, 0), 1.0)
        b_sh = jnp.where(keep, pltpu.roll(b, sh, 0), 0.0)
        b = a * b_sh + b
        a = a * a_sh
        sh *= 2
    hseq = a * h_ref[...] + b
    h_ref[...] = hseq[tt - 1:tt, :]
    y = hseq * pm_ref[:, C_RG:C_RG + RG_WIDTH]
    br_out_ref[:, 2 * BRANCH_W:3 * BRANCH_W] = y.astype(BF16)

    ps = ps_ref[...]
    a_pre = _dot(ps.astype(BF16), wa2_ref[...]) + ba_ref[...]
    log_a = _log_sigmoid(a_pre) * (1.0 / GLA_GATE_NORM)
    tril32, full32 = _chunk_masks(tt, GLA_CHUNK)
    sums = _dot_mask_exact(jnp.concatenate([tril32, full32], axis=0).astype(BF16), log_a)
    b_cum = sums[:tt]
    b_end = sums[tt:]
    q = pm_ref[:, C_GQ:C_GQ + HEADS * DK] * (DK ** -0.5)
    k = pm_ref[:, C_GK:C_GK + HEADS * DK]
    qi_f = q * jnp.exp(b_cum)
    qi_all = qi_f.astype(BF16)
    ki_all = (k * jnp.exp(-b_cum)).astype(BF16)
    ke_t = (k * jnp.exp(b_end - b_cum)).T
    de_t = jnp.exp(b_end).T
    in_chunk = tril32 > 0.0
    n_chunks = tt // GLA_CHUNK
    cshift = GLA_CHUNK.bit_length() - 1
    chunk_of_lane = lax.shift_right_logical(lax.broadcasted_iota(jnp.int32, (DK, tt), 1), cshift)
    chunk_of_row = lax.shift_right_logical(lax.broadcasted_iota(jnp.int32, (tt, 2 * DK), 0), cshift)
    upper_half = (lax.broadcasted_iota(jnp.int32, (tt, 2 * DK), 1) >= DK).astype(jnp.int32)
    for h in range(HEADS):
        kcols = slice(h * DK, (h + 1) * DK)
        v = pm_ref[:, C_GV + h * DV:C_GV + (h + 1) * DV].astype(BF16)
        att = jnp.where(in_chunk, _dot_nt(qi_all[:, kcols], ki_all[:, kcols]), 0.0)
        o_intra = _dot(att.astype(BF16), v)
        ke_h = ke_t[kcols, :]
        ke_blk = jnp.concatenate(
            [jnp.where(chunk_of_lane == c, ke_h, 0.0) for c in range(n_chunks)], axis=0)
        upd = _dot(ke_blk.astype(BF16), v)
        s_cur = s_ref[h]
        s_start = []
        for c in range(n_chunks):
            s_start.append(s_cur)
            de_col = de_t[kcols, c * GLA_CHUNK:c * GLA_CHUNK + 1]
            s_cur = de_col * s_cur + upd[c * DK:(c + 1) * DK, :]
        s_ref[h] = s_cur
        qi_h = qi_f[:, kcols]
        qi_dup = jnp.concatenate([qi_h, qi_h], axis=1)
        q_blk = jnp.concatenate(
            [jnp.where(chunk_of_row == 2 * jj + upper_half, qi_dup, 0.0)
             for jj in range(n_chunks // 2)], axis=1)
        o_inter = _dot(q_blk.astype(BF16), jnp.concatenate(s_start, axis=0).astype(BF16))
        og_ref[:, h * DV:(h + 1) * DV] = o_intra + o_inter

    gnorm = gnorm_ref[...]
    for h in range(HEADS):
        cols = slice(h * DV, (h + 1) * DV)
        y = _rms(og_ref[:, cols], gnorm) * pm_ref[:, C_GG + h * DV:C_GG + (h + 1) * DV]
        br_out_ref[:, cols] = y.astype(BF16)

    pre = ps + bif_ref[...]
    rr = lax.broadcasted_iota(jnp.int32, (tt, tt), 0)
    cc = lax.broadcasted_iota(jnp.int32, (tt, tt), 1)
    causal = cc <= rr
    fcum = _dot_mask_exact(jnp.where(causal, 1.0, 0.0).astype(BF16), _log_sigmoid(pre))
    pre_t = pre.T
    fcum_t = fcum.T
    q_all = pm_ref[:, C_MQ:C_MQ + HEADS * DK] * (DK ** -0.5)
    k_all = pm_ref[:, C_MK:C_MK + HEADS * DK]
    for h in range(HEADS):
        kcols = slice(h * DK, (h + 1) * DK)
        fc_col = fcum[:, L_MF + h:L_MF + h + 1]
        ic_col = pre[:, L_MI + h:L_MI + h + 1]
        fc_row = fcum_t[L_MF + h:L_MF + h + 1, :]
        ic_row = pre_t[L_MI + h:L_MI + h + 1, :]
        m_prev = m_ref[:, h:h + 1]
        qf = q_all[:, kcols]
        kf = k_all[:, kcols]
        qc = qf.astype(BF16)
        kb = kf.astype(BF16)
        vc = pm_ref[:, C_MV + h * DV:C_MV + (h + 1) * DV].astype(BF16)
        c_old = c_ref[h]
        n_old = n_ref[h]
        gate_row = ic_row - fc_row
        band = tt // MLSTM_BANDS
        for r in range(MLSTM_BANDS):
            rows = slice(r * band, (r + 1) * band)
            ncol = (r + 1) * band
            d = jnp.where(causal[rows, :ncol], fc_col[rows] + gate_row[:, :ncol], -jnp.inf)
            inter = fc_col[rows] + m_prev
            m_t = jnp.maximum(inter, jnp.max(d, axis=1, keepdims=True))
            w_inter = jnp.exp(inter - m_t)
            s = _dot_nt(qc[rows], kb[:ncol]) * jnp.exp(d - m_t)
            num = w_inter * _dot(qc[rows], c_old.astype(BF16)) + _dot(s.astype(BF16), vc[:ncol])
            den = (w_inter * jnp.sum(qf[rows] * n_old, axis=1, keepdims=True)
                   + jnp.sum(s, axis=1, keepdims=True))
            og_ref[rows, h * DV:(h + 1) * DV] = num / jnp.maximum(jnp.abs(den), jnp.exp(-m_t))
        f_end = fc_col[tt - 1:tt, :]
        m_end = m_t[band - 1:band, :]
        w_k = jnp.exp(f_end - fc_col + ic_col - m_end)
        w_c = jnp.exp(f_end + m_prev - m_end)
        kw = w_k * kf
        c_ref[h] = w_c * c_old + _dot_tn(kw.astype(BF16), vc)
        n_ref[h] = w_c * n_old + jnp.sum(kw, axis=0, keepdims=True)
        m_ref[:, h:h + 1] = m_end

    for h in range(HEADS):
        cols = slice(h * DV, (h + 1) * DV)
        y = (_rms(og_ref[:, cols], mnorm_ref[:, cols])
             * pm_ref[:, C_MO + h * DV:C_MO + (h + 1) * DV])
        br_out_ref[:, BRANCH_W + h * DV:BRANCH_W + (h + 1) * DV] = y.astype(BF16)


def _merge_stage(br_ref, sg_ref, xr_ref, wb_ref, wo_ref, xo_ref, mg_ref):
    n_tiles = D_MODEL // MERGE_TILE
    for jt in range(n_tiles):
        cols = slice(jt * MERGE_TILE, (jt + 1) * MERGE_TILE)
        merged = None
        for b in range(N_BRANCH):
            z = _dot(br_ref[:, b * BRANCH_W:(b + 1) * BRANCH_W], wb_ref[b, :, cols])
            term = sg_ref[:, b * D_MODEL + jt * MERGE_TILE:b * D_MODEL + (jt + 1) * MERGE_TILE] * z
            merged = term if merged is None else merged + term
        mg_ref[:, cols] = merged.astype(BF16)
    for jt in range(n_tiles):
        cols = slice(jt * MERGE_TILE, (jt + 1) * MERGE_TILE)
        xo_ref[:, cols] = xr_ref[:, cols] + _dot(mg_ref[...], wo_ref[:, cols])


N_MIXER_WEIGHTS = 12


def _mix_prompt_kernel(x_ref, g_ref, wmain_ref, wsmall_ref, *refs):
    mixer_w = refs[:N_MIXER_WEIGHTS]
    wb_ref, wo_ref = refs[N_MIXER_WEIGHTS:N_MIXER_WEIGHTS + 2]
    xo_ref, s_ref, c_ref, n_ref, m_ref, h_ref, buf_ref = refs[N_MIXER_WEIGHTS + 2:N_MIXER_WEIGHTS + 9]
    pm_ref, ps_ref, xs_ref, br_ref, sg_ref, mg_ref, og_ref, xpad_ref = refs[N_MIXER_WEIGHTS + 9:]

    @pl.when(pl.program_id(1) == 0)
    def _():
        for ref in (s_ref, c_ref, n_ref, m_ref, h_ref):
            ref[...] = jnp.zeros_like(ref)
        xpad_ref[pl.ds(0, 8), :] = jnp.zeros((8, RG_WIDTH), F32)

    _project_stage(x_ref, g_ref, wmain_ref, wsmall_ref, pm_ref, ps_ref, xs_ref, sg_ref)
    _mixer_stage(pm_ref, ps_ref, *mixer_w, s_ref, c_ref, n_ref, m_ref, h_ref, buf_ref,
                 br_ref, og_ref, xpad_ref)
    _merge_stage(br_ref, sg_ref, xs_ref, wb_ref, wo_ref, xo_ref, mg_ref)


def _mix_prompt(x, g, wts, lw, layer, *, nb, t, tt):
    d = x.shape[1]
    const = lambda shape: pl.BlockSpec(shape, lambda b, j: (0,) * len(shape),
                                       pipeline_mode=pl.Buffered(1))
    slab = lambda shape: pl.BlockSpec((None,) + shape, lambda b, j: (layer,) + (0,) * len(shape),
                                      pipeline_mode=pl.Buffered(1))
    nt = t // tt
    return pl.pallas_call(
        _mix_prompt_kernel,
        grid=(nb, nt),
        in_specs=[pl.BlockSpec((tt, d), lambda b, j: (b * nt + j, 0)),
                  const((1, d)), slab((d, MAIN_COLS)), slab((d, SMALL_COLS)),
                  slab((SMALL_COLS, HEADS * DK)), const((1, HEADS * DK)), const((1, DV)),
                  const((1, SMALL_COLS)), const((1, BRANCH_W)),
                  const((RG_CONV, RG_WIDTH)), const((1, RG_WIDTH)),
                  slab((RG_WIDTH, RG_WIDTH)), const((1, RG_WIDTH)),
                  slab((RG_WIDTH, RG_WIDTH)), const((1, RG_WIDTH)), const((1, RG_WIDTH)),
                  slab((N_BRANCH, BRANCH_W, d)), slab((d, d))],
        out_specs=[pl.BlockSpec((tt, d), lambda b, j: (b * nt + j, 0)),
                   pl.BlockSpec((None, HEADS, DK, DV), lambda b, j: (b, 0, 0, 0)),
                   pl.BlockSpec((None, HEADS, DK, DV), lambda b, j: (b, 0, 0, 0)),
                   pl.BlockSpec((None, HEADS, 1, DK), lambda b, j: (b, 0, 0, 0)),
                   pl.BlockSpec((None, 1, HEADS), lambda b, j: (b, 0, 0)),
                   pl.BlockSpec((None, 1, RG_WIDTH), lambda b, j: (b, 0, 0)),
                   pl.BlockSpec((None, RG_CONV - 1, RG_WIDTH), lambda b, j: (b, 0, 0))],
        out_shape=[jax.ShapeDtypeStruct((nb * t, d), F32),
                   jax.ShapeDtypeStruct((nb, HEADS, DK, DV), F32),
                   jax.ShapeDtypeStruct((nb, HEADS, DK, DV), F32),
                   jax.ShapeDtypeStruct((nb, HEADS, 1, DK), F32),
                   jax.ShapeDtypeStruct((nb, 1, HEADS), F32),
                   jax.ShapeDtypeStruct((nb, 1, RG_WIDTH), F32),
                   jax.ShapeDtypeStruct((nb, RG_CONV - 1, RG_WIDTH), F32)],
        scratch_shapes=[pltpu.VMEM((tt, MIX_COLS), F32),
                        pltpu.VMEM((tt, SMALL_COLS), F32),
                        pltpu.VMEM((tt, d), F32),
                        pltpu.VMEM((tt, N_BRANCH * BRANCH_W), BF16),
                        pltpu.VMEM((tt, GATE_COLS), F32),
                        pltpu.VMEM((tt, d), BF16),
                        pltpu.VMEM((tt, BRANCH_W), F32),
                        pltpu.VMEM((tt + 8, RG_WIDTH), F32)],
        compiler_params=pltpu.CompilerParams(
            dimension_semantics=("parallel", "arbitrary"), vmem_limit_bytes=VMEM_LIMIT),
        name="mix_prompt",
    )(x, g.reshape(1, d), wts["w_main"], wts["w_small"],
      wts["wa2"], lw["ba"], lw["gnorm"], lw["bif"], lw["mnorm"], lw["convw"], lw["convb"],
      wts["wr"], lw["br"], wts["wi"], lw["bi"], lw["lam"], wts["wb"], wts["wo"])


def _mix_sample_kernel(pm_ref, ps_ref, s0_ref, c0_ref, n0_ref, m0_ref, h0_ref, buf0_ref,
                       wa2_ref, ba_ref, gnorm_ref, bif_ref, mnorm_ref,
                       convw_ref, convb_ref, wr_ref, br_ref, wi_ref, bi_ref, lam_ref,
                       br_out_ref, s_ref, c_ref, n_ref, m_ref, h_ref, buf_ref,
                       og_ref, om_ref):
    bs = pm_ref.shape[0]
    ps = ps_ref[...]
    a_pre = _dot(ps.astype(BF16), wa2_ref[...]) + ba_ref[...]
    a = jnp.exp(_log_sigmoid(a_pre) * (1.0 / GLA_GATE_NORM))
    a_hi = a.astype(BF16)
    a_r1 = a - a_hi.astype(F32)
    a_mid = a_r1.astype(BF16)
    a_lo = (a_r1 - a_mid.astype(F32)).astype(BF16)
    mq = pm_ref[:, C_MQ:C_MQ + HEADS * DK] * (DK ** -0.5)
    mk = pm_ref[:, C_MK:C_MK + HEADS * DK]
    kd = HEADS * DK

    pre = ps + bif_ref[...]
    f_log = pltpu.roll(_log_sigmoid(pre), SMALL_COLS - (L_MF - L_MI), 1)
    inter = f_log + m0_ref[...]
    m_t = jnp.maximum(inter, pre)
    lane = lax.broadcasted_iota(jnp.int32, (bs, SMALL_COLS), 1)
    is_head = (lane >= L_MI) & (lane < L_MI + HEADS)
    w_intra = jnp.where(is_head, jnp.exp(pre - m_t), 0.0)
    w_inter = jnp.where(is_head, jnp.exp(inter - m_t), 0.0)
    kd_row = lax.broadcasted_iota(jnp.int32, (kd, SMALL_COLS), 0)
    head_sum = jnp.where(
        lax.shift_right_logical(kd_row, DK.bit_length() - 1) + L_MI
        == lax.broadcasted_iota(jnp.int32, (kd, SMALL_COLS), 1), 1.0, 0.0).astype(BF16)
    sc = _dot_mask_exact_rhs(mq * mk, head_sum) * w_intra
    den = w_inter * _dot_mask_exact_rhs(mq * n0_ref[...], head_sum) + sc
    dmax = jnp.where(is_head, jnp.maximum(jnp.abs(den), jnp.exp(-m_t)), 1.0)
    m_ref[...] = m_t
    lane_head = lax.broadcasted_iota(jnp.int32, (SMALL_COLS, kd), 0) - L_MI
    head_spread = jnp.where(
        lane_head == lax.shift_right_logical(
            lax.broadcasted_iota(jnp.int32, (SMALL_COLS, kd), 1), DK.bit_length() - 1),
        1.0, 0.0).astype(BF16)
    n_ref[...] = (_dot_mask_exact_rhs(w_inter, head_spread) * n0_ref[...]
                  + _dot_mask_exact_rhs(w_intra, head_spread) * mk)

    scalars = jnp.concatenate([w_inter, w_intra, sc, dmax], axis=1)
    sc_hi = scalars.astype(BF16)
    sc_r1 = scalars - sc_hi.astype(F32)
    sc_mid = sc_r1.astype(BF16)
    sc_lo = (sc_r1 - sc_mid.astype(F32)).astype(BF16)
    n_scalar = 4

    def scalar_rows(piece):
        pt = piece.astype(F32).T
        return jnp.concatenate(
            [pt[i * SMALL_COLS + L_MI:i * SMALL_COLS + L_MI + 8, :] for i in range(n_scalar)],
            axis=0)

    cols_t = jnp.concatenate(
        [(pm_ref[:, C_GQ:C_GQ + kd] * (DK ** -0.5)).T, pm_ref[:, C_GK:C_GK + kd].T,
         mq.T, mk.T, a_hi.astype(F32).T, a_mid.astype(F32).T, a_lo.astype(F32).T,
         scalar_rows(sc_hi), scalar_rows(sc_mid), scalar_rows(sc_lo)],
        axis=0).astype(BF16)
    o_gq, o_gk, o_mq, o_mk, o_a0, o_a1, o_a2 = (i * kd for i in range(7))
    o_scalar = 7 * kd
    sample_id = lax.broadcasted_iota(jnp.int32, (bs, DV), 0)

    for s in range(bs):
        row = slice(s, s + 1)
        bc = _dot(cols_t, jnp.where(sample_id == s, 1.0, 0.0).astype(BF16))
        for h in range(HEADS):
            col = lambda off: bc[off + h * DK:off + (h + 1) * DK, :]

            def scalar(i):
                r0 = o_scalar + i * 8 + h
                return (bc[r0:r0 + 1, :] + bc[r0 + 8 * n_scalar:r0 + 8 * n_scalar + 1, :]
                        + bc[r0 + 16 * n_scalar:r0 + 16 * n_scalar + 1, :])

            v_row = pm_ref[row, C_GV + h * DV:C_GV + (h + 1) * DV]
            a_col = col(o_a0) + col(o_a1) + col(o_a2)
            s_new = a_col * s0_ref[s, h] + col(o_gk) * v_row
            s_ref[s, h] = s_new
            og_ref[row, h * DV:(h + 1) * DV] = jnp.sum(col(o_gq) * s_new,
                                                       axis=0, keepdims=True)
            v_row = pm_ref[row, C_MV + h * DV:C_MV + (h + 1) * DV]
            c_old = c0_ref[s, h]
            qc = jnp.sum(col(o_mq) * c_old, axis=0, keepdims=True)
            om_ref[row, h * DV:(h + 1) * DV] = (scalar(0) * qc + scalar(2) * v_row) / scalar(3)
            c_ref[s, h] = scalar(0) * c_old + (scalar(1) * col(o_mk)) * v_row

    gnorm = gnorm_ref[...]
    for h in range(HEADS):
        cols = slice(h * DV, (h + 1) * DV)
        y = _rms(og_ref[:, cols], gnorm) * _silu(pm_ref[:, C_GG + h * DV:C_GG + (h + 1) * DV])
        br_out_ref[:, cols] = y.astype(BF16)
        y = (_rms(om_ref[:, cols], mnorm_ref[:, cols])
             * _sigmoid(pm_ref[:, C_MO + h * DV:C_MO + (h + 1) * DV]))
        br_out_ref[:, BRANCH_W + h * DV:BRANCH_W + (h + 1) * DV] = y.astype(BF16)

    x = pm_ref[:, C_RX:C_RX + RG_WIDTH]
    xc = convb_ref[...] + convw_ref[3:4, :] * x
    for jj in range(RG_CONV - 1):
        xc = xc + convw_ref[jj:jj + 1, :] * buf0_ref[:, jj * RG_WIDTH:(jj + 1) * RG_WIDTH]
    buf_ref[:, 0:2 * RG_WIDTH] = buf0_ref[:, RG_WIDTH:3 * RG_WIDTH]
    buf_ref[:, 2 * RG_WIDTH:3 * RG_WIDTH] = x
    a, b = _rg_gates(xc, wr_ref, br_ref, wi_ref, bi_ref, lam_ref)
    hnew = a * h0_ref[...] + b
    h_ref[...] = hnew
    y = hnew * _gelu_tanh(pm_ref[:, C_RG:C_RG + RG_WIDTH])
    br_out_ref[:, 2 * BRANCH_W:3 * BRANCH_W] = y.astype(BF16)


def _mix_sample(pm, ps, st, wts, lw, layer, *, bs):
    n = pm.shape[0]
    const = lambda shape: pl.BlockSpec(shape, lambda i: (0,) * len(shape))
    slab = lambda shape: pl.BlockSpec((None,) + shape, lambda i: (layer,) + (0,) * len(shape))
    blk = lambda shape: pl.BlockSpec(shape, lambda i: (i,) + (0,) * (len(shape) - 1))
    lblk = lambda shape: pl.BlockSpec((None,) + shape,
                                      lambda i: (layer, i) + (0,) * (len(shape) - 1))
    s0, c0, n0, m0, h0, buf0 = st
    shapes = [(bs, HEADS, DK, DV), (bs, HEADS, DK, DV), (bs, HEADS * DK),
              (bs, SMALL_COLS), (bs, RG_WIDTH), (bs, (RG_CONV - 1) * RG_WIDTH)]
    return pl.pallas_call(
        _mix_sample_kernel,
        grid=(n // bs,),
        in_specs=[blk((bs, MIX_COLS)), blk((bs, SMALL_COLS))] + [lblk(sh) for sh in shapes] + [
            slab((SMALL_COLS, HEADS * DK)), const((1, HEADS * DK)), const((1, DV)),
            const((1, SMALL_COLS)), const((1, BRANCH_W)),
            const((RG_CONV, RG_WIDTH)), const((1, RG_WIDTH)),
            slab((RG_WIDTH, RG_WIDTH)), const((1, RG_WIDTH)),
            slab((RG_WIDTH, RG_WIDTH)), const((1, RG_WIDTH)), const((1, RG_WIDTH))],
        out_specs=[blk((bs, N_BRANCH * BRANCH_W))] + [blk(sh) for sh in shapes],
        out_shape=[jax.ShapeDtypeStruct((n, N_BRANCH * BRANCH_W), BF16)]
        + [jax.ShapeDtypeStruct(a.shape[1:], F32) for a in st],
        scratch_shapes=[pltpu.VMEM((bs, BRANCH_W), F32),
                        pltpu.VMEM((bs, BRANCH_W), F32)],
        compiler_params=pltpu.CompilerParams(
            dimension_semantics=("parallel",), vmem_limit_bytes=VMEM_LIMIT),
        name="mix_sample",
    )(pm, ps, s0, c0, n0, m0, h0, buf0,
      wts["wa2"], lw["ba"], lw["gnorm"], lw["bif"], lw["mnorm"], lw["convw"], lw["convb"],
      wts["wr"], lw["br"], wts["wi"], lw["bi"], lw["lam"])


def _merge_kernel(x_ref, br_ref, g0_ref, g1_ref, g2_ref, wb_ref, wo_ref, o_ref):
    merged = None
    for b, g_ref in enumerate((g0_ref, g1_ref, g2_ref)):
        z = _dot(br_ref[:, b * BRANCH_W:(b + 1) * BRANCH_W], wb_ref[b])
        term = _sigmoid(g_ref[...]) * z
        merged = term if merged is None else merged + term
    o_ref[...] = x_ref[...] + _dot(merged.astype(BF16), wo_ref[...])


def _merge(x, br, pm, wb, wo, layer, *, tm):
    n, d = x.shape
    gate_blk0 = MIX_COLS // d
    gate_spec = lambda b: pl.BlockSpec((tm, d), lambda i: (i, gate_blk0 + b))
    return pl.pallas_call(
        _merge_kernel,
        grid=(n // tm,),
        in_specs=[pl.BlockSpec((tm, d), lambda i: (i, 0)),
                  pl.BlockSpec((tm, N_BRANCH * BRANCH_W), lambda i: (i, 0)),
                  gate_spec(0), gate_spec(1), gate_spec(2),
                  pl.BlockSpec((None, N_BRANCH, BRANCH_W, d), lambda i: (layer, 0, 0, 0)),
                  pl.BlockSpec((None, d, d), lambda i: (layer, 0, 0))],
        out_specs=pl.BlockSpec((tm, d), lambda i: (i, 0)),
        out_shape=jax.ShapeDtypeStruct((n, d), F32),
        compiler_params=pltpu.CompilerParams(
            dimension_semantics=("parallel",), vmem_limit_bytes=VMEM_LIMIT),
        name="merge",
    )(x, br, pm, pm, pm, wb, wo)


def _xattn_ffn_prompt_kernel(x_ref, gx_ref, wq_ref, wo_ref, k_ref, v_ref,
                             gf_ref, wg_ref, wu_ref, wd_ref, gfin_ref,
                             o_ref, kb_ref, vb_ref, hid_ref, *, final_norm):
    @pl.when(pl.program_id(1) == 0)
    def _():
        kb_ref[...] = k_ref[...].astype(BF16)
        vb_ref[...] = v_ref[...].astype(BF16)

    x = x_ref[...]
    q = _dot(_rms(x, gx_ref[...]).astype(BF16), wq_ref[...]).astype(BF16)
    heads = []
    for h in range(XA_HEADS):
        cols = slice(h * XA_HEAD_DIM, (h + 1) * XA_HEAD_DIM)
        s = _dot_nt(q[:, cols], kb_ref[:, cols]) * (XA_HEAD_DIM ** -0.5)
        p = jnp.exp(s - jnp.max(s, axis=-1, keepdims=True))
        p = p / jnp.sum(p, axis=-1, keepdims=True)
        heads.append(_dot(p.astype(BF16), vb_ref[:, cols]))
    o = jnp.concatenate(heads, axis=1).astype(BF16)
    x = x + _dot(o, wo_ref[...])
    o_ref[...] = _ffn_block(x, gf_ref, wg_ref, wu_ref, wd_ref, gfin_ref, hid_ref, final_norm)


def _xattn_ffn_prompt(x, gx, wq, wo, mem_k, mem_v, gf, wg, wu, wd, gfin, layer, *,
                      nb, t, tm, final_norm):
    d = x.shape[1]
    f = wg.shape[2]
    nt = t // tm
    const = lambda shape: pl.BlockSpec(shape, lambda b, j: (0,) * len(shape),
                                       pipeline_mode=pl.Buffered(1))
    slab = lambda shape: pl.BlockSpec((None,) + shape, lambda b, j: (layer,) + (0,) * len(shape),
                                      pipeline_mode=pl.Buffered(1))
    return pl.pallas_call(
        functools.partial(_xattn_ffn_prompt_kernel, final_norm=final_norm),
        grid=(nb, nt),
        in_specs=[pl.BlockSpec((tm, d), lambda b, j: (b * nt + j, 0)),
                  const((1, d)), slab((d, d)), slab((d, d)),
                  pl.BlockSpec((None, N_MEM, d), lambda b, j: (b, 0, 0)),
                  pl.BlockSpec((None, N_MEM, d), lambda b, j: (b, 0, 0)),
                  const((1, d)), slab((d, f)), slab((d, f)), slab((f, d)), const((1, d))],
        out_specs=pl.BlockSpec((tm, d), lambda b, j: (b * nt + j, 0)),
        out_shape=jax.ShapeDtypeStruct(x.shape, F32),
        scratch_shapes=[pltpu.VMEM((N_MEM, d), BF16), pltpu.VMEM((N_MEM, d), BF16),
                        pltpu.VMEM((tm, f), BF16)],
        compiler_params=pltpu.CompilerParams(
            dimension_semantics=("parallel", "arbitrary"), vmem_limit_bytes=VMEM_LIMIT),
        name="xattn_ffn_prompt",
    )(x, gx.reshape(1, d), wq, wo, mem_k, mem_v, gf.reshape(1, d), wg, wu, wd, gfin.reshape(1, d))


def _xattn_sample_kernel(q_ref, k_ref, v_ref, o_ref):
    bs = q_ref.shape[0]
    nrow = N_MEM * XA_HEADS
    row_head = lax.broadcasted_iota(jnp.int32, (XA_HEADS, nrow), 1) & (XA_HEADS - 1)
    own_head = row_head == lax.broadcasted_iota(jnp.int32, (XA_HEADS, nrow), 0)
    for s in range(bs):
        q = q_ref[s].astype(BF16)
        k2 = k_ref[s].reshape(nrow, XA_HEAD_DIM).astype(BF16)
        v2 = v_ref[s].reshape(nrow, XA_HEAD_DIM).astype(BF16)
        sc = jnp.where(own_head, _dot_nt(q, k2) * (XA_HEAD_DIM ** -0.5), -jnp.inf)
        p = jnp.exp(sc - jnp.max(sc, axis=1, keepdims=True))
        p = p / jnp.sum(p, axis=1, keepdims=True)
        o_ref[s] = _dot(p.astype(BF16), v2)


def _xattn_sample(q, cache_k, cache_v, layer, *, bs):
    n = q.shape[0]
    blk = (None, bs, N_MEM, XA_HEADS, XA_HEAD_DIM)
    return pl.pallas_call(
        _xattn_sample_kernel,
        grid=(n // bs,),
        in_specs=[pl.BlockSpec((bs, XA_HEADS, XA_HEAD_DIM), lambda i: (i, 0, 0)),
                  pl.BlockSpec(blk, lambda i: (layer, i, 0, 0, 0)),
                  pl.BlockSpec(blk, lambda i: (layer, i, 0, 0, 0))],
        out_specs=pl.BlockSpec((bs, XA_HEADS, XA_HEAD_DIM), lambda i: (i, 0, 0)),
        out_shape=jax.ShapeDtypeStruct((n, XA_HEADS, XA_HEAD_DIM), F32),
        compiler_params=pltpu.CompilerParams(
            dimension_semantics=("parallel",), vmem_limit_bytes=VMEM_LIMIT),
        name="xattn_sample",
    )(q, cache_k, cache_v)


def _matmul_residual_kernel(x_ref, a_ref, w_ref, o_ref):
    o_ref[...] = x_ref[...] + _dot(a_ref[...].astype(BF16), w_ref[...])


def _matmul_residual(x, a, w, layer):
    n, d = x.shape
    return pl.pallas_call(
        _matmul_residual_kernel,
        grid=(1,),
        in_specs=[pl.BlockSpec((n, d), lambda i: (0, 0)),
                  pl.BlockSpec((n, d), lambda i: (0, 0)),
                  pl.BlockSpec((None, d, d), lambda i: (layer, 0, 0))],
        out_specs=pl.BlockSpec((n, d), lambda i: (0, 0)),
        out_shape=jax.ShapeDtypeStruct((n, d), F32),
        compiler_params=pltpu.CompilerParams(vmem_limit_bytes=VMEM_LIMIT),
        name="matmul_residual",
    )(x, a, w)


def _ffn_block(x, g_ref, wg_ref, wu_ref, wd_ref, gfin_ref, hid_ref, final_norm):
    u = _rms(x, g_ref[...]).astype(BF16)
    for jt in range(D_FF // FFN_TILE):
        cols = slice(jt * FFN_TILE, (jt + 1) * FFN_TILE)
        hid = _silu(_dot(u, wg_ref[:, cols])) * _dot(u, wu_ref[:, cols])
        hid_ref[:, cols] = hid.astype(BF16)
    y = x + _dot(hid_ref[...], wd_ref[...])
    if final_norm:
        y = _rms(y, gfin_ref[...])
    return y


def _ffn_kernel(x_ref, g_ref, wg_ref, wu_ref, wd_ref, gf_ref, o_ref, hid_ref, *, final_norm):
    o_ref[...] = _ffn_block(x_ref[...], g_ref, wg_ref, wu_ref, wd_ref, gf_ref, hid_ref, final_norm)


def _ffn(x, g, wg, wu, wd, gf, layer, *, tm, final_norm):
    n, d = x.shape
    f = wg.shape[2]
    const = lambda shape: pl.BlockSpec(shape, lambda i: (0,) * len(shape),
                                       pipeline_mode=pl.Buffered(1))
    slab = lambda shape: pl.BlockSpec((None,) + shape, lambda i: (layer,) + (0,) * len(shape),
                                      pipeline_mode=pl.Buffered(1))
    return pl.pallas_call(
        functools.partial(_ffn_kernel, final_norm=final_norm),
        grid=(n // tm,),
        in_specs=[pl.BlockSpec((tm, d), lambda i: (i, 0)),
                  const((1, d)), slab((d, f)), slab((d, f)), slab((f, d)), const((1, d))],
        out_specs=pl.BlockSpec((tm, d), lambda i: (i, 0)),
        out_shape=jax.ShapeDtypeStruct((n, d), F32),
        scratch_shapes=[pltpu.VMEM((tm, f), BF16)],
        compiler_params=pltpu.CompilerParams(
            dimension_semantics=("parallel",), vmem_limit_bytes=VMEM_LIMIT),
        name="ffn",
    )(x, g.reshape(1, d), wg, wu, wd, gf.reshape(1, d))


def _split_w_in(w_in):
    widths = (HEADS * DK, HEADS * DK, HEADS * DV, HEADS * DV, GLA_RANK,
              HEADS * DK, HEADS * DK, HEADS * DV, HEADS * DV, HEADS, HEADS,
              RG_WIDTH, RG_WIDTH, GATE_COLS)
    parts, acc = [], 0
    for w in widths:
        parts.append(w_in[..., acc:acc + w])
        acc += w
    return parts


def _block_diag(w):
    eye = jnp.eye(RG_BLOCKS, dtype=w.dtype)
    return jnp.einsum("lnde,nm->lndme", w, eye).reshape(DEPTH, RG_WIDTH, RG_WIDTH)


def _matmul_weights(p):
    (g_q, g_k, g_v, g_g, g_a, m_q, m_k, m_v, m_o, m_i, m_f, r_x, r_g, gates) = _split_w_in(p["w_in"])
    d = D_MODEL
    w_main = jnp.concatenate([g_q, g_k, g_v, g_g, m_q, m_k, m_v, m_o, r_x, r_g, gates], axis=2)
    w_small = jnp.concatenate(
        [g_a, m_i, m_f, jnp.zeros((DEPTH, d, SMALL_COLS - GLA_RANK - 2 * HEADS), F32)], axis=2)
    wa2 = jnp.concatenate(
        [p["gla_w_a2"], jnp.zeros((DEPTH, SMALL_COLS - GLA_RANK, HEADS * DK), F32)], axis=1)
    return {
        "w_main": w_main.astype(BF16), "w_small": w_small.astype(BF16), "wa2": wa2.astype(BF16),
        "wr": _block_diag(p["rg_w_r"]).astype(BF16), "wi": _block_diag(p["rg_w_i"]).astype(BF16),
        "wb": p["w_branch"].astype(BF16), "wo": p["w_out"].astype(BF16),
        "xa_wq": p["xa_wq"].astype(BF16), "xa_wo": p["xa_wo"].astype(BF16),
        "xa_wkv": jnp.concatenate([p["xa_wk"], p["xa_wv"]], axis=2).astype(BF16),
        "wg": p["ffn_w_gate"].astype(BF16), "wu": p["ffn_w_up"].astype(BF16),
        "wd": p["ffn_w_down"].astype(BF16),
    }


def _vector_params(l, p):
    bif = jnp.concatenate(
        [jnp.zeros((L_MI,), F32), p["mlstm_b_i"][l], p["mlstm_b_f"][l],
         jnp.zeros((SMALL_COLS - L_MF - HEADS,), F32)]).reshape(1, SMALL_COLS)
    return {
        "ba": p["gla_b_a"][l].reshape(1, -1), "gnorm": p["gla_norm"][l].reshape(1, DV),
        "bif": bif, "mnorm": p["mlstm_norm"][l].reshape(1, BRANCH_W),
        "convw": p["rg_conv_w"][l], "convb": p["rg_conv_b"][l].reshape(1, -1),
        "br": p["rg_b_r"][l].reshape(1, -1), "bi": p["rg_b_i"][l].reshape(1, -1),
        "lam": p["rg_lambda"][l].reshape(1, -1),
    }


def kernel(x_prompt, x_sample, mem_prompt, cache_mem_k, cache_mem_v, state_gla, state_mlstm_c, state_mlstm_n, state_mlstm_m, state_rglru_h, state_rglru_conv, norm_mix, w_in, gla_w_a2, gla_b_a, gla_norm, mlstm_b_i, mlstm_b_f, mlstm_norm, rg_conv_w, rg_conv_b, rg_w_r, rg_b_r, rg_w_i, rg_b_i, rg_lambda, w_branch, w_out, norm_xa, norm_mem, xa_wq, xa_wk, xa_wv, xa_wo, norm_ffn, ffn_w_gate, ffn_w_up, ffn_w_down, norm_final):
    p = {"w_in": w_in, "gla_w_a2": gla_w_a2, "gla_b_a": gla_b_a, "gla_norm": gla_norm,
         "mlstm_b_i": mlstm_b_i, "mlstm_b_f": mlstm_b_f, "mlstm_norm": mlstm_norm,
         "rg_conv_w": rg_conv_w, "rg_conv_b": rg_conv_b, "rg_w_r": rg_w_r, "rg_b_r": rg_b_r,
         "rg_w_i": rg_w_i, "rg_b_i": rg_b_i, "rg_lambda": rg_lambda, "w_branch": w_branch,
         "w_out": w_out, "xa_wq": xa_wq, "xa_wk": xa_wk, "xa_wv": xa_wv, "xa_wo": xa_wo,
         "ffn_w_gate": ffn_w_gate, "ffn_w_up": ffn_w_up, "ffn_w_down": ffn_w_down}
    nb, t, d = x_prompt.shape
    ns = x_sample.shape[0]
    xp = x_prompt.reshape(nb * t, d)
    xs = x_sample.reshape(ns, d)
    mem = mem_prompt.reshape(nb * N_MEM, d)
    wts = _matmul_weights(p)

    st_sample = (state_gla, state_mlstm_c, state_mlstm_n.reshape(DEPTH, ns, HEADS * DK),
                 jnp.pad(state_mlstm_m, ((0, 0), (0, 0), (L_MI, SMALL_COLS - L_MI - HEADS))),
                 state_rglru_h,
                 state_rglru_conv.reshape(DEPTH, ns, (RG_CONV - 1) * RG_WIDTH))

    new_p = [[] for _ in range(8)]
    new_s = [[] for _ in range(6)]
    for l in range(DEPTH):
        lw = _vector_params(l, p)
        last = l == DEPTH - 1

        k_p, v_p = _norm_matmul(mem, norm_mem[l], wts["xa_wkv"], l, tm=PROMPT_ROWS, tn=d)
        k_p = k_p.reshape(nb, N_MEM, d)
        v_p = v_p.reshape(nb, N_MEM, d)

        xp, g_s, c_s, n_s, m_s, h_s, buf_s = _mix_prompt(xp, norm_mix[l], wts, lw, l,
                                                         nb=nb, t=t, tt=MIX_TIME_BLOCK)
        xp = _xattn_ffn_prompt(xp, norm_xa[l], wts["xa_wq"], wts["xa_wo"], k_p, v_p,
                               norm_ffn[l], wts["wg"], wts["wu"], wts["wd"], norm_final, l,
                               nb=nb, t=t, tm=PROMPT_ROWS, final_norm=last)
        for lst, val in zip(new_p, (k_p.reshape(nb, N_MEM, XA_HEADS, XA_HEAD_DIM),
                                    v_p.reshape(nb, N_MEM, XA_HEADS, XA_HEAD_DIM),
                                    g_s, c_s, n_s.reshape(nb, HEADS, DK), m_s.reshape(nb, HEADS),
                                    h_s.reshape(nb, RG_WIDTH), buf_s)):
            lst.append(val)

        pm, ps = _in_proj(xs, norm_mix[l], wts["w_main"], wts["w_small"], l,
                          tm=ns, tn=IN_PROJ_COLS)
        br, g_s, c_s, n_s, m_s, h_s, buf_s = _mix_sample(pm, ps, st_sample, wts, lw, l,
                                                         bs=SAMPLE_BLOCK)
        xs = _merge(xs, br, pm, wts["wb"], wts["wo"], l, tm=ns)
        (q,) = _norm_matmul(xs, norm_xa[l], wts["xa_wq"], l, tm=ns, tn=d)
        att = _xattn_sample(q.reshape(ns, XA_HEADS, XA_HEAD_DIM), cache_mem_k, cache_mem_v, l,
                            bs=SAMPLE_BLOCK)
        xs = _matmul_residual(xs, att.reshape(ns, d), wts["xa_wo"], l)
        xs = _ffn(xs, norm_ffn[l], wts["wg"], wts["wu"], wts["wd"], norm_final, l,
                  tm=ns, final_norm=last)
        for lst, val in zip(new_s, (g_s, c_s, n_s.reshape(ns, HEADS, DK),
                                    m_s[:, L_MI:L_MI + HEADS], h_s,
                                    buf_s.reshape(ns, RG_CONV - 1, RG_WIDTH))):
            lst.append(val)

    y_prompt = xp.reshape(nb, t, d)
    y_sample = xs.reshape(ns, 1, d)
    outs_p = [jnp.stack(v, axis=0) for v in new_p]
    outs_s = [jnp.stack(v, axis=0) for v in new_s]
    return (y_prompt, y_sample, *outs_p, *outs_s)
```
